```python
import math
import jax, jax.numpy as jnp
from jax import lax
import numpy as np

D_MODEL = 1024
BATCH = 8
SEQ = 2048
DEPTH = 4

CTX_LEN = 256
GRID_W = 64

D_GMLP = D_MODEL // 4
D_NA = D_MODEL // 2
D_HYENA = D_MODEL // 4
D_MIX = D_GMLP + D_NA + D_HYENA
D_IN = 2 * D_GMLP + 3 * D_NA + 3 * D_HYENA
KV_START = 2 * D_GMLP + D_NA
HY_START = 2 * D_GMLP + 3 * D_NA

GMLP_GROUPS = 4
GMLP_GROUP_DIM = D_GMLP // GMLP_GROUPS
GMLP_CHUNK = 128

NA_HEAD_DIM = 64
NA_HEADS = D_NA // NA_HEAD_DIM
NA_SCALE = NA_HEAD_DIM ** -0.5
NA_WIN_ROWS = 8
NA_WIN_COLS = 16
NA_COL_BLOCK = 16
NA_ROW_BLOCK_CHOICES = (8, 4, 2, 1)

HYENA_ORDER = 2
HYENA_SHORT = 3
HYENA_POS_BANDS = 16
HYENA_EMB = 1 + 2 * HYENA_POS_BANDS
HYENA_FILTER_HIDDEN = 64
HYENA_DECAY_TARGET = 1e-2
HYENA_FAST_DECAY = 0.3
HYENA_SLOW_DECAY = 1.5

MOE_GROUPS = 4
MOE_EXPERTS_PER_GROUP = 8
MOE_TOP_K = 2
MOE_HIDDEN = 256

N_MOD = 6
RMS_EPS = 1e-6
LN_EPS = 1e-5

kernel_name = "hybrid_gmlp_natten_hyena_hmoe_dit"


def rms_norm(x, gain):
    xf = x.astype(jnp.float32)
    xf = xf * lax.rsqrt(jnp.mean(jnp.square(xf), axis=-1, keepdims=True) + RMS_EPS)
    return (xf * gain.astype(jnp.float32)).astype(x.dtype)


def layer_norm(x, gain):
    xf = x.astype(jnp.float32)
    xf = xf - jnp.mean(xf, axis=-1, keepdims=True)
    xf = xf * lax.rsqrt(jnp.mean(jnp.square(xf), axis=-1, keepdims=True) + LN_EPS)
    return (xf * gain.astype(jnp.float32)).astype(x.dtype)


def adaln(x, gain, shift, scale):
    return rms_norm(x, gain) * (1 + scale) + shift


def to_heads(t):
    return t.reshape(*t.shape[:-1], NA_HEADS, NA_HEAD_DIM)


def qk_norm_heads(t, gain):
    return rms_norm(to_heads(t), gain)


def gmlp_mix(a_u, a_v, v_gain, w_s, b_s):
    bsz, length, _ = a_u.shape
    u = jax.nn.gelu(a_u)
    v = jax.nn.gelu(a_v).reshape(bsz, length // GMLP_CHUNK, GMLP_CHUNK, GMLP_GROUPS, GMLP_GROUP_DIM)
    v = layer_norm(v, v_gain.reshape(GMLP_GROUPS, GMLP_GROUP_DIM))
    s = jnp.einsum('gij,bnjgd->bnigd', w_s, v) + b_s.T[:, :, None]
    return u * s.reshape(bsz, length, D_GMLP)


def na_geometry(rows):
    kr = min(NA_WIN_ROWS, rows)
    kc = NA_WIN_COLS
    qr = next(b for b in NA_ROW_BLOCK_CHOICES if rows % b == 0)
    qc = NA_COL_BLOCK
    br = min(kr + qr, rows)
    bc = min(kc + qc, GRID_W)
    n_rb, n_cb = rows // qr, GRID_W // qc
    row_idx = np.clip(np.arange(n_rb) * qr - kr // 2, 0, rows - br)[:, None] + np.arange(br)
    col_idx = np.clip(np.arange(n_cb) * qc - kc // 2, 0, GRID_W - bc)[:, None] + np.arange(bc)
    q_row = np.arange(n_rb)[:, None] * qr + np.arange(qr)
    q_col = np.arange(n_cb)[:, None] * qc + np.arange(qc)
    win_r = np.clip(q_row - kr // 2, 0, rows - kr)
    win_c = np.clip(q_col - kc // 2, 0, GRID_W - kc)
    qrow_b = q_row[:, None, :, None, None, None]
    wr_b = win_r[:, None, :, None, None, None]
    krow_b = row_idx[:, None, None, None, :, None]
    qcol_b = q_col[None, :, None, :, None, None]
    wc_b = win_c[None, :, None, :, None, None]
    kcol_b = col_idx[None, :, None, None, None, :]
    full = (n_rb, n_cb, qr, qc, br, bc)
    mask = np.broadcast_to((krow_b >= wr_b) & (krow_b < wr_b + kr) & (kcol_b >= wc_b) & (kcol_b < wc_b + kc), full)
    dr = np.broadcast_to(np.clip(krow_b - qrow_b + NA_WIN_ROWS - 1, 0, 2 * NA_WIN_ROWS - 2), full)
    dc = np.broadcast_to(np.clip(kcol_b - qcol_b + NA_WIN_COLS - 1, 0, 2 * NA_WIN_COLS - 2), full)
    return qr, br, bc, n_rb, n_cb, row_idx, col_idx, mask, dr, dc


def neighbourhood_attention(q, k, v, k_ctx, v_ctx, rpb):
    bsz, seq, n_heads, dh = q.shape
    rows = seq // GRID_W
    qr, br, bc, n_rb, n_cb, row_idx, col_idx, mask, dr, dc = na_geometry(rows)
    qg = q.reshape(bsz, n_rb, qr, n_cb, NA_COL_BLOCK, n_heads, dh)
    kg = k.reshape(bsz, rows, GRID_W, n_heads, dh)[:, :, col_idx][:, row_idx]
    vg = v.reshape(bsz, rows, GRID_W, n_heads, dh)[:, :, col_idx][:, row_idx]
    s_win = jnp.einsum('bipjqhd,biajchd->bijhpqac', qg, kg).astype(jnp.float32)
    bias = jnp.moveaxis(rpb[:, dr, dc], 0, 2).astype(jnp.float32)
    s_win = jnp.where(mask[:, :, None], s_win + bias, -jnp.inf)
    s_win = s_win.reshape(*s_win.shape[:6], br * bc)
    s_ctx = jnp.einsum('bipjqhd,bkhd->bijhpqk', qg, k_ctx).astype(jnp.float32)
    p = jax.nn.softmax(jnp.concatenate([s_win, s_ctx], axis=-1), axis=-1).astype(v.dtype)
    p_win = p[..., :br * bc].reshape(*p.shape[:6], br, bc)
    p_ctx = p[..., br * bc:]
    o = jnp.einsum('bijhpqac,biajchd->bipjqhd', p_win, vg) + jnp.einsum('bijhpqk,bkhd->bipjqhd', p_ctx, v_ctx)
    return o.reshape(bsz, seq, n_heads * dh)


def context_attention(q, k, v):
    bsz, length = q.shape[:2]
    s = jnp.einsum('bqhd,bkhd->bhqk', q, k).astype(jnp.float32)
    p = jax.nn.softmax(s, axis=-1).astype(v.dtype)
    return jnp.einsum('bhqk,bkhd->bqhd', p, v).reshape(bsz, length, NA_HEADS * NA_HEAD_DIM)


def hyena_positional_features(length):
    t = jnp.linspace(0.0, 1.0, length, dtype=jnp.float32)[:, None]
    w = 2.0 * math.pi * jnp.arange(length, dtype=jnp.float32)[:, None] / length
    f = jnp.linspace(1e-4, HYENA_POS_BANDS - 1, HYENA_POS_BANDS, dtype=jnp.float32)[None, :]
    return jnp.concatenate([t, jnp.cos(f * w), -jnp.sin(f * w)], axis=-1)


def hyena_filters(length, w1, b1, w2, b2, w3, freq):
    f32 = jnp.float32
    z = hyena_positional_features(length)
    hdn = jnp.sin(freq[0].astype(f32) * (z @ w1.astype(f32) + b1.astype(f32)))
    hdn = jnp.sin(freq[1].astype(f32) * (hdn @ w2.astype(f32) + b2.astype(f32)))
    h = (hdn @ w3.astype(f32)).reshape(length, HYENA_ORDER, 2, D_HYENA)
    t = jnp.linspace(0.0, 1.0, length, dtype=f32)[:, None]
    min_decay = math.log(HYENA_DECAY_TARGET) / HYENA_SLOW_DECAY
    max_decay = math.log(HYENA_DECAY_TARGET) / HYENA_FAST_DECAY
    deltas = jnp.abs(jnp.linspace(min_decay, max_decay, D_HYENA, dtype=f32))[None, :]
    h = h * jnp.exp(-t * deltas)[:, None, None, :]
    return h / jnp.sum(jnp.abs(h), axis=(0, 2), keepdims=True)


def bidirectional_long_conv(u, h_fwd, h_bwd):
    length = u.shape[1]
    k = jnp.concatenate([h_fwd, jnp.zeros_like(h_fwd[:1]), h_bwd[:0:-1]], axis=0)
    kf = jnp.fft.rfft(k, n=2 * length, axis=0)
    uf = jnp.fft.rfft(u, n=2 * length, axis=1)
    return jnp.fft.irfft(uf * kf[None], n=2 * length, axis=1)[:, :length]


def hyena_mix(a, short_w, short_b, h, d_bias):
    n_ch = a.shape[-1]
    pad = HYENA_SHORT // 2
    a = lax.conv_general_dilated(a, short_w[:, None, :].astype(a.dtype), window_strides=(1,),
                                 padding=((pad, pad),), dimension_numbers=('NWC', 'WIO', 'NWC'),
                                 feature_group_count=n_ch) + short_b
    v, x1, x2 = jnp.split(a, 3, axis=-1)
    z = v.astype(jnp.float32)
    for n, gate in enumerate((x1, x2)):
        z = gate.astype(jnp.float32) * (bidirectional_long_conv(z, h[:, n, 0], h[:, n, 1])
                                        + d_bias[n].astype(jnp.float32) * z)
    return z.astype(a.dtype)


def hier_moe(h, w_rg, b_rg, w_re, b_re, w_gate, w_up, w_down):
    bsz, length, d = h.shape
    t = h.reshape(bsz * length, d)
    g_prob = jax.nn.softmax((t @ w_rg + b_rg).astype(jnp.float32), axis=-1)
    g_p, g_idx = lax.top_k(g_prob, 1)
    g_onehot = jax.nn.one_hot(g_idx[:, 0], MOE_GROUPS, dtype=jnp.float32)
    e_logits = (t @ w_re + b_re).astype(jnp.float32).reshape(-1, MOE_GROUPS, MOE_EXPERTS_PER_GROUP)
    e_prob = jax.nn.softmax(jnp.einsum('tge,tg->te', e_logits, g_onehot), axis=-1)
    e_p, e_idx = lax.top_k(e_prob, MOE_TOP_K)
    e_p = e_p / jnp.sum(e_p, axis=-1, keepdims=True)
    e_w = jnp.einsum('tk,tke->te', e_p, jax.nn.one_hot(e_idx, MOE_EXPERTS_PER_GROUP, dtype=jnp.float32))
    combine = (g_p[:, :, None] * g_onehot[:, :, None] * e_w[:, None, :]).astype(h.dtype)
    out = jnp.zeros_like(t)
    for g in range(MOE_GROUPS):
        act = jax.nn.silu(jnp.einsum('td,edf->tef', t, w_gate[g])) * jnp.einsum('td,edf->tef', t, w_up[g])
        out = out + jnp.einsum('tef,efd->td', act * combine[:, g, :, None], w_down[g])
    return out.reshape(bsz, length, d)


def setup_inputs(seed: int = 0) -> dict:
    key = jax.random.key(seed)
    ks = jax.random.split(key, 32)
    f32 = jnp.float32

    def nrm(k, shape, scale):
        return scale * jax.random.normal(k, shape, f32)

    D, L_ = D_MODEL, DEPTH
    G, E, F = MOE_GROUPS, MOE_EXPERTS_PER_GROUP, MOE_HIDDEN
    return {
        'x': nrm(ks[0], (BATCH, SEQ, D), 1.0),
        'c': nrm(ks[1], (BATCH, D), 1.0),
        'ctx': nrm(ks[2], (BATCH, CTX_LEN, D), 1.0),
        'c_ctx': nrm(ks[3], (D,), 1.0),
        'w_ada': nrm(ks[4], (L_, D, N_MOD * D), 0.5 * D ** -0.5),
        'b_ada': nrm(ks[5], (L_, N_MOD * D), 0.02),
        'g_mix': 1.0 + nrm(ks[6], (L_, D), 0.05),
        'g_ffn': 1.0 + nrm(ks[7], (L_, D), 0.05),
        'w_in': nrm(ks[8], (L_, D, D_IN), D ** -0.5),
        'w_out': nrm(ks[9], (L_, D_MIX, D), D_MIX ** -0.5),
        'gmlp_v_gain': 1.0 + nrm(ks[10], (L_, D_GMLP), 0.05),
        'gmlp_ws': nrm(ks[11], (L_, GMLP_GROUPS, GMLP_CHUNK, GMLP_CHUNK), GMLP_CHUNK ** -0.5),
        'gmlp_bs': 1.0 + nrm(ks[12], (L_, GMLP_GROUPS, GMLP_CHUNK), 0.05),
        'na_q_gain': 1.0 + nrm(ks[13], (L_, NA_HEAD_DIM), 0.05),
        'na_k_gain': 1.0 + nrm(ks[14], (L_, NA_HEAD_DIM), 0.05),
        'na_rpb': nrm(ks[15], (L_, NA_HEADS, 2 * NA_WIN_ROWS - 1, 2 * NA_WIN_COLS - 1), 0.2),
        'hy_short_w': nrm(ks[16], (L_, HYENA_SHORT, 3 * D_HYENA), HYENA_SHORT ** -0.5),
        'hy_short_b': nrm(ks[17], (L_, 3 * D_HYENA), 0.02),
        'hy_w1': nrm(ks[18], (L_, HYENA_EMB, HYENA_FILTER_HIDDEN), HYENA_EMB ** -0.5),
        'hy_b1': nrm(ks[19], (L_, HYENA_FILTER_HIDDEN), 0.02),
        'hy_w2': nrm(ks[20], (L_, HYENA_FILTER_HIDDEN, HYENA_FILTER_HIDDEN), HYENA_FILTER_HIDDEN ** -0.5),
        'hy_b2': nrm(ks[21], (L_, HYENA_FILTER_HIDDEN), 0.02),
        'hy_w3': nrm(ks[22], (L_, HYENA_FILTER_HIDDEN, HYENA_ORDER * 2 * D_HYENA), HYENA_FILTER_HIDDEN ** -0.5),
        'hy_freq': 1.0 + nrm(ks[23], (L_, 2, HYENA_FILTER_HIDDEN), 0.05),
        'hy_bias': nrm(ks[24], (L_, HYENA_ORDER, D_HYENA), 0.1),
        'moe_w_rg': nrm(ks[25], (L_, D, G), D ** -0.5),
        'moe_b_rg': nrm(ks[26], (L_, G), 0.01),
        'moe_w_re': nrm(ks[27], (L_, D, G * E), D ** -0.5),
        'moe_b_re': nrm(ks[28], (L_, G * E), 0.01),
        'moe_w_gate': nrm(ks[29], (L_, G, E, D, F), D ** -0.5),
        'moe_w_up': nrm(ks[30], (L_, G, E, D, F), D ** -0.5),
        'moe_w_down': nrm(ks[31], (L_, G, E, F, D), F ** -0.5),
    }


def reference(x, c, ctx, c_ctx, w_ada, b_ada, g_mix, g_ffn, w_in, w_out,
              gmlp_v_gain, gmlp_ws, gmlp_bs, na_q_gain, na_k_gain, na_rpb,
              hy_short_w, hy_short_b, hy_w1, hy_b1, hy_w2, hy_b2, hy_w3, hy_freq, hy_bias,
              moe_w_rg, moe_b_rg, moe_w_re, moe_b_re, moe_w_gate, moe_w_up, moe_w_down):
    seq = x.shape[1]
    silu_c = jax.nn.silu(c)[:, None, :]
    silu_cc = jax.nn.silu(c_ctx)
    xc = ctx
    for l in range(DEPTH):
        last = l == DEPTH - 1
        m_lat = jnp.split(silu_c @ w_ada[l] + b_ada[l], N_MOD, axis=-1)
        m_ctx = jnp.split(silu_cc @ w_ada[l] + b_ada[l], N_MOD, axis=-1)
        filt = (hy_w1[l], hy_b1[l], hy_w2[l], hy_b2[l], hy_w3[l], hy_freq[l])
        moe_p = (moe_w_rg[l], moe_b_rg[l], moe_w_re[l], moe_b_re[l], moe_w_gate[l], moe_w_up[l], moe_w_down[l])

        hl = adaln(x, g_mix[l], m_lat[0], m_lat[1])
        hc = adaln(xc, g_mix[l], m_ctx[0], m_ctx[1])
        pl = hl @ w_in[l]
        if last:
            pc_kv = hc @ w_in[l][:, KV_START:HY_START]
        else:
            pc = hc @ w_in[l]
            pc_kv = pc[..., KV_START:HY_START]
        k_ctx = qk_norm_heads(pc_kv[..., :D_NA], na_k_gain[l])
        v_ctx = to_heads(pc_kv[..., D_NA:])

        q = qk_norm_heads(pl[..., 2 * D_GMLP:KV_START], na_q_gain[l]) * NA_SCALE
        k = qk_norm_heads(pl[..., KV_START:KV_START + D_NA], na_k_gain[l])
        v = to_heads(pl[..., KV_START + D_NA:HY_START])
        mix_lat = jnp.concatenate([
            gmlp_mix(pl[..., :D_GMLP], pl[..., D_GMLP:2 * D_GMLP], gmlp_v_gain[l], gmlp_ws[l], gmlp_bs[l]),
            neighbourhood_attention(q, k, v, k_ctx, v_ctx, na_rpb[l]),
            hyena_mix(pl[..., HY_START:], hy_short_w[l], hy_short_b[l], hyena_filters(seq, *filt), hy_bias[l]),
        ], axis=-1)
        x = x + m_lat[2] * (mix_lat @ w_out[l])

        if not last:
            q_ctx = qk_norm_heads(pc[..., 2 * D_GMLP:KV_START], na_q_gain[l]) * NA_SCALE
            mix_ctx = jnp.concatenate([
                gmlp_mix(pc[..., :D_GMLP], pc[..., D_GMLP:2 * D_GMLP], gmlp_v_gain[l], gmlp_ws[l], gmlp_bs[l]),
                context_attention(q_ctx, k_ctx, v_ctx),
                hyena_mix(pc[..., HY_START:], hy_short_w[l], hy_short_b[l],
                          hyena_filters(xc.shape[1], *filt), hy_bias[l]),
            ], axis=-1)
            xc = xc + m_ctx[2] * (mix_ctx @ w_out[l])

        x = x + m_lat[5] * hier_moe(adaln(x, g_ffn[l], m_lat[3], m_lat[4]), *moe_p)
        if not last:
            xc = xc + m_ctx[5] * hier_moe(adaln(xc, g_ffn[l], m_ctx[3], m_ctx[4]), *moe_p)
    return x
```

```python
import functools
import math

import numpy as np
import jax
import jax.numpy as jnp
from jax import lax
from jax.experimental import pallas as pl
from jax.experimental.pallas import tpu as pltpu

F32 = jnp.float32
BF16 = jnp.bfloat16
HI = lax.Precision.HIGHEST

D_MODEL = 1024
GRID_W = 64
D_GMLP = D_MODEL // 4
D_NA = D_MODEL // 2
D_HYENA = D_MODEL // 4
D_IN = 2 * D_GMLP + 3 * D_NA + 3 * D_HYENA
Q_START = 2 * D_GMLP
KV_START = 2 * D_GMLP + D_NA
HY_START = 2 * D_GMLP + 3 * D_NA
GMLP_GROUPS = 4
GMLP_GROUP_DIM = D_GMLP // GMLP_GROUPS
GMLP_CHUNK = 128
NA_HEAD_DIM = 64
NA_HEADS = D_NA // NA_HEAD_DIM
NA_SCALE = NA_HEAD_DIM ** -0.5
NA_WIN_ROWS = 8
NA_WIN_COLS = 16
HYENA_ORDER = 2
HYENA_POS_BANDS = 16
HYENA_EMB = 1 + 2 * HYENA_POS_BANDS
HYENA_FILTER_HIDDEN = 64
HYENA_DECAY_TARGET = 1e-2
HYENA_FAST_DECAY = 0.3
HYENA_SLOW_DECAY = 1.5
MOE_GROUPS = 4
MOE_EXPERTS_PER_GROUP = 8
MOE_EXPERTS = MOE_GROUPS * MOE_EXPERTS_PER_GROUP
MOE_HIDDEN = 256
N_MOD = 6
RMS_EPS = 1e-6
LN_EPS = 1e-5

LANES = 128
TOKEN_TILE = 512
NA_Q_ROWS = 4
NA_Q_BLOCK = NA_Q_ROWS * GRID_W
DFT_TILE = 512
MASK_VALUE = -1e30
VMEM_LIMIT = 56 * 1024 * 1024


def _cparams(*sem):
    return pltpu.CompilerParams(dimension_semantics=sem, vmem_limit_bytes=VMEM_LIMIT)


def _silu(x):
    return x * jax.nn.sigmoid(x)


def _rms_rows(x):
    return x * lax.rsqrt(jnp.mean(x * x, axis=-1, keepdims=True) + RMS_EPS)


def _mods_kernel(cs_ref, w_ref, b_ref, o_ref):
    s = _silu(cs_ref[...])
    o_ref[...] = jnp.dot(s, w_ref[...], preferred_element_type=F32, precision=HI) + b_ref[...]


def _modulation(cs, w_ada, b_ada):
    depth, d, nd = w_ada.shape
    rows = cs.shape[0]
    col = 1024
    return pl.pallas_call(
        _mods_kernel,
        grid=(depth, nd // col),
        in_specs=[
            pl.BlockSpec((rows, d), lambda l, j: (0, 0)),
            pl.BlockSpec((None, d, col), lambda l, j: (l, 0, j)),
            pl.BlockSpec((None, 1, col), lambda l, j: (l, 0, j)),
        ],
        out_specs=pl.BlockSpec((None, rows, col), lambda l, j: (l, 0, j)),
        out_shape=jax.ShapeDtypeStruct((depth, rows, nd), F32),
        compiler_params=_cparams("arbitrary", "arbitrary"),
        name="modulation",
    )(cs, w_ada, b_ada.reshape(depth, 1, nd))


def _in_proj_kernel(x_ref, g_ref, mod_ref, w_ref, o_ref):
    mod = mod_ref[...]
    h = _rms_rows(x_ref[...]) * g_ref[...]
    h = h * (1.0 + mod[1:2]) + mod[0:1]
    o_ref[...] = jnp.dot(h.astype(BF16), w_ref[...], preferred_element_type=F32)


def _mod_index(tiles_per_batch, n_batch):
    return lambda i: (jnp.minimum(i // tiles_per_batch, n_batch), 0, 0)


def _in_proj(x_all, gain, mods_l, w_bf, n_batch, seq):
    n, d = x_all.shape
    d_in = w_bf.shape[1]
    return pl.pallas_call(
        _in_proj_kernel,
        grid=(n // TOKEN_TILE,),
        in_specs=[
            pl.BlockSpec((TOKEN_TILE, d), lambda i: (i, 0)),
            pl.BlockSpec((1, d), lambda i: (0, 0)),
            pl.BlockSpec((None, N_MOD, d), _mod_index(seq // TOKEN_TILE, n_batch)),
            pl.BlockSpec((d, d_in), lambda i: (0, 0)),
        ],
        out_specs=pl.BlockSpec((TOKEN_TILE, d_in), lambda i: (i, 0)),
        out_shape=jax.ShapeDtypeStruct((n, d_in), F32),
        compiler_params=_cparams("arbitrary"),
        name="in_proj",
    )(x_all, gain.reshape(1, d), mods_l, w_bf)


def _gmlp_kernel(u_ref, v_ref, gain_ref, ws_ref, bs_ref, o_ref):
    u = jax.nn.gelu(u_ref[...])
    v = jax.nn.gelu(v_ref[...])
    gain = gain_ref[...]
    bs = bs_ref[...]
    outs = []
    for g in range(GMLP_GROUPS):
        cols = slice(g * GMLP_GROUP_DIM, (g + 1) * GMLP_GROUP_DIM)
        vg = v[:, cols]
        vg = vg - jnp.mean(vg, axis=-1, keepdims=True)
        vg = vg * lax.rsqrt(jnp.mean(vg * vg, axis=-1, keepdims=True) + LN_EPS) * gain[:, cols]
        s = jnp.dot(ws_ref[g], vg, preferred_element_type=F32, precision=HI) + bs[:, g:g + 1]
        outs.append(u[:, cols] * s)
    o_ref[...] = jnp.concatenate(outs, axis=-1)


def _gmlp(p_all, n, v_gain, ws, bs_t):
    return pl.pallas_call(
        _gmlp_kernel,
        grid=(n // GMLP_CHUNK,),
        in_specs=[
            pl.BlockSpec((GMLP_CHUNK, D_GMLP), lambda i: (i, 0)),
            pl.BlockSpec((GMLP_CHUNK, D_GMLP), lambda i: (i, 1)),
            pl.BlockSpec((1, D_GMLP), lambda i: (0, 0)),
            pl.BlockSpec((GMLP_GROUPS, GMLP_CHUNK, GMLP_CHUNK), lambda i: (0, 0, 0)),
            pl.BlockSpec((GMLP_CHUNK, GMLP_GROUPS), lambda i: (0, 0)),
        ],
        out_specs=pl.BlockSpec((GMLP_CHUNK, D_GMLP), lambda i: (i, 0)),
        out_shape=jax.ShapeDtypeStruct((n, D_GMLP), F32),
        compiler_params=_cparams("arbitrary"),
        name="gmlp",
    )(p_all, p_all, v_gain.reshape(1, D_GMLP), ws, bs_t)


def _na_bias_tables(rpb):
    qcol = np.arange(GRID_W)[:, None]
    kcol = np.arange(GRID_W)[None, :]
    win_c = np.clip(qcol - NA_WIN_COLS // 2, 0, GRID_W - NA_WIN_COLS)
    col_ok = (kcol >= win_c) & (kcol < win_c + NA_WIN_COLS)
    dc = np.clip(kcol - qcol + NA_WIN_COLS - 1, 0, 2 * NA_WIN_COLS - 2)
    offs = np.arange(NA_WIN_ROWS)[:, None] + np.arange(NA_WIN_ROWS)[None, :]
    dr = np.clip(offs, 0, 2 * NA_WIN_ROWS - 2)
    tab = rpb[:, dr[:, None, :, None], dc[None, :, None, :]]
    tab = jnp.where(col_ok[None, None, :, None, :], tab.astype(F32), MASK_VALUE)
    return tab.reshape(rpb.shape[0], NA_WIN_ROWS, GRID_W, NA_WIN_ROWS * GRID_W)


def _norm_heads(t, gain):
    outs = []
    for h in range(NA_HEADS):
        th = t[:, h * NA_HEAD_DIM:(h + 1) * NA_HEAD_DIM]
        outs.append(_rms_rows(th) * gain)
    return jnp.concatenate(outs, axis=-1)


def _na_kernel(q_ref, k_ref, v_ref, kc_ref, vc_ref, bias_ref, qg_ref, kg_ref, o_ref,
               kn_s, vb_s, kcn_s, vcb_s, qn_s, *, n_lat_steps, grid_rows):
    step = pl.program_id(1)
    kg = kg_ref[...]
    nt = (((1,), (1,)), ((), ()))
    win_keys = NA_WIN_ROWS * GRID_W

    @pl.when(step == 0)
    def _prepare_keys():
        chunk = 256

        def body(c, carry):
            rows = pl.ds(pl.multiple_of(c * chunk, chunk), chunk)
            kn_s[rows, :] = _norm_heads(k_ref[rows, :], kg).astype(BF16)
            vb_s[rows, :] = v_ref[rows, :].astype(BF16)
            return carry

        lax.fori_loop(0, k_ref.shape[0] // chunk, body, 0)
        kcn_s[...] = _norm_heads(kc_ref[...], kg).astype(BF16)
        vcb_s[...] = vc_ref[...].astype(BF16)

    qn_s[...] = (_norm_heads(q_ref[...], qg_ref[...]) * NA_SCALE).astype(BF16)

    @pl.when(step < n_lat_steps)
    def _latent_queries():
        def row_body(r, carry):
            r_abs = step * NA_Q_ROWS + r
            wr = jnp.clip(r_abs - NA_WIN_ROWS // 2, 0, grid_rows - NA_WIN_ROWS)
            off = wr - r_abs + NA_WIN_ROWS - 1
            krows = pl.ds(pl.multiple_of(wr * GRID_W, GRID_W), win_keys)
            qrows = pl.ds(pl.multiple_of(r * GRID_W, GRID_W), GRID_W)
            for h in range(NA_HEADS):
                hs = slice(h * NA_HEAD_DIM, (h + 1) * NA_HEAD_DIM)
                qh = qn_s[qrows, hs]
                s_w = lax.dot_general(qh, kn_s[krows, hs], nt, preferred_element_type=F32)
                s_w = s_w + bias_ref[h, off]
                s_c = lax.dot_general(qh, kcn_s[:, hs], nt, preferred_element_type=F32)
                m = jnp.maximum(jnp.max(s_w, axis=-1, keepdims=True), jnp.max(s_c, axis=-1, keepdims=True))
                p_w = jnp.exp(s_w - m)
                p_c = jnp.exp(s_c - m)
                denom = jnp.sum(p_w, axis=-1, keepdims=True) + jnp.sum(p_c, axis=-1, keepdims=True)
                acc = jnp.dot(p_w.astype(BF16), vb_s[krows, hs], preferred_element_type=F32)
                acc = acc + jnp.dot(p_c.astype(BF16), vcb_s[:, hs], preferred_element_type=F32)
                o_ref[qrows, hs] = acc / denom
            return carry

        lax.fori_loop(0, NA_Q_ROWS, row_body, 0)

    @pl.when(step >= n_lat_steps)
    def _context_queries():
        for h in range(NA_HEADS):
            hs = slice(h * NA_HEAD_DIM, (h + 1) * NA_HEAD_DIM)
            s = lax.dot_general(qn_s[:, hs], kcn_s[:, hs], nt, preferred_element_type=F32)
            p = jnp.exp(s - jnp.max(s, axis=-1, keepdims=True))
            acc = jnp.dot(p.astype(BF16), vcb_s[:, hs], preferred_element_type=F32)
            o_ref[:, hs] = acc / jnp.sum(p, axis=-1, keepdims=True)


def _attention(p_all, bias_tab, q_gain, k_gain, n_batch, seq, ctx_len, with_ctx_queries):
    n = p_all.shape[0]
    assert ctx_len == NA_Q_BLOCK
    n_lat_steps = seq // NA_Q_BLOCK
    n_steps = n_lat_steps + (1 if with_ctx_queries else 0)
    ctx_block0 = n_batch * seq // ctx_len
    qcol, kcol, vcol = Q_START // D_NA, KV_START // D_NA, (KV_START + D_NA) // D_NA

    def q_index(col):
        return lambda b, s: (jnp.where(s < n_lat_steps, b * n_lat_steps + s, ctx_block0 + b), col)

    kern = functools.partial(_na_kernel, n_lat_steps=n_lat_steps, grid_rows=seq // GRID_W)
    return pl.pallas_call(
        kern,
        grid=(n_batch, n_steps),
        in_specs=[
            pl.BlockSpec((NA_Q_BLOCK, D_NA), q_index(qcol)),
            pl.BlockSpec((seq, D_NA), lambda b, s: (b, kcol)),
            pl.BlockSpec((seq, D_NA), lambda b, s: (b, vcol)),
            pl.BlockSpec((ctx_len, D_NA), lambda b, s: (ctx_block0 + b, kcol)),
            pl.BlockSpec((ctx_len, D_NA), lambda b, s: (ctx_block0 + b, vcol)),
            pl.BlockSpec(bias_tab.shape, lambda b, s: (0, 0, 0, 0)),
            pl.BlockSpec((1, NA_HEAD_DIM), lambda b, s: (0, 0)),
            pl.BlockSpec((1, NA_HEAD_DIM), lambda b, s: (0, 0)),
        ],
        out_specs=pl.BlockSpec((NA_Q_BLOCK, D_NA), q_index(0)),
        out_shape=jax.ShapeDtypeStruct((n, D_NA), F32),
        scratch_shapes=[
            pltpu.VMEM((seq, D_NA), BF16),
            pltpu.VMEM((seq, D_NA), BF16),
            pltpu.VMEM((ctx_len, D_NA), BF16),
            pltpu.VMEM((ctx_len, D_NA), BF16),
            pltpu.VMEM((NA_Q_BLOCK, D_NA), BF16),
        ],
        compiler_params=_cparams("arbitrary", "arbitrary"),
        name="attention",
    )(p_all, p_all, p_all, p_all, p_all, bias_tab,
      q_gain.reshape(1, NA_HEAD_DIM), k_gain.reshape(1, NA_HEAD_DIM))


def _dft_matrices(length):
    idx = jnp.arange(length, dtype=jnp.int32)
    ang = ((idx[:, None] * idx[None, :]) % (2 * length)).astype(F32) * (math.pi / length)
    gc = jnp.cos(ang)
    nyq = jnp.where(idx % 2 == 0, 1.0, -1.0).astype(F32)
    gs = jnp.where(idx[:, None] == 0, nyq[None, :], jnp.sin(ang))
    return gc.astype(BF16), gs.astype(BF16), gs.T.astype(BF16)


def _hyena_positions(length):
    t = jnp.linspace(0.0, 1.0, length, dtype=F32)[:, None]
    w = 2.0 * math.pi * jnp.arange(length, dtype=F32)[:, None] / length
    f = jnp.linspace(1e-4, HYENA_POS_BANDS - 1, HYENA_POS_BANDS, dtype=F32)[None, :]
    z = jnp.concatenate([t, jnp.cos(f * w), -jnp.sin(f * w)], axis=-1)
    z = jnp.pad(z, ((0, 0), (0, LANES - HYENA_EMB)))
    min_decay = math.log(HYENA_DECAY_TARGET) / HYENA_SLOW_DECAY
    max_decay = math.log(HYENA_DECAY_TARGET) / HYENA_FAST_DECAY
    deltas = jnp.abs(jnp.linspace(min_decay, max_decay, D_HYENA, dtype=F32))[None, :]
    return z, jnp.exp(-t * deltas)


def _filter_kernel(z_ref, decay_ref, w1_ref, b1_ref, w2_ref, b2_ref, w3_ref, freq_ref, o_ref):
    freq = freq_ref[...]
    hdn = jnp.dot(z_ref[...], w1_ref[...], preferred_element_type=F32, precision=HI) + b1_ref[...]
    hdn = jnp.sin(freq[0:1] * hdn)
    hdn = jnp.dot(hdn, w2_ref[...], preferred_element_type=F32, precision=HI) + b2_ref[...]
    hdn = jnp.sin(freq[1:2] * hdn)
    h = jnp.dot(hdn, w3_ref[...], preferred_element_type=F32, precision=HI)
    decay = decay_ref[...]
    first_row = lax.broadcasted_iota(jnp.int32, decay.shape, 0) == 0
    outs = []
    for n in range(HYENA_ORDER):
        base = 2 * n * D_HYENA
        hf = h[:, base:base + D_HYENA] * decay
        hb = h[:, base + D_HYENA:base + 2 * D_HYENA] * decay
        norm = jnp.sum(jnp.abs(hf), axis=0, keepdims=True) + jnp.sum(jnp.abs(hb), axis=0, keepdims=True)
        outs.append(hf / norm)
        outs.append(jnp.where(first_row, 0.0, hb / norm))
    o_ref[...] = jnp.concatenate(outs, axis=-1)


def _hyena_filters(z, decay, w1p, b1, w2, b2, w3, freq):
    length = z.shape[0]
    full = lambda a: pl.BlockSpec(a.shape, lambda i: (0,) * a.ndim)
    args = (z, decay, w1p, b1.reshape(1, -1), w2, b2.reshape(1, -1), w3, freq)
    return pl.pallas_call(
        _filter_kernel,
        grid=(1,),
        in_specs=[full(a) for a in args],
        out_specs=pl.BlockSpec((length, 2 * HYENA_ORDER * D_HYENA), lambda i: (0, 0)),
        out_shape=jax.ShapeDtypeStruct((length, 2 * HYENA_ORDER * D_HYENA), F32),
        compiler_params=_cparams("arbitrary"),
        name="hyena_filter",
    )(*args)


def _filter_dft_kernel(gc_ref, gs_ref, h_ref, o_ref, *, length):
    hb = h_ref[...].astype(BF16)
    fa = jnp.dot(gc_ref[...], hb, preferred_element_type=F32)
    fb = jnp.dot(gs_ref[...], hb, preferred_element_type=F32)
    rows = lax.broadcasted_iota(jnp.int32, (fa.shape[0], D_HYENA), 0) + pl.program_id(0) * fa.shape[0]
    dc_row = rows == 0
    inv_n = 1.0 / (2 * length)
    outs = []
    for n in range(HYENA_ORDER):
        base = 2 * n * D_HYENA
        f_sl = slice(base, base + D_HYENA)
        b_sl = slice(base + D_HYENA, base + 2 * D_HYENA)
        kr = fa[:, f_sl] + fa[:, b_sl]
        ki = fb[:, b_sl] - fb[:, f_sl]
        k_nyq = fb[:, f_sl] + fb[:, b_sl]
        outs.append(jnp.where(dc_row, kr * inv_n, 2.0 * inv_n * kr))
        outs.append(jnp.where(dc_row, 0.0, 2.0 * inv_n * ki))
        outs.append(jnp.where(dc_row, 0.0, -2.0 * inv_n * ki))
        outs.append(jnp.where(dc_row, k_nyq * inv_n, 2.0 * inv_n * kr))
    o_ref[...] = jnp.concatenate(outs, axis=-1)


def _filter_dft(gc, gs, hfilt):
    length = gc.shape[0]
    tile = min(DFT_TILE, length)
    width = 4 * HYENA_ORDER * D_HYENA
    return pl.pallas_call(
        functools.partial(_filter_dft_kernel, length=length),
        grid=(length // tile,),
        in_specs=[
            pl.BlockSpec((tile, length), lambda j: (j, 0)),
            pl.BlockSpec((tile, length), lambda j: (j, 0)),
            pl.BlockSpec(hfilt.shape, lambda j: (0, 0)),
        ],
        out_specs=pl.BlockSpec((tile, width), lambda j: (j, 0)),
        out_shape=jax.ShapeDtypeStruct((length, width), F32),
        compiler_params=_cparams("arbitrary"),
        name="hyena_filter_dft",
    )(gc, gs, hfilt)


def _short_conv_kernel(a0_ref, a1_ref, a2_ref, w_ref, b_ref, o_ref):
    w = w_ref[...]
    b = b_ref[...]
    length = a0_ref.shape[0]
    rows = lax.broadcasted_iota(jnp.int32, (length, D_HYENA), 0)
    for j, a_ref in enumerate((a0_ref, a1_ref, a2_ref)):
        cols = slice(j * D_HYENA, (j + 1) * D_HYENA)
        a = a_ref[...]
        prev = jnp.where(rows == 0, 0.0, pltpu.roll(a, 1, 0))
        nxt = jnp.where(rows == length - 1, 0.0, pltpu.roll(a, length - 1, 0))
        o_ref[:, cols] = prev * w[0:1, cols] + a * w[1:2, cols] + nxt * w[2:3, cols] + b[:, cols]


def _short_conv(p_all, short_w, short_b, n_batch, length, row_block0):
    c0 = HY_START // D_HYENA
    spec = lambda j: pl.BlockSpec((length, D_HYENA), lambda b: (row_block0 + b, c0 + j))
    return pl.pallas_call(
        _short_conv_kernel,
        grid=(n_batch,),
        in_specs=[spec(0), spec(1), spec(2),
                  pl.BlockSpec((3, 3 * D_HYENA), lambda b: (0, 0)),
                  pl.BlockSpec((1, 3 * D_HYENA), lambda b: (0, 0))],
        out_specs=pl.BlockSpec((length, 3 * D_HYENA), lambda b: (b, 0)),
        out_shape=jax.ShapeDtypeStruct((n_batch * length, 3 * D_HYENA), F32),
        compiler_params=_cparams("arbitrary"),
        name="hyena_short_conv",
    )(p_all, p_all, p_all, short_w, short_b.reshape(1, -1))


def _conv_fwd_kernel(gc_ref, gs_ref, u_ref, k_ref, pa_ref, pb_ref):
    u = u_ref[...].astype(BF16)
    a = jnp.dot(gc_ref[...], u, preferred_element_type=F32)
    b = jnp.dot(gs_ref[...], u, preferred_element_type=F32)
    k = k_ref[...]
    c = D_HYENA
    pa_ref[...] = (a * k[:, 0:c] + b * k[:, c:2 * c]).astype(BF16)
    pb_ref[...] = (a * k[:, 2 * c:3 * c] + b * k[:, 3 * c:4 * c]).astype(BF16)


def _conv_fwd(gc, gs, u, u_col, kpack, order, n_batch):
    length = gc.shape[0]
    tile = min(DFT_TILE, length)
    nt = length // tile
    out = jax.ShapeDtypeStruct((n_batch * length, D_HYENA), BF16)
    return pl.pallas_call(
        _conv_fwd_kernel,
        grid=(nt, n_batch),
        in_specs=[
            pl.BlockSpec((tile, length), lambda j, b: (j, 0)),
            pl.BlockSpec((tile, length), lambda j, b: (j, 0)),
            pl.BlockSpec((length, D_HYENA), lambda j, b: (b, u_col)),
            pl.BlockSpec((tile, 4 * D_HYENA), lambda j, b: (j, order)),
        ],
        out_specs=[pl.BlockSpec((tile, D_HYENA), lambda j, b: (b * nt + j, 0))] * 2,
        out_shape=[out, out],
        compiler_params=_cparams("arbitrary", "arbitrary"),
        name="hyena_conv_fwd",
    )(gc, gs, u, kpack)


def _conv_inv_kernel(gc_ref, gst_ref, pa_ref, pb_ref, z_ref, gate_ref, d_ref, o_ref):
    y = jnp.dot(gc_ref[...], pa_ref[...], preferred_element_type=F32)
    y = y + jnp.dot(gst_ref[...], pb_ref[...], preferred_element_type=F32)
    o_ref[...] = gate_ref[...] * (y + d_ref[...] * z_ref[...])


def _conv_inv(gc, gst, pa, pb, z_prev, z_col, a3, gate_col, d_bias, n_batch):
    length = gc.shape[0]
    tile = min(DFT_TILE, length)
    nt = length // tile
    return pl.pallas_call(
        _conv_inv_kernel,
        grid=(nt, n_batch),
        in_specs=[
            pl.BlockSpec((tile, length), lambda j, b: (j, 0)),
            pl.BlockSpec((tile, length), lambda j, b: (j, 0)),
            pl.BlockSpec((length, D_HYENA), lambda j, b: (b, 0)),
            pl.BlockSpec((length, D_HYENA), lambda j, b: (b, 0)),
            pl.BlockSpec((tile, D_HYENA), lambda j, b: (b * nt + j, z_col)),
            pl.BlockSpec((tile, D_HYENA), lambda j, b: (b * nt + j, gate_col)),
            pl.BlockSpec((1, D_HYENA), lambda j, b: (0, 0)),
        ],
        out_specs=pl.BlockSpec((tile, D_HYENA), lambda j, b: (b * nt + j, 0)),
        out_shape=jax.ShapeDtypeStruct((n_batch * length, D_HYENA), F32),
        compiler_params=_cparams("arbitrary", "arbitrary"),
        name="hyena_conv_inv",
    )(gc, gst, pa, pb, z_prev, a3, d_bias.reshape(1, D_HYENA))


def _hyena(p_all, short_w, short_b, d_bias, kpack, mats, n_batch, length, row_block0):
    gc, gs, gst = mats
    a3 = _short_conv(p_all, short_w, short_b, n_batch, length, row_block0)
    pa, pb = _conv_fwd(gc, gs, a3, 0, kpack, 0, n_batch)
    z1 = _conv_inv(gc, gst, pa, pb, a3, 0, a3, 1, d_bias[0], n_batch)
    pa, pb = _conv_fwd(gc, gs, z1, 0, kpack, 1, n_batch)
    return _conv_inv(gc, gst, pa, pb, z1, 0, a3, 2, d_bias[1], n_batch)


def _out_proj_kernel(x_ref, gm_ref, na_ref, hyl_ref, hyc_ref, w_ref, mod_ref, o_ref, *, n_lat_tiles):
    is_lat = pl.program_id(0) < n_lat_tiles
    hy = jnp.where(is_lat, hyl_ref[...], hyc_ref[...])
    y = jnp.dot(gm_ref[...].astype(BF16), w_ref[0:D_GMLP, :], preferred_element_type=F32)
    y = y + jnp.dot(na_ref[...].astype(BF16), w_ref[D_GMLP:D_GMLP + D_NA, :], preferred_element_type=F32)
    y = y + jnp.dot(hy.astype(BF16), w_ref[D_GMLP + D_NA:, :], preferred_element_type=F32)
    o_ref[...] = x_ref[...] + mod_ref[2:3, :] * y


def _out_proj(x_all, gm, na, hy_lat, hy_ctx, w_bf, mods_l, n_batch, seq, n_tiles):
    d = x_all.shape[1]
    n_lat_tiles = n_batch * seq // TOKEN_TILE
    n_ctx_tiles = hy_ctx.shape[0] // TOKEN_TILE
    row = lambda w: pl.BlockSpec((TOKEN_TILE, w), lambda i: (i, 0))
    return pl.pallas_call(
        functools.partial(_out_proj_kernel, n_lat_tiles=n_lat_tiles),
        grid=(n_tiles,),
        in_specs=[
            row(d), row(D_GMLP), row(D_NA),
            pl.BlockSpec((TOKEN_TILE, D_HYENA), lambda i: (jnp.minimum(i, n_lat_tiles - 1), 0)),
            pl.BlockSpec((TOKEN_TILE, D_HYENA),
                         lambda i: (jnp.clip(i - n_lat_tiles, 0, n_ctx_tiles - 1), 0)),
            pl.BlockSpec(w_bf.shape, lambda i: (0, 0)),
            pl.BlockSpec((None, N_MOD, d), _mod_index(seq // TOKEN_TILE, n_batch)),
        ],
        out_specs=row(d),
        out_shape=jax.ShapeDtypeStruct((n_tiles * TOKEN_TILE, d), F32),
        compiler_params=_cparams("arbitrary"),
        name="out_proj",
    )(x_all, gm, na, hy_lat, hy_ctx, w_bf, mods_l)


def _router_kernel(x_ref, g_ref, mod_ref, wr_ref, br_ref, h_ref, comb_ref):
    mod = mod_ref[...]
    h = _rms_rows(x_ref[...]) * g_ref[...]
    h = h * (1.0 + mod[4:5]) + mod[3:4]
    h_ref[...] = h.astype(BF16)
    logits = jnp.dot(h, wr_ref[...], preferred_element_type=F32, precision=HI) + br_ref[...]
    lane = lax.broadcasted_iota(jnp.int32, logits.shape, 1)
    neg = -jnp.inf
    is_group = (lane >= MOE_EXPERTS) & (lane < MOE_EXPERTS + MOE_GROUPS)
    lg = jnp.where(is_group, logits, neg)
    mg = jnp.max(lg, axis=-1, keepdims=True)
    g_p = 1.0 / jnp.sum(jnp.exp(lg - mg), axis=-1, keepdims=True)
    g_idx = jnp.min(jnp.where(lg == mg, lane, 2 * LANES), axis=-1, keepdims=True) - MOE_EXPERTS
    in_group = (lane >= g_idx * MOE_EXPERTS_PER_GROUP) & (lane < (g_idx + 1) * MOE_EXPERTS_PER_GROUP)
    le = jnp.where(in_group, logits, neg)
    me = jnp.max(le, axis=-1, keepdims=True)
    pe = jnp.exp(le - me)
    pe = pe / jnp.sum(pe, axis=-1, keepdims=True)
    p1 = jnp.max(pe, axis=-1, keepdims=True)
    i1 = jnp.min(jnp.where(in_group & (pe == p1), lane, 2 * LANES), axis=-1, keepdims=True)
    pe2 = jnp.where(in_group & (lane != i1), pe, neg)
    p2 = jnp.max(pe2, axis=-1, keepdims=True)
    i2 = jnp.min(jnp.where(pe2 == p2, lane, 2 * LANES), axis=-1, keepdims=True)
    tot = p1 + p2
    comb_ref[...] = g_p * jnp.where(lane == i1, p1 / tot, jnp.where(lane == i2, p2 / tot, 0.0))


def _router(x_all, gain, mods_l, w_router, b_router, n_batch, seq, n_tiles):
    d = x_all.shape[1]
    return pl.pallas_call(
        _router_kernel,
        grid=(n_tiles,),
        in_specs=[
            pl.BlockSpec((TOKEN_TILE, d), lambda i: (i, 0)),
            pl.BlockSpec((1, d), lambda i: (0, 0)),
            pl.BlockSpec((None, N_MOD, d), _mod_index(seq // TOKEN_TILE, n_batch)),
            pl.BlockSpec((d, LANES), lambda i: (0, 0)),
            pl.BlockSpec((1, LANES), lambda i: (0, 0)),
        ],
        out_specs=[pl.BlockSpec((TOKEN_TILE, d), lambda i: (i, 0)),
                   pl.BlockSpec((TOKEN_TILE, LANES), lambda i: (i, 0))],
        out_shape=[jax.ShapeDtypeStruct((n_tiles * TOKEN_TILE, d), BF16),
                   jax.ShapeDtypeStruct((n_tiles * TOKEN_TILE, LANES), F32)],
        compiler_params=_cparams("arbitrary"),
        name="moe_router",
    )(x_all, gain.reshape(1, d), mods_l, w_router, b_router)


def _experts_kernel(x_ref, h_ref, comb_ref, wg_ref, wu_ref, wd_ref, mod_ref, o_ref, acc_ref):
    e = pl.program_id(1)

    @pl.when(e == 0)
    def _init():
        acc_ref[...] = jnp.zeros_like(acc_ref)

    h = h_ref[...]
    comb = comb_ref[...]
    lane = lax.broadcasted_iota(jnp.int32, comb.shape, 1)
    c_e = jnp.sum(jnp.where(lane == e, comb, 0.0), axis=-1, keepdims=True)
    gate = jnp.dot(h, wg_ref[...], preferred_element_type=F32)
    up = jnp.dot(h, wu_ref[...], preferred_element_type=F32)
    act = _silu(gate) * up * c_e
    acc_ref[...] += jnp.dot(act.astype(BF16), wd_ref[...], preferred_element_type=F32)

    @pl.when(e == pl.num_programs(1) - 1)
    def _finish():
        o_ref[...] = x_ref[...] + mod_ref[5:6, :] * acc_ref[...]


def _experts(x_all, h_bf, comb, wg, wu, wd, mods_l, n_batch, seq, n_tiles):
    d = x_all.shape[1]
    f = wg.shape[-1]
    return pl.pallas_call(
        _experts_kernel,
        grid=(n_tiles, MOE_EXPERTS),
        in_specs=[
            pl.BlockSpec((TOKEN_TILE, d), lambda i, e: (i, 0)),
            pl.BlockSpec((TOKEN_TILE, d), lambda i, e: (i, 0)),
            pl.BlockSpec((TOKEN_TILE, LANES), lambda i, e: (i, 0)),
            pl.BlockSpec((None, d, f), lambda i, e: (e, 0, 0)),
            pl.BlockSpec((None, d, f), lambda i, e: (e, 0, 0)),
            pl.BlockSpec((None, f, d), lambda i, e: (e, 0, 0)),
            pl.BlockSpec((None, N_MOD, d), lambda i, e: _mod_index(seq // TOKEN_TILE, n_batch)(i)),
        ],
        out_specs=pl.BlockSpec((TOKEN_TILE, d), lambda i, e: (i, 0)),
        out_shape=jax.ShapeDtypeStruct((n_tiles * TOKEN_TILE, d), F32),
        scratch_shapes=[pltpu.VMEM((TOKEN_TILE, d), F32)],
        compiler_params=_cparams("arbitrary", "arbitrary"),
        name="moe_experts",
    )(x_all, h_bf, comb, wg, wu, wd, mods_l)


def kernel(x, c, ctx, c_ctx, w_ada, b_ada, g_mix, g_ffn, w_in, w_out, gmlp_v_gain, gmlp_ws, gmlp_bs,
           na_q_gain, na_k_gain, na_rpb, hy_short_w, hy_short_b, hy_w1, hy_b1, hy_w2, hy_b2, hy_w3,
           hy_freq, hy_bias, moe_w_rg, moe_b_rg, moe_w_re, moe_b_re, moe_w_gate, moe_w_up, moe_w_down):
    n_batch, seq, d = x.shape
    ctx_len = ctx.shape[1]
    depth = w_ada.shape[0]
    n_lat = n_batch * seq
    n_ctx = n_batch * ctx_len
    assert d == D_MODEL and seq % TOKEN_TILE == 0 and n_ctx % TOKEN_TILE == 0
    assert seq % GMLP_CHUNK == 0 and ctx_len % GMLP_CHUNK == 0

    pad_rows = -(n_batch + 1) % 8
    cs = jnp.concatenate([c, c_ctx[None, :], jnp.zeros((pad_rows, d), F32)], axis=0)
    mods = _modulation(cs, w_ada, b_ada)[:, :n_batch + 1].reshape(depth, n_batch + 1, N_MOD, d)

    x_all = jnp.concatenate([x.reshape(n_lat, d), ctx.reshape(n_ctx, d)], axis=0)
    n_all_tiles = (n_lat + n_ctx) // TOKEN_TILE
    n_lat_tiles = n_lat // TOKEN_TILE

    mats_lat = _dft_matrices(seq)
    mats_ctx = _dft_matrices(ctx_len)
    pos_lat = _hyena_positions(seq)
    pos_ctx = _hyena_positions(ctx_len)

    w_in_bf = w_in.astype(BF16)
    w_out_bf = w_out.astype(BF16)
    wg_bf = moe_w_gate.astype(BF16).reshape(depth, MOE_EXPERTS, d, MOE_HIDDEN)
    wu_bf = moe_w_up.astype(BF16).reshape(depth, MOE_EXPERTS, d, MOE_HIDDEN)
    wd_bf = moe_w_down.astype(BF16).reshape(depth, MOE_EXPERTS, MOE_HIDDEN, d)
    lane_pad = LANES - MOE_EXPERTS - MOE_GROUPS
    w_router = jnp.pad(jnp.concatenate([moe_w_re, moe_w_rg], axis=-1), ((0, 0), (0, 0), (0, lane_pad)))
    b_router = jnp.pad(jnp.concatenate([moe_b_re, moe_b_rg], axis=-1), ((0, 0), (0, lane_pad)))[:, None, :]
    w1_pad = jnp.pad(hy_w1, ((0, 0), (0, LANES - HYENA_EMB), (0, 0)))

    for l in range(depth):
        last = l == depth - 1
        mods_l = mods[l]
        n_tiles = n_lat_tiles if last else n_all_tiles

        p_all = _in_proj(x_all, g_mix[l], mods_l, w_in_bf[l], n_batch, seq)

        filt = (w1_pad[l], hy_b1[l], hy_w2[l], hy_b2[l], hy_w3[l], hy_freq[l])
        kpack_lat = _filter_dft(mats_lat[0], mats_lat[1], _hyena_filters(*pos_lat, *filt))
        hy_lat = _hyena(p_all, hy_short_w[l], hy_short_b[l], hy_bias[l], kpack_lat, mats_lat,
                        n_batch, seq, 0)
        if last:
            hy_ctx = hy_lat
            gm = _gmlp(p_all, n_lat, gmlp_v_gain[l], gmlp_ws[l], gmlp_bs[l].T)
        else:
            kpack_ctx = _filter_dft(mats_ctx[0], mats_ctx[1], _hyena_filters(*pos_ctx, *filt))
            hy_ctx = _hyena(p_all, hy_short_w[l], hy_short_b[l], hy_bias[l], kpack_ctx, mats_ctx,
                            n_batch, ctx_len, n_lat // ctx_len)
            gm = _gmlp(p_all, n_lat + n_ctx, gmlp_v_gain[l], gmlp_ws[l], gmlp_bs[l].T)

        na = _attention(p_all, _na_bias_tables(na_rpb[l]), na_q_gain[l], na_k_gain[l],
                        n_batch, seq, ctx_len, not last)

        x_all = _out_proj(x_all, gm, na, hy_lat, hy_ctx, w_out_bf[l], mods_l, n_batch, seq, n_tiles)

        h_bf, comb = _router(x_all, g_ffn[l], mods_l, w_router[l], b_router[l], n_batch, seq, n_tiles)
        x_all = _experts(x_all, h_bf, comb, wg_bf[l], wu_bf[l], wd_bf[l], mods_l, n_batch, seq, n_tiles)

    return x_all[:n_lat].reshape(n_batch, seq, d)
```

```python
import functools
import math

import numpy as np
import jax
import jax.numpy as jnp
from jax import lax
from jax.experimental import pallas as pl
from jax.experimental.pallas import tpu as pltpu

F32 = jnp.float32
BF16 = jnp.bfloat16
HI = lax.Precision.HIGHEST

D_MODEL = 1024
GRID_W = 64
D_GMLP = D_MODEL // 4
D_NA = D_MODEL // 2
D_HYENA = D_MODEL // 4
D_IN = 2 * D_GMLP + 3 * D_NA + 3 * D_HYENA
Q_START = 2 * D_GMLP
KV_START = 2 * D_GMLP + D_NA
HY_START = 2 * D_GMLP + 3 * D_NA
GMLP_GROUPS = 4
GMLP_GROUP_DIM = D_GMLP // GMLP_GROUPS
GMLP_CHUNK = 128
NA_HEAD_DIM = 64
NA_HEADS = D_NA // NA_HEAD_DIM
NA_SCALE = NA_HEAD_DIM ** -0.5
NA_WIN_ROWS = 8
NA_WIN_COLS = 16
HYENA_ORDER = 2
HYENA_POS_BANDS = 16
HYENA_EMB = 1 + 2 * HYENA_POS_BANDS
HYENA_FILTER_HIDDEN = 64
HYENA_DECAY_TARGET = 1e-2
HYENA_FAST_DECAY = 0.3
HYENA_SLOW_DECAY = 1.5
MOE_GROUPS = 4
MOE_EXPERTS_PER_GROUP = 8
MOE_EXPERTS = MOE_GROUPS * MOE_EXPERTS_PER_GROUP
MOE_HIDDEN = 256
N_MOD = 6
RMS_EPS = 1e-6
LN_EPS = 1e-5

LANES = 128
TOKEN_TILE = 512
NA_Q_ROWS = 4
NA_Q_BLOCK = NA_Q_ROWS * GRID_W
NA_BAND_ROWS = NA_WIN_ROWS + NA_Q_ROWS
DFT_TILE = 512
MASK_VALUE = -1e30
VMEM_LIMIT = 56 * 1024 * 1024


def _cparams(*sem):
    return pltpu.CompilerParams(dimension_semantics=sem, vmem_limit_bytes=VMEM_LIMIT)


def _silu(x):
    return x * jax.nn.sigmoid(x)


def _rms_rows(x):
    return x * lax.rsqrt(jnp.mean(x * x, axis=-1, keepdims=True) + RMS_EPS)


def _mods_kernel(cs_ref, w_ref, b_ref, o_ref):
    s = _silu(cs_ref[...])
    o_ref[...] = jnp.dot(s, w_ref[...], preferred_element_type=F32, precision=HI) + b_ref[...]


def _modulation(cs, w_ada, b_ada):
    depth, d, nd = w_ada.shape
    rows = cs.shape[0]
    col = 1024
    return pl.pallas_call(
        _mods_kernel,
        grid=(depth, nd // col),
        in_specs=[
            pl.BlockSpec((rows, d), lambda l, j: (0, 0)),
            pl.BlockSpec((None, d, col), lambda l, j: (l, 0, j)),
            pl.BlockSpec((None, 1, col), lambda l, j: (l, 0, j)),
        ],
        out_specs=pl.BlockSpec((None, rows, col), lambda l, j: (l, 0, j)),
        out_shape=jax.ShapeDtypeStruct((depth, rows, nd), F32),
        compiler_params=_cparams("arbitrary", "arbitrary"),
        name="modulation",
    )(cs, w_ada, b_ada.reshape(depth, 1, nd))


def _in_proj_kernel(x_ref, g_ref, mod_ref, w_ref, o_ref):
    mod = mod_ref[...]
    h = _rms_rows(x_ref[...]) * g_ref[...]
    h = h * (1.0 + mod[1:2]) + mod[0:1]
    o_ref[...] = jnp.dot(h.astype(BF16), w_ref[...], preferred_element_type=F32)


def _mod_index(tiles_per_batch, n_batch):
    return lambda i: (jnp.minimum(i // tiles_per_batch, n_batch), 0, 0)


def _in_proj(x_all, gain, mods_l, w_bf, n_batch, seq):
    n, d = x_all.shape
    d_in = w_bf.shape[1]
    return pl.pallas_call(
        _in_proj_kernel,
        grid=(n // TOKEN_TILE,),
        in_specs=[
            pl.BlockSpec((TOKEN_TILE, d), lambda i: (i, 0)),
            pl.BlockSpec((1, d), lambda i: (0, 0)),
            pl.BlockSpec((None, N_MOD, d), _mod_index(seq // TOKEN_TILE, n_batch)),
            pl.BlockSpec((d, d_in), lambda i: (0, 0)),
        ],
        out_specs=pl.BlockSpec((TOKEN_TILE, d_in), lambda i: (i, 0)),
        out_shape=jax.ShapeDtypeStruct((n, d_in), F32),
        compiler_params=_cparams("arbitrary"),
        name="in_proj",
    )(x_all, gain.reshape(1, d), mods_l, w_bf)


def _group_avg_matrix(groups, width):
    return np.kron(np.eye(groups), np.full((width, width), 1.0 / width))


def _group_mean(t, avg):
    hi = t.astype(BF16)
    lo = (t - hi.astype(F32)).astype(BF16)
    return (jnp.dot(hi, avg, preferred_element_type=F32) + jnp.dot(lo, avg, preferred_element_type=F32))


def _gmlp_kernel(u_ref, v_ref, gain_ref, avg_ref, ws_ref, bs_ref, o_ref):
    avg = avg_ref[...]
    v = jax.nn.gelu(v_ref[...])
    v = v - _group_mean(v, avg)
    v = v * lax.rsqrt(_group_mean(v * v, avg) + LN_EPS) * gain_ref[...]
    vb = v.astype(BF16)
    bs = bs_ref[...]
    for c in range(u_ref.shape[0] // GMLP_CHUNK):
        rows = slice(c * GMLP_CHUNK, (c + 1) * GMLP_CHUNK)
        outs = []
        for g in range(GMLP_GROUPS):
            cols = slice(g * GMLP_GROUP_DIM, (g + 1) * GMLP_GROUP_DIM)
            s = jnp.dot(ws_ref[g], vb[rows, cols], preferred_element_type=F32) + bs[:, g:g + 1]
            outs.append(s)
        o_ref[rows, :] = jax.nn.gelu(u_ref[rows, :]) * jnp.concatenate(outs, axis=-1)


def _gmlp(p_all, n, v_gain, ws, bs_t):
    avg = _group_avg_matrix(GMLP_GROUPS, GMLP_GROUP_DIM)
    return pl.pallas_call(
        _gmlp_kernel,
        grid=(n // TOKEN_TILE,),
        in_specs=[
            pl.BlockSpec((TOKEN_TILE, D_GMLP), lambda i: (i, 0)),
            pl.BlockSpec((TOKEN_TILE, D_GMLP), lambda i: (i, 1)),
            pl.BlockSpec((1, D_GMLP), lambda i: (0, 0)),
            pl.BlockSpec((D_GMLP, D_GMLP), lambda i: (0, 0)),
            pl.BlockSpec((GMLP_GROUPS, GMLP_CHUNK, GMLP_CHUNK), lambda i: (0, 0, 0)),
            pl.BlockSpec((GMLP_CHUNK, GMLP_GROUPS), lambda i: (0, 0)),
        ],
        out_specs=pl.BlockSpec((TOKEN_TILE, D_GMLP), lambda i: (i, 0)),
        out_shape=jax.ShapeDtypeStruct((n, D_GMLP), F32),
        compiler_params=_cparams("arbitrary"),
        name="gmlp",
    )(p_all, p_all, v_gain.reshape(1, D_GMLP), jnp.asarray(avg, BF16), ws.astype(BF16), bs_t)


def _rpb_expand_kernel(rpb_ref, sel_ref, o_ref):
    o_ref[...] = jnp.dot(rpb_ref[...], sel_ref[...], preferred_element_type=F32, precision=HI)


def _na_geometry(grid_rows):
    variants, step_variant, band_start = [], [], []
    for r0 in range(0, grid_rows, NA_Q_ROWS):
        b0 = int(np.clip(r0 - NA_WIN_ROWS // 2, 0, grid_rows - NA_BAND_ROWS))
        geo = []
        for r in range(r0, r0 + NA_Q_ROWS):
            wr = int(np.clip(r - NA_WIN_ROWS // 2, 0, grid_rows - NA_WIN_ROWS))
            assert b0 <= wr and wr + NA_WIN_ROWS <= b0 + NA_BAND_ROWS
            geo.append((wr - b0, wr - r + NA_WIN_ROWS - 1))
        geo = tuple(geo)
        if geo not in variants:
            variants.append(geo)
        step_variant.append(variants.index(geo))
        band_start.append(b0)
    return variants, step_variant, band_start


def _na_bias_tables(rpb, variants):
    n_heads, n_dr, n_dc = rpb.shape
    qcol = np.arange(GRID_W)[:, None]
    kcol = np.arange(GRID_W)[None, :]
    win_c = np.clip(qcol - NA_WIN_COLS // 2, 0, GRID_W - NA_WIN_COLS)
    col_ok = (kcol >= win_c) & (kcol < win_c + NA_WIN_COLS)
    dc = np.clip(kcol - qcol + NA_WIN_COLS - 1, 0, 2 * NA_WIN_COLS - 2)
    dc_pad = -n_dc % 8
    sel = (np.arange(n_dc + dc_pad)[:, None] == dc.reshape(1, -1)).astype(np.float32)
    rpb2 = jnp.pad(rpb.reshape(n_heads * n_dr, n_dc), ((0, 0), (0, dc_pad)))
    toep = pl.pallas_call(
        _rpb_expand_kernel,
        out_shape=jax.ShapeDtypeStruct((n_heads * n_dr, GRID_W * GRID_W), F32),
        name="rpb_expand",
    )(rpb2, jnp.asarray(sel))
    toep = toep.reshape(n_heads, n_dr, GRID_W, GRID_W)
    toep = jnp.where(col_ok[None, None], toep, MASK_VALUE)
    tabs = []
    for geo in variants:
        rows = []
        for a0, dr0 in geo:
            rows.append(jnp.pad(toep[:, dr0:dr0 + NA_WIN_ROWS],
                                ((0, 0), (a0, NA_BAND_ROWS - NA_WIN_ROWS - a0), (0, 0), (0, 0)),
                                constant_values=MASK_VALUE))
        tab = jnp.stack(rows, axis=1)
        tabs.append(jnp.transpose(tab, (0, 1, 3, 2, 4)).reshape(n_heads, NA_Q_BLOCK, NA_BAND_ROWS * GRID_W))
    return jnp.stack(tabs, axis=1)


def _store_heads(dst, rows, t, gain, avg):
    if gain is not None:
        t = t * lax.rsqrt(_group_mean(t * t, avg) + RMS_EPS) * gain
    for h in range(NA_HEADS):
        dst[h, rows, :] = t[:, h * NA_HEAD_DIM:(h + 1) * NA_HEAD_DIM].astype(BF16)


def _na_kernel(q_ref, k_ref, v_ref, kc_ref, vc_ref, bias_ref, qg_ref, kg_ref, avg_ref, o_ref,
               kn_s, vb_s, kcn_s, vcb_s, qn_s, o_s, *, n_lat_steps, step_variant, band_start):
    step = pl.program_id(1)
    kg = kg_ref[...]
    avg = avg_ref[...]
    nt = (((1,), (1,)), ((), ()))
    band_keys = NA_BAND_ROWS * GRID_W
    all_rows = slice(None)

    @pl.when(step == 0)
    def _prepare_keys():
        chunk = 256

        def body(c, carry):
            rows = pl.ds(pl.multiple_of(c * chunk, chunk), chunk)
            _store_heads(kn_s, rows, k_ref[rows, :], kg, avg)
            _store_heads(vb_s, rows, v_ref[rows, :], None, None)
            return carry

        lax.fori_loop(0, k_ref.shape[0] // chunk, body, 0)
        _store_heads(kcn_s, all_rows, kc_ref[...], kg, avg)
        _store_heads(vcb_s, all_rows, vc_ref[...], None, None)

    _store_heads(qn_s, all_rows, q_ref[...], qg_ref[...] * NA_SCALE, avg)

    def finish(h, scores, v_parts):
        m = jnp.max(scores, axis=-1, keepdims=True)
        p = jnp.exp(scores - m)
        denom = jnp.sum(p, axis=-1, keepdims=True)
        pb = p.astype(BF16)
        acc = None
        col = 0
        for v in v_parts:
            part = jnp.dot(pb[:, col:col + v.shape[0]], v, preferred_element_type=F32)
            acc = part if acc is None else acc + part
            col += v.shape[0]
        o_s[h] = acc / denom

    @pl.when(step < n_lat_steps)
    def _latent_queries():
        variant = jnp.int32(0)
        band0 = jnp.int32(0)
        for s_, (v_, b_) in enumerate(zip(step_variant, band_start)):
            variant = jnp.where(step == s_, v_, variant)
            band0 = jnp.where(step == s_, b_ * GRID_W, band0)
        krows = pl.ds(pl.multiple_of(band0, NA_Q_BLOCK), band_keys)

        def head_body(h, carry):
            qh = qn_s[h]
            s_w = lax.dot_general(qh, kn_s[h, krows, :], nt, preferred_element_type=F32)
            s_w = s_w + bias_ref[h, variant]
            s_c = lax.dot_general(qh, kcn_s[h], nt, preferred_element_type=F32)
            finish(h, jnp.concatenate([s_w, s_c], axis=-1), (vb_s[h, krows, :], vcb_s[h]))
            return carry

        lax.fori_loop(0, NA_HEADS, head_body, 0, unroll=2)

    @pl.when(step >= n_lat_steps)
    def _context_queries():
        def head_body(h, carry):
            s = lax.dot_general(qn_s[h], kcn_s[h], nt, preferred_element_type=F32)
            finish(h, s, (vcb_s[h],))
            return carry

        lax.fori_loop(0, NA_HEADS, head_body, 0)

    o_ref[...] = jnp.concatenate([o_s[h] for h in range(NA_HEADS)], axis=-1)


def _attention(p_all, rpb, q_gain, k_gain, n_batch, seq, ctx_len, with_ctx_queries):
    n = p_all.shape[0]
    assert ctx_len == NA_Q_BLOCK and seq % NA_Q_BLOCK == 0
    n_lat_steps = seq // NA_Q_BLOCK
    n_steps = n_lat_steps + (1 if with_ctx_queries else 0)
    ctx_block0 = n_batch * seq // ctx_len
    qcol, kcol, vcol = Q_START // D_NA, KV_START // D_NA, (KV_START + D_NA) // D_NA
    variants, step_variant, band_start = _na_geometry(seq // GRID_W)
    bias_tab = _na_bias_tables(rpb, variants)

    def q_index(col):
        return lambda b, s: (jnp.where(s < n_lat_steps, b * n_lat_steps + s, ctx_block0 + b), col)

    kern = functools.partial(_na_kernel, n_lat_steps=n_lat_steps, step_variant=tuple(step_variant),
                             band_start=tuple(band_start))
    head_major = lambda rows: pltpu.VMEM((NA_HEADS, rows, NA_HEAD_DIM), BF16)
    return pl.pallas_call(
        kern,
        grid=(n_batch, n_steps),
        in_specs=[
            pl.BlockSpec((NA_Q_BLOCK, D_NA), q_index(qcol)),
            pl.BlockSpec((seq, D_NA), lambda b, s: (b, kcol)),
            pl.BlockSpec((seq, D_NA), lambda b, s: (b, vcol)),
            pl.BlockSpec((ctx_len, D_NA), lambda b, s: (ctx_block0 + b, kcol)),
            pl.BlockSpec((ctx_len, D_NA), lambda b, s: (ctx_block0 + b, vcol)),
            pl.BlockSpec(bias_tab.shape, lambda b, s: (0, 0, 0, 0), pipeline_mode=pl.Buffered(1)),
            pl.BlockSpec((1, D_NA), lambda b, s: (0, 0)),
            pl.BlockSpec((1, D_NA), lambda b, s: (0, 0)),
            pl.BlockSpec((D_NA, D_NA), lambda b, s: (0, 0)),
        ],
        out_specs=pl.BlockSpec((NA_Q_BLOCK, D_NA), q_index(0)),
        out_shape=jax.ShapeDtypeStruct((n, D_NA), F32),
        scratch_shapes=[
            head_major(seq), head_major(seq), head_major(ctx_len), head_major(ctx_len),
            head_major(NA_Q_BLOCK),
            pltpu.VMEM((NA_HEADS, NA_Q_BLOCK, NA_HEAD_DIM), F32),
        ],
        compiler_params=_cparams("arbitrary", "arbitrary"),
        name="attention",
    )(p_all, p_all, p_all, p_all, p_all, bias_tab,
      jnp.tile(q_gain.reshape(1, NA_HEAD_DIM), (1, NA_HEADS)),
      jnp.tile(k_gain.reshape(1, NA_HEAD_DIM), (1, NA_HEADS)),
      jnp.asarray(_group_avg_matrix(NA_HEADS, NA_HEAD_DIM), BF16))


def _dft_matrices(length):
    idx = jnp.arange(length, dtype=jnp.int32)
    step = 64
    t_hi = jnp.arange(length // step, dtype=jnp.int32) * step
    t_lo = jnp.arange(step, dtype=jnp.int32)
    ang_hi = ((idx[:, None] * t_hi[None, :]) % (2 * length)).astype(F32) * (math.pi / length)
    ang_lo = ((idx[:, None] * t_lo[None, :]) % (2 * length)).astype(F32) * (math.pi / length)
    c_hi, s_hi = jnp.cos(ang_hi)[:, :, None], jnp.sin(ang_hi)[:, :, None]
    c_lo, s_lo = jnp.cos(ang_lo)[:, None, :], jnp.sin(ang_lo)[:, None, :]
    gc = (c_hi * c_lo - s_hi * s_lo).reshape(length, length)
    gs = (s_hi * c_lo + c_hi * s_lo).reshape(length, length)
    nyq = jnp.where(idx % 2 == 0, 1.0, -1.0).astype(F32)
    gs = jnp.where(idx[:, None] == 0, nyq[None, :], gs)
    return gc.astype(BF16), gs.astype(BF16), gs.T.astype(BF16)


def _hyena_positions(length):
    t = jnp.linspace(0.0, 1.0, length, dtype=F32)[:, None]
    w = 2.0 * math.pi * jnp.arange(length, dtype=F32)[:, None] / length
    f = jnp.linspace(1e-4, HYENA_POS_BANDS - 1, HYENA_POS_BANDS, dtype=F32)[None, :]
    z = jnp.concatenate([t, jnp.cos(f * w), -jnp.sin(f * w)], axis=-1)
    z = jnp.pad(z, ((0, 0), (0, LANES - HYENA_EMB)))
    min_decay = math.log(HYENA_DECAY_TARGET) / HYENA_SLOW_DECAY
    max_decay = math.log(HYENA_DECAY_TARGET) / HYENA_FAST_DECAY
    deltas = jnp.abs(jnp.linspace(min_decay, max_decay, D_HYENA, dtype=F32))[None, :]
    return z, jnp.exp(-t * deltas)


def _filter_kernel(z_ref, decay_ref, w1_ref, b1_ref, w2_ref, b2_ref, w3_ref, freq_ref, o_ref):
    freq = freq_ref[...]
    hdn = jnp.dot(z_ref[...], w1_ref[...], preferred_element_type=F32, precision=HI) + b1_ref[...]
    hdn = jnp.sin(freq[0:1] * hdn)
    hdn = jnp.dot(hdn, w2_ref[...], preferred_element_type=F32, precision=HI) + b2_ref[...]
    hdn = jnp.sin(freq[1:2] * hdn)
    h = jnp.dot(hdn, w3_ref[...], preferred_element_type=F32, precision=HI)
    decay = decay_ref[...]
    first_row = lax.broadcasted_iota(jnp.int32, decay.shape, 0) == 0
    outs = []
    for n in range(HYENA_ORDER):
        base = 2 * n * D_HYENA
        hf = h[:, base:base + D_HYENA] * decay
        hb = h[:, base + D_HYENA:base + 2 * D_HYENA] * decay
        norm = jnp.sum(jnp.abs(hf), axis=0, keepdims=True) + jnp.sum(jnp.abs(hb), axis=0, keepdims=True)
        outs.append(hf / norm)
        outs.append(jnp.where(first_row, 0.0, hb / norm))
    o_ref[...] = jnp.concatenate(outs, axis=-1)


def _hyena_filters(z, decay, w1p, b1, w2, b2, w3, freq):
    length = z.shape[0]
    full = lambda a: pl.BlockSpec(a.shape, lambda i: (0,) * a.ndim)
    args = (z, decay, w1p, b1.reshape(1, -1), w2, b2.reshape(1, -1), w3, freq)
    return pl.pallas_call(
        _filter_kernel,
        grid=(1,),
        in_specs=[full(a) for a in args],
        out_specs=pl.BlockSpec((length, 2 * HYENA_ORDER * D_HYENA), lambda i: (0, 0)),
        out_shape=jax.ShapeDtypeStruct((length, 2 * HYENA_ORDER * D_HYENA), F32),
        compiler_params=_cparams("arbitrary"),
        name="hyena_filter",
    )(*args)


def _filter_dft_kernel(gc_ref, gs_ref, h_ref, o_ref, *, length):
    hb = h_ref[...].astype(BF16)
    fa = jnp.dot(gc_ref[...], hb, preferred_element_type=F32)
    fb = jnp.dot(gs_ref[...], hb, preferred_element_type=F32)
    rows = lax.broadcasted_iota(jnp.int32, (fa.shape[0], D_HYENA), 0) + pl.program_id(0) * fa.shape[0]
    dc_row = rows == 0
    inv_n = 1.0 / (2 * length)
    outs = []
    for n in range(HYENA_ORDER):
        base = 2 * n * D_HYENA
        f_sl = slice(base, base + D_HYENA)
        b_sl = slice(base + D_HYENA, base + 2 * D_HYENA)
        kr = fa[:, f_sl] + fa[:, b_sl]
        ki = fb[:, b_sl] - fb[:, f_sl]
        k_nyq = fb[:, f_sl] + fb[:, b_sl]
        outs.append(jnp.where(dc_row, kr * inv_n, 2.0 * inv_n * kr))
        outs.append(jnp.where(dc_row, 0.0, 2.0 * inv_n * ki))
        outs.append(jnp.where(dc_row, 0.0, -2.0 * inv_n * ki))
        outs.append(jnp.where(dc_row, k_nyq * inv_n, 2.0 * inv_n * kr))
    o_ref[...] = jnp.concatenate(outs, axis=-1)


def _filter_dft(gc, gs, hfilt):
    length = gc.shape[0]
    tile = min(DFT_TILE, length)
    width = 4 * HYENA_ORDER * D_HYENA
    return pl.pallas_call(
        functools.partial(_filter_dft_kernel, length=length),
        grid=(length // tile,),
        in_specs=[
            pl.BlockSpec((tile, length), lambda j: (j, 0)),
            pl.BlockSpec((tile, length), lambda j: (j, 0)),
            pl.BlockSpec(hfilt.shape, lambda j: (0, 0)),
        ],
        out_specs=pl.BlockSpec((tile, width), lambda j: (j, 0)),
        out_shape=jax.ShapeDtypeStruct((length, width), F32),
        compiler_params=_cparams("arbitrary"),
        name="hyena_filter_dft",
    )(gc, gs, hfilt)


def _short_conv_kernel(a0_ref, a1_ref, a2_ref, w_ref, b_ref, o_ref):
    w = w_ref[...]
    b = b_ref[...]
    length = a0_ref.shape[0]
    rows = lax.broadcasted_iota(jnp.int32, (length, D_HYENA), 0)
    for j, a_ref in enumerate((a0_ref, a1_ref, a2_ref)):
        cols = slice(j * D_HYENA, (j + 1) * D_HYENA)
        a = a_ref[...]
        prev = jnp.where(rows == 0, 0.0, pltpu.roll(a, 1, 0))
        nxt = jnp.where(rows == length - 1, 0.0, pltpu.roll(a, length - 1, 0))
        o_ref[:, cols] = prev * w[0:1, cols] + a * w[1:2, cols] + nxt * w[2:3, cols] + b[:, cols]


def _short_conv(p_all, short_w, short_b, n_batch, length, row_block0):
    c0 = HY_START // D_HYENA
    spec = lambda j: pl.BlockSpec((length, D_HYENA), lambda b: (row_block0 + b, c0 + j))
    return pl.pallas_call(
        _short_conv_kernel,
        grid=(n_batch,),
        in_specs=[spec(0), spec(1), spec(2),
                  pl.BlockSpec((3, 3 * D_HYENA), lambda b: (0, 0)),
                  pl.BlockSpec((1, 3 * D_HYENA), lambda b: (0, 0))],
        out_specs=pl.BlockSpec((length, 3 * D_HYENA), lambda b: (b, 0)),
        out_shape=jax.ShapeDtypeStruct((n_batch * length, 3 * D_HYENA), F32),
        compiler_params=_cparams("arbitrary"),
        name="hyena_short_conv",
    )(p_all, p_all, p_all, short_w, short_b.reshape(1, -1))


def _conv_fwd_kernel(gc_ref, gs_ref, u_ref, k_ref, pa_ref, pb_ref):
    u = u_ref[...].astype(BF16)
    a = jnp.dot(gc_ref[...], u, preferred_element_type=F32)
    b = jnp.dot(gs_ref[...], u, preferred_element_type=F32)
    k = k_ref[...]
    c = D_HYENA
    pa_ref[...] = (a * k[:, 0:c] + b * k[:, c:2 * c]).astype(BF16)
    pb_ref[...] = (a * k[:, 2 * c:3 * c] + b * k[:, 3 * c:4 * c]).astype(BF16)


def _conv_fwd(gc, gs, u, u_col, kpack, order, n_batch):
    length = gc.shape[0]
    tile = min(DFT_TILE, length)
    nt = length // tile
    out = jax.ShapeDtypeStruct((n_batch * length, D_HYENA), BF16)
    return pl.pallas_call(
        _conv_fwd_kernel,
        grid=(nt, n_batch),
        in_specs=[
            pl.BlockSpec((tile, length), lambda j, b: (j, 0)),
            pl.BlockSpec((tile, length), lambda j, b: (j, 0)),
            pl.BlockSpec((length, D_HYENA), lambda j, b: (b, u_col)),
            pl.BlockSpec((tile, 4 * D_HYENA), lambda j, b: (j, order)),
        ],
        out_specs=[pl.BlockSpec((tile, D_HYENA), lambda j, b: (b * nt + j, 0))] * 2,
        out_shape=[out, out],
        compiler_params=_cparams("arbitrary", "arbitrary"),
        name="hyena_conv_fwd",
    )(gc, gs, u, kpack)


def _conv_inv_kernel(gc_ref, gst_ref, pa_ref, pb_ref, z_ref, gate_ref, d_ref, o_ref):
    y = jnp.dot(gc_ref[...], pa_ref[...], preferred_element_type=F32)
    y = y + jnp.dot(gst_ref[...], pb_ref[...], preferred_element_type=F32)
    o_ref[...] = gate_ref[...] * (y + d_ref[...] * z_ref[...])


def _conv_inv(gc, gst, pa, pb, z_prev, z_col, a3, gate_col, d_bias, n_batch):
    length = gc.shape[0]
    tile = min(DFT_TILE, length)
    nt = length // tile
    return pl.pallas_call(
        _conv_inv_kernel,
        grid=(nt, n_batch),
        in_specs=[
            pl.BlockSpec((tile, length), lambda j, b: (j, 0)),
            pl.BlockSpec((tile, length), lambda j, b: (j, 0)),
            pl.BlockSpec((length, D_HYENA), lambda j, b: (b, 0)),
            pl.BlockSpec((length, D_HYENA), lambda j, b: (b, 0)),
            pl.BlockSpec((tile, D_HYENA), lambda j, b: (b * nt + j, z_col)),
            pl.BlockSpec((tile, D_HYENA), lambda j, b: (b * nt + j, gate_col)),
            pl.BlockSpec((1, D_HYENA), lambda j, b: (0, 0)),
        ],
        out_specs=pl.BlockSpec((tile, D_HYENA), lambda j, b: (b * nt + j, 0)),
        out_shape=jax.ShapeDtypeStruct((n_batch * length, D_HYENA), F32),
        compiler_params=_cparams("arbitrary", "arbitrary"),
        name="hyena_conv_inv",
    )(gc, gst, pa, pb, z_prev, a3, d_bias.reshape(1, D_HYENA))


def _hyena(p_all, short_w, short_b, d_bias, kpack, mats, n_batch, length, row_block0):
    gc, gs, gst = mats
    a3 = _short_conv(p_all, short_w, short_b, n_batch, length, row_block0)
    pa, pb = _conv_fwd(gc, gs, a3, 0, kpack, 0, n_batch)
    z1 = _conv_inv(gc, gst, pa, pb, a3, 0, a3, 1, d_bias[0], n_batch)
    pa, pb = _conv_fwd(gc, gs, z1, 0, kpack, 1, n_batch)
    return _conv_inv(gc, gst, pa, pb, z1, 0, a3, 2, d_bias[1], n_batch)


def _out_proj_kernel(x_ref, gm_ref, na_ref, hyl_ref, hyc_ref, w_ref, mod_ref, o_ref, *, n_lat_tiles):
    is_lat = pl.program_id(0) < n_lat_tiles
    hy = jnp.where(is_lat, hyl_ref[...], hyc_ref[...])
    y = jnp.dot(gm_ref[...].astype(BF16), w_ref[0:D_GMLP, :], preferred_element_type=F32)
    y = y + jnp.dot(na_ref[...].astype(BF16), w_ref[D_GMLP:D_GMLP + D_NA, :], preferred_element_type=F32)
    y = y + jnp.dot(hy.astype(BF16), w_ref[D_GMLP + D_NA:, :], preferred_element_type=F32)
    o_ref[...] = x_ref[...] + mod_ref[2:3, :] * y


def _out_proj(x_all, gm, na, hy_lat, hy_ctx, w_bf, mods_l, n_batch, seq, n_tiles):
    d = x_all.shape[1]
    n_lat_tiles = n_batch * seq // TOKEN_TILE
    n_ctx_tiles = hy_ctx.shape[0] // TOKEN_TILE
    row = lambda w: pl.BlockSpec((TOKEN_TILE, w), lambda i: (i, 0))
    return pl.pallas_call(
        functools.partial(_out_proj_kernel, n_lat_tiles=n_lat_tiles),
        grid=(n_tiles,),
        in_specs=[
            row(d), row(D_GMLP), row(D_NA),
            pl.BlockSpec((TOKEN_TILE, D_HYENA), lambda i: (jnp.minimum(i, n_lat_tiles - 1), 0)),
            pl.BlockSpec((TOKEN_TILE, D_HYENA),
                         lambda i: (jnp.clip(i - n_lat_tiles, 0, n_ctx_tiles - 1), 0)),
            pl.BlockSpec(w_bf.shape, lambda i: (0, 0)),
            pl.BlockSpec((None, N_MOD, d), _mod_index(seq // TOKEN_TILE, n_batch)),
        ],
        out_specs=row(d),
        out_shape=jax.ShapeDtypeStruct((n_tiles * TOKEN_TILE, d), F32),
        compiler_params=_cparams("arbitrary"),
        name="out_proj",
    )(x_all, gm, na, hy_lat, hy_ctx, w_bf, mods_l)


def _router_kernel(x_ref, g_ref, mod_ref, wr_ref, br_ref, h_ref, comb_ref):
    mod = mod_ref[...]
    h = _rms_rows(x_ref[...]) * g_ref[...]
    h = h * (1.0 + mod[4:5]) + mod[3:4]
    h_ref[...] = h.astype(BF16)
    logits = jnp.dot(h, wr_ref[...], preferred_element_type=F32, precision=HI) + br_ref[...]
    lane = lax.broadcasted_iota(jnp.int32, logits.shape, 1)
    neg = -jnp.inf
    is_group = (lane >= MOE_EXPERTS) & (lane < MOE_EXPERTS + MOE_GROUPS)
    lg = jnp.where(is_group, logits, neg)
    mg = jnp.max(lg, axis=-1, keepdims=True)
    g_p = 1.0 / jnp.sum(jnp.exp(lg - mg), axis=-1, keepdims=True)
    g_idx = jnp.min(jnp.where(lg == mg, lane, 2 * LANES), axis=-1, keepdims=True) - MOE_EXPERTS
    in_group = (lane >= g_idx * MOE_EXPERTS_PER_GROUP) & (lane < (g_idx + 1) * MOE_EXPERTS_PER_GROUP)
    le = jnp.where(in_group, logits, neg)
    me = jnp.max(le, axis=-1, keepdims=True)
    pe = jnp.exp(le - me)
    pe = pe / jnp.sum(pe, axis=-1, keepdims=True)
    p1 = jnp.max(pe, axis=-1, keepdims=True)
    i1 = jnp.min(jnp.where(in_group & (pe == p1), lane, 2 * LANES), axis=-1, keepdims=True)
    pe2 = jnp.where(in_group & (lane != i1), pe, neg)
    p2 = jnp.max(pe2, axis=-1, keepdims=True)
    i2 = jnp.min(jnp.where(pe2 == p2, lane, 2 * LANES), axis=-1, keepdims=True)
    tot = p1 + p2
    comb_ref[...] = g_p * jnp.where(lane == i1, p1 / tot, jnp.where(lane == i2, p2 / tot, 0.0))


def _router(x_all, gain, mods_l, w_router, b_router, n_batch, seq, n_tiles):
    d = x_all.shape[1]
    return pl.pallas_call(
        _router_kernel,
        grid=(n_tiles,),
        in_specs=[
            pl.BlockSpec((TOKEN_TILE, d), lambda i: (i, 0)),
            pl.BlockSpec((1, d), lambda i: (0, 0)),
            pl.BlockSpec((None, N_MOD, d), _mod_index(seq // TOKEN_TILE, n_batch)),
            pl.BlockSpec((d, LANES), lambda i: (0, 0)),
            pl.BlockSpec((1, LANES), lambda i: (0, 0)),
        ],
        out_specs=[pl.BlockSpec((TOKEN_TILE, d), lambda i: (i, 0)),
                   pl.BlockSpec((TOKEN_TILE, LANES), lambda i: (i, 0))],
        out_shape=[jax.ShapeDtypeStruct((n_tiles * TOKEN_TILE, d), BF16),
                   jax.ShapeDtypeStruct((n_tiles * TOKEN_TILE, LANES), F32)],
        compiler_params=_cparams("arbitrary"),
        name="moe_router",
    )(x_all, gain.reshape(1, d), mods_l, w_router, b_router)


def _experts_kernel(x_ref, h_ref, comb_ref, wg_ref, wu_ref, wd_ref, mod_ref, o_ref, acc_ref):
    e = pl.program_id(1)

    @pl.when(e == 0)
    def _init():
        acc_ref[...] = jnp.zeros_like(acc_ref)

    h = h_ref[...]
    comb = comb_ref[...]
    lane = lax.broadcasted_iota(jnp.int32, comb.shape, 1)
    c_e = jnp.sum(jnp.where(lane == e, comb, 0.0), axis=-1, keepdims=True)
    gate = jnp.dot(h, wg_ref[...], preferred_element_type=F32)
    up = jnp.dot(h, wu_ref[...], preferred_element_type=F32)
    act = _silu(gate) * up * c_e
    acc_ref[...] += jnp.dot(act.astype(BF16), wd_ref[...], preferred_element_type=F32)

    @pl.when(e == pl.num_programs(1) - 1)
    def _finish():
        o_ref[...] = x_ref[...] + mod_ref[5:6, :] * acc_ref[...]


def _experts(x_all, h_bf, comb, wg, wu, wd, mods_l, n_batch, seq, n_tiles):
    d = x_all.shape[1]
    f = wg.shape[-1]
    return pl.pallas_call(
        _experts_kernel,
        grid=(n_tiles, MOE_EXPERTS),
        in_specs=[
            pl.BlockSpec((TOKEN_TILE, d), lambda i, e: (i, 0)),
            pl.BlockSpec((TOKEN_TILE, d), lambda i, e: (i, 0)),
            pl.BlockSpec((TOKEN_TILE, LANES), lambda i, e: (i, 0)),
            pl.BlockSpec((None, d, f), lambda i, e: (e, 0, 0)),
            pl.BlockSpec((None, d, f), lambda i, e: (e, 0, 0)),
            pl.BlockSpec((None, f, d), lambda i, e: (e, 0, 0)),
            pl.BlockSpec((None, N_MOD, d), lambda i, e: _mod_index(seq // TOKEN_TILE, n_batch)(i)),
        ],
        out_specs=pl.BlockSpec((TOKEN_TILE, d), lambda i, e: (i, 0)),
        out_shape=jax.ShapeDtypeStruct((n_tiles * TOKEN_TILE, d), F32),
        scratch_shapes=[pltpu.VMEM((TOKEN_TILE, d), F32)],
        compiler_params=_cparams("arbitrary", "arbitrary"),
        name="moe_experts",
    )(x_all, h_bf, comb, wg, wu, wd, mods_l)


def kernel(x, c, ctx, c_ctx, w_ada, b_ada, g_mix, g_ffn, w_in, w_out, gmlp_v_gain, gmlp_ws, gmlp_bs,
           na_q_gain, na_k_gain, na_rpb, hy_short_w, hy_short_b, hy_w1, hy_b1, hy_w2, hy_b2, hy_w3,
           hy_freq, hy_bias, moe_w_rg, moe_b_rg, moe_w_re, moe_b_re, moe_w_gate, moe_w_up, moe_w_down):
    n_batch, seq, d = x.shape
    ctx_len = ctx.shape[1]
    depth = w_ada.shape[0]
    n_lat = n_batch * seq
    n_ctx = n_batch * ctx_len
    assert d == D_MODEL and seq % TOKEN_TILE == 0 and n_ctx % TOKEN_TILE == 0
    assert seq % GMLP_CHUNK == 0 and ctx_len % GMLP_CHUNK == 0

    pad_rows = -(n_batch + 1) % 8
    cs = jnp.concatenate([c, c_ctx[None, :], jnp.zeros((pad_rows, d), F32)], axis=0)
    mods = _modulation(cs, w_ada, b_ada)[:, :n_batch + 1].reshape(depth, n_batch + 1, N_MOD, d)

    x_all = jnp.concatenate([x.reshape(n_lat, d), ctx.reshape(n_ctx, d)], axis=0)
    n_all_tiles = (n_lat + n_ctx) // TOKEN_TILE
    n_lat_tiles = n_lat // TOKEN_TILE

    mats_lat = _dft_matrices(seq)
    mats_ctx = _dft_matrices(ctx_len)
    pos_lat = _hyena_positions(seq)
    pos_ctx = _hyena_positions(ctx_len)

    w_in_bf = w_in.astype(BF16)
    w_out_bf = w_out.astype(BF16)
    wg_bf = moe_w_gate.astype(BF16).reshape(depth, MOE_EXPERTS, d, MOE_HIDDEN)
    wu_bf = moe_w_up.astype(BF16).reshape(depth, MOE_EXPERTS, d, MOE_HIDDEN)
    wd_bf = moe_w_down.astype(BF16).reshape(depth, MOE_EXPERTS, MOE_HIDDEN, d)
    lane_pad = LANES - MOE_EXPERTS - MOE_GROUPS
    w_router = jnp.pad(jnp.concatenate([moe_w_re, moe_w_rg], axis=-1), ((0, 0), (0, 0), (0, lane_pad)))
    b_router = jnp.pad(jnp.concatenate([moe_b_re, moe_b_rg], axis=-1), ((0, 0), (0, lane_pad)))[:, None, :]
    w1_pad = jnp.pad(hy_w1, ((0, 0), (0, LANES - HYENA_EMB), (0, 0)))

    for l in range(depth):
        last = l == depth - 1
        mods_l = mods[l]
        n_tiles = n_lat_tiles if last else n_all_tiles

        p_all = _in_proj(x_all, g_mix[l], mods_l, w_in_bf[l], n_batch, seq)

        filt = (w1_pad[l], hy_b1[l], hy_w2[l], hy_b2[l], hy_w3[l], hy_freq[l])
        kpack_lat = _filter_dft(mats_lat[0], mats_lat[1], _hyena_filters(*pos_lat, *filt))
        hy_lat = _hyena(p_all, hy_short_w[l], hy_short_b[l], hy_bias[l], kpack_lat, mats_lat,
                        n_batch, seq, 0)
        if last:
            hy_ctx = hy_lat
            gm = _gmlp(p_all, n_lat, gmlp_v_gain[l], gmlp_ws[l], gmlp_bs[l].T)
        else:
            kpack_ctx = _filter_dft(mats_ctx[0], mats_ctx[1], _hyena_filters(*pos_ctx, *filt))
            hy_ctx = _hyena(p_all, hy_short_w[l], hy_short_b[l], hy_bias[l], kpack_ctx, mats_ctx,
                            n_batch, ctx_len, n_lat // ctx_len)
            gm = _gmlp(p_all, n_lat + n_ctx, gmlp_v_gain[l], gmlp_ws[l], gmlp_bs[l].T)

        na = _attention(p_all, na_rpb[l], na_q_gain[l], na_k_gain[l],
                        n_batch, seq, ctx_len, not last)

        x_all = _out_proj(x_all, gm, na, hy_lat, hy_ctx, w_out_bf[l], mods_l, n_batch, seq, n_tiles)

        h_bf, comb = _router(x_all, g_ffn[l], mods_l, w_router[l], b_router[l], n_batch, seq, n_tiles)
        x_all = _experts(x_all, h_bf, comb, wg_bf[l], wu_bf[l], wd_bf[l], mods_l, n_batch, seq, n_tiles)

    return x_all[:n_lat].reshape(n_batch, seq, d)
```

```python
import functools
import math

import numpy as np
import jax
import jax.numpy as jnp
from jax import lax
from jax.experimental import pallas as pl
from jax.experimental.pallas import tpu as pltpu
from jax.experimental.pallas import tpu_sc as plsc

F32 = jnp.float32
BF16 = jnp.bfloat16
HI = lax.Precision.HIGHEST

D_MODEL = 1024
GRID_W = 64
D_GMLP = D_MODEL // 4
D_NA = D_MODEL // 2
D_HYENA = D_MODEL // 4
D_IN = 2 * D_GMLP + 3 * D_NA + 3 * D_HYENA
Q_START = 2 * D_GMLP
KV_START = 2 * D_GMLP + D_NA
HY_START = 2 * D_GMLP + 3 * D_NA
GMLP_GROUPS = 4
GMLP_GROUP_DIM = D_GMLP // GMLP_GROUPS
GMLP_CHUNK = 128
NA_HEAD_DIM = 64
NA_HEADS = D_NA // NA_HEAD_DIM
NA_SCALE = NA_HEAD_DIM ** -0.5
NA_WIN_ROWS = 8
NA_WIN_COLS = 16
HYENA_ORDER = 2
HYENA_POS_BANDS = 16
HYENA_EMB = 1 + 2 * HYENA_POS_BANDS
HYENA_FILTER_HIDDEN = 64
HYENA_DECAY_TARGET = 1e-2
HYENA_FAST_DECAY = 0.3
HYENA_SLOW_DECAY = 1.5
MOE_GROUPS = 4
MOE_EXPERTS_PER_GROUP = 8
MOE_EXPERTS = MOE_GROUPS * MOE_EXPERTS_PER_GROUP
MOE_HIDDEN = 256
MOE_PAIRS = MOE_EXPERTS_PER_GROUP * (MOE_EXPERTS_PER_GROUP - 1) // 2
MOE_BUCKETS = MOE_GROUPS * MOE_PAIRS
N_MOD = 6
RMS_EPS = 1e-6
LN_EPS = 1e-5

LANES = 128
TOKEN_TILE = 512
NA_Q_ROWS = 4
NA_Q_BLOCK = NA_Q_ROWS * GRID_W
NA_BAND_ROWS = NA_WIN_ROWS + NA_Q_ROWS
DFT_TILE = 512
MOE_SORT_TILE = 128
SC_GATHER_ROWS = 64
SC_WORKERS = 32
MASK_VALUE = -1e30
VMEM_LIMIT = 56 * 1024 * 1024


def _cparams(*sem):
    return pltpu.CompilerParams(dimension_semantics=sem, vmem_limit_bytes=VMEM_LIMIT)


def _silu(x):
    return x * jax.nn.sigmoid(x)


def _rms_rows(x):
    return x * lax.rsqrt(jnp.mean(x * x, axis=-1, keepdims=True) + RMS_EPS)


def _mods_kernel(cs_ref, w_ref, b_ref, o_ref):
    s = _silu(cs_ref[...])
    o_ref[...] = jnp.dot(s, w_ref[...], preferred_element_type=F32, precision=HI) + b_ref[...]


def _modulation(cs, w_ada, b_ada):
    depth, d, nd = w_ada.shape
    rows = cs.shape[0]
    col = 1024
    return pl.pallas_call(
        _mods_kernel,
        grid=(depth, nd // col),
        in_specs=[
            pl.BlockSpec((rows, d), lambda l, j: (0, 0)),
            pl.BlockSpec((None, d, col), lambda l, j: (l, 0, j)),
            pl.BlockSpec((None, 1, col), lambda l, j: (l, 0, j)),
        ],
        out_specs=pl.BlockSpec((None, rows, col), lambda l, j: (l, 0, j)),
        out_shape=jax.ShapeDtypeStruct((depth, rows, nd), F32),
        compiler_params=_cparams("arbitrary", "arbitrary"),
        name="modulation",
    )(cs, w_ada, b_ada.reshape(depth, 1, nd))


def _in_proj_kernel(x_ref, g_ref, mod_ref, w_ref, o_ref):
    mod = mod_ref[...]
    h = _rms_rows(x_ref[...]) * g_ref[...]
    h = h * (1.0 + mod[1:2]) + mod[0:1]
    o_ref[...] = jnp.dot(h.astype(BF16), w_ref[...], preferred_element_type=F32)


def _mod_index(tiles_per_batch, n_batch):
    return lambda i: (jnp.minimum(i // tiles_per_batch, n_batch), 0, 0)


def _in_proj(x_all, gain, mods_l, w_bf, n_batch, seq):
    n, d = x_all.shape
    d_in = w_bf.shape[1]
    return pl.pallas_call(
        _in_proj_kernel,
        grid=(n // TOKEN_TILE,),
        in_specs=[
            pl.BlockSpec((TOKEN_TILE, d), lambda i: (i, 0)),
            pl.BlockSpec((1, d), lambda i: (0, 0)),
            pl.BlockSpec((None, N_MOD, d), _mod_index(seq // TOKEN_TILE, n_batch)),
            pl.BlockSpec((d, d_in), lambda i: (0, 0)),
        ],
        out_specs=pl.BlockSpec((TOKEN_TILE, d_in), lambda i: (i, 0)),
        out_shape=jax.ShapeDtypeStruct((n, d_in), F32),
        compiler_params=_cparams("arbitrary"),
        name="in_proj",
    )(x_all, gain.reshape(1, d), mods_l, w_bf)


def _group_avg_matrix(groups, width):
    return np.kron(np.eye(groups), np.full((width, width), 1.0 / width))


def _group_mean(t, avg):
    hi = t.astype(BF16)
    lo = (t - hi.astype(F32)).astype(BF16)
    return (jnp.dot(hi, avg, preferred_element_type=F32) + jnp.dot(lo, avg, preferred_element_type=F32))


def _gmlp_kernel(u_ref, v_ref, gain_ref, avg_ref, ws_ref, bs_ref, o_ref):
    avg = avg_ref[...]
    v = jax.nn.gelu(v_ref[...])
    v = v - _group_mean(v, avg)
    v = v * lax.rsqrt(_group_mean(v * v, avg) + LN_EPS) * gain_ref[...]
    vb = v.astype(BF16)
    bs = bs_ref[...]
    for c in range(u_ref.shape[0] // GMLP_CHUNK):
        rows = slice(c * GMLP_CHUNK, (c + 1) * GMLP_CHUNK)
        outs = []
        for g in range(GMLP_GROUPS):
            cols = slice(g * GMLP_GROUP_DIM, (g + 1) * GMLP_GROUP_DIM)
            s = jnp.dot(ws_ref[g], vb[rows, cols], preferred_element_type=F32) + bs[:, g:g + 1]
            outs.append(s)
        o_ref[rows, :] = jax.nn.gelu(u_ref[rows, :]) * jnp.concatenate(outs, axis=-1)


def _gmlp(p_all, n, v_gain, ws, bs_t):
    avg = _group_avg_matrix(GMLP_GROUPS, GMLP_GROUP_DIM)
    return pl.pallas_call(
        _gmlp_kernel,
        grid=(n // TOKEN_TILE,),
        in_specs=[
            pl.BlockSpec((TOKEN_TILE, D_GMLP), lambda i: (i, 0)),
            pl.BlockSpec((TOKEN_TILE, D_GMLP), lambda i: (i, 1)),
            pl.BlockSpec((1, D_GMLP), lambda i: (0, 0)),
            pl.BlockSpec((D_GMLP, D_GMLP), lambda i: (0, 0)),
            pl.BlockSpec((GMLP_GROUPS, GMLP_CHUNK, GMLP_CHUNK), lambda i: (0, 0, 0)),
            pl.BlockSpec((GMLP_CHUNK, GMLP_GROUPS), lambda i: (0, 0)),
        ],
        out_specs=pl.BlockSpec((TOKEN_TILE, D_GMLP), lambda i: (i, 0)),
        out_shape=jax.ShapeDtypeStruct((n, D_GMLP), F32),
        compiler_params=_cparams("arbitrary"),
        name="gmlp",
    )(p_all, p_all, v_gain.reshape(1, D_GMLP), jnp.asarray(avg, BF16), ws.astype(BF16), bs_t)


def _rpb_expand_kernel(rpb_ref, sel_ref, o_ref):
    o_ref[...] = jnp.dot(rpb_ref[...], sel_ref[...], preferred_element_type=F32, precision=HI)


def _na_geometry(grid_rows):
    variants, step_variant, band_start = [], [], []
    for r0 in range(0, grid_rows, NA_Q_ROWS):
        b0 = int(np.clip(r0 - NA_WIN_ROWS // 2, 0, grid_rows - NA_BAND_ROWS))
        geo = []
        for r in range(r0, r0 + NA_Q_ROWS):
            wr = int(np.clip(r - NA_WIN_ROWS // 2, 0, grid_rows - NA_WIN_ROWS))
            assert b0 <= wr and wr + NA_WIN_ROWS <= b0 + NA_BAND_ROWS
            geo.append((wr - b0, wr - r + NA_WIN_ROWS - 1))
        geo = tuple(geo)
        if geo not in variants:
            variants.append(geo)
        step_variant.append(variants.index(geo))
        band_start.append(b0)
    return variants, step_variant, band_start


def _na_bias_tables(rpb, variants):
    n_heads, n_dr, n_dc = rpb.shape
    qcol = np.arange(GRID_W)[:, None]
    kcol = np.arange(GRID_W)[None, :]
    win_c = np.clip(qcol - NA_WIN_COLS // 2, 0, GRID_W - NA_WIN_COLS)
    col_ok = (kcol >= win_c) & (kcol < win_c + NA_WIN_COLS)
    dc = np.clip(kcol - qcol + NA_WIN_COLS - 1, 0, 2 * NA_WIN_COLS - 2)
    dc_pad = -n_dc % 8
    sel = (np.arange(n_dc + dc_pad)[:, None] == dc.reshape(1, -1)).astype(np.float32)
    rpb2 = jnp.pad(rpb.reshape(n_heads * n_dr, n_dc), ((0, 0), (0, dc_pad)))
    toep = pl.pallas_call(
        _rpb_expand_kernel,
        out_shape=jax.ShapeDtypeStruct((n_heads * n_dr, GRID_W * GRID_W), F32),
        name="rpb_expand",
    )(rpb2, jnp.asarray(sel))
    toep = toep.reshape(n_heads, n_dr, GRID_W, GRID_W)
    toep = jnp.where(col_ok[None, None], toep, MASK_VALUE)
    tabs = []
    for geo in variants:
        rows = []
        for a0, dr0 in geo:
            rows.append(jnp.pad(toep[:, dr0:dr0 + NA_WIN_ROWS],
                                ((0, 0), (a0, NA_BAND_ROWS - NA_WIN_ROWS - a0), (0, 0), (0, 0)),
                                constant_values=MASK_VALUE))
        tab = jnp.stack(rows, axis=1)
        tabs.append(jnp.transpose(tab, (0, 1, 3, 2, 4)).reshape(n_heads, NA_Q_BLOCK, NA_BAND_ROWS * GRID_W))
    return jnp.stack(tabs, axis=1)


def _store_heads(dst, rows, t, gain, avg):
    if gain is not None:
        t = t * lax.rsqrt(_group_mean(t * t, avg) + RMS_EPS) * gain
    for h in range(NA_HEADS):
        dst[h, rows, :] = t[:, h * NA_HEAD_DIM:(h + 1) * NA_HEAD_DIM].astype(BF16)


def _na_kernel(q_ref, k_ref, v_ref, kc_ref, vc_ref, bias_ref, qg_ref, kg_ref, avg_ref, o_ref,
               kn_s, vb_s, kcn_s, vcb_s, qn_s, o_s, *, n_lat_steps, step_variant, band_start):
    step = pl.program_id(1)
    kg = kg_ref[...]
    avg = avg_ref[...]
    nt = (((1,), (1,)), ((), ()))
    band_keys = NA_BAND_ROWS * GRID_W
    all_rows = slice(None)

    @pl.when(step == 0)
    def _prepare_keys():
        chunk = 256

        def body(c, carry):
            rows = pl.ds(pl.multiple_of(c * chunk, chunk), chunk)
            _store_heads(kn_s, rows, k_ref[rows, :], kg, avg)
            _store_heads(vb_s, rows, v_ref[rows, :], None, None)
            return carry

        lax.fori_loop(0, k_ref.shape[0] // chunk, body, 0)
        _store_heads(kcn_s, all_rows, kc_ref[...], kg, avg)
        _store_heads(vcb_s, all_rows, vc_ref[...], None, None)

    _store_heads(qn_s, all_rows, q_ref[...], qg_ref[...] * NA_SCALE, avg)

    def finish(h, scores, v_parts):
        m = jnp.max(scores, axis=-1, keepdims=True)
        p = jnp.exp(scores - m)
        denom = jnp.sum(p, axis=-1, keepdims=True)
        pb = p.astype(BF16)
        acc = None
        col = 0
        for v in v_parts:
            part = jnp.dot(pb[:, col:col + v.shape[0]], v, preferred_element_type=F32)
            acc = part if acc is None else acc + part
            col += v.shape[0]
        o_s[h] = acc / denom

    @pl.when(step < n_lat_steps)
    def _latent_queries():
        variant = jnp.int32(0)
        band0 = jnp.int32(0)
        for s_, (v_, b_) in enumerate(zip(step_variant, band_start)):
            variant = jnp.where(step == s_, v_, variant)
            band0 = jnp.where(step == s_, b_ * GRID_W, band0)
        krows = pl.ds(pl.multiple_of(band0, NA_Q_BLOCK), band_keys)

        def head_body(h, carry):
            qh = qn_s[h]
            s_w = lax.dot_general(qh, kn_s[h, krows, :], nt, preferred_element_type=F32)
            s_w = s_w + bias_ref[h, variant]
            s_c = lax.dot_general(qh, kcn_s[h], nt, preferred_element_type=F32)
            finish(h, jnp.concatenate([s_w, s_c], axis=-1), (vb_s[h, krows, :], vcb_s[h]))
            return carry

        lax.fori_loop(0, NA_HEADS, head_body, 0, unroll=2)

    @pl.when(step >= n_lat_steps)
    def _context_queries():
        def head_body(h, carry):
            s = lax.dot_general(qn_s[h], kcn_s[h], nt, preferred_element_type=F32)
            finish(h, s, (vcb_s[h],))
            return carry

        lax.fori_loop(0, NA_HEADS, head_body, 0)

    o_ref[...] = jnp.concatenate([o_s[h] for h in range(NA_HEADS)], axis=-1)


def _attention(p_all, rpb, q_gain, k_gain, n_batch, seq, ctx_len, with_ctx_queries):
    n = p_all.shape[0]
    assert ctx_len == NA_Q_BLOCK and seq % NA_Q_BLOCK == 0
    n_lat_steps = seq // NA_Q_BLOCK
    n_steps = n_lat_steps + (1 if with_ctx_queries else 0)
    ctx_block0 = n_batch * seq // ctx_len
    qcol, kcol, vcol = Q_START // D_NA, KV_START // D_NA, (KV_START + D_NA) // D_NA
    variants, step_variant, band_start = _na_geometry(seq // GRID_W)
    bias_tab = _na_bias_tables(rpb, variants)

    def q_index(col):
        return lambda b, s: (jnp.where(s < n_lat_steps, b * n_lat_steps + s, ctx_block0 + b), col)

    kern = functools.partial(_na_kernel, n_lat_steps=n_lat_steps, step_variant=tuple(step_variant),
                             band_start=tuple(band_start))
    head_major = lambda rows: pltpu.VMEM((NA_HEADS, rows, NA_HEAD_DIM), BF16)
    return pl.pallas_call(
        kern,
        grid=(n_batch, n_steps),
        in_specs=[
            pl.BlockSpec((NA_Q_BLOCK, D_NA), q_index(qcol)),
            pl.BlockSpec((seq, D_NA), lambda b, s: (b, kcol)),
            pl.BlockSpec((seq, D_NA), lambda b, s: (b, vcol)),
            pl.BlockSpec((ctx_len, D_NA), lambda b, s: (ctx_block0 + b, kcol)),
            pl.BlockSpec((ctx_len, D_NA), lambda b, s: (ctx_block0 + b, vcol)),
            pl.BlockSpec(bias_tab.shape, lambda b, s: (0, 0, 0, 0), pipeline_mode=pl.Buffered(1)),
            pl.BlockSpec((1, D_NA), lambda b, s: (0, 0)),
            pl.BlockSpec((1, D_NA), lambda b, s: (0, 0)),
            pl.BlockSpec((D_NA, D_NA), lambda b, s: (0, 0)),
        ],
        out_specs=pl.BlockSpec((NA_Q_BLOCK, D_NA), q_index(0)),
        out_shape=jax.ShapeDtypeStruct((n, D_NA), F32),
        scratch_shapes=[
            head_major(seq), head_major(seq), head_major(ctx_len), head_major(ctx_len),
            head_major(NA_Q_BLOCK),
            pltpu.VMEM((NA_HEADS, NA_Q_BLOCK, NA_HEAD_DIM), F32),
        ],
        compiler_params=_cparams("arbitrary", "arbitrary"),
        name="attention",
    )(p_all, p_all, p_all, p_all, p_all, bias_tab,
      jnp.tile(q_gain.reshape(1, NA_HEAD_DIM), (1, NA_HEADS)),
      jnp.tile(k_gain.reshape(1, NA_HEAD_DIM), (1, NA_HEADS)),
      jnp.asarray(_group_avg_matrix(NA_HEADS, NA_HEAD_DIM), BF16))


def _dft_matrices(length):
    idx = jnp.arange(length, dtype=jnp.int32)
    step = 64
    t_hi = jnp.arange(length // step, dtype=jnp.int32) * step
    t_lo = jnp.arange(step, dtype=jnp.int32)
    ang_hi = ((idx[:, None] * t_hi[None, :]) % (2 * length)).astype(F32) * (math.pi / length)
    ang_lo = ((idx[:, None] * t_lo[None, :]) % (2 * length)).astype(F32) * (math.pi / length)
    c_hi, s_hi = jnp.cos(ang_hi)[:, :, None], jnp.sin(ang_hi)[:, :, None]
    c_lo, s_lo = jnp.cos(ang_lo)[:, None, :], jnp.sin(ang_lo)[:, None, :]
    gc = (c_hi * c_lo - s_hi * s_lo).reshape(length, length)
    gs = (s_hi * c_lo + c_hi * s_lo).reshape(length, length)
    nyq = jnp.where(idx % 2 == 0, 1.0, -1.0).astype(F32)
    gs = jnp.where(idx[:, None] == 0, nyq[None, :], gs)
    return gc.astype(BF16), gs.astype(BF16), gs.T.astype(BF16)


def _hyena_positions(length):
    t = jnp.linspace(0.0, 1.0, length, dtype=F32)[:, None]
    w = 2.0 * math.pi * jnp.arange(length, dtype=F32)[:, None] / length
    f = jnp.linspace(1e-4, HYENA_POS_BANDS - 1, HYENA_POS_BANDS, dtype=F32)[None, :]
    z = jnp.concatenate([t, jnp.cos(f * w), -jnp.sin(f * w)], axis=-1)
    z = jnp.pad(z, ((0, 0), (0, LANES - HYENA_EMB)))
    min_decay = math.log(HYENA_DECAY_TARGET) / HYENA_SLOW_DECAY
    max_decay = math.log(HYENA_DECAY_TARGET) / HYENA_FAST_DECAY
    deltas = jnp.abs(jnp.linspace(min_decay, max_decay, D_HYENA, dtype=F32))[None, :]
    return z, jnp.exp(-t * deltas)


def _filter_kernel(z_ref, decay_ref, w1_ref, b1_ref, w2_ref, b2_ref, w3_ref, freq_ref, o_ref):
    freq = freq_ref[...]
    hdn = jnp.dot(z_ref[...], w1_ref[...], preferred_element_type=F32, precision=HI) + b1_ref[...]
    hdn = jnp.sin(freq[0:1] * hdn)
    hdn = jnp.dot(hdn, w2_ref[...], preferred_element_type=F32, precision=HI) + b2_ref[...]
    hdn = jnp.sin(freq[1:2] * hdn)
    h = jnp.dot(hdn, w3_ref[...], preferred_element_type=F32, precision=HI)
    decay = decay_ref[...]
    first_row = lax.broadcasted_iota(jnp.int32, decay.shape, 0) == 0
    outs = []
    for n in range(HYENA_ORDER):
        base = 2 * n * D_HYENA
        hf = h[:, base:base + D_HYENA] * decay
        hb = h[:, base + D_HYENA:base + 2 * D_HYENA] * decay
        norm = jnp.sum(jnp.abs(hf), axis=0, keepdims=True) + jnp.sum(jnp.abs(hb), axis=0, keepdims=True)
        outs.append(hf / norm)
        outs.append(jnp.where(first_row, 0.0, hb / norm))
    o_ref[...] = jnp.concatenate(outs, axis=-1)


def _hyena_filters(z, decay, w1p, b1, w2, b2, w3, freq):
    length = z.shape[0]
    full = lambda a: pl.BlockSpec(a.shape, lambda i: (0,) * a.ndim)
    args = (z, decay, w1p, b1.reshape(1, -1), w2, b2.reshape(1, -1), w3, freq)
    return pl.pallas_call(
        _filter_kernel,
        grid=(1,),
        in_specs=[full(a) for a in args],
        out_specs=pl.BlockSpec((length, 2 * HYENA_ORDER * D_HYENA), lambda i: (0, 0)),
        out_shape=jax.ShapeDtypeStruct((length, 2 * HYENA_ORDER * D_HYENA), F32),
        compiler_params=_cparams("arbitrary"),
        name="hyena_filter",
    )(*args)


def _filter_dft_kernel(gc_ref, gs_ref, h_ref, o_ref, *, length):
    hb = h_ref[...].astype(BF16)
    fa = jnp.dot(gc_ref[...], hb, preferred_element_type=F32)
    fb = jnp.dot(gs_ref[...], hb, preferred_element_type=F32)
    rows = lax.broadcasted_iota(jnp.int32, (fa.shape[0], D_HYENA), 0) + pl.program_id(0) * fa.shape[0]
    dc_row = rows == 0
    inv_n = 1.0 / (2 * length)
    outs = []
    for n in range(HYENA_ORDER):
        base = 2 * n * D_HYENA
        f_sl = slice(base, base + D_HYENA)
        b_sl = slice(base + D_HYENA, base + 2 * D_HYENA)
        kr = fa[:, f_sl] + fa[:, b_sl]
        ki = fb[:, b_sl] - fb[:, f_sl]
        k_nyq = fb[:, f_sl] + fb[:, b_sl]
        outs.append(jnp.where(dc_row, kr * inv_n, 2.0 * inv_n * kr))
        outs.append(jnp.where(dc_row, 0.0, 2.0 * inv_n * ki))
        outs.append(jnp.where(dc_row, 0.0, -2.0 * inv_n * ki))
        outs.append(jnp.where(dc_row, k_nyq * inv_n, 2.0 * inv_n * kr))
    o_ref[...] = jnp.concatenate(outs, axis=-1)


def _filter_dft(gc, gs, hfilt):
    length = gc.shape[0]
    tile = min(DFT_TILE, length)
    width = 4 * HYENA_ORDER * D_HYENA
    return pl.pallas_call(
        functools.partial(_filter_dft_kernel, length=length),
        grid=(length // tile,),
        in_specs=[
            pl.BlockSpec((tile, length), lambda j: (j, 0)),
            pl.BlockSpec((tile, length), lambda j: (j, 0)),
            pl.BlockSpec(hfilt.shape, lambda j: (0, 0)),
        ],
        out_specs=pl.BlockSpec((tile, width), lambda j: (j, 0)),
        out_shape=jax.ShapeDtypeStruct((length, width), F32),
        compiler_params=_cparams("arbitrary"),
        name="hyena_filter_dft",
    )(gc, gs, hfilt)


def _short_conv_kernel(a0_ref, a1_ref, a2_ref, w_ref, b_ref, o_ref):
    w = w_ref[...]
    b = b_ref[...]
    length = a0_ref.shape[0]
    rows = lax.broadcasted_iota(jnp.int32, (length, D_HYENA), 0)
    for j, a_ref in enumerate((a0_ref, a1_ref, a2_ref)):
        cols = slice(j * D_HYENA, (j + 1) * D_HYENA)
        a = a_ref[...]
        prev = jnp.where(rows == 0, 0.0, pltpu.roll(a, 1, 0))
        nxt = jnp.where(rows == length - 1, 0.0, pltpu.roll(a, length - 1, 0))
        o_ref[:, cols] = prev * w[0:1, cols] + a * w[1:2, cols] + nxt * w[2:3, cols] + b[:, cols]


def _short_conv(p_all, short_w, short_b, n_batch, length, row_block0):
    c0 = HY_START // D_HYENA
    spec = lambda j: pl.BlockSpec((length, D_HYENA), lambda b: (row_block0 + b, c0 + j))
    return pl.pallas_call(
        _short_conv_kernel,
        grid=(n_batch,),
        in_specs=[spec(0), spec(1), spec(2),
                  pl.BlockSpec((3, 3 * D_HYENA), lambda b: (0, 0)),
                  pl.BlockSpec((1, 3 * D_HYENA), lambda b: (0, 0))],
        out_specs=pl.BlockSpec((length, 3 * D_HYENA), lambda b: (b, 0)),
        out_shape=jax.ShapeDtypeStruct((n_batch * length, 3 * D_HYENA), F32),
        compiler_params=_cparams("arbitrary"),
        name="hyena_short_conv",
    )(p_all, p_all, p_all, short_w, short_b.reshape(1, -1))


def _conv_fwd_kernel(gc_ref, gs_ref, u_ref, k_ref, pa_ref, pb_ref):
    u = u_ref[...].astype(BF16)
    a = jnp.dot(gc_ref[...], u, preferred_element_type=F32)
    b = jnp.dot(gs_ref[...], u, preferred_element_type=F32)
    k = k_ref[...]
    c = D_HYENA
    pa_ref[...] = (a * k[:, 0:c] + b * k[:, c:2 * c]).astype(BF16)
    pb_ref[...] = (a * k[:, 2 * c:3 * c] + b * k[:, 3 * c:4 * c]).astype(BF16)


def _conv_fwd(gc, gs, u, u_col, kpack, order, n_batch):
    length = gc.shape[0]
    tile = min(DFT_TILE, length)
    nt = length // tile
    out = jax.ShapeDtypeStruct((n_batch * length, D_HYENA), BF16)
    return pl.pallas_call(
        _conv_fwd_kernel,
        grid=(nt, n_batch),
        in_specs=[
            pl.BlockSpec((tile, length), lambda j, b: (j, 0)),
            pl.BlockSpec((tile, length), lambda j, b: (j, 0)),
            pl.BlockSpec((length, D_HYENA), lambda j, b: (b, u_col)),
            pl.BlockSpec((tile, 4 * D_HYENA), lambda j, b: (j, order)),
        ],
        out_specs=[pl.BlockSpec((tile, D_HYENA), lambda j, b: (b * nt + j, 0))] * 2,
        out_shape=[out, out],
        compiler_params=_cparams("arbitrary", "arbitrary"),
        name="hyena_conv_fwd",
    )(gc, gs, u, kpack)


def _conv_inv_kernel(gc_ref, gst_ref, pa_ref, pb_ref, z_ref, gate_ref, d_ref, o_ref):
    y = jnp.dot(gc_ref[...], pa_ref[...], preferred_element_type=F32)
    y = y + jnp.dot(gst_ref[...], pb_ref[...], preferred_element_type=F32)
    o_ref[...] = gate_ref[...] * (y + d_ref[...] * z_ref[...])


def _conv_inv(gc, gst, pa, pb, z_prev, z_col, a3, gate_col, d_bias, n_batch):
    length = gc.shape[0]
    tile = min(DFT_TILE, length)
    nt = length // tile
    return pl.pallas_call(
        _conv_inv_kernel,
        grid=(nt, n_batch),
        in_specs=[
            pl.BlockSpec((tile, length), lambda j, b: (j, 0)),
            pl.BlockSpec((tile, length), lambda j, b: (j, 0)),
            pl.BlockSpec((length, D_HYENA), lambda j, b: (b, 0)),
            pl.BlockSpec((length, D_HYENA), lambda j, b: (b, 0)),
            pl.BlockSpec((tile, D_HYENA), lambda j, b: (b * nt + j, z_col)),
            pl.BlockSpec((tile, D_HYENA), lambda j, b: (b * nt + j, gate_col)),
            pl.BlockSpec((1, D_HYENA), lambda j, b: (0, 0)),
        ],
        out_specs=pl.BlockSpec((tile, D_HYENA), lambda j, b: (b * nt + j, 0)),
        out_shape=jax.ShapeDtypeStruct((n_batch * length, D_HYENA), F32),
        compiler_params=_cparams("arbitrary", "arbitrary"),
        name="hyena_conv_inv",
    )(gc, gst, pa, pb, z_prev, a3, d_bias.reshape(1, D_HYENA))


def _hyena(p_all, short_w, short_b, d_bias, kpack, mats, n_batch, length, row_block0):
    gc, gs, gst = mats
    a3 = _short_conv(p_all, short_w, short_b, n_batch, length, row_block0)
    pa, pb = _conv_fwd(gc, gs, a3, 0, kpack, 0, n_batch)
    z1 = _conv_inv(gc, gst, pa, pb, a3, 0, a3, 1, d_bias[0], n_batch)
    pa, pb = _conv_fwd(gc, gs, z1, 0, kpack, 1, n_batch)
    return _conv_inv(gc, gst, pa, pb, z1, 0, a3, 2, d_bias[1], n_batch)


def _out_proj_kernel(x_ref, gm_ref, na_ref, hyl_ref, hyc_ref, w_ref, mod_ref, o_ref, *, n_lat_tiles):
    is_lat = pl.program_id(0) < n_lat_tiles
    hy = jnp.where(is_lat, hyl_ref[...], hyc_ref[...])
    y = jnp.dot(gm_ref[...].astype(BF16), w_ref[0:D_GMLP, :], preferred_element_type=F32)
    y = y + jnp.dot(na_ref[...].astype(BF16), w_ref[D_GMLP:D_GMLP + D_NA, :], preferred_element_type=F32)
    y = y + jnp.dot(hy.astype(BF16), w_ref[D_GMLP + D_NA:, :], preferred_element_type=F32)
    o_ref[...] = x_ref[...] + mod_ref[2:3, :] * y


def _out_proj(x_all, gm, na, hy_lat, hy_ctx, w_bf, mods_l, n_batch, seq, n_tiles):
    d = x_all.shape[1]
    n_lat_tiles = n_batch * seq // TOKEN_TILE
    n_ctx_tiles = hy_ctx.shape[0] // TOKEN_TILE
    row = lambda w: pl.BlockSpec((TOKEN_TILE, w), lambda i: (i, 0))
    return pl.pallas_call(
        functools.partial(_out_proj_kernel, n_lat_tiles=n_lat_tiles),
        grid=(n_tiles,),
        in_specs=[
            row(d), row(D_GMLP), row(D_NA),
            pl.BlockSpec((TOKEN_TILE, D_HYENA), lambda i: (jnp.minimum(i, n_lat_tiles - 1), 0)),
            pl.BlockSpec((TOKEN_TILE, D_HYENA),
                         lambda i: (jnp.clip(i - n_lat_tiles, 0, n_ctx_tiles - 1), 0)),
            pl.BlockSpec(w_bf.shape, lambda i: (0, 0)),
            pl.BlockSpec((None, N_MOD, d), _mod_index(seq // TOKEN_TILE, n_batch)),
        ],
        out_specs=row(d),
        out_shape=jax.ShapeDtypeStruct((n_tiles * TOKEN_TILE, d), F32),
        compiler_params=_cparams("arbitrary"),
        name="out_proj",
    )(x_all, gm, na, hy_lat, hy_ctx, w_bf, mods_l)


def _router_kernel(x_ref, g_ref, mod_ref, wr_ref, br_ref, tril_ref, h_ref, route_ref, count_ref, run_ref):
    @pl.when(pl.program_id(0) == 0)
    def _init():
        run_ref[...] = jnp.zeros_like(run_ref)

    mod = mod_ref[...]
    h = _rms_rows(x_ref[...]) * g_ref[...]
    h = h * (1.0 + mod[4:5]) + mod[3:4]
    h_ref[...] = h
    logits = jnp.dot(h, wr_ref[...], preferred_element_type=F32, precision=HI) + br_ref[...]
    lane = lax.broadcasted_iota(jnp.int32, logits.shape, 1)
    neg = -jnp.inf
    is_group = (lane >= MOE_EXPERTS) & (lane < MOE_EXPERTS + MOE_GROUPS)
    lg = jnp.where(is_group, logits, neg)
    mg = jnp.max(lg, axis=-1, keepdims=True)
    g_p = 1.0 / jnp.sum(jnp.exp(lg - mg), axis=-1, keepdims=True)
    g_idx = jnp.min(jnp.where(lg == mg, lane, 2 * LANES), axis=-1, keepdims=True) - MOE_EXPERTS
    in_group = (lane >= g_idx * MOE_EXPERTS_PER_GROUP) & (lane < (g_idx + 1) * MOE_EXPERTS_PER_GROUP)
    le = jnp.where(in_group, logits, neg)
    me = jnp.max(le, axis=-1, keepdims=True)
    pe = jnp.exp(le - me)
    pe = pe / jnp.sum(pe, axis=-1, keepdims=True)
    p1 = jnp.max(pe, axis=-1, keepdims=True)
    i1 = jnp.min(jnp.where(in_group & (pe == p1), lane, 2 * LANES), axis=-1, keepdims=True)
    pe2 = jnp.where(in_group & (lane != i1), pe, neg)
    p2 = jnp.max(pe2, axis=-1, keepdims=True)
    i2 = jnp.min(jnp.where(pe2 == p2, lane, 2 * LANES), axis=-1, keepdims=True)
    tot = p1 + p2
    w_lo = g_p * jnp.where(i1 < i2, p1, p2) / tot
    w_hi = g_p * jnp.where(i1 < i2, p2, p1) / tot
    a = jnp.minimum(i1, i2) - g_idx * MOE_EXPERTS_PER_GROUP
    b = jnp.maximum(i1, i2) - g_idx * MOE_EXPERTS_PER_GROUP
    pair = a * (MOE_EXPERTS_PER_GROUP - 1) - ((a * (a - 1)) >> 1) + (b - a - 1)
    bucket = g_idx * MOE_PAIRS + pair
    onehot = lane == bucket
    prefix = jnp.dot(tril_ref[...], onehot.astype(BF16), preferred_element_type=F32)
    run = run_ref[...]
    rank = jnp.sum(jnp.where(onehot, prefix + run, 0.0), axis=-1, keepdims=True) - 1.0
    run = run + prefix[TOKEN_TILE - 1:TOKEN_TILE, :]
    run_ref[...] = run
    count_ref[...] = run
    route_ref[...] = jnp.where(lane == 0, bucket.astype(F32),
                               jnp.where(lane == 1, rank,
                                         jnp.where(lane == 2, w_lo, jnp.where(lane == 3, w_hi, 0.0))))


def _router(x_all, gain, mods_l, w_router, b_router, n_batch, seq, n_tiles):
    d = x_all.shape[1]
    tril = np.tril(np.ones((TOKEN_TILE, TOKEN_TILE), np.float32))
    return pl.pallas_call(
        _router_kernel,
        grid=(n_tiles,),
        in_specs=[
            pl.BlockSpec((TOKEN_TILE, d), lambda i: (i, 0)),
            pl.BlockSpec((1, d), lambda i: (0, 0)),
            pl.BlockSpec((None, N_MOD, d), _mod_index(seq // TOKEN_TILE, n_batch)),
            pl.BlockSpec((d, LANES), lambda i: (0, 0)),
            pl.BlockSpec((1, LANES), lambda i: (0, 0)),
            pl.BlockSpec((TOKEN_TILE, TOKEN_TILE), lambda i: (0, 0)),
        ],
        out_specs=[pl.BlockSpec((TOKEN_TILE, d), lambda i: (i, 0)),
                   pl.BlockSpec((TOKEN_TILE, LANES), lambda i: (i, 0)),
                   pl.BlockSpec((1, LANES), lambda i: (0, 0))],
        out_shape=[jax.ShapeDtypeStruct((n_tiles * TOKEN_TILE, d), F32),
                   jax.ShapeDtypeStruct((n_tiles * TOKEN_TILE, LANES), F32),
                   jax.ShapeDtypeStruct((1, LANES), F32)],
        scratch_shapes=[pltpu.VMEM((1, LANES), F32)],
        compiler_params=_cparams("arbitrary"),
        name="moe_router",
    )(x_all, gain.reshape(1, d), mods_l, w_router, b_router, jnp.asarray(tril, BF16))


def _gather_rows(table, idx):
    n_rows, width = idx.shape[0], table.shape[1]
    info = plsc.get_sparse_core_info()
    n_workers = info.num_cores * info.num_subcores
    assert n_workers == SC_WORKERS
    per_worker = n_rows // n_workers
    assert n_rows % (n_workers * SC_GATHER_ROWS) == 0
    mesh = plsc.VectorSubcoreMesh(core_axis_name="c", subcore_axis_name="s")

    @functools.partial(
        pl.kernel, mesh=mesh,
        out_type=jax.ShapeDtypeStruct((n_rows, width), table.dtype),
        scratch_types=[
            pltpu.VMEM((SC_GATHER_ROWS,), jnp.int32),
            pltpu.VMEM((SC_GATHER_ROWS, width), table.dtype),
            pltpu.SemaphoreType.DMA,
        ],
    )
    def gather(table_hbm, idx_hbm, out_hbm, idx_v, rows_v, sem):
        worker = lax.axis_index("s") * info.num_cores + lax.axis_index("c")
        base = worker * per_worker

        @pl.loop(0, per_worker // SC_GATHER_ROWS)
        def _(j):
            off = base + j * SC_GATHER_ROWS
            pltpu.sync_copy(idx_hbm.at[pl.ds(off, SC_GATHER_ROWS)], idx_v)
            pltpu.async_copy(table_hbm.at[idx_v], rows_v, sem).wait()
            pltpu.sync_copy(rows_v, out_hbm.at[pl.ds(off, SC_GATHER_ROWS)])

    return gather(table, idx)


def _bucket_experts():
    lo, hi = [], []
    for g in range(MOE_GROUPS):
        for a in range(MOE_EXPERTS_PER_GROUP):
            for b in range(a + 1, MOE_EXPERTS_PER_GROUP):
                lo.append(g * MOE_EXPERTS_PER_GROUP + a)
                hi.append(g * MOE_EXPERTS_PER_GROUP + b)
    return np.asarray(lo, np.int32), np.asarray(hi, np.int32)


def _sorted_experts_kernel(lo_ref, hi_ref, act_ref, x_ref, r_ref, wgl_ref, wul_ref, wdl_ref,
                           wgh_ref, wuh_ref, wdh_ref, o_ref):
    del lo_ref, hi_ref

    @pl.when(act_ref[pl.program_id(0)] != 0)
    def _active_tile():
        xb = x_ref[...].astype(BF16)
        r = r_ref[...]
        acc = None
        for wg_ref, wu_ref, wd_ref, lane in ((wgl_ref, wul_ref, wdl_ref, 2), (wgh_ref, wuh_ref, wdh_ref, 3)):
            gate = jnp.dot(xb, wg_ref[...], preferred_element_type=F32)
            up = jnp.dot(xb, wu_ref[...], preferred_element_type=F32)
            act = _silu(gate) * up * r[:, lane:lane + 1]
            part = jnp.dot(act.astype(BF16), wd_ref[...], preferred_element_type=F32)
            acc = part if acc is None else acc + part
        o_ref[...] = acc


def _sorted_experts(x_sorted, r_sorted, tile_lo, tile_hi, tile_active, wg, wu, wd):
    n_rows, d = x_sorted.shape
    f = wg.shape[-1]
    lo_w = lambda shape: pl.BlockSpec(shape, lambda t, lo, hi, act: (lo[t], 0, 0))
    hi_w = lambda shape: pl.BlockSpec(shape, lambda t, lo, hi, act: (hi[t], 0, 0))
    row = lambda w: pl.BlockSpec((MOE_SORT_TILE, w), lambda t, lo, hi, act: (t, 0))
    return pl.pallas_call(
        _sorted_experts_kernel,
        grid_spec=pltpu.PrefetchScalarGridSpec(
            num_scalar_prefetch=3,
            grid=(n_rows // MOE_SORT_TILE,),
            in_specs=[row(d), row(LANES),
                      lo_w((None, d, f)), lo_w((None, d, f)), lo_w((None, f, d)),
                      hi_w((None, d, f)), hi_w((None, d, f)), hi_w((None, f, d))],
            out_specs=row(d),
        ),
        out_shape=jax.ShapeDtypeStruct((n_rows, d), F32),
        compiler_params=_cparams("arbitrary"),
        name="moe_experts",
    )(tile_lo, tile_hi, tile_active, x_sorted, r_sorted, wg, wu, wd, wg, wu, wd)


def _moe_residual_kernel(x_ref, y_ref, mod_ref, o_ref):
    o_ref[...] = x_ref[...] + mod_ref[5:6, :] * y_ref[...]


def _moe_residual(x_all, y, mods_l, n_batch, seq, n_tiles):
    d = x_all.shape[1]
    row = pl.BlockSpec((TOKEN_TILE, d), lambda i: (i, 0))
    return pl.pallas_call(
        _moe_residual_kernel,
        grid=(n_tiles,),
        in_specs=[row, row, pl.BlockSpec((None, N_MOD, d), _mod_index(seq // TOKEN_TILE, n_batch))],
        out_specs=row,
        out_shape=jax.ShapeDtypeStruct((n_tiles * TOKEN_TILE, d), F32),
        compiler_params=_cparams("arbitrary"),
        name="moe_residual",
    )(x_all, y, mods_l)


def _moe(x_all, gain, mods_l, w_router, b_router, wg, wu, wd, n_batch, seq, n_tiles):
    n = n_tiles * TOKEN_TILE
    h, route, counts = _router(x_all, gain, mods_l, w_router, b_router, n_batch, seq, n_tiles)

    gather_unit = SC_WORKERS * SC_GATHER_ROWS
    n_sorted = -(-(n + MOE_BUCKETS * (MOE_SORT_TILE - 1)) // gather_unit) * gather_unit
    n_sorted_tiles = n_sorted // MOE_SORT_TILE
    counts = counts[0, :MOE_BUCKETS].astype(jnp.int32)
    bucket_tiles = (counts + MOE_SORT_TILE - 1) // MOE_SORT_TILE
    tile_end = jnp.cumsum(bucket_tiles)
    row_start = (tile_end - bucket_tiles) * MOE_SORT_TILE
    bucket = route[:, 0].astype(jnp.int32)
    rank = route[:, 1].astype(jnp.int32)
    dest = jnp.sum(jnp.where(bucket[:, None] == jnp.arange(MOE_BUCKETS)[None, :], row_start[None, :], 0),
                   axis=1) + rank
    src = jnp.zeros((n_sorted,), jnp.int32).at[dest].set(jnp.arange(n, dtype=jnp.int32))
    tiles = jnp.arange(n_sorted_tiles, dtype=jnp.int32)
    tile_active = (tiles < tile_end[-1]).astype(jnp.int32)
    tile_bucket = jnp.sum(jnp.minimum(tiles, tile_end[-1] - 1)[:, None] >= tile_end[None, :], axis=1)
    tile_bucket = jnp.minimum(tile_bucket, MOE_BUCKETS - 1)
    lo_ids, hi_ids = _bucket_experts()
    tile_lo = jnp.asarray(lo_ids)[tile_bucket]
    tile_hi = jnp.asarray(hi_ids)[tile_bucket]

    x_sorted = _gather_rows(h, src)
    r_sorted = _gather_rows(route, src)
    y_sorted = _sorted_experts(x_sorted, r_sorted, tile_lo, tile_hi, tile_active, wg, wu, wd)
    y = _gather_rows(y_sorted, dest)
    return _moe_residual(x_all, y, mods_l, n_batch, seq, n_tiles)


def kernel(x, c, ctx, c_ctx, w_ada, b_ada, g_mix, g_ffn, w_in, w_out, gmlp_v_gain, gmlp_ws, gmlp_bs,
           na_q_gain, na_k_gain, na_rpb, hy_short_w, hy_short_b, hy_w1, hy_b1, hy_w2, hy_b2, hy_w3,
           hy_freq, hy_bias, moe_w_rg, moe_b_rg, moe_w_re, moe_b_re, moe_w_gate, moe_w_up, moe_w_down):
    n_batch, seq, d = x.shape
    ctx_len = ctx.shape[1]
    depth = w_ada.shape[0]
    n_lat = n_batch * seq
    n_ctx = n_batch * ctx_len
    assert d == D_MODEL and seq % TOKEN_TILE == 0 and n_ctx % TOKEN_TILE == 0
    assert seq % GMLP_CHUNK == 0 and ctx_len % GMLP_CHUNK == 0

    pad_rows = -(n_batch + 1) % 8
    cs = jnp.concatenate([c, c_ctx[None, :], jnp.zeros((pad_rows, d), F32)], axis=0)
    mods = _modulation(cs, w_ada, b_ada)[:, :n_batch + 1].reshape(depth, n_batch + 1, N_MOD, d)

    x_all = jnp.concatenate([x.reshape(n_lat, d), ctx.reshape(n_ctx, d)], axis=0)
    n_all_tiles = (n_lat + n_ctx) // TOKEN_TILE
    n_lat_tiles = n_lat // TOKEN_TILE

    mats_lat = _dft_matrices(seq)
    mats_ctx = _dft_matrices(ctx_len)
    pos_lat = _hyena_positions(seq)
    pos_ctx = _hyena_positions(ctx_len)

    w_in_bf = w_in.astype(BF16)
    w_out_bf = w_out.astype(BF16)
    wg_bf = moe_w_gate.astype(BF16).reshape(depth, MOE_EXPERTS, d, MOE_HIDDEN)
    wu_bf = moe_w_up.astype(BF16).reshape(depth, MOE_EXPERTS, d, MOE_HIDDEN)
    wd_bf = moe_w_down.astype(BF16).reshape(depth, MOE_EXPERTS, MOE_HIDDEN, d)
    lane_pad = LANES - MOE_EXPERTS - MOE_GROUPS
    w_router = jnp.pad(jnp.concatenate([moe_w_re, moe_w_rg], axis=-1), ((0, 0), (0, 0), (0, lane_pad)))
    b_router = jnp.pad(jnp.concatenate([moe_b_re, moe_b_rg], axis=-1), ((0, 0), (0, lane_pad)))[:, None, :]
    w1_pad = jnp.pad(hy_w1, ((0, 0), (0, LANES - HYENA_EMB), (0, 0)))

    for l in range(depth):
        last = l == depth - 1
        mods_l = mods[l]
        n_tiles = n_lat_tiles if last else n_all_tiles

        p_all = _in_proj(x_all, g_mix[l], mods_l, w_in_bf[l], n_batch, seq)

        filt = (w1_pad[l], hy_b1[l], hy_w2[l], hy_b2[l], hy_w3[l], hy_freq[l])
        kpack_lat = _filter_dft(mats_lat[0], mats_lat[1], _hyena_filters(*pos_lat, *filt))
        hy_lat = _hyena(p_all, hy_short_w[l], hy_short_b[l], hy_bias[l], kpack_lat, mats_lat,
                        n_batch, seq, 0)
        if last:
            hy_ctx = hy_lat
            gm = _gmlp(p_all, n_lat, gmlp_v_gain[l], gmlp_ws[l], gmlp_bs[l].T)
        else:
            kpack_ctx = _filter_dft(mats_ctx[0], mats_ctx[1], _hyena_filters(*pos_ctx, *filt))
            hy_ctx = _hyena(p_all, hy_short_w[l], hy_short_b[l], hy_bias[l], kpack_ctx, mats_ctx,
                            n_batch, ctx_len, n_lat // ctx_len)
            gm = _gmlp(p_all, n_lat + n_ctx, gmlp_v_gain[l], gmlp_ws[l], gmlp_bs[l].T)

        na = _attention(p_all, na_rpb[l], na_q_gain[l], na_k_gain[l],
                        n_batch, seq, ctx_len, not last)

        x_all = _out_proj(x_all, gm, na, hy_lat, hy_ctx, w_out_bf[l], mods_l, n_batch, seq, n_tiles)

        x_all = _moe(x_all, g_ffn[l], mods_l, w_router[l], b_router[l], wg_bf[l], wu_bf[l], wd_bf[l],
                     n_batch, seq, n_tiles)

    return x_all[:n_lat].reshape(n_batch, seq, d)
```

```python
import functools
import math

import numpy as np
import jax
import jax.numpy as jnp
from jax import lax
from jax.experimental import pallas as pl
from jax.experimental.pallas import tpu as pltpu

F32 = jnp.float32
BF16 = jnp.bfloat16
HI = lax.Precision.HIGHEST

D_MODEL = 1024
GRID_W = 64
D_GMLP = D_MODEL // 4
D_NA = D_MODEL // 2
D_HYENA = D_MODEL // 4
D_IN = 2 * D_GMLP + 3 * D_NA + 3 * D_HYENA
Q_START = 2 * D_GMLP
KV_START = 2 * D_GMLP + D_NA
HY_START = 2 * D_GMLP + 3 * D_NA
GMLP_GROUPS = 4
GMLP_GROUP_DIM = D_GMLP // GMLP_GROUPS
GMLP_CHUNK = 128
NA_HEAD_DIM = 64
NA_HEADS = D_NA // NA_HEAD_DIM
NA_SCALE = NA_HEAD_DIM ** -0.5
NA_WIN_ROWS = 8
NA_WIN_COLS = 16
HYENA_ORDER = 2
HYENA_POS_BANDS = 16
HYENA_EMB = 1 + 2 * HYENA_POS_BANDS
HYENA_FILTER_HIDDEN = 64
HYENA_DECAY_TARGET = 1e-2
HYENA_FAST_DECAY = 0.3
HYENA_SLOW_DECAY = 1.5
MOE_GROUPS = 4
MOE_EXPERTS_PER_GROUP = 8
MOE_EXPERTS = MOE_GROUPS * MOE_EXPERTS_PER_GROUP
MOE_HIDDEN = 256
MOE_PAIRS = MOE_EXPERTS_PER_GROUP * (MOE_EXPERTS_PER_GROUP - 1) // 2
MOE_BUCKETS = MOE_GROUPS * MOE_PAIRS
N_MOD = 6
RMS_EPS = 1e-6
LN_EPS = 1e-5

LANES = 128
TOKEN_TILE = 512
NA_Q_ROWS = 4
NA_Q_BLOCK = NA_Q_ROWS * GRID_W
NA_BAND_ROWS = NA_WIN_ROWS + NA_Q_ROWS
DFT_TILE = 512
MOE_SORT_TILE = 128
GATHER_ROWS = 512
GATHER_UNROLL = 8
MASK_VALUE = -1e30
VMEM_LIMIT = 56 * 1024 * 1024


def _cparams(*sem):
    return pltpu.CompilerParams(dimension_semantics=sem, vmem_limit_bytes=VMEM_LIMIT)


def _silu(x):
    return x * jax.nn.sigmoid(x)


def _rms_rows(x):
    return x * lax.rsqrt(jnp.mean(x * x, axis=-1, keepdims=True) + RMS_EPS)


def _mods_kernel(cs_ref, w_ref, b_ref, o_ref):
    s = _silu(cs_ref[...])
    o_ref[...] = jnp.dot(s, w_ref[...], preferred_element_type=F32, precision=HI) + b_ref[...]


def _modulation(cs, w_ada, b_ada):
    depth, d, nd = w_ada.shape
    rows = cs.shape[0]
    col = 1024
    return pl.pallas_call(
        _mods_kernel,
        grid=(depth, nd // col),
        in_specs=[
            pl.BlockSpec((rows, d), lambda l, j: (0, 0)),
            pl.BlockSpec((None, d, col), lambda l, j: (l, 0, j)),
            pl.BlockSpec((None, 1, col), lambda l, j: (l, 0, j)),
        ],
        out_specs=pl.BlockSpec((None, rows, col), lambda l, j: (l, 0, j)),
        out_shape=jax.ShapeDtypeStruct((depth, rows, nd), F32),
        compiler_params=_cparams("arbitrary", "arbitrary"),
        name="modulation",
    )(cs, w_ada, b_ada.reshape(depth, 1, nd))


def _in_proj_kernel(x_ref, g_ref, mod_ref, w_ref, o_ref):
    mod = mod_ref[...]
    h = _rms_rows(x_ref[...]) * g_ref[...]
    h = h * (1.0 + mod[1:2]) + mod[0:1]
    o_ref[...] = jnp.dot(h.astype(BF16), w_ref[...], preferred_element_type=F32)


def _mod_index(tiles_per_batch, n_batch):
    return lambda i: (jnp.minimum(i // tiles_per_batch, n_batch), 0, 0)


def _in_proj(x_all, gain, mods_l, w_bf, n_batch, seq):
    n, d = x_all.shape
    d_in = w_bf.shape[1]
    return pl.pallas_call(
        _in_proj_kernel,
        grid=(n // TOKEN_TILE,),
        in_specs=[
            pl.BlockSpec((TOKEN_TILE, d), lambda i: (i, 0)),
            pl.BlockSpec((1, d), lambda i: (0, 0)),
            pl.BlockSpec((None, N_MOD, d), _mod_index(seq // TOKEN_TILE, n_batch)),
            pl.BlockSpec((d, d_in), lambda i: (0, 0)),
        ],
        out_specs=pl.BlockSpec((TOKEN_TILE, d_in), lambda i: (i, 0)),
        out_shape=jax.ShapeDtypeStruct((n, d_in), F32),
        compiler_params=_cparams("arbitrary"),
        name="in_proj",
    )(x_all, gain.reshape(1, d), mods_l, w_bf)


def _group_avg_matrix(groups, width):
    return np.kron(np.eye(groups), np.full((width, width), 1.0 / width))


def _group_mean(t, avg):
    hi = t.astype(BF16)
    lo = (t - hi.astype(F32)).astype(BF16)
    return (jnp.dot(hi, avg, preferred_element_type=F32) + jnp.dot(lo, avg, preferred_element_type=F32))


def _gmlp_kernel(u_ref, v_ref, gain_ref, avg_ref, ws_ref, bs_ref, o_ref):
    avg = avg_ref[...]
    v = jax.nn.gelu(v_ref[...])
    v = v - _group_mean(v, avg)
    v = v * lax.rsqrt(_group_mean(v * v, avg) + LN_EPS) * gain_ref[...]
    vb = v.astype(BF16)
    bs = bs_ref[...]
    for c in range(u_ref.shape[0] // GMLP_CHUNK):
        rows = slice(c * GMLP_CHUNK, (c + 1) * GMLP_CHUNK)
        outs = []
        for g in range(GMLP_GROUPS):
            cols = slice(g * GMLP_GROUP_DIM, (g + 1) * GMLP_GROUP_DIM)
            s = jnp.dot(ws_ref[g], vb[rows, cols], preferred_element_type=F32) + bs[:, g:g + 1]
            outs.append(s)
        o_ref[rows, :] = jax.nn.gelu(u_ref[rows, :]) * jnp.concatenate(outs, axis=-1)


def _gmlp(p_all, n, v_gain, ws, bs_t):
    avg = _group_avg_matrix(GMLP_GROUPS, GMLP_GROUP_DIM)
    return pl.pallas_call(
        _gmlp_kernel,
        grid=(n // TOKEN_TILE,),
        in_specs=[
            pl.BlockSpec((TOKEN_TILE, D_GMLP), lambda i: (i, 0)),
            pl.BlockSpec((TOKEN_TILE, D_GMLP), lambda i: (i, 1)),
            pl.BlockSpec((1, D_GMLP), lambda i: (0, 0)),
            pl.BlockSpec((D_GMLP, D_GMLP), lambda i: (0, 0)),
            pl.BlockSpec((GMLP_GROUPS, GMLP_CHUNK, GMLP_CHUNK), lambda i: (0, 0, 0)),
            pl.BlockSpec((GMLP_CHUNK, GMLP_GROUPS), lambda i: (0, 0)),
        ],
        out_specs=pl.BlockSpec((TOKEN_TILE, D_GMLP), lambda i: (i, 0)),
        out_shape=jax.ShapeDtypeStruct((n, D_GMLP), F32),
        compiler_params=_cparams("arbitrary"),
        name="gmlp",
    )(p_all, p_all, v_gain.reshape(1, D_GMLP), jnp.asarray(avg, BF16), ws.astype(BF16), bs_t)


def _rpb_expand_kernel(rpb_ref, sel_ref, o_ref):
    o_ref[...] = jnp.dot(rpb_ref[...], sel_ref[...], preferred_element_type=F32, precision=HI)


def _na_geometry(grid_rows):
    variants, step_variant, band_start = [], [], []
    for r0 in range(0, grid_rows, NA_Q_ROWS):
        b0 = int(np.clip(r0 - NA_WIN_ROWS // 2, 0, grid_rows - NA_BAND_ROWS))
        geo = []
        for r in range(r0, r0 + NA_Q_ROWS):
            wr = int(np.clip(r - NA_WIN_ROWS // 2, 0, grid_rows - NA_WIN_ROWS))
            assert b0 <= wr and wr + NA_WIN_ROWS <= b0 + NA_BAND_ROWS
            geo.append((wr - b0, wr - r + NA_WIN_ROWS - 1))
        geo = tuple(geo)
        if geo not in variants:
            variants.append(geo)
        step_variant.append(variants.index(geo))
        band_start.append(b0)
    return variants, step_variant, band_start


def _na_bias_tables(rpb, variants):
    n_heads, n_dr, n_dc = rpb.shape
    qcol = np.arange(GRID_W)[:, None]
    kcol = np.arange(GRID_W)[None, :]
    win_c = np.clip(qcol - NA_WIN_COLS // 2, 0, GRID_W - NA_WIN_COLS)
    col_ok = (kcol >= win_c) & (kcol < win_c + NA_WIN_COLS)
    dc = np.clip(kcol - qcol + NA_WIN_COLS - 1, 0, 2 * NA_WIN_COLS - 2)
    dc_pad = -n_dc % 8
    sel = (np.arange(n_dc + dc_pad)[:, None] == dc.reshape(1, -1)).astype(np.float32)
    rpb2 = jnp.pad(rpb.reshape(n_heads * n_dr, n_dc), ((0, 0), (0, dc_pad)))
    toep = pl.pallas_call(
        _rpb_expand_kernel,
        out_shape=jax.ShapeDtypeStruct((n_heads * n_dr, GRID_W * GRID_W), F32),
        name="rpb_expand",
    )(rpb2, jnp.asarray(sel))
    toep = toep.reshape(n_heads, n_dr, GRID_W, GRID_W)
    toep = jnp.where(col_ok[None, None], toep, MASK_VALUE)
    tabs = []
    for geo in variants:
        rows = []
        for a0, dr0 in geo:
            rows.append(jnp.pad(toep[:, dr0:dr0 + NA_WIN_ROWS],
                                ((0, 0), (a0, NA_BAND_ROWS - NA_WIN_ROWS - a0), (0, 0), (0, 0)),
                                constant_values=MASK_VALUE))
        tab = jnp.stack(rows, axis=1)
        tabs.append(jnp.transpose(tab, (0, 1, 3, 2, 4)).reshape(n_heads, NA_Q_BLOCK, NA_BAND_ROWS * GRID_W))
    return jnp.stack(tabs, axis=1)


def _store_heads(dst, rows, t, gain, avg):
    if gain is not None:
        t = t * lax.rsqrt(_group_mean(t * t, avg) + RMS_EPS) * gain
    for h in range(NA_HEADS):
        dst[h, rows, :] = t[:, h * NA_HEAD_DIM:(h + 1) * NA_HEAD_DIM].astype(BF16)


def _na_kernel(q_ref, k_ref, v_ref, kc_ref, vc_ref, bias_ref, qg_ref, kg_ref, avg_ref, o_ref,
               kn_s, vb_s, kcn_s, vcb_s, qn_s, o_s, *, n_lat_steps, step_variant, band_start):
    step = pl.program_id(1)
    kg = kg_ref[...]
    avg = avg_ref[...]
    nt = (((1,), (1,)), ((), ()))
    band_keys = NA_BAND_ROWS * GRID_W
    all_rows = slice(None)

    @pl.when(step == 0)
    def _prepare_keys():
        chunk = 256

        def body(c, carry):
            rows = pl.ds(pl.multiple_of(c * chunk, chunk), chunk)
            _store_heads(kn_s, rows, k_ref[rows, :], kg, avg)
            _store_heads(vb_s, rows, v_ref[rows, :], None, None)
            return carry

        lax.fori_loop(0, k_ref.shape[0] // chunk, body, 0)
        _store_heads(kcn_s, all_rows, kc_ref[...], kg, avg)
        _store_heads(vcb_s, all_rows, vc_ref[...], None, None)

    _store_heads(qn_s, all_rows, q_ref[...], qg_ref[...] * NA_SCALE, avg)

    def finish(h, scores, v_parts):
        m = jnp.max(scores, axis=-1, keepdims=True)
        p = jnp.exp(scores - m)
        denom = jnp.sum(p, axis=-1, keepdims=True)
        pb = p.astype(BF16)
        acc = None
        col = 0
        for v in v_parts:
            part = jnp.dot(pb[:, col:col + v.shape[0]], v, preferred_element_type=F32)
            acc = part if acc is None else acc + part
            col += v.shape[0]
        o_s[h] = acc / denom

    @pl.when(step < n_lat_steps)
    def _latent_queries():
        variant = jnp.int32(0)
        band0 = jnp.int32(0)
        for s_, (v_, b_) in enumerate(zip(step_variant, band_start)):
            variant = jnp.where(step == s_, v_, variant)
            band0 = jnp.where(step == s_, b_ * GRID_W, band0)
        krows = pl.ds(pl.multiple_of(band0, NA_Q_BLOCK), band_keys)

        def head_body(h, carry):
            qh = qn_s[h]
            s_w = lax.dot_general(qh, kn_s[h, krows, :], nt, preferred_element_type=F32)
            s_w = s_w + bias_ref[h, variant]
            s_c = lax.dot_general(qh, kcn_s[h], nt, preferred_element_type=F32)
            finish(h, jnp.concatenate([s_w, s_c], axis=-1), (vb_s[h, krows, :], vcb_s[h]))
            return carry

        lax.fori_loop(0, NA_HEADS, head_body, 0, unroll=2)

    @pl.when(step >= n_lat_steps)
    def _context_queries():
        def head_body(h, carry):
            s = lax.dot_general(qn_s[h], kcn_s[h], nt, preferred_element_type=F32)
            finish(h, s, (vcb_s[h],))
            return carry

        lax.fori_loop(0, NA_HEADS, head_body, 0)

    o_ref[...] = jnp.concatenate([o_s[h] for h in range(NA_HEADS)], axis=-1)


def _attention(p_all, rpb, q_gain, k_gain, n_batch, seq, ctx_len, with_ctx_queries):
    n = p_all.shape[0]
    assert ctx_len == NA_Q_BLOCK and seq % NA_Q_BLOCK == 0
    n_lat_steps = seq // NA_Q_BLOCK
    n_steps = n_lat_steps + (1 if with_ctx_queries else 0)
    ctx_block0 = n_batch * seq // ctx_len
    qcol, kcol, vcol = Q_START // D_NA, KV_START // D_NA, (KV_START + D_NA) // D_NA
    variants, step_variant, band_start = _na_geometry(seq // GRID_W)
    bias_tab = _na_bias_tables(rpb, variants)

    def q_index(col):
        return lambda b, s: (jnp.where(s < n_lat_steps, b * n_lat_steps + s, ctx_block0 + b), col)

    kern = functools.partial(_na_kernel, n_lat_steps=n_lat_steps, step_variant=tuple(step_variant),
                             band_start=tuple(band_start))
    head_major = lambda rows: pltpu.VMEM((NA_HEADS, rows, NA_HEAD_DIM), BF16)
    return pl.pallas_call(
        kern,
        grid=(n_batch, n_steps),
        in_specs=[
            pl.BlockSpec((NA_Q_BLOCK, D_NA), q_index(qcol)),
            pl.BlockSpec((seq, D_NA), lambda b, s: (b, kcol)),
            pl.BlockSpec((seq, D_NA), lambda b, s: (b, vcol)),
            pl.BlockSpec((ctx_len, D_NA), lambda b, s: (ctx_block0 + b, kcol)),
            pl.BlockSpec((ctx_len, D_NA), lambda b, s: (ctx_block0 + b, vcol)),
            pl.BlockSpec(bias_tab.shape, lambda b, s: (0, 0, 0, 0), pipeline_mode=pl.Buffered(1)),
            pl.BlockSpec((1, D_NA), lambda b, s: (0, 0)),
            pl.BlockSpec((1, D_NA), lambda b, s: (0, 0)),
            pl.BlockSpec((D_NA, D_NA), lambda b, s: (0, 0)),
        ],
        out_specs=pl.BlockSpec((NA_Q_BLOCK, D_NA), q_index(0)),
        out_shape=jax.ShapeDtypeStruct((n if with_ctx_queries else n_batch * seq, D_NA), F32),
        scratch_shapes=[
            head_major(seq), head_major(seq), head_major(ctx_len), head_major(ctx_len),
            head_major(NA_Q_BLOCK),
            pltpu.VMEM((NA_HEADS, NA_Q_BLOCK, NA_HEAD_DIM), F32),
        ],
        compiler_params=_cparams("arbitrary", "arbitrary"),
        name="attention",
    )(p_all, p_all, p_all, p_all, p_all, bias_tab,
      jnp.tile(q_gain.reshape(1, NA_HEAD_DIM), (1, NA_HEADS)),
      jnp.tile(k_gain.reshape(1, NA_HEAD_DIM), (1, NA_HEADS)),
      jnp.asarray(_group_avg_matrix(NA_HEADS, NA_HEAD_DIM), BF16))


def _dft_matrices(length):
    idx = jnp.arange(length, dtype=jnp.int32)
    step = 64
    t_hi = jnp.arange(length // step, dtype=jnp.int32) * step
    t_lo = jnp.arange(step, dtype=jnp.int32)
    ang_hi = ((idx[:, None] * t_hi[None, :]) % (2 * length)).astype(F32) * (math.pi / length)
    ang_lo = ((idx[:, None] * t_lo[None, :]) % (2 * length)).astype(F32) * (math.pi / length)
    c_hi, s_hi = jnp.cos(ang_hi)[:, :, None], jnp.sin(ang_hi)[:, :, None]
    c_lo, s_lo = jnp.cos(ang_lo)[:, None, :], jnp.sin(ang_lo)[:, None, :]
    gc = (c_hi * c_lo - s_hi * s_lo).reshape(length, length)
    gs = (s_hi * c_lo + c_hi * s_lo).reshape(length, length)
    nyq = jnp.where(idx % 2 == 0, 1.0, -1.0).astype(F32)
    gs = jnp.where(idx[:, None] == 0, nyq[None, :], gs)
    return gc.astype(BF16), gs.astype(BF16), gs.T.astype(BF16)


def _hyena_positions(length):
    t = jnp.linspace(0.0, 1.0, length, dtype=F32)[:, None]
    w = 2.0 * math.pi * jnp.arange(length, dtype=F32)[:, None] / length
    f = jnp.linspace(1e-4, HYENA_POS_BANDS - 1, HYENA_POS_BANDS, dtype=F32)[None, :]
    z = jnp.concatenate([t, jnp.cos(f * w), -jnp.sin(f * w)], axis=-1)
    z = jnp.pad(z, ((0, 0), (0, LANES - HYENA_EMB)))
    min_decay = math.log(HYENA_DECAY_TARGET) / HYENA_SLOW_DECAY
    max_decay = math.log(HYENA_DECAY_TARGET) / HYENA_FAST_DECAY
    deltas = jnp.abs(jnp.linspace(min_decay, max_decay, D_HYENA, dtype=F32))[None, :]
    return z, jnp.exp(-t * deltas)


def _filter_kernel(z_ref, decay_ref, w1_ref, b1_ref, w2_ref, b2_ref, w3_ref, freq_ref, o_ref):
    freq = freq_ref[...]
    hdn = jnp.dot(z_ref[...], w1_ref[...], preferred_element_type=F32, precision=HI) + b1_ref[...]
    hdn = jnp.sin(freq[0:1] * hdn)
    hdn = jnp.dot(hdn, w2_ref[...], preferred_element_type=F32, precision=HI) + b2_ref[...]
    hdn = jnp.sin(freq[1:2] * hdn)
    h = jnp.dot(hdn, w3_ref[...], preferred_element_type=F32, precision=HI)
    decay = decay_ref[...]
    first_row = lax.broadcasted_iota(jnp.int32, decay.shape, 0) == 0
    outs = []
    for n in range(HYENA_ORDER):
        base = 2 * n * D_HYENA
        hf = h[:, base:base + D_HYENA] * decay
        hb = h[:, base + D_HYENA:base + 2 * D_HYENA] * decay
        norm = jnp.sum(jnp.abs(hf), axis=0, keepdims=True) + jnp.sum(jnp.abs(hb), axis=0, keepdims=True)
        outs.append(hf / norm)
        outs.append(jnp.where(first_row, 0.0, hb / norm))
    o_ref[...] = jnp.concatenate(outs, axis=-1)


def _hyena_filters(z, decay, w1p, b1, w2, b2, w3, freq):
    length = z.shape[0]
    full = lambda a: pl.BlockSpec(a.shape, lambda i: (0,) * a.ndim)
    args = (z, decay, w1p, b1.reshape(1, -1), w2, b2.reshape(1, -1), w3, freq)
    return pl.pallas_call(
        _filter_kernel,
        grid=(1,),
        in_specs=[full(a) for a in args],
        out_specs=pl.BlockSpec((length, 2 * HYENA_ORDER * D_HYENA), lambda i: (0, 0)),
        out_shape=jax.ShapeDtypeStruct((length, 2 * HYENA_ORDER * D_HYENA), F32),
        compiler_params=_cparams("arbitrary"),
        name="hyena_filter",
    )(*args)


def _filter_dft_kernel(gc_ref, gs_ref, h_ref, o_ref, *, length):
    hb = h_ref[...].astype(BF16)
    fa = jnp.dot(gc_ref[...], hb, preferred_element_type=F32)
    fb = jnp.dot(gs_ref[...], hb, preferred_element_type=F32)
    rows = lax.broadcasted_iota(jnp.int32, (fa.shape[0], D_HYENA), 0) + pl.program_id(0) * fa.shape[0]
    dc_row = rows == 0
    inv_n = 1.0 / (2 * length)
    outs = []
    for n in range(HYENA_ORDER):
        base = 2 * n * D_HYENA
        f_sl = slice(base, base + D_HYENA)
        b_sl = slice(base + D_HYENA, base + 2 * D_HYENA)
        kr = fa[:, f_sl] + fa[:, b_sl]
        ki = fb[:, b_sl] - fb[:, f_sl]
        k_nyq = fb[:, f_sl] + fb[:, b_sl]
        outs.append(jnp.where(dc_row, kr * inv_n, 2.0 * inv_n * kr))
        outs.append(jnp.where(dc_row, 0.0, 2.0 * inv_n * ki))
        outs.append(jnp.where(dc_row, 0.0, -2.0 * inv_n * ki))
        outs.append(jnp.where(dc_row, k_nyq * inv_n, 2.0 * inv_n * kr))
    o_ref[...] = jnp.concatenate(outs, axis=-1)


def _filter_dft(gc, gs, hfilt):
    length = gc.shape[0]
    tile = min(DFT_TILE, length)
    width = 4 * HYENA_ORDER * D_HYENA
    return pl.pallas_call(
        functools.partial(_filter_dft_kernel, length=length),
        grid=(length // tile,),
        in_specs=[
            pl.BlockSpec((tile, length), lambda j: (j, 0)),
            pl.BlockSpec((tile, length), lambda j: (j, 0)),
            pl.BlockSpec(hfilt.shape, lambda j: (0, 0)),
        ],
        out_specs=pl.BlockSpec((tile, width), lambda j: (j, 0)),
        out_shape=jax.ShapeDtypeStruct((length, width), F32),
        compiler_params=_cparams("arbitrary"),
        name="hyena_filter_dft",
    )(gc, gs, hfilt)


def _short_conv_kernel(a0_ref, a1_ref, a2_ref, w_ref, b_ref, o_ref):
    w = w_ref[...]
    b = b_ref[...]
    length = a0_ref.shape[0]
    rows = lax.broadcasted_iota(jnp.int32, (length, D_HYENA), 0)
    for j, a_ref in enumerate((a0_ref, a1_ref, a2_ref)):
        cols = slice(j * D_HYENA, (j + 1) * D_HYENA)
        a = a_ref[...]
        prev = jnp.where(rows == 0, 0.0, pltpu.roll(a, 1, 0))
        nxt = jnp.where(rows == length - 1, 0.0, pltpu.roll(a, length - 1, 0))
        o_ref[:, cols] = prev * w[0:1, cols] + a * w[1:2, cols] + nxt * w[2:3, cols] + b[:, cols]


def _short_conv(p_all, short_w, short_b, n_batch, length, row_block0):
    c0 = HY_START // D_HYENA
    spec = lambda j: pl.BlockSpec((length, D_HYENA), lambda b: (row_block0 + b, c0 + j))
    return pl.pallas_call(
        _short_conv_kernel,
        grid=(n_batch,),
        in_specs=[spec(0), spec(1), spec(2),
                  pl.BlockSpec((3, 3 * D_HYENA), lambda b: (0, 0)),
                  pl.BlockSpec((1, 3 * D_HYENA), lambda b: (0, 0))],
        out_specs=pl.BlockSpec((length, 3 * D_HYENA), lambda b: (b, 0)),
        out_shape=jax.ShapeDtypeStruct((n_batch * length, 3 * D_HYENA), F32),
        compiler_params=_cparams("arbitrary"),
        name="hyena_short_conv",
    )(p_all, p_all, p_all, short_w, short_b.reshape(1, -1))


def _conv_fwd_kernel(gc_ref, gs_ref, u_ref, k_ref, pa_ref, pb_ref):
    u = u_ref[...].astype(BF16)
    a = jnp.dot(gc_ref[...], u, preferred_element_type=F32)
    b = jnp.dot(gs_ref[...], u, preferred_element_type=F32)
    k = k_ref[...]
    c = D_HYENA
    pa_ref[...] = (a * k[:, 0:c] + b * k[:, c:2 * c]).astype(BF16)
    pb_ref[...] = (a * k[:, 2 * c:3 * c] + b * k[:, 3 * c:4 * c]).astype(BF16)


def _conv_fwd(gc, gs, u, u_col, kpack, order, n_batch):
    length = gc.shape[0]
    tile = min(DFT_TILE, length)
    nt = length // tile
    out = jax.ShapeDtypeStruct((n_batch * length, D_HYENA), BF16)
    return pl.pallas_call(
        _conv_fwd_kernel,
        grid=(nt, n_batch),
        in_specs=[
            pl.BlockSpec((tile, length), lambda j, b: (j, 0)),
            pl.BlockSpec((tile, length), lambda j, b: (j, 0)),
            pl.BlockSpec((length, D_HYENA), lambda j, b: (b, u_col)),
            pl.BlockSpec((tile, 4 * D_HYENA), lambda j, b: (j, order)),
        ],
        out_specs=[pl.BlockSpec((tile, D_HYENA), lambda j, b: (b * nt + j, 0))] * 2,
        out_shape=[out, out],
        compiler_params=_cparams("arbitrary", "arbitrary"),
        name="hyena_conv_fwd",
    )(gc, gs, u, kpack)


def _conv_inv_kernel(gc_ref, gst_ref, pa_ref, pb_ref, z_ref, gate_ref, d_ref, o_ref):
    y = jnp.dot(gc_ref[...], pa_ref[...], preferred_element_type=F32)
    y = y + jnp.dot(gst_ref[...], pb_ref[...], preferred_element_type=F32)
    o_ref[...] = gate_ref[...] * (y + d_ref[...] * z_ref[...])


def _conv_inv(gc, gst, pa, pb, z_prev, z_col, a3, gate_col, d_bias, n_batch):
    length = gc.shape[0]
    tile = min(DFT_TILE, length)
    nt = length // tile
    return pl.pallas_call(
        _conv_inv_kernel,
        grid=(nt, n_batch),
        in_specs=[
            pl.BlockSpec((tile, length), lambda j, b: (j, 0)),
            pl.BlockSpec((tile, length), lambda j, b: (j, 0)),
            pl.BlockSpec((length, D_HYENA), lambda j, b: (b, 0)),
            pl.BlockSpec((length, D_HYENA), lambda j, b: (b, 0)),
            pl.BlockSpec((tile, D_HYENA), lambda j, b: (b * nt + j, z_col)),
            pl.BlockSpec((tile, D_HYENA), lambda j, b: (b * nt + j, gate_col)),
            pl.BlockSpec((1, D_HYENA), lambda j, b: (0, 0)),
        ],
        out_specs=pl.BlockSpec((tile, D_HYENA), lambda j, b: (b * nt + j, 0)),
        out_shape=jax.ShapeDtypeStruct((n_batch * length, D_HYENA), F32),
        compiler_params=_cparams("arbitrary", "arbitrary"),
        name="hyena_conv_inv",
    )(gc, gst, pa, pb, z_prev, a3, d_bias.reshape(1, D_HYENA))


def _hyena(p_all, short_w, short_b, d_bias, kpack, mats, n_batch, length, row_block0):
    gc, gs, gst = mats
    a3 = _short_conv(p_all, short_w, short_b, n_batch, length, row_block0)
    pa, pb = _conv_fwd(gc, gs, a3, 0, kpack, 0, n_batch)
    z1 = _conv_inv(gc, gst, pa, pb, a3, 0, a3, 1, d_bias[0], n_batch)
    pa, pb = _conv_fwd(gc, gs, z1, 0, kpack, 1, n_batch)
    return _conv_inv(gc, gst, pa, pb, z1, 0, a3, 2, d_bias[1], n_batch)


def _out_proj_kernel(x_ref, gm_ref, na_ref, hyl_ref, hyc_ref, w_ref, mod_ref, o_ref, *, n_lat_tiles):
    is_lat = pl.program_id(0) < n_lat_tiles
    hy = jnp.where(is_lat, hyl_ref[...], hyc_ref[...])
    y = jnp.dot(gm_ref[...].astype(BF16), w_ref[0:D_GMLP, :], preferred_element_type=F32)
    y = y + jnp.dot(na_ref[...].astype(BF16), w_ref[D_GMLP:D_GMLP + D_NA, :], preferred_element_type=F32)
    y = y + jnp.dot(hy.astype(BF16), w_ref[D_GMLP + D_NA:, :], preferred_element_type=F32)
    o_ref[...] = x_ref[...] + mod_ref[2:3, :] * y


def _out_proj(x_all, gm, na, hy_lat, hy_ctx, w_bf, mods_l, n_batch, seq, n_tiles):
    d = x_all.shape[1]
    n_lat_tiles = n_batch * seq // TOKEN_TILE
    n_ctx_tiles = hy_ctx.shape[0] // TOKEN_TILE
    row = lambda w: pl.BlockSpec((TOKEN_TILE, w), lambda i: (i, 0))
    return pl.pallas_call(
        functools.partial(_out_proj_kernel, n_lat_tiles=n_lat_tiles),
        grid=(n_tiles,),
        in_specs=[
            row(d), row(D_GMLP), row(D_NA),
            pl.BlockSpec((TOKEN_TILE, D_HYENA), lambda i: (jnp.minimum(i, n_lat_tiles - 1), 0)),
            pl.BlockSpec((TOKEN_TILE, D_HYENA),
                         lambda i: (jnp.clip(i - n_lat_tiles, 0, n_ctx_tiles - 1), 0)),
            pl.BlockSpec(w_bf.shape, lambda i: (0, 0)),
            pl.BlockSpec((None, N_MOD, d), _mod_index(seq // TOKEN_TILE, n_batch)),
        ],
        out_specs=row(d),
        out_shape=jax.ShapeDtypeStruct((n_tiles * TOKEN_TILE, d), F32),
        compiler_params=_cparams("arbitrary"),
        name="out_proj",
    )(x_all, gm, na, hy_lat, hy_ctx, w_bf, mods_l)


def _router_kernel(x_ref, g_ref, mod_ref, wr_ref, br_ref, tril_ref, h_ref, route_ref, count_ref, run_ref):
    @pl.when(pl.program_id(0) == 0)
    def _init():
        run_ref[...] = jnp.zeros_like(run_ref)

    mod = mod_ref[...]
    h = _rms_rows(x_ref[...]) * g_ref[...]
    h = h * (1.0 + mod[4:5]) + mod[3:4]
    h_ref[:, :D_MODEL] = h
    logits = jnp.dot(h, wr_ref[...], preferred_element_type=F32, precision=HI) + br_ref[...]
    lane = lax.broadcasted_iota(jnp.int32, logits.shape, 1)
    neg = -jnp.inf
    is_group = (lane >= MOE_EXPERTS) & (lane < MOE_EXPERTS + MOE_GROUPS)
    lg = jnp.where(is_group, logits, neg)
    mg = jnp.max(lg, axis=-1, keepdims=True)
    g_p = 1.0 / jnp.sum(jnp.exp(lg - mg), axis=-1, keepdims=True)
    g_idx = jnp.min(jnp.where(lg == mg, lane, 2 * LANES), axis=-1, keepdims=True) - MOE_EXPERTS
    in_group = (lane >= g_idx * MOE_EXPERTS_PER_GROUP) & (lane < (g_idx + 1) * MOE_EXPERTS_PER_GROUP)
    le = jnp.where(in_group, logits, neg)
    me = jnp.max(le, axis=-1, keepdims=True)
    pe = jnp.exp(le - me)
    pe = pe / jnp.sum(pe, axis=-1, keepdims=True)
    p1 = jnp.max(pe, axis=-1, keepdims=True)
    i1 = jnp.min(jnp.where(in_group & (pe == p1), lane, 2 * LANES), axis=-1, keepdims=True)
    pe2 = jnp.where(in_group & (lane != i1), pe, neg)
    p2 = jnp.max(pe2, axis=-1, keepdims=True)
    i2 = jnp.min(jnp.where(pe2 == p2, lane, 2 * LANES), axis=-1, keepdims=True)
    tot = p1 + p2
    w_lo = g_p * jnp.where(i1 < i2, p1, p2) / tot
    w_hi = g_p * jnp.where(i1 < i2, p2, p1) / tot
    a = jnp.minimum(i1, i2) - g_idx * MOE_EXPERTS_PER_GROUP
    b = jnp.maximum(i1, i2) - g_idx * MOE_EXPERTS_PER_GROUP
    pair = a * (MOE_EXPERTS_PER_GROUP - 1) - ((a * (a - 1)) >> 1) + (b - a - 1)
    bucket = g_idx * MOE_PAIRS + pair
    onehot = lane == bucket
    prefix = jnp.dot(tril_ref[...], onehot.astype(BF16), preferred_element_type=F32)
    run = run_ref[...]
    rank = jnp.sum(jnp.where(onehot, prefix + run, 0.0), axis=-1, keepdims=True) - 1.0
    run = run + prefix[TOKEN_TILE - 1:TOKEN_TILE, :]
    run_ref[...] = run
    count_ref[...] = run
    route = jnp.where(lane == 0, bucket.astype(F32),
                      jnp.where(lane == 1, rank,
                                jnp.where(lane == 2, w_lo, jnp.where(lane == 3, w_hi, 0.0))))
    route_ref[...] = route
    h_ref[:, D_MODEL:] = route


def _router(x_all, gain, mods_l, w_router, b_router, n_batch, seq, n_tiles):
    d = x_all.shape[1]
    tril = np.tril(np.ones((TOKEN_TILE, TOKEN_TILE), np.float32))
    return pl.pallas_call(
        _router_kernel,
        grid=(n_tiles,),
        in_specs=[
            pl.BlockSpec((TOKEN_TILE, d), lambda i: (i, 0)),
            pl.BlockSpec((1, d), lambda i: (0, 0)),
            pl.BlockSpec((None, N_MOD, d), _mod_index(seq // TOKEN_TILE, n_batch)),
            pl.BlockSpec((d, LANES), lambda i: (0, 0)),
            pl.BlockSpec((1, LANES), lambda i: (0, 0)),
            pl.BlockSpec((TOKEN_TILE, TOKEN_TILE), lambda i: (0, 0)),
        ],
        out_specs=[pl.BlockSpec((TOKEN_TILE, d + LANES), lambda i: (i, 0)),
                   pl.BlockSpec((TOKEN_TILE, LANES), lambda i: (i, 0)),
                   pl.BlockSpec((1, LANES), lambda i: (0, 0))],
        out_shape=[jax.ShapeDtypeStruct((n_tiles * TOKEN_TILE, d + LANES), F32),
                   jax.ShapeDtypeStruct((n_tiles * TOKEN_TILE, LANES), F32),
                   jax.ShapeDtypeStruct((1, LANES), F32)],
        scratch_shapes=[pltpu.VMEM((1, LANES), F32)],
        compiler_params=_cparams("arbitrary"),
        name="moe_router",
    )(x_all, gain.reshape(1, d), mods_l, w_router, b_router, jnp.asarray(tril, BF16))


def _row_copy(table_hbm, dst, sem, src_row, dst_row):
    return pltpu.make_async_copy(table_hbm.at[pl.ds(src_row, 1)], dst.at[pl.ds(dst_row, 1)], sem)


def _gather_block(idx_ref, table_hbm, dst, sem):
    rows = dst.shape[0]
    base = pl.program_id(0) * rows

    def issue(r, carry):
        _row_copy(table_hbm, dst, sem, idx_ref[base + r], r).start()
        return carry

    def drain(r, carry):
        _row_copy(table_hbm, dst, sem, 0, r).wait()
        return carry

    lax.fori_loop(0, rows, issue, 0, unroll=GATHER_UNROLL)
    lax.fori_loop(0, rows, drain, 0, unroll=GATHER_UNROLL)


def _gather_rows_kernel(idx_ref, n_ref, table_hbm, o_ref, sem):
    used = pl.program_id(0) * o_ref.shape[0] < n_ref[0]

    @pl.when(used)
    def _():
        _gather_block(idx_ref, table_hbm, o_ref, sem)

    @pl.when(jnp.logical_not(used))
    def _():
        o_ref[...] = jnp.zeros_like(o_ref)


def _gather_rows(table, idx, n_used):
    n_rows, width = idx.shape[0], table.shape[1]
    return pl.pallas_call(
        _gather_rows_kernel,
        grid_spec=pltpu.PrefetchScalarGridSpec(
            num_scalar_prefetch=2,
            grid=(n_rows // GATHER_ROWS,),
            in_specs=[pl.BlockSpec(memory_space=pl.ANY)],
            out_specs=pl.BlockSpec((GATHER_ROWS, width), lambda i, idx, n: (i, 0)),
            scratch_shapes=[pltpu.SemaphoreType.DMA],
        ),
        out_shape=jax.ShapeDtypeStruct((n_rows, width), table.dtype),
        compiler_params=_cparams("arbitrary"),
        name="gather_rows",
    )(idx, n_used, table)


def _gather_residual_kernel(idx_ref, y_hbm, x_ref, mod_ref, o_ref, buf, sem):
    _gather_block(idx_ref, y_hbm, buf, sem)
    o_ref[...] = x_ref[...] + mod_ref[5:6, :] * buf[...]


def _gather_residual(x_all, y_sorted, dest, mods_l, n_batch, seq, n_tiles):
    d = x_all.shape[1]
    row = pl.BlockSpec((TOKEN_TILE, d), lambda i, idx: (i, 0))
    mod_index = _mod_index(seq // TOKEN_TILE, n_batch)
    return pl.pallas_call(
        _gather_residual_kernel,
        grid_spec=pltpu.PrefetchScalarGridSpec(
            num_scalar_prefetch=1,
            grid=(n_tiles,),
            in_specs=[pl.BlockSpec(memory_space=pl.ANY), row,
                      pl.BlockSpec((None, N_MOD, d), lambda i, idx: mod_index(i))],
            out_specs=row,
            scratch_shapes=[pltpu.VMEM((TOKEN_TILE, d), F32), pltpu.SemaphoreType.DMA],
        ),
        out_shape=jax.ShapeDtypeStruct((n_tiles * TOKEN_TILE, d), F32),
        compiler_params=_cparams("arbitrary"),
        name="moe_combine",
    )(dest, y_sorted, x_all, mods_l)


def _bucket_experts():
    lo, hi = [], []
    for g in range(MOE_GROUPS):
        for a in range(MOE_EXPERTS_PER_GROUP):
            for b in range(a + 1, MOE_EXPERTS_PER_GROUP):
                lo.append(g * MOE_EXPERTS_PER_GROUP + a)
                hi.append(g * MOE_EXPERTS_PER_GROUP + b)
    return np.asarray(lo, np.int32), np.asarray(hi, np.int32)


def _sorted_experts_kernel(lo_ref, hi_ref, act_ref, x_ref, wgl_ref, wul_ref, wdl_ref,
                           wgh_ref, wuh_ref, wdh_ref, o_ref):
    del lo_ref, hi_ref
    active = act_ref[pl.program_id(0)] != 0

    @pl.when(jnp.logical_not(active))
    def _unused_tile():
        o_ref[...] = jnp.zeros_like(o_ref)

    @pl.when(active)
    def _active_tile():
        xb = x_ref[:, :D_MODEL].astype(BF16)
        r = x_ref[:, D_MODEL:]
        acc = None
        for wg_ref, wu_ref, wd_ref, lane in ((wgl_ref, wul_ref, wdl_ref, 2), (wgh_ref, wuh_ref, wdh_ref, 3)):
            gate = jnp.dot(xb, wg_ref[...], preferred_element_type=F32)
            up = jnp.dot(xb, wu_ref[...], preferred_element_type=F32)
            act = _silu(gate) * up * r[:, lane:lane + 1]
            part = jnp.dot(act.astype(BF16), wd_ref[...], preferred_element_type=F32)
            acc = part if acc is None else acc + part
        o_ref[...] = acc


def _sorted_experts(x_sorted, tile_lo, tile_hi, tile_active, wg, wu, wd):
    n_rows = x_sorted.shape[0]
    d, f = wg.shape[-2:]
    lo_w = lambda shape: pl.BlockSpec(shape, lambda t, lo, hi, act: (lo[t], 0, 0))
    hi_w = lambda shape: pl.BlockSpec(shape, lambda t, lo, hi, act: (hi[t], 0, 0))
    row = lambda w: pl.BlockSpec((MOE_SORT_TILE, w), lambda t, lo, hi, act: (t, 0))
    return pl.pallas_call(
        _sorted_experts_kernel,
        grid_spec=pltpu.PrefetchScalarGridSpec(
            num_scalar_prefetch=3,
            grid=(n_rows // MOE_SORT_TILE,),
            in_specs=[row(d + LANES),
                      lo_w((None, d, f)), lo_w((None, d, f)), lo_w((None, f, d)),
                      hi_w((None, d, f)), hi_w((None, d, f)), hi_w((None, f, d))],
            out_specs=row(d),
        ),
        out_shape=jax.ShapeDtypeStruct((n_rows, d), F32),
        compiler_params=_cparams("arbitrary"),
        name="moe_experts",
    )(tile_lo, tile_hi, tile_active, x_sorted, wg, wu, wd, wg, wu, wd)


def _moe(x_all, gain, mods_l, w_router, b_router, wg, wu, wd, n_batch, seq, n_tiles):
    n = n_tiles * TOKEN_TILE
    h, route, counts = _router(x_all, gain, mods_l, w_router, b_router, n_batch, seq, n_tiles)

    n_sorted = -(-(n + MOE_BUCKETS * (MOE_SORT_TILE - 1)) // GATHER_ROWS) * GATHER_ROWS
    n_sorted_tiles = n_sorted // MOE_SORT_TILE
    counts = counts[0, :MOE_BUCKETS].astype(jnp.int32)
    bucket_tiles = (counts + MOE_SORT_TILE - 1) // MOE_SORT_TILE
    tile_end = jnp.cumsum(bucket_tiles)
    row_start = (tile_end - bucket_tiles) * MOE_SORT_TILE
    bucket = route[:, 0].astype(jnp.int32)
    rank = route[:, 1].astype(jnp.int32)
    dest = jnp.sum(jnp.where(bucket[:, None] == jnp.arange(MOE_BUCKETS)[None, :], row_start[None, :], 0),
                   axis=1) + rank
    src = jnp.zeros((n_sorted,), jnp.int32).at[dest].set(jnp.arange(n, dtype=jnp.int32))
    tiles = jnp.arange(n_sorted_tiles, dtype=jnp.int32)
    tile_active = (tiles < tile_end[-1]).astype(jnp.int32)
    tile_bucket = jnp.sum(jnp.minimum(tiles, tile_end[-1] - 1)[:, None] >= tile_end[None, :], axis=1)
    tile_bucket = jnp.minimum(tile_bucket, MOE_BUCKETS - 1)
    lo_ids, hi_ids = _bucket_experts()
    tile_lo = jnp.asarray(lo_ids)[tile_bucket]
    tile_hi = jnp.asarray(hi_ids)[tile_bucket]

    n_used = (tile_end[-1:] * MOE_SORT_TILE).astype(jnp.int32)
    x_sorted = _gather_rows(h, src, n_used)
    y_sorted = _sorted_experts(x_sorted, tile_lo, tile_hi, tile_active, wg, wu, wd)
    return _gather_residual(x_all, y_sorted, dest, mods_l, n_batch, seq, n_tiles)


def kernel(x, c, ctx, c_ctx, w_ada, b_ada, g_mix, g_ffn, w_in, w_out, gmlp_v_gain, gmlp_ws, gmlp_bs,
           na_q_gain, na_k_gain, na_rpb, hy_short_w, hy_short_b, hy_w1, hy_b1, hy_w2, hy_b2, hy_w3,
           hy_freq, hy_bias, moe_w_rg, moe_b_rg, moe_w_re, moe_b_re, moe_w_gate, moe_w_up, moe_w_down):
    n_batch, seq, d = x.shape
    ctx_len = ctx.shape[1]
    depth = w_ada.shape[0]
    n_lat = n_batch * seq
    n_ctx = n_batch * ctx_len
    assert d == D_MODEL and seq % TOKEN_TILE == 0 and n_ctx % TOKEN_TILE == 0
    assert seq % GMLP_CHUNK == 0 and ctx_len % GMLP_CHUNK == 0

    pad_rows = -(n_batch + 1) % 8
    cs = jnp.concatenate([c, c_ctx[None, :], jnp.zeros((pad_rows, d), F32)], axis=0)
    mods = _modulation(cs, w_ada, b_ada)[:, :n_batch + 1].reshape(depth, n_batch + 1, N_MOD, d)

    x_all = jnp.concatenate([x.reshape(n_lat, d), ctx.reshape(n_ctx, d)], axis=0)
    n_all_tiles = (n_lat + n_ctx) // TOKEN_TILE
    n_lat_tiles = n_lat // TOKEN_TILE

    mats_lat = _dft_matrices(seq)
    mats_ctx = _dft_matrices(ctx_len)
    pos_lat = _hyena_positions(seq)
    pos_ctx = _hyena_positions(ctx_len)

    w_in_bf = w_in.astype(BF16)
    w_out_bf = w_out.astype(BF16)
    wg_bf = moe_w_gate.astype(BF16).reshape(depth, MOE_EXPERTS, d, MOE_HIDDEN)
    wu_bf = moe_w_up.astype(BF16).reshape(depth, MOE_EXPERTS, d, MOE_HIDDEN)
    wd_bf = moe_w_down.astype(BF16).reshape(depth, MOE_EXPERTS, MOE_HIDDEN, d)
    lane_pad = LANES - MOE_EXPERTS - MOE_GROUPS
    w_router = jnp.pad(jnp.concatenate([moe_w_re, moe_w_rg], axis=-1), ((0, 0), (0, 0), (0, lane_pad)))
    b_router = jnp.pad(jnp.concatenate([moe_b_re, moe_b_rg], axis=-1), ((0, 0), (0, lane_pad)))[:, None, :]
    w1_pad = jnp.pad(hy_w1, ((0, 0), (0, LANES - HYENA_EMB), (0, 0)))

    for l in range(depth):
        last = l == depth - 1
        mods_l = mods[l]
        n_tiles = n_lat_tiles if last else n_all_tiles

        p_all = _in_proj(x_all, g_mix[l], mods_l, w_in_bf[l], n_batch, seq)

        filt = (w1_pad[l], hy_b1[l], hy_w2[l], hy_b2[l], hy_w3[l], hy_freq[l])
        kpack_lat = _filter_dft(mats_lat[0], mats_lat[1], _hyena_filters(*pos_lat, *filt))
        hy_lat = _hyena(p_all, hy_short_w[l], hy_short_b[l], hy_bias[l], kpack_lat, mats_lat,
                        n_batch, seq, 0)
        if last:
            hy_ctx = hy_lat
            gm = _gmlp(p_all, n_lat, gmlp_v_gain[l], gmlp_ws[l], gmlp_bs[l].T)
        else:
            kpack_ctx = _filter_dft(mats_ctx[0], mats_ctx[1], _hyena_filters(*pos_ctx, *filt))
            hy_ctx = _hyena(p_all, hy_short_w[l], hy_short_b[l], hy_bias[l], kpack_ctx, mats_ctx,
                            n_batch, ctx_len, n_lat // ctx_len)
            gm = _gmlp(p_all, n_lat + n_ctx, gmlp_v_gain[l], gmlp_ws[l], gmlp_bs[l].T)

        na = _attention(p_all, na_rpb[l], na_q_gain[l], na_k_gain[l],
                        n_batch, seq, ctx_len, not last)

        x_all = _out_proj(x_all, gm, na, hy_lat, hy_ctx, w_out_bf[l], mods_l, n_batch, seq, n_tiles)

        x_all = _moe(x_all, g_ffn[l], mods_l, w_router[l], b_router[l], wg_bf[l], wu_bf[l], wd_bf[l],
                     n_batch, seq, n_tiles)

    return x_all[:n_lat].reshape(n_batch, seq, d)
```

```python
import functools
import math

import numpy as np
import jax
import jax.numpy as jnp
from jax import lax
from jax.experimental import pallas as pl
from jax.experimental.pallas import tpu as pltpu

F32 = jnp.float32
BF16 = jnp.bfloat16
HI = lax.Precision.HIGHEST

D_MODEL = 1024
GRID_W = 64
D_GMLP = D_MODEL // 4
D_NA = D_MODEL // 2
D_HYENA = D_MODEL // 4
D_IN = 2 * D_GMLP + 3 * D_NA + 3 * D_HYENA
Q_START = 2 * D_GMLP
KV_START = 2 * D_GMLP + D_NA
HY_START = 2 * D_GMLP + 3 * D_NA
GMLP_GROUPS = 4
GMLP_GROUP_DIM = D_GMLP // GMLP_GROUPS
GMLP_CHUNK = 128
NA_HEAD_DIM = 64
NA_HEADS = D_NA // NA_HEAD_DIM
NA_SCALE = NA_HEAD_DIM ** -0.5
NA_WIN_ROWS = 8
NA_WIN_COLS = 16
HYENA_ORDER = 2
HYENA_POS_BANDS = 16
HYENA_EMB = 1 + 2 * HYENA_POS_BANDS
HYENA_FILTER_HIDDEN = 64
HYENA_DECAY_TARGET = 1e-2
HYENA_FAST_DECAY = 0.3
HYENA_SLOW_DECAY = 1.5
MOE_GROUPS = 4
MOE_EXPERTS_PER_GROUP = 8
MOE_EXPERTS = MOE_GROUPS * MOE_EXPERTS_PER_GROUP
MOE_HIDDEN = 256
MOE_PAIRS = MOE_EXPERTS_PER_GROUP * (MOE_EXPERTS_PER_GROUP - 1) // 2
MOE_BUCKETS = MOE_GROUPS * MOE_PAIRS
N_MOD = 6
RMS_EPS = 1e-6
LN_EPS = 1e-5

LANES = 128
TOKEN_TILE = 512
NA_Q_ROWS = 4
NA_Q_BLOCK = NA_Q_ROWS * GRID_W
NA_BAND_ROWS = NA_WIN_ROWS + NA_Q_ROWS
DFT_TILE = 512
MOE_SORT_TILE = 128
GATHER_UNROLL = 8
MASK_VALUE = -1e30
VMEM_LIMIT = 56 * 1024 * 1024


def _cparams(*sem):
    return pltpu.CompilerParams(dimension_semantics=sem, vmem_limit_bytes=VMEM_LIMIT)


def _silu(x):
    return x * jax.nn.sigmoid(x)


def _rms_rows(x):
    return x * lax.rsqrt(jnp.mean(x * x, axis=-1, keepdims=True) + RMS_EPS)


def _mods_kernel(cs_ref, w_ref, b_ref, o_ref):
    s = _silu(cs_ref[...])
    o_ref[...] = jnp.dot(s, w_ref[...], preferred_element_type=F32, precision=HI) + b_ref[...]


def _modulation(cs, w_ada, b_ada):
    depth, d, nd = w_ada.shape
    rows = cs.shape[0]
    col = 1024
    return pl.pallas_call(
        _mods_kernel,
        grid=(depth, nd // col),
        in_specs=[
            pl.BlockSpec((rows, d), lambda l, j: (0, 0)),
            pl.BlockSpec((None, d, col), lambda l, j: (l, 0, j)),
            pl.BlockSpec((None, 1, col), lambda l, j: (l, 0, j)),
        ],
        out_specs=pl.BlockSpec((None, rows, col), lambda l, j: (l, 0, j)),
        out_shape=jax.ShapeDtypeStruct((depth, rows, nd), F32),
        compiler_params=_cparams("arbitrary", "arbitrary"),
        name="modulation",
    )(cs, w_ada, b_ada.reshape(depth, 1, nd))


def _in_proj_kernel(x_ref, g_ref, mod_ref, w_ref, o_ref):
    mod = mod_ref[...]
    h = _rms_rows(x_ref[...]) * g_ref[...]
    h = h * (1.0 + mod[1:2]) + mod[0:1]
    o_ref[...] = jnp.dot(h.astype(BF16), w_ref[...], preferred_element_type=F32)


def _mod_index(tiles_per_batch, n_batch):
    return lambda i: (jnp.minimum(i // tiles_per_batch, n_batch), 0, 0)


def _in_proj(x_all, gain, mods_l, w_bf, n_batch, seq):
    n, d = x_all.shape
    d_in = w_bf.shape[1]
    return pl.pallas_call(
        _in_proj_kernel,
        grid=(n // TOKEN_TILE,),
        in_specs=[
            pl.BlockSpec((TOKEN_TILE, d), lambda i: (i, 0)),
            pl.BlockSpec((1, d), lambda i: (0, 0)),
            pl.BlockSpec((None, N_MOD, d), _mod_index(seq // TOKEN_TILE, n_batch)),
            pl.BlockSpec((d, d_in), lambda i: (0, 0)),
        ],
        out_specs=pl.BlockSpec((TOKEN_TILE, d_in), lambda i: (i, 0)),
        out_shape=jax.ShapeDtypeStruct((n, d_in), F32),
        compiler_params=_cparams("arbitrary"),
        name="in_proj",
    )(x_all, gain.reshape(1, d), mods_l, w_bf)


def _group_avg_matrix(groups, width):
    return np.kron(np.eye(groups), np.full((width, width), 1.0 / width))


def _group_mean(t, avg):
    hi = t.astype(BF16)
    lo = (t - hi.astype(F32)).astype(BF16)
    return (jnp.dot(hi, avg, preferred_element_type=F32) + jnp.dot(lo, avg, preferred_element_type=F32))


def _gmlp_kernel(u_ref, v_ref, gain_ref, avg_ref, ws_ref, bs_ref, o_ref):
    avg = avg_ref[...]
    v = jax.nn.gelu(v_ref[...])
    v = v - _group_mean(v, avg)
    v = v * lax.rsqrt(_group_mean(v * v, avg) + LN_EPS) * gain_ref[...]
    vb = v.astype(BF16)
    bs = bs_ref[...]
    for c in range(u_ref.shape[0] // GMLP_CHUNK):
        rows = slice(c * GMLP_CHUNK, (c + 1) * GMLP_CHUNK)
        outs = []
        for g in range(GMLP_GROUPS):
            cols = slice(g * GMLP_GROUP_DIM, (g + 1) * GMLP_GROUP_DIM)
            s = jnp.dot(ws_ref[g], vb[rows, cols], preferred_element_type=F32) + bs[:, g:g + 1]
            outs.append(s)
        o_ref[rows, :] = jax.nn.gelu(u_ref[rows, :]) * jnp.concatenate(outs, axis=-1)


def _gmlp(p_all, n, v_gain, ws, bs_t):
    avg = _group_avg_matrix(GMLP_GROUPS, GMLP_GROUP_DIM)
    return pl.pallas_call(
        _gmlp_kernel,
        grid=(n // TOKEN_TILE,),
        in_specs=[
            pl.BlockSpec((TOKEN_TILE, D_GMLP), lambda i: (i, 0)),
            pl.BlockSpec((TOKEN_TILE, D_GMLP), lambda i: (i, 1)),
            pl.BlockSpec((1, D_GMLP), lambda i: (0, 0)),
            pl.BlockSpec((D_GMLP, D_GMLP), lambda i: (0, 0)),
            pl.BlockSpec((GMLP_GROUPS, GMLP_CHUNK, GMLP_CHUNK), lambda i: (0, 0, 0)),
            pl.BlockSpec((GMLP_CHUNK, GMLP_GROUPS), lambda i: (0, 0)),
        ],
        out_specs=pl.BlockSpec((TOKEN_TILE, D_GMLP), lambda i: (i, 0)),
        out_shape=jax.ShapeDtypeStruct((n, D_GMLP), F32),
        compiler_params=_cparams("arbitrary"),
        name="gmlp",
    )(p_all, p_all, v_gain.reshape(1, D_GMLP), jnp.asarray(avg, BF16), ws.astype(BF16), bs_t)


def _rpb_expand_kernel(rpb_ref, sel_ref, o_ref):
    o_ref[...] = jnp.dot(rpb_ref[...], sel_ref[...], preferred_element_type=F32, precision=HI)


def _na_geometry(grid_rows):
    variants, step_variant, band_start = [], [], []
    for r0 in range(0, grid_rows, NA_Q_ROWS):
        b0 = int(np.clip(r0 - NA_WIN_ROWS // 2, 0, grid_rows - NA_BAND_ROWS))
        geo = []
        for r in range(r0, r0 + NA_Q_ROWS):
            wr = int(np.clip(r - NA_WIN_ROWS // 2, 0, grid_rows - NA_WIN_ROWS))
            assert b0 <= wr and wr + NA_WIN_ROWS <= b0 + NA_BAND_ROWS
            geo.append((wr - b0, wr - r + NA_WIN_ROWS - 1))
        geo = tuple(geo)
        if geo not in variants:
            variants.append(geo)
        step_variant.append(variants.index(geo))
        band_start.append(b0)
    return variants, step_variant, band_start


def _na_bias_tables(rpb, variants):
    n_heads, n_dr, n_dc = rpb.shape
    qcol = np.arange(GRID_W)[:, None]
    kcol = np.arange(GRID_W)[None, :]
    win_c = np.clip(qcol - NA_WIN_COLS // 2, 0, GRID_W - NA_WIN_COLS)
    col_ok = (kcol >= win_c) & (kcol < win_c + NA_WIN_COLS)
    dc = np.clip(kcol - qcol + NA_WIN_COLS - 1, 0, 2 * NA_WIN_COLS - 2)
    dc_pad = -n_dc % 8
    sel = (np.arange(n_dc + dc_pad)[:, None] == dc.reshape(1, -1)).astype(np.float32)
    rpb2 = jnp.pad(rpb.reshape(n_heads * n_dr, n_dc), ((0, 0), (0, dc_pad)))
    toep = pl.pallas_call(
        _rpb_expand_kernel,
        out_shape=jax.ShapeDtypeStruct((n_heads * n_dr, GRID_W * GRID_W), F32),
        name="rpb_expand",
    )(rpb2, jnp.asarray(sel))
    toep = toep.reshape(n_heads, n_dr, GRID_W, GRID_W)
    toep = jnp.where(col_ok[None, None], toep, MASK_VALUE)
    tabs = []
    for geo in variants:
        rows = []
        for a0, dr0 in geo:
            rows.append(jnp.pad(toep[:, dr0:dr0 + NA_WIN_ROWS],
                                ((0, 0), (a0, NA_BAND_ROWS - NA_WIN_ROWS - a0), (0, 0), (0, 0)),
                                constant_values=MASK_VALUE))
        tab = jnp.stack(rows, axis=1)
        tabs.append(jnp.transpose(tab, (0, 1, 3, 2, 4)).reshape(n_heads, NA_Q_BLOCK, NA_BAND_ROWS * GRID_W))
    return jnp.stack(tabs, axis=1)


def _store_heads(dst, rows, t, gain, avg):
    if gain is not None:
        t = t * lax.rsqrt(_group_mean(t * t, avg) + RMS_EPS) * gain
    for h in range(NA_HEADS):
        dst[h, rows, :] = t[:, h * NA_HEAD_DIM:(h + 1) * NA_HEAD_DIM].astype(BF16)


def _na_kernel(q_ref, k_ref, v_ref, kc_ref, vc_ref, bias_ref, qg_ref, kg_ref, avg_ref, o_ref,
               kn_s, vb_s, kcn_s, vcb_s, qn_s, o_s, *, n_lat_steps, step_variant, band_start):
    step = pl.program_id(1)
    kg = kg_ref[...]
    avg = avg_ref[...]
    nt = (((1,), (1,)), ((), ()))
    band_keys = NA_BAND_ROWS * GRID_W
    all_rows = slice(None)

    @pl.when(step == 0)
    def _prepare_keys():
        chunk = 256

        def body(c, carry):
            rows = pl.ds(pl.multiple_of(c * chunk, chunk), chunk)
            _store_heads(kn_s, rows, k_ref[rows, :], kg, avg)
            _store_heads(vb_s, rows, v_ref[rows, :], None, None)
            return carry

        lax.fori_loop(0, k_ref.shape[0] // chunk, body, 0)
        _store_heads(kcn_s, all_rows, kc_ref[...], kg, avg)
        _store_heads(vcb_s, all_rows, vc_ref[...], None, None)

    _store_heads(qn_s, all_rows, q_ref[...], qg_ref[...] * NA_SCALE, avg)

    def finish(h, scores, v_parts):
        m = jnp.max(scores, axis=-1, keepdims=True)
        p = jnp.exp(scores - m)
        denom = jnp.sum(p, axis=-1, keepdims=True)
        pb = p.astype(BF16)
        acc = None
        col = 0
        for v in v_parts:
            part = jnp.dot(pb[:, col:col + v.shape[0]], v, preferred_element_type=F32)
            acc = part if acc is None else acc + part
            col += v.shape[0]
        o_s[h] = acc / denom

    @pl.when(step < n_lat_steps)
    def _latent_queries():
        variant = jnp.int32(0)
        band0 = jnp.int32(0)
        for s_, (v_, b_) in enumerate(zip(step_variant, band_start)):
            variant = jnp.where(step == s_, v_, variant)
            band0 = jnp.where(step == s_, b_ * GRID_W, band0)
        krows = pl.ds(pl.multiple_of(band0, NA_Q_BLOCK), band_keys)

        def head_body(h, carry):
            qh = qn_s[h]
            s_w = lax.dot_general(qh, kn_s[h, krows, :], nt, preferred_element_type=F32)
            s_w = s_w + bias_ref[h, variant]
            s_c = lax.dot_general(qh, kcn_s[h], nt, preferred_element_type=F32)
            finish(h, jnp.concatenate([s_w, s_c], axis=-1), (vb_s[h, krows, :], vcb_s[h]))
            return carry

        lax.fori_loop(0, NA_HEADS, head_body, 0, unroll=2)

    @pl.when(step >= n_lat_steps)
    def _context_queries():
        def head_body(h, carry):
            s = lax.dot_general(qn_s[h], kcn_s[h], nt, preferred_element_type=F32)
            finish(h, s, (vcb_s[h],))
            return carry

        lax.fori_loop(0, NA_HEADS, head_body, 0)

    o_ref[...] = jnp.concatenate([o_s[h] for h in range(NA_HEADS)], axis=-1)


def _attention(p_all, rpb, q_gain, k_gain, n_batch, seq, ctx_len, with_ctx_queries):
    n = p_all.shape[0]
    assert ctx_len == NA_Q_BLOCK and seq % NA_Q_BLOCK == 0
    n_lat_steps = seq // NA_Q_BLOCK
    n_steps = n_lat_steps + (1 if with_ctx_queries else 0)
    ctx_block0 = n_batch * seq // ctx_len
    qcol, kcol, vcol = Q_START // D_NA, KV_START // D_NA, (KV_START + D_NA) // D_NA
    variants, step_variant, band_start = _na_geometry(seq // GRID_W)
    bias_tab = _na_bias_tables(rpb, variants)

    def q_index(col):
        return lambda b, s: (jnp.where(s < n_lat_steps, b * n_lat_steps + s, ctx_block0 + b), col)

    kern = functools.partial(_na_kernel, n_lat_steps=n_lat_steps, step_variant=tuple(step_variant),
                             band_start=tuple(band_start))
    head_major = lambda rows: pltpu.VMEM((NA_HEADS, rows, NA_HEAD_DIM), BF16)
    return pl.pallas_call(
        kern,
        grid=(n_batch, n_steps),
        in_specs=[
            pl.BlockSpec((NA_Q_BLOCK, D_NA), q_index(qcol)),
            pl.BlockSpec((seq, D_NA), lambda b, s: (b, kcol)),
            pl.BlockSpec((seq, D_NA), lambda b, s: (b, vcol)),
            pl.BlockSpec((ctx_len, D_NA), lambda b, s: (ctx_block0 + b, kcol)),
            pl.BlockSpec((ctx_len, D_NA), lambda b, s: (ctx_block0 + b, vcol)),
            pl.BlockSpec(bias_tab.shape, lambda b, s: (0, 0, 0, 0), pipeline_mode=pl.Buffered(1)),
            pl.BlockSpec((1, D_NA), lambda b, s: (0, 0)),
            pl.BlockSpec((1, D_NA), lambda b, s: (0, 0)),
            pl.BlockSpec((D_NA, D_NA), lambda b, s: (0, 0)),
        ],
        out_specs=pl.BlockSpec((NA_Q_BLOCK, D_NA), q_index(0)),
        out_shape=jax.ShapeDtypeStruct((n if with_ctx_queries else n_batch * seq, D_NA), F32),
        scratch_shapes=[
            head_major(seq), head_major(seq), head_major(ctx_len), head_major(ctx_len),
            head_major(NA_Q_BLOCK),
            pltpu.VMEM((NA_HEADS, NA_Q_BLOCK, NA_HEAD_DIM), F32),
        ],
        compiler_params=_cparams("arbitrary", "arbitrary"),
        name="attention",
    )(p_all, p_all, p_all, p_all, p_all, bias_tab,
      jnp.tile(q_gain.reshape(1, NA_HEAD_DIM), (1, NA_HEADS)),
      jnp.tile(k_gain.reshape(1, NA_HEAD_DIM), (1, NA_HEADS)),
      jnp.asarray(_group_avg_matrix(NA_HEADS, NA_HEAD_DIM), BF16))


def _dft_matrices(length):
    idx = jnp.arange(length, dtype=jnp.int32)
    step = 64
    t_hi = jnp.arange(length // step, dtype=jnp.int32) * step
    t_lo = jnp.arange(step, dtype=jnp.int32)
    ang_hi = ((idx[:, None] * t_hi[None, :]) % (2 * length)).astype(F32) * (math.pi / length)
    ang_lo = ((idx[:, None] * t_lo[None, :]) % (2 * length)).astype(F32) * (math.pi / length)
    c_hi, s_hi = jnp.cos(ang_hi)[:, :, None], jnp.sin(ang_hi)[:, :, None]
    c_lo, s_lo = jnp.cos(ang_lo)[:, None, :], jnp.sin(ang_lo)[:, None, :]
    gc = (c_hi * c_lo - s_hi * s_lo).reshape(length, length)
    gs = (s_hi * c_lo + c_hi * s_lo).reshape(length, length)
    nyq = jnp.where(idx % 2 == 0, 1.0, -1.0).astype(F32)
    gs = jnp.where(idx[:, None] == 0, nyq[None, :], gs)
    return gc.astype(BF16), gs.astype(BF16), gs.T.astype(BF16)


def _hyena_positions(length):
    t = jnp.linspace(0.0, 1.0, length, dtype=F32)[:, None]
    w = 2.0 * math.pi * jnp.arange(length, dtype=F32)[:, None] / length
    f = jnp.linspace(1e-4, HYENA_POS_BANDS - 1, HYENA_POS_BANDS, dtype=F32)[None, :]
    z = jnp.concatenate([t, jnp.cos(f * w), -jnp.sin(f * w)], axis=-1)
    z = jnp.pad(z, ((0, 0), (0, LANES - HYENA_EMB)))
    min_decay = math.log(HYENA_DECAY_TARGET) / HYENA_SLOW_DECAY
    max_decay = math.log(HYENA_DECAY_TARGET) / HYENA_FAST_DECAY
    deltas = jnp.abs(jnp.linspace(min_decay, max_decay, D_HYENA, dtype=F32))[None, :]
    return z, jnp.exp(-t * deltas)


def _filter_kernel(z_ref, decay_ref, w1_ref, b1_ref, w2_ref, b2_ref, w3_ref, freq_ref, o_ref):
    freq = freq_ref[...]
    hdn = jnp.dot(z_ref[...], w1_ref[...], preferred_element_type=F32, precision=HI) + b1_ref[...]
    hdn = jnp.sin(freq[0:1] * hdn)
    hdn = jnp.dot(hdn, w2_ref[...], preferred_element_type=F32, precision=HI) + b2_ref[...]
    hdn = jnp.sin(freq[1:2] * hdn)
    h = jnp.dot(hdn, w3_ref[...], preferred_element_type=F32, precision=HI)
    decay = decay_ref[...]
    first_row = lax.broadcasted_iota(jnp.int32, decay.shape, 0) == 0
    outs = []
    for n in range(HYENA_ORDER):
        base = 2 * n * D_HYENA
        hf = h[:, base:base + D_HYENA] * decay
        hb = h[:, base + D_HYENA:base + 2 * D_HYENA] * decay
        norm = jnp.sum(jnp.abs(hf), axis=0, keepdims=True) + jnp.sum(jnp.abs(hb), axis=0, keepdims=True)
        outs.append(hf / norm)
        outs.append(jnp.where(first_row, 0.0, hb / norm))
    o_ref[...] = jnp.concatenate(outs, axis=-1)


def _hyena_filters(z, decay, w1p, b1, w2, b2, w3, freq):
    length = z.shape[0]
    full = lambda a: pl.BlockSpec(a.shape, lambda i: (0,) * a.ndim)
    args = (z, decay, w1p, b1.reshape(1, -1), w2, b2.reshape(1, -1), w3, freq)
    return pl.pallas_call(
        _filter_kernel,
        grid=(1,),
        in_specs=[full(a) for a in args],
        out_specs=pl.BlockSpec((length, 2 * HYENA_ORDER * D_HYENA), lambda i: (0, 0)),
        out_shape=jax.ShapeDtypeStruct((length, 2 * HYENA_ORDER * D_HYENA), F32),
        compiler_params=_cparams("arbitrary"),
        name="hyena_filter",
    )(*args)


def _filter_dft_kernel(gc_ref, gs_ref, h_ref, o_ref, *, length):
    hb = h_ref[...].astype(BF16)
    fa = jnp.dot(gc_ref[...], hb, preferred_element_type=F32)
    fb = jnp.dot(gs_ref[...], hb, preferred_element_type=F32)
    rows = lax.broadcasted_iota(jnp.int32, (fa.shape[0], D_HYENA), 0) + pl.program_id(0) * fa.shape[0]
    dc_row = rows == 0
    inv_n = 1.0 / (2 * length)
    outs = []
    for n in range(HYENA_ORDER):
        base = 2 * n * D_HYENA
        f_sl = slice(base, base + D_HYENA)
        b_sl = slice(base + D_HYENA, base + 2 * D_HYENA)
        kr = fa[:, f_sl] + fa[:, b_sl]
        ki = fb[:, b_sl] - fb[:, f_sl]
        k_nyq = fb[:, f_sl] + fb[:, b_sl]
        outs.append(jnp.where(dc_row, kr * inv_n, 2.0 * inv_n * kr))
        outs.append(jnp.where(dc_row, 0.0, 2.0 * inv_n * ki))
        outs.append(jnp.where(dc_row, 0.0, -2.0 * inv_n * ki))
        outs.append(jnp.where(dc_row, k_nyq * inv_n, 2.0 * inv_n * kr))
    o_ref[...] = jnp.concatenate(outs, axis=-1)


def _filter_dft(gc, gs, hfilt):
    length = gc.shape[0]
    tile = min(DFT_TILE, length)
    width = 4 * HYENA_ORDER * D_HYENA
    return pl.pallas_call(
        functools.partial(_filter_dft_kernel, length=length),
        grid=(length // tile,),
        in_specs=[
            pl.BlockSpec((tile, length), lambda j: (j, 0)),
            pl.BlockSpec((tile, length), lambda j: (j, 0)),
            pl.BlockSpec(hfilt.shape, lambda j: (0, 0)),
        ],
        out_specs=pl.BlockSpec((tile, width), lambda j: (j, 0)),
        out_shape=jax.ShapeDtypeStruct((length, width), F32),
        compiler_params=_cparams("arbitrary"),
        name="hyena_filter_dft",
    )(gc, gs, hfilt)


def _short_conv_kernel(a0_ref, a1_ref, a2_ref, w_ref, b_ref, o_ref):
    w = w_ref[...]
    b = b_ref[...]
    length = a0_ref.shape[0]
    rows = lax.broadcasted_iota(jnp.int32, (length, D_HYENA), 0)
    for j, a_ref in enumerate((a0_ref, a1_ref, a2_ref)):
        cols = slice(j * D_HYENA, (j + 1) * D_HYENA)
        a = a_ref[...]
        prev = jnp.where(rows == 0, 0.0, pltpu.roll(a, 1, 0))
        nxt = jnp.where(rows == length - 1, 0.0, pltpu.roll(a, length - 1, 0))
        o_ref[:, cols] = prev * w[0:1, cols] + a * w[1:2, cols] + nxt * w[2:3, cols] + b[:, cols]


def _short_conv(p_all, short_w, short_b, n_batch, length, row_block0):
    c0 = HY_START // D_HYENA
    spec = lambda j: pl.BlockSpec((length, D_HYENA), lambda b: (row_block0 + b, c0 + j))
    return pl.pallas_call(
        _short_conv_kernel,
        grid=(n_batch,),
        in_specs=[spec(0), spec(1), spec(2),
                  pl.BlockSpec((3, 3 * D_HYENA), lambda b: (0, 0)),
                  pl.BlockSpec((1, 3 * D_HYENA), lambda b: (0, 0))],
        out_specs=pl.BlockSpec((length, 3 * D_HYENA), lambda b: (b, 0)),
        out_shape=jax.ShapeDtypeStruct((n_batch * length, 3 * D_HYENA), F32),
        compiler_params=_cparams("arbitrary"),
        name="hyena_short_conv",
    )(p_all, p_all, p_all, short_w, short_b.reshape(1, -1))


def _conv_fwd_kernel(gc_ref, gs_ref, u_ref, k_ref, pa_ref, pb_ref):
    u = u_ref[...].astype(BF16)
    a = jnp.dot(gc_ref[...], u, preferred_element_type=F32)
    b = jnp.dot(gs_ref[...], u, preferred_element_type=F32)
    k = k_ref[...]
    c = D_HYENA
    pa_ref[...] = (a * k[:, 0:c] + b * k[:, c:2 * c]).astype(BF16)
    pb_ref[...] = (a * k[:, 2 * c:3 * c] + b * k[:, 3 * c:4 * c]).astype(BF16)


def _conv_fwd(gc, gs, u, u_col, kpack, order, n_batch):
    length = gc.shape[0]
    tile = min(DFT_TILE, length)
    nt = length // tile
    out = jax.ShapeDtypeStruct((n_batch * length, D_HYENA), BF16)
    return pl.pallas_call(
        _conv_fwd_kernel,
        grid=(nt, n_batch),
        in_specs=[
            pl.BlockSpec((tile, length), lambda j, b: (j, 0)),
            pl.BlockSpec((tile, length), lambda j, b: (j, 0)),
            pl.BlockSpec((length, D_HYENA), lambda j, b: (b, u_col)),
            pl.BlockSpec((tile, 4 * D_HYENA), lambda j, b: (j, order)),
        ],
        out_specs=[pl.BlockSpec((tile, D_HYENA), lambda j, b: (b * nt + j, 0))] * 2,
        out_shape=[out, out],
        compiler_params=_cparams("arbitrary", "arbitrary"),
        name="hyena_conv_fwd",
    )(gc, gs, u, kpack)


def _conv_inv_kernel(gc_ref, gst_ref, pa_ref, pb_ref, z_ref, gate_ref, d_ref, o_ref):
    y = jnp.dot(gc_ref[...], pa_ref[...], preferred_element_type=F32)
    y = y + jnp.dot(gst_ref[...], pb_ref[...], preferred_element_type=F32)
    o_ref[...] = gate_ref[...] * (y + d_ref[...] * z_ref[...])


def _conv_inv(gc, gst, pa, pb, z_prev, z_col, a3, gate_col, d_bias, n_batch):
    length = gc.shape[0]
    tile = min(DFT_TILE, length)
    nt = length // tile
    return pl.pallas_call(
        _conv_inv_kernel,
        grid=(nt, n_batch),
        in_specs=[
            pl.BlockSpec((tile, length), lambda j, b: (j, 0)),
            pl.BlockSpec((tile, length), lambda j, b: (j, 0)),
            pl.BlockSpec((length, D_HYENA), lambda j, b: (b, 0)),
            pl.BlockSpec((length, D_HYENA), lambda j, b: (b, 0)),
            pl.BlockSpec((tile, D_HYENA), lambda j, b: (b * nt + j, z_col)),
            pl.BlockSpec((tile, D_HYENA), lambda j, b: (b * nt + j, gate_col)),
            pl.BlockSpec((1, D_HYENA), lambda j, b: (0, 0)),
        ],
        out_specs=pl.BlockSpec((tile, D_HYENA), lambda j, b: (b * nt + j, 0)),
        out_shape=jax.ShapeDtypeStruct((n_batch * length, D_HYENA), F32),
        compiler_params=_cparams("arbitrary", "arbitrary"),
        name="hyena_conv_inv",
    )(gc, gst, pa, pb, z_prev, a3, d_bias.reshape(1, D_HYENA))


def _hyena(p_all, short_w, short_b, d_bias, kpack, mats, n_batch, length, row_block0):
    gc, gs, gst = mats
    a3 = _short_conv(p_all, short_w, short_b, n_batch, length, row_block0)
    pa, pb = _conv_fwd(gc, gs, a3, 0, kpack, 0, n_batch)
    z1 = _conv_inv(gc, gst, pa, pb, a3, 0, a3, 1, d_bias[0], n_batch)
    pa, pb = _conv_fwd(gc, gs, z1, 0, kpack, 1, n_batch)
    return _conv_inv(gc, gst, pa, pb, z1, 0, a3, 2, d_bias[1], n_batch)


def _out_proj_kernel(x_ref, gm_ref, na_ref, hyl_ref, hyc_ref, w_ref, mod_ref, o_ref, *, n_lat_tiles):
    is_lat = pl.program_id(0) < n_lat_tiles
    hy = jnp.where(is_lat, hyl_ref[...], hyc_ref[...])
    y = jnp.dot(gm_ref[...].astype(BF16), w_ref[0:D_GMLP, :], preferred_element_type=F32)
    y = y + jnp.dot(na_ref[...].astype(BF16), w_ref[D_GMLP:D_GMLP + D_NA, :], preferred_element_type=F32)
    y = y + jnp.dot(hy.astype(BF16), w_ref[D_GMLP + D_NA:, :], preferred_element_type=F32)
    o_ref[...] = x_ref[...] + mod_ref[2:3, :] * y


def _out_proj(x_all, gm, na, hy_lat, hy_ctx, w_bf, mods_l, n_batch, seq, n_tiles):
    d = x_all.shape[1]
    n_lat_tiles = n_batch * seq // TOKEN_TILE
    n_ctx_tiles = hy_ctx.shape[0] // TOKEN_TILE
    row = lambda w: pl.BlockSpec((TOKEN_TILE, w), lambda i: (i, 0))
    return pl.pallas_call(
        functools.partial(_out_proj_kernel, n_lat_tiles=n_lat_tiles),
        grid=(n_tiles,),
        in_specs=[
            row(d), row(D_GMLP), row(D_NA),
            pl.BlockSpec((TOKEN_TILE, D_HYENA), lambda i: (jnp.minimum(i, n_lat_tiles - 1), 0)),
            pl.BlockSpec((TOKEN_TILE, D_HYENA),
                         lambda i: (jnp.clip(i - n_lat_tiles, 0, n_ctx_tiles - 1), 0)),
            pl.BlockSpec(w_bf.shape, lambda i: (0, 0)),
            pl.BlockSpec((None, N_MOD, d), _mod_index(seq // TOKEN_TILE, n_batch)),
        ],
        out_specs=row(d),
        out_shape=jax.ShapeDtypeStruct((n_tiles * TOKEN_TILE, d), F32),
        compiler_params=_cparams("arbitrary"),
        name="out_proj",
    )(x_all, gm, na, hy_lat, hy_ctx, w_bf, mods_l)


def _router_kernel(x_ref, g_ref, mod_ref, wr_ref, br_ref, tril_ref, h_ref, route_ref, count_ref, run_ref):
    @pl.when(pl.program_id(0) == 0)
    def _init():
        run_ref[...] = jnp.zeros_like(run_ref)

    mod = mod_ref[...]
    h = _rms_rows(x_ref[...]) * g_ref[...]
    h = h * (1.0 + mod[4:5]) + mod[3:4]
    logits = jnp.dot(h, wr_ref[...], preferred_element_type=F32, precision=HI) + br_ref[...]
    lane = lax.broadcasted_iota(jnp.int32, logits.shape, 1)
    neg = -jnp.inf
    is_group = (lane >= MOE_EXPERTS) & (lane < MOE_EXPERTS + MOE_GROUPS)
    lg = jnp.where(is_group, logits, neg)
    mg = jnp.max(lg, axis=-1, keepdims=True)
    g_p = 1.0 / jnp.sum(jnp.exp(lg - mg), axis=-1, keepdims=True)
    g_idx = jnp.min(jnp.where(lg == mg, lane, 2 * LANES), axis=-1, keepdims=True) - MOE_EXPERTS
    in_group = (lane >= g_idx * MOE_EXPERTS_PER_GROUP) & (lane < (g_idx + 1) * MOE_EXPERTS_PER_GROUP)
    le = jnp.where(in_group, logits, neg)
    me = jnp.max(le, axis=-1, keepdims=True)
    pe = jnp.exp(le - me)
    pe = pe / jnp.sum(pe, axis=-1, keepdims=True)
    p1 = jnp.max(pe, axis=-1, keepdims=True)
    i1 = jnp.min(jnp.where(in_group & (pe == p1), lane, 2 * LANES), axis=-1, keepdims=True)
    pe2 = jnp.where(in_group & (lane != i1), pe, neg)
    p2 = jnp.max(pe2, axis=-1, keepdims=True)
    i2 = jnp.min(jnp.where(pe2 == p2, lane, 2 * LANES), axis=-1, keepdims=True)
    tot = p1 + p2
    w_lo = g_p * jnp.where(i1 < i2, p1, p2) / tot
    w_hi = g_p * jnp.where(i1 < i2, p2, p1) / tot
    a = jnp.minimum(i1, i2) - g_idx * MOE_EXPERTS_PER_GROUP
    b = jnp.maximum(i1, i2) - g_idx * MOE_EXPERTS_PER_GROUP
    pair = a * (MOE_EXPERTS_PER_GROUP - 1) - ((a * (a - 1)) >> 1) + (b - a - 1)
    bucket = g_idx * MOE_PAIRS + pair
    onehot = lane == bucket
    prefix = jnp.dot(tril_ref[...], onehot.astype(BF16), preferred_element_type=F32)
    run = run_ref[...]
    rank = jnp.sum(jnp.where(onehot, prefix + run, 0.0), axis=-1, keepdims=True) - 1.0
    run = run + prefix[TOKEN_TILE - 1:TOKEN_TILE, :]
    run_ref[...] = run
    count_ref[...] = run
    route = jnp.where(lane == 0, bucket.astype(F32),
                      jnp.where(lane == 1, rank,
                                jnp.where(lane == 2, w_lo, jnp.where(lane == 3, w_hi, 0.0))))
    route_ref[...] = route
    half = D_MODEL // 2
    bits = pltpu.bitcast(h.astype(BF16).astype(F32), jnp.uint32)
    words = (bits[:, :half] >> 16) | (bits[:, half:] & jnp.uint32(0xFFFF0000))
    h_ref[:, :half] = pltpu.bitcast(words, jnp.int32)
    h_ref[:, half:half + LANES] = pltpu.bitcast(route, jnp.int32)
    h_ref[:, half + LANES:] = jnp.zeros((TOKEN_TILE, half - LANES), jnp.int32)


def _router(x_all, gain, mods_l, w_router, b_router, n_batch, seq, n_tiles):
    d = x_all.shape[1]
    tril = np.tril(np.ones((TOKEN_TILE, TOKEN_TILE), np.float32))
    return pl.pallas_call(
        _router_kernel,
        grid=(n_tiles,),
        in_specs=[
            pl.BlockSpec((TOKEN_TILE, d), lambda i: (i, 0)),
            pl.BlockSpec((1, d), lambda i: (0, 0)),
            pl.BlockSpec((None, N_MOD, d), _mod_index(seq // TOKEN_TILE, n_batch)),
            pl.BlockSpec((d, LANES), lambda i: (0, 0)),
            pl.BlockSpec((1, LANES), lambda i: (0, 0)),
            pl.BlockSpec((TOKEN_TILE, TOKEN_TILE), lambda i: (0, 0)),
        ],
        out_specs=[pl.BlockSpec((TOKEN_TILE, d), lambda i: (i, 0)),
                   pl.BlockSpec((TOKEN_TILE, LANES), lambda i: (i, 0)),
                   pl.BlockSpec((1, LANES), lambda i: (0, 0))],
        out_shape=[jax.ShapeDtypeStruct((n_tiles * TOKEN_TILE, d), jnp.int32),
                   jax.ShapeDtypeStruct((n_tiles * TOKEN_TILE, LANES), F32),
                   jax.ShapeDtypeStruct((1, LANES), F32)],
        scratch_shapes=[pltpu.VMEM((1, LANES), F32)],
        compiler_params=_cparams("arbitrary"),
        name="moe_router",
    )(x_all, gain.reshape(1, d), mods_l, w_router, b_router, jnp.asarray(tril, BF16))


def _row_copy(table_hbm, dst, sem, src_row, dst_row):
    return pltpu.make_async_copy(table_hbm.at[pl.ds(src_row, 1)], dst.at[pl.ds(dst_row, 1)], sem)


def _start_row_gather(idx_ref, base, table_hbm, dst, sem, unrolled):
    rows = dst.shape[0]
    if unrolled:
        for r in range(rows):
            _row_copy(table_hbm, dst, sem, idx_ref[base + r], r).start()
    else:
        def issue(r, carry):
            _row_copy(table_hbm, dst, sem, idx_ref[base + r], r).start()
            return carry

        lax.fori_loop(0, rows, issue, 0, unroll=GATHER_UNROLL)


def _wait_row_gather(table_hbm, dst, sem):
    def drain(r, carry):
        _row_copy(table_hbm, dst, sem, 0, r).wait()
        return carry

    lax.fori_loop(0, dst.shape[0], drain, 0, unroll=GATHER_UNROLL)


def _gather_residual_kernel(idx_ref, y_hbm, x_ref, mod_ref, o_ref, buf, sem):
    _start_row_gather(idx_ref, pl.program_id(0) * buf.shape[0], y_hbm, buf, sem, unrolled=False)
    _wait_row_gather(y_hbm, buf, sem)
    o_ref[...] = x_ref[...] + mod_ref[5:6, :] * buf[...]


def _gather_residual(x_all, y_sorted, dest, mods_l, n_batch, seq, n_tiles):
    d = x_all.shape[1]
    row = pl.BlockSpec((TOKEN_TILE, d), lambda i, idx: (i, 0))
    mod_index = _mod_index(seq // TOKEN_TILE, n_batch)
    return pl.pallas_call(
        _gather_residual_kernel,
        grid_spec=pltpu.PrefetchScalarGridSpec(
            num_scalar_prefetch=1,
            grid=(n_tiles,),
            in_specs=[pl.BlockSpec(memory_space=pl.ANY), row,
                      pl.BlockSpec((None, N_MOD, d), lambda i, idx: mod_index(i))],
            out_specs=row,
            scratch_shapes=[pltpu.VMEM((TOKEN_TILE, d), F32), pltpu.SemaphoreType.DMA],
        ),
        out_shape=jax.ShapeDtypeStruct((n_tiles * TOKEN_TILE, d), F32),
        compiler_params=_cparams("arbitrary"),
        name="moe_combine",
    )(dest, y_sorted, x_all, mods_l)


def _bucket_experts():
    lo, hi = [], []
    for g in range(MOE_GROUPS):
        for a in range(MOE_EXPERTS_PER_GROUP):
            for b in range(a + 1, MOE_EXPERTS_PER_GROUP):
                lo.append(g * MOE_EXPERTS_PER_GROUP + a)
                hi.append(g * MOE_EXPERTS_PER_GROUP + b)
    return np.asarray(lo, np.int32), np.asarray(hi, np.int32)


def _sorted_experts_kernel(src_ref, lo_ref, hi_ref, nact_ref, tab_hbm, wgl_ref, wul_ref, wdl_ref,
                           wgh_ref, wuh_ref, wdh_ref, o_ref, xbuf, sems):
    del lo_ref, hi_ref
    t = pl.program_id(0)
    n_active = nact_ref[0]
    slot = t % 2
    half = D_MODEL // 2

    @pl.when(t == 0)
    def _first_tile():
        _start_row_gather(src_ref, 0, tab_hbm, xbuf.at[0], sems.at[0], unrolled=False)

    @pl.when(t <= n_active)
    def _retire():
        _wait_row_gather(tab_hbm, xbuf.at[slot], sems.at[slot])

    @pl.when(t >= n_active)
    def _unused_tile():
        o_ref[...] = jnp.zeros_like(o_ref)

    @pl.when(t < n_active)
    def _active_tile():
        _start_row_gather(src_ref, (t + 1) * MOE_SORT_TILE, tab_hbm, xbuf.at[1 - slot], sems.at[1 - slot],
                          unrolled=True)
        words = pltpu.bitcast(xbuf[slot, :, :half], jnp.uint32)
        x_lo = pltpu.bitcast(words << 16, F32)
        x_hi = pltpu.bitcast(words & jnp.uint32(0xFFFF0000), F32)
        xb = jnp.concatenate([x_lo, x_hi], axis=-1).astype(BF16)
        r = pltpu.bitcast(xbuf[slot, :, half:half + LANES], F32)
        acc = None
        for wg_ref, wu_ref, wd_ref, lane in ((wgl_ref, wul_ref, wdl_ref, 2), (wgh_ref, wuh_ref, wdh_ref, 3)):
            gate = jnp.dot(xb, wg_ref[...], preferred_element_type=F32)
            up = jnp.dot(xb, wu_ref[...], preferred_element_type=F32)
            act = _silu(gate) * up * r[:, lane:lane + 1]
            part = jnp.dot(act.astype(BF16), wd_ref[...], preferred_element_type=F32)
            acc = part if acc is None else acc + part
        o_ref[...] = acc


def _sorted_experts(table, src, tile_lo, tile_hi, n_active, wg, wu, wd):
    n_rows = src.shape[0]
    d, f = wg.shape[-2:]
    lo_w = lambda shape: pl.BlockSpec(shape, lambda t, src, lo, hi, n: (lo[t], 0, 0))
    hi_w = lambda shape: pl.BlockSpec(shape, lambda t, src, lo, hi, n: (hi[t], 0, 0))
    return pl.pallas_call(
        _sorted_experts_kernel,
        grid_spec=pltpu.PrefetchScalarGridSpec(
            num_scalar_prefetch=4,
            grid=(n_rows // MOE_SORT_TILE,),
            in_specs=[pl.BlockSpec(memory_space=pl.ANY),
                      lo_w((None, d, f)), lo_w((None, d, f)), lo_w((None, f, d)),
                      hi_w((None, d, f)), hi_w((None, d, f)), hi_w((None, f, d))],
            out_specs=pl.BlockSpec((MOE_SORT_TILE, d), lambda t, src, lo, hi, n: (t, 0)),
            scratch_shapes=[pltpu.VMEM((2, MOE_SORT_TILE, table.shape[1]), table.dtype),
                            pltpu.SemaphoreType.DMA((2,))],
        ),
        out_shape=jax.ShapeDtypeStruct((n_rows, d), F32),
        compiler_params=_cparams("arbitrary"),
        name="moe_experts",
    )(src, tile_lo, tile_hi, n_active, table, wg, wu, wd, wg, wu, wd)


def _moe(x_all, gain, mods_l, w_router, b_router, wg, wu, wd, n_batch, seq, n_tiles):
    n = n_tiles * TOKEN_TILE
    table, route, counts = _router(x_all, gain, mods_l, w_router, b_router, n_batch, seq, n_tiles)

    n_sorted_tiles = (n + MOE_BUCKETS * (MOE_SORT_TILE - 1)) // MOE_SORT_TILE + 1
    n_sorted = n_sorted_tiles * MOE_SORT_TILE
    counts = counts[0, :MOE_BUCKETS].astype(jnp.int32)
    bucket_tiles = (counts + MOE_SORT_TILE - 1) // MOE_SORT_TILE
    tile_end = jnp.cumsum(bucket_tiles)
    row_start = (tile_end - bucket_tiles) * MOE_SORT_TILE
    bucket = route[:, 0].astype(jnp.int32)
    rank = route[:, 1].astype(jnp.int32)
    dest = jnp.sum(jnp.where(bucket[:, None] == jnp.arange(MOE_BUCKETS)[None, :], row_start[None, :], 0),
                   axis=1) + rank
    src = jnp.zeros((n_sorted,), jnp.int32).at[dest].set(jnp.arange(n, dtype=jnp.int32))
    tiles = jnp.arange(n_sorted_tiles, dtype=jnp.int32)
    tile_bucket = jnp.sum(jnp.minimum(tiles, tile_end[-1] - 1)[:, None] >= tile_end[None, :], axis=1)
    tile_bucket = jnp.minimum(tile_bucket, MOE_BUCKETS - 1)
    lo_ids, hi_ids = _bucket_experts()
    tile_lo = jnp.asarray(lo_ids)[tile_bucket]
    tile_hi = jnp.asarray(hi_ids)[tile_bucket]

    n_active = tile_end[-1:].astype(jnp.int32)
    y_sorted = _sorted_experts(table, src, tile_lo, tile_hi, n_active, wg, wu, wd)
    return _gather_residual(x_all, y_sorted, dest, mods_l, n_batch, seq, n_tiles)


def kernel(x, c, ctx, c_ctx, w_ada, b_ada, g_mix, g_ffn, w_in, w_out, gmlp_v_gain, gmlp_ws, gmlp_bs,
           na_q_gain, na_k_gain, na_rpb, hy_short_w, hy_short_b, hy_w1, hy_b1, hy_w2, hy_b2, hy_w3,
           hy_freq, hy_bias, moe_w_rg, moe_b_rg, moe_w_re, moe_b_re, moe_w_gate, moe_w_up, moe_w_down):
    n_batch, seq, d = x.shape
    ctx_len = ctx.shape[1]
    depth = w_ada.shape[0]
    n_lat = n_batch * seq
    n_ctx = n_batch * ctx_len
    assert d == D_MODEL and seq % TOKEN_TILE == 0 and n_ctx % TOKEN_TILE == 0
    assert seq % GMLP_CHUNK == 0 and ctx_len % GMLP_CHUNK == 0

    pad_rows = -(n_batch + 1) % 8
    cs = jnp.concatenate([c, c_ctx[None, :], jnp.zeros((pad_rows, d), F32)], axis=0)
    mods = _modulation(cs, w_ada, b_ada)[:, :n_batch + 1].reshape(depth, n_batch + 1, N_MOD, d)

    x_all = jnp.concatenate([x.reshape(n_lat, d), ctx.reshape(n_ctx, d)], axis=0)
    n_all_tiles = (n_lat + n_ctx) // TOKEN_TILE
    n_lat_tiles = n_lat // TOKEN_TILE

    mats_lat = _dft_matrices(seq)
    mats_ctx = _dft_matrices(ctx_len)
    pos_lat = _hyena_positions(seq)
    pos_ctx = _hyena_positions(ctx_len)

    w_in_bf = w_in.astype(BF16)
    w_out_bf = w_out.astype(BF16)
    wg_bf = moe_w_gate.astype(BF16).reshape(depth, MOE_EXPERTS, d, MOE_HIDDEN)
    wu_bf = moe_w_up.astype(BF16).reshape(depth, MOE_EXPERTS, d, MOE_HIDDEN)
    wd_bf = moe_w_down.astype(BF16).reshape(depth, MOE_EXPERTS, MOE_HIDDEN, d)
    lane_pad = LANES - MOE_EXPERTS - MOE_GROUPS
    w_router = jnp.pad(jnp.concatenate([moe_w_re, moe_w_rg], axis=-1), ((0, 0), (0, 0), (0, lane_pad)))
    b_router = jnp.pad(jnp.concatenate([moe_b_re, moe_b_rg], axis=-1), ((0, 0), (0, lane_pad)))[:, None, :]
    w1_pad = jnp.pad(hy_w1, ((0, 0), (0, LANES - HYENA_EMB), (0, 0)))

    for l in range(depth):
        last = l == depth - 1
        mods_l = mods[l]
        n_tiles = n_lat_tiles if last else n_all_tiles

        p_all = _in_proj(x_all, g_mix[l], mods_l, w_in_bf[l], n_batch, seq)

        filt = (w1_pad[l], hy_b1[l], hy_w2[l], hy_b2[l], hy_w3[l], hy_freq[l])
        kpack_lat = _filter_dft(mats_lat[0], mats_lat[1], _hyena_filters(*pos_lat, *filt))
        hy_lat = _hyena(p_all, hy_short_w[l], hy_short_b[l], hy_bias[l], kpack_lat, mats_lat,
                        n_batch, seq, 0)
        if last:
            hy_ctx = hy_lat
            gm = _gmlp(p_all, n_lat, gmlp_v_gain[l], gmlp_ws[l], gmlp_bs[l].T)
        else:
            kpack_ctx = _filter_dft(mats_ctx[0], mats_ctx[1], _hyena_filters(*pos_ctx, *filt))
            hy_ctx = _hyena(p_all, hy_short_w[l], hy_short_b[l], hy_bias[l], kpack_ctx, mats_ctx,
                            n_batch, ctx_len, n_lat // ctx_len)
            gm = _gmlp(p_all, n_lat + n_ctx, gmlp_v_gain[l], gmlp_ws[l], gmlp_bs[l].T)

        na = _attention(p_all, na_rpb[l], na_q_gain[l], na_k_gain[l],
                        n_batch, seq, ctx_len, not last)

        x_all = _out_proj(x_all, gm, na, hy_lat, hy_ctx, w_out_bf[l], mods_l, n_batch, seq, n_tiles)

        x_all = _moe(x_all, g_ffn[l], mods_l, w_router[l], b_router[l], wg_bf[l], wu_bf[l], wd_bf[l],
                     n_batch, seq, n_tiles)

    return x_all[:n_lat].reshape(n_batch, seq, d)
```

```python
import functools
import math

import numpy as np
import jax
import jax.numpy as jnp
from jax import lax
from jax.experimental import pallas as pl
from jax.experimental.pallas import tpu as pltpu

F32 = jnp.float32
BF16 = jnp.bfloat16
HI = lax.Precision.HIGHEST

D_MODEL = 1024
GRID_W = 64
D_GMLP = D_MODEL // 4
D_NA = D_MODEL // 2
D_HYENA = D_MODEL // 4
D_IN = 2 * D_GMLP + 3 * D_NA + 3 * D_HYENA
Q_START = 2 * D_GMLP
KV_START = 2 * D_GMLP + D_NA
HY_START = 2 * D_GMLP + 3 * D_NA
GMLP_GROUPS = 4
GMLP_GROUP_DIM = D_GMLP // GMLP_GROUPS
GMLP_CHUNK = 128
NA_HEAD_DIM = 64
NA_HEADS = D_NA // NA_HEAD_DIM
NA_SCALE = NA_HEAD_DIM ** -0.5
NA_WIN_ROWS = 8
NA_WIN_COLS = 16
HYENA_ORDER = 2
HYENA_POS_BANDS = 16
HYENA_EMB = 1 + 2 * HYENA_POS_BANDS
HYENA_FILTER_HIDDEN = 64
HYENA_DECAY_TARGET = 1e-2
HYENA_FAST_DECAY = 0.3
HYENA_SLOW_DECAY = 1.5
MOE_GROUPS = 4
MOE_EXPERTS_PER_GROUP = 8
MOE_EXPERTS = MOE_GROUPS * MOE_EXPERTS_PER_GROUP
MOE_HIDDEN = 256
MOE_PAIRS = MOE_EXPERTS_PER_GROUP * (MOE_EXPERTS_PER_GROUP - 1) // 2
MOE_BUCKETS = MOE_GROUPS * MOE_PAIRS
N_MOD = 6
RMS_EPS = 1e-6
LN_EPS = 1e-5

LANES = 128
TOKEN_TILE = 512
NA_Q_ROWS = 4
NA_Q_BLOCK = NA_Q_ROWS * GRID_W
NA_BAND_ROWS = NA_WIN_ROWS + NA_Q_ROWS
DFT_TILE = 512
MOE_SORT_TILE = 128
GATHER_UNROLL = 8
MASK_VALUE = -1e30
VMEM_LIMIT = 56 * 1024 * 1024


def _cparams(*sem):
    return pltpu.CompilerParams(dimension_semantics=sem, vmem_limit_bytes=VMEM_LIMIT)


def _silu(x):
    return x * jax.nn.sigmoid(x)


def _rms_rows(x):
    return x * lax.rsqrt(jnp.mean(x * x, axis=-1, keepdims=True) + RMS_EPS)


def _mods_kernel(cs_ref, w_ref, b_ref, o_ref):
    s = _silu(cs_ref[...])
    o_ref[...] = jnp.dot(s, w_ref[...], preferred_element_type=F32, precision=HI) + b_ref[...]


def _modulation(cs, w_ada, b_ada):
    depth, d, nd = w_ada.shape
    rows = cs.shape[0]
    col = 1024
    return pl.pallas_call(
        _mods_kernel,
        grid=(depth, nd // col),
        in_specs=[
            pl.BlockSpec((rows, d), lambda l, j: (0, 0)),
            pl.BlockSpec((None, d, col), lambda l, j: (l, 0, j)),
            pl.BlockSpec((None, 1, col), lambda l, j: (l, 0, j)),
        ],
        out_specs=pl.BlockSpec((None, rows, col), lambda l, j: (l, 0, j)),
        out_shape=jax.ShapeDtypeStruct((depth, rows, nd), F32),
        compiler_params=_cparams("arbitrary", "arbitrary"),
        name="modulation",
    )(cs, w_ada, b_ada.reshape(depth, 1, nd))


def _in_proj_kernel(x_ref, g_ref, mod_ref, w_ref, o_ref):
    mod = mod_ref[...]
    h = _rms_rows(x_ref[...]) * g_ref[...]
    h = h * (1.0 + mod[1:2]) + mod[0:1]
    o_ref[...] = jnp.dot(h.astype(BF16), w_ref[...], preferred_element_type=F32)


def _mod_index(tiles_per_batch, n_batch):
    return lambda i: (jnp.minimum(i // tiles_per_batch, n_batch), 0, 0)


def _in_proj(x_all, gain, mods_l, w_bf, n_batch, seq):
    n, d = x_all.shape
    d_in = w_bf.shape[1]
    return pl.pallas_call(
        _in_proj_kernel,
        grid=(n // TOKEN_TILE,),
        in_specs=[
            pl.BlockSpec((TOKEN_TILE, d), lambda i: (i, 0)),
            pl.BlockSpec((1, d), lambda i: (0, 0)),
            pl.BlockSpec((None, N_MOD, d), _mod_index(seq // TOKEN_TILE, n_batch)),
            pl.BlockSpec((d, d_in), lambda i: (0, 0)),
        ],
        out_specs=pl.BlockSpec((TOKEN_TILE, d_in), lambda i: (i, 0)),
        out_shape=jax.ShapeDtypeStruct((n, d_in), F32),
        compiler_params=_cparams("arbitrary"),
        name="in_proj",
    )(x_all, gain.reshape(1, d), mods_l, w_bf)


def _group_avg_matrix(groups, width):
    return np.kron(np.eye(groups), np.full((width, width), 1.0 / width))


def _group_mean(t, avg):
    hi = t.astype(BF16)
    lo = (t - hi.astype(F32)).astype(BF16)
    return (jnp.dot(hi, avg, preferred_element_type=F32) + jnp.dot(lo, avg, preferred_element_type=F32))


def _gmlp_kernel(u_ref, v_ref, gain_ref, avg_ref, ws_ref, bs_ref, o_ref):
    avg = avg_ref[...]
    v = jax.nn.gelu(v_ref[...])
    v = v - _group_mean(v, avg)
    v = v * lax.rsqrt(_group_mean(v * v, avg) + LN_EPS) * gain_ref[...]
    vb = v.astype(BF16)
    bs = bs_ref[...]
    for c in range(u_ref.shape[0] // GMLP_CHUNK):
        rows = slice(c * GMLP_CHUNK, (c + 1) * GMLP_CHUNK)
        outs = []
        for g in range(GMLP_GROUPS):
            cols = slice(g * GMLP_GROUP_DIM, (g + 1) * GMLP_GROUP_DIM)
            s = jnp.dot(ws_ref[g], vb[rows, cols], preferred_element_type=F32) + bs[:, g:g + 1]
            outs.append(s)
        o_ref[rows, :] = jax.nn.gelu(u_ref[rows, :]) * jnp.concatenate(outs, axis=-1)


def _gmlp(p_all, n, v_gain, ws, bs_t):
    avg = _group_avg_matrix(GMLP_GROUPS, GMLP_GROUP_DIM)
    return pl.pallas_call(
        _gmlp_kernel,
        grid=(n // TOKEN_TILE,),
        in_specs=[
            pl.BlockSpec((TOKEN_TILE, D_GMLP), lambda i: (i, 0)),
            pl.BlockSpec((TOKEN_TILE, D_GMLP), lambda i: (i, 1)),
            pl.BlockSpec((1, D_GMLP), lambda i: (0, 0)),
            pl.BlockSpec((D_GMLP, D_GMLP), lambda i: (0, 0)),
            pl.BlockSpec((GMLP_GROUPS, GMLP_CHUNK, GMLP_CHUNK), lambda i: (0, 0, 0)),
            pl.BlockSpec((GMLP_CHUNK, GMLP_GROUPS), lambda i: (0, 0)),
        ],
        out_specs=pl.BlockSpec((TOKEN_TILE, D_GMLP), lambda i: (i, 0)),
        out_shape=jax.ShapeDtypeStruct((n, D_GMLP), F32),
        compiler_params=_cparams("arbitrary"),
        name="gmlp",
    )(p_all, p_all, v_gain.reshape(1, D_GMLP), jnp.asarray(avg, BF16), ws.astype(BF16), bs_t)


def _rpb_expand_kernel(rpb_ref, sel_ref, o_ref):
    o_ref[...] = jnp.dot(rpb_ref[...], sel_ref[...], preferred_element_type=F32, precision=HI)


def _na_geometry(grid_rows):
    variants, step_variant, band_start = [], [], []
    for r0 in range(0, grid_rows, NA_Q_ROWS):
        b0 = int(np.clip(r0 - NA_WIN_ROWS // 2, 0, grid_rows - NA_BAND_ROWS))
        geo = []
        for r in range(r0, r0 + NA_Q_ROWS):
            wr = int(np.clip(r - NA_WIN_ROWS // 2, 0, grid_rows - NA_WIN_ROWS))
            assert b0 <= wr and wr + NA_WIN_ROWS <= b0 + NA_BAND_ROWS
            geo.append((wr - b0, wr - r + NA_WIN_ROWS - 1))
        geo = tuple(geo)
        if geo not in variants:
            variants.append(geo)
        step_variant.append(variants.index(geo))
        band_start.append(b0)
    return variants, step_variant, band_start


def _na_bias_tables(rpb, variants):
    n_heads, n_dr, n_dc = rpb.shape
    qcol = np.arange(GRID_W)[:, None]
    kcol = np.arange(GRID_W)[None, :]
    win_c = np.clip(qcol - NA_WIN_COLS // 2, 0, GRID_W - NA_WIN_COLS)
    col_ok = (kcol >= win_c) & (kcol < win_c + NA_WIN_COLS)
    dc = np.clip(kcol - qcol + NA_WIN_COLS - 1, 0, 2 * NA_WIN_COLS - 2)
    dc_pad = -n_dc % 8
    sel = (np.arange(n_dc + dc_pad)[:, None] == dc.reshape(1, -1)).astype(np.float32)
    rpb2 = jnp.pad(rpb.reshape(n_heads * n_dr, n_dc), ((0, 0), (0, dc_pad)))
    toep = pl.pallas_call(
        _rpb_expand_kernel,
        out_shape=jax.ShapeDtypeStruct((n_heads * n_dr, GRID_W * GRID_W), F32),
        name="rpb_expand",
    )(rpb2, jnp.asarray(sel))
    toep = toep.reshape(n_heads, n_dr, GRID_W, GRID_W)
    toep = jnp.where(col_ok[None, None], toep, MASK_VALUE)
    tabs = []
    for geo in variants:
        rows = []
        for a0, dr0 in geo:
            rows.append(jnp.pad(toep[:, dr0:dr0 + NA_WIN_ROWS],
                                ((0, 0), (a0, NA_BAND_ROWS - NA_WIN_ROWS - a0), (0, 0), (0, 0)),
                                constant_values=MASK_VALUE))
        tab = jnp.stack(rows, axis=1)
        tabs.append(jnp.transpose(tab, (0, 1, 3, 2, 4)).reshape(n_heads, NA_Q_BLOCK, NA_BAND_ROWS * GRID_W))
    return jnp.stack(tabs, axis=1)


def _store_heads(dst, rows, t, gain, avg):
    if gain is not None:
        t = t * lax.rsqrt(_group_mean(t * t, avg) + RMS_EPS) * gain
    for h in range(NA_HEADS):
        dst[h, rows, :] = t[:, h * NA_HEAD_DIM:(h + 1) * NA_HEAD_DIM].astype(BF16)


def _na_kernel(q_ref, k_ref, v_ref, kc_ref, vc_ref, bias_ref, qg_ref, kg_ref, avg_ref, o_ref,
               kn_s, vb_s, kcn_s, vcb_s, qn_s, o_s, *, n_lat_steps, step_variant, band_start):
    step = pl.program_id(1)
    kg = kg_ref[...]
    avg = avg_ref[...]
    nt = (((1,), (1,)), ((), ()))
    band_keys = NA_BAND_ROWS * GRID_W
    all_rows = slice(None)

    @pl.when(step == 0)
    def _prepare_keys():
        chunk = 256

        def body(c, carry):
            rows = pl.ds(pl.multiple_of(c * chunk, chunk), chunk)
            _store_heads(kn_s, rows, k_ref[rows, :], kg, avg)
            _store_heads(vb_s, rows, v_ref[rows, :], None, None)
            return carry

        lax.fori_loop(0, k_ref.shape[0] // chunk, body, 0)
        _store_heads(kcn_s, all_rows, kc_ref[...], kg, avg)
        _store_heads(vcb_s, all_rows, vc_ref[...], None, None)

    _store_heads(qn_s, all_rows, q_ref[...], qg_ref[...] * NA_SCALE, avg)

    def finish(h, scores, v_parts):
        m = jnp.max(scores, axis=-1, keepdims=True)
        p = jnp.exp(scores - m)
        denom = jnp.sum(p, axis=-1, keepdims=True)
        pb = p.astype(BF16)
        acc = None
        col = 0
        for v in v_parts:
            part = jnp.dot(pb[:, col:col + v.shape[0]], v, preferred_element_type=F32)
            acc = part if acc is None else acc + part
            col += v.shape[0]
        o_s[h] = acc / denom

    @pl.when(step < n_lat_steps)
    def _latent_queries():
        variant = jnp.int32(0)
        band0 = jnp.int32(0)
        for s_, (v_, b_) in enumerate(zip(step_variant, band_start)):
            variant = jnp.where(step == s_, v_, variant)
            band0 = jnp.where(step == s_, b_ * GRID_W, band0)
        krows = pl.ds(pl.multiple_of(band0, NA_Q_BLOCK), band_keys)

        def head_body(h, carry):
            qh = qn_s[h]
            s_w = lax.dot_general(qh, kn_s[h, krows, :], nt, preferred_element_type=F32)
            s_w = s_w + bias_ref[h, variant]
            s_c = lax.dot_general(qh, kcn_s[h], nt, preferred_element_type=F32)
            finish(h, jnp.concatenate([s_w, s_c], axis=-1), (vb_s[h, krows, :], vcb_s[h]))
            return carry

        lax.fori_loop(0, NA_HEADS, head_body, 0, unroll=2)

    @pl.when(step >= n_lat_steps)
    def _context_queries():
        def head_body(h, carry):
            s = lax.dot_general(qn_s[h], kcn_s[h], nt, preferred_element_type=F32)
            finish(h, s, (vcb_s[h],))
            return carry

        lax.fori_loop(0, NA_HEADS, head_body, 0)

    o_ref[...] = jnp.concatenate([o_s[h] for h in range(NA_HEADS)], axis=-1)


def _attention(p_all, rpb, q_gain, k_gain, n_batch, seq, ctx_len, with_ctx_queries):
    n = p_all.shape[0]
    assert ctx_len == NA_Q_BLOCK and seq % NA_Q_BLOCK == 0
    n_lat_steps = seq // NA_Q_BLOCK
    n_steps = n_lat_steps + (1 if with_ctx_queries else 0)
    ctx_block0 = n_batch * seq // ctx_len
    qcol, kcol, vcol = Q_START // D_NA, KV_START // D_NA, (KV_START + D_NA) // D_NA
    variants, step_variant, band_start = _na_geometry(seq // GRID_W)
    bias_tab = _na_bias_tables(rpb, variants)

    def q_index(col):
        return lambda b, s: (jnp.where(s < n_lat_steps, b * n_lat_steps + s, ctx_block0 + b), col)

    kern = functools.partial(_na_kernel, n_lat_steps=n_lat_steps, step_variant=tuple(step_variant),
                             band_start=tuple(band_start))
    head_major = lambda rows: pltpu.VMEM((NA_HEADS, rows, NA_HEAD_DIM), BF16)
    return pl.pallas_call(
        kern,
        grid=(n_batch, n_steps),
        in_specs=[
            pl.BlockSpec((NA_Q_BLOCK, D_NA), q_index(qcol)),
            pl.BlockSpec((seq, D_NA), lambda b, s: (b, kcol)),
            pl.BlockSpec((seq, D_NA), lambda b, s: (b, vcol)),
            pl.BlockSpec((ctx_len, D_NA), lambda b, s: (ctx_block0 + b, kcol)),
            pl.BlockSpec((ctx_len, D_NA), lambda b, s: (ctx_block0 + b, vcol)),
            pl.BlockSpec(bias_tab.shape, lambda b, s: (0, 0, 0, 0), pipeline_mode=pl.Buffered(1)),
            pl.BlockSpec((1, D_NA), lambda b, s: (0, 0)),
            pl.BlockSpec((1, D_NA), lambda b, s: (0, 0)),
            pl.BlockSpec((D_NA, D_NA), lambda b, s: (0, 0)),
        ],
        out_specs=pl.BlockSpec((NA_Q_BLOCK, D_NA), q_index(0)),
        out_shape=jax.ShapeDtypeStruct((n if with_ctx_queries else n_batch * seq, D_NA), F32),
        scratch_shapes=[
            head_major(seq), head_major(seq), head_major(ctx_len), head_major(ctx_len),
            head_major(NA_Q_BLOCK),
            pltpu.VMEM((NA_HEADS, NA_Q_BLOCK, NA_HEAD_DIM), F32),
        ],
        compiler_params=_cparams("arbitrary", "arbitrary"),
        name="attention",
    )(p_all, p_all, p_all, p_all, p_all, bias_tab,
      jnp.tile(q_gain.reshape(1, NA_HEAD_DIM), (1, NA_HEADS)),
      jnp.tile(k_gain.reshape(1, NA_HEAD_DIM), (1, NA_HEADS)),
      jnp.asarray(_group_avg_matrix(NA_HEADS, NA_HEAD_DIM), BF16))


def _dft_matrices(length):
    idx = jnp.arange(length, dtype=jnp.int32)
    step = 64
    t_hi = jnp.arange(length // step, dtype=jnp.int32) * step
    t_lo = jnp.arange(step, dtype=jnp.int32)
    ang_hi = ((idx[:, None] * t_hi[None, :]) % (2 * length)).astype(F32) * (math.pi / length)
    ang_lo = ((idx[:, None] * t_lo[None, :]) % (2 * length)).astype(F32) * (math.pi / length)
    c_hi, s_hi = jnp.cos(ang_hi)[:, :, None], jnp.sin(ang_hi)[:, :, None]
    c_lo, s_lo = jnp.cos(ang_lo)[:, None, :], jnp.sin(ang_lo)[:, None, :]
    gc = (c_hi * c_lo - s_hi * s_lo).reshape(length, length)
    gs = (s_hi * c_lo + c_hi * s_lo).reshape(length, length)
    nyq = jnp.where(idx % 2 == 0, 1.0, -1.0).astype(F32)
    gs = jnp.where(idx[:, None] == 0, nyq[None, :], gs)
    return gc.astype(BF16), gs.astype(BF16), gs.T.astype(BF16)


def _hyena_positions(length):
    t = jnp.linspace(0.0, 1.0, length, dtype=F32)[:, None]
    w = 2.0 * math.pi * jnp.arange(length, dtype=F32)[:, None] / length
    f = jnp.linspace(1e-4, HYENA_POS_BANDS - 1, HYENA_POS_BANDS, dtype=F32)[None, :]
    z = jnp.concatenate([t, jnp.cos(f * w), -jnp.sin(f * w)], axis=-1)
    z = jnp.pad(z, ((0, 0), (0, LANES - HYENA_EMB)))
    min_decay = math.log(HYENA_DECAY_TARGET) / HYENA_SLOW_DECAY
    max_decay = math.log(HYENA_DECAY_TARGET) / HYENA_FAST_DECAY
    deltas = jnp.abs(jnp.linspace(min_decay, max_decay, D_HYENA, dtype=F32))[None, :]
    return z, jnp.exp(-t * deltas)


def _filter_kernel(z_ref, decay_ref, w1_ref, b1_ref, w2_ref, b2_ref, w3_ref, freq_ref, o_ref):
    freq = freq_ref[...]
    hdn = jnp.dot(z_ref[...], w1_ref[...], preferred_element_type=F32, precision=HI) + b1_ref[...]
    hdn = jnp.sin(freq[0:1] * hdn)
    hdn = jnp.dot(hdn, w2_ref[...], preferred_element_type=F32, precision=HI) + b2_ref[...]
    hdn = jnp.sin(freq[1:2] * hdn)
    h = jnp.dot(hdn, w3_ref[...], preferred_element_type=F32, precision=HI)
    decay = decay_ref[...]
    first_row = lax.broadcasted_iota(jnp.int32, decay.shape, 0) == 0
    outs = []
    for n in range(HYENA_ORDER):
        base = 2 * n * D_HYENA
        hf = h[:, base:base + D_HYENA] * decay
        hb = h[:, base + D_HYENA:base + 2 * D_HYENA] * decay
        norm = jnp.sum(jnp.abs(hf), axis=0, keepdims=True) + jnp.sum(jnp.abs(hb), axis=0, keepdims=True)
        outs.append(hf / norm)
        outs.append(jnp.where(first_row, 0.0, hb / norm))
    o_ref[...] = jnp.concatenate(outs, axis=-1)


def _hyena_filters(z, decay, w1p, b1, w2, b2, w3, freq):
    length = z.shape[0]
    full = lambda a: pl.BlockSpec(a.shape, lambda i: (0,) * a.ndim)
    args = (z, decay, w1p, b1.reshape(1, -1), w2, b2.reshape(1, -1), w3, freq)
    return pl.pallas_call(
        _filter_kernel,
        grid=(1,),
        in_specs=[full(a) for a in args],
        out_specs=pl.BlockSpec((length, 2 * HYENA_ORDER * D_HYENA), lambda i: (0, 0)),
        out_shape=jax.ShapeDtypeStruct((length, 2 * HYENA_ORDER * D_HYENA), F32),
        compiler_params=_cparams("arbitrary"),
        name="hyena_filter",
    )(*args)


def _filter_dft_kernel(gc_ref, gs_ref, h_ref, o_ref, *, length):
    hb = h_ref[...].astype(BF16)
    fa = jnp.dot(gc_ref[...], hb, preferred_element_type=F32)
    fb = jnp.dot(gs_ref[...], hb, preferred_element_type=F32)
    rows = lax.broadcasted_iota(jnp.int32, (fa.shape[0], D_HYENA), 0) + pl.program_id(0) * fa.shape[0]
    dc_row = rows == 0
    inv_n = 1.0 / (2 * length)
    outs = []
    for n in range(HYENA_ORDER):
        base = 2 * n * D_HYENA
        f_sl = slice(base, base + D_HYENA)
        b_sl = slice(base + D_HYENA, base + 2 * D_HYENA)
        kr = fa[:, f_sl] + fa[:, b_sl]
        ki = fb[:, b_sl] - fb[:, f_sl]
        k_nyq = fb[:, f_sl] + fb[:, b_sl]
        outs.append(jnp.where(dc_row, kr * inv_n, 2.0 * inv_n * kr))
        outs.append(jnp.where(dc_row, 0.0, 2.0 * inv_n * ki))
        outs.append(jnp.where(dc_row, 0.0, -2.0 * inv_n * ki))
        outs.append(jnp.where(dc_row, k_nyq * inv_n, 2.0 * inv_n * kr))
    o_ref[...] = jnp.concatenate(outs, axis=-1)


def _filter_dft(gc, gs, hfilt):
    length = gc.shape[0]
    tile = min(DFT_TILE, length)
    width = 4 * HYENA_ORDER * D_HYENA
    return pl.pallas_call(
        functools.partial(_filter_dft_kernel, length=length),
        grid=(length // tile,),
        in_specs=[
            pl.BlockSpec((tile, length), lambda j: (j, 0)),
            pl.BlockSpec((tile, length), lambda j: (j, 0)),
            pl.BlockSpec(hfilt.shape, lambda j: (0, 0)),
        ],
        out_specs=pl.BlockSpec((tile, width), lambda j: (j, 0)),
        out_shape=jax.ShapeDtypeStruct((length, width), F32),
        compiler_params=_cparams("arbitrary"),
        name="hyena_filter_dft",
    )(gc, gs, hfilt)


def _short_conv_kernel(a0_ref, a1_ref, a2_ref, w_ref, b_ref, o_ref):
    w = w_ref[...]
    b = b_ref[...]
    length = a0_ref.shape[0]
    rows = lax.broadcasted_iota(jnp.int32, (length, D_HYENA), 0)
    for j, a_ref in enumerate((a0_ref, a1_ref, a2_ref)):
        cols = slice(j * D_HYENA, (j + 1) * D_HYENA)
        a = a_ref[...]
        prev = jnp.where(rows == 0, 0.0, pltpu.roll(a, 1, 0))
        nxt = jnp.where(rows == length - 1, 0.0, pltpu.roll(a, length - 1, 0))
        o_ref[:, cols] = prev * w[0:1, cols] + a * w[1:2, cols] + nxt * w[2:3, cols] + b[:, cols]


def _short_conv(p_all, short_w, short_b, n_batch, length, row_block0):
    c0 = HY_START // D_HYENA
    spec = lambda j: pl.BlockSpec((length, D_HYENA), lambda b: (row_block0 + b, c0 + j))
    return pl.pallas_call(
        _short_conv_kernel,
        grid=(n_batch,),
        in_specs=[spec(0), spec(1), spec(2),
                  pl.BlockSpec((3, 3 * D_HYENA), lambda b: (0, 0)),
                  pl.BlockSpec((1, 3 * D_HYENA), lambda b: (0, 0))],
        out_specs=pl.BlockSpec((length, 3 * D_HYENA), lambda b: (b, 0)),
        out_shape=jax.ShapeDtypeStruct((n_batch * length, 3 * D_HYENA), F32),
        compiler_params=_cparams("arbitrary"),
        name="hyena_short_conv",
    )(p_all, p_all, p_all, short_w, short_b.reshape(1, -1))


def _conv_fwd_kernel(gc_ref, gs_ref, u_ref, k_ref, pa_ref, pb_ref):
    u = u_ref[...].astype(BF16)
    a = jnp.dot(gc_ref[...], u, preferred_element_type=F32)
    b = jnp.dot(gs_ref[...], u, preferred_element_type=F32)
    k = k_ref[...]
    c = D_HYENA
    pa_ref[...] = (a * k[:, 0:c] + b * k[:, c:2 * c]).astype(BF16)
    pb_ref[...] = (a * k[:, 2 * c:3 * c] + b * k[:, 3 * c:4 * c]).astype(BF16)


def _conv_fwd(gc, gs, u, u_col, kpack, order, n_batch):
    length = gc.shape[0]
    tile = min(DFT_TILE, length)
    nt = length // tile
    out = jax.ShapeDtypeStruct((n_batch * length, D_HYENA), BF16)
    return pl.pallas_call(
        _conv_fwd_kernel,
        grid=(nt, n_batch),
        in_specs=[
            pl.BlockSpec((tile, length), lambda j, b: (j, 0)),
            pl.BlockSpec((tile, length), lambda j, b: (j, 0)),
            pl.BlockSpec((length, D_HYENA), lambda j, b: (b, u_col)),
            pl.BlockSpec((tile, 4 * D_HYENA), lambda j, b: (j, order)),
        ],
        out_specs=[pl.BlockSpec((tile, D_HYENA), lambda j, b: (b * nt + j, 0))] * 2,
        out_shape=[out, out],
        compiler_params=_cparams("arbitrary", "arbitrary"),
        name="hyena_conv_fwd",
    )(gc, gs, u, kpack)


def _conv_inv_kernel(gc_ref, gst_ref, pa_ref, pb_ref, z_ref, gate_ref, d_ref, o_ref):
    y = jnp.dot(gc_ref[...], pa_ref[...], preferred_element_type=F32)
    y = y + jnp.dot(gst_ref[...], pb_ref[...], preferred_element_type=F32)
    o_ref[...] = gate_ref[...] * (y + d_ref[...] * z_ref[...])


def _conv_inv(gc, gst, pa, pb, z_prev, z_col, a3, gate_col, d_bias, n_batch):
    length = gc.shape[0]
    tile = min(DFT_TILE, length)
    nt = length // tile
    return pl.pallas_call(
        _conv_inv_kernel,
        grid=(nt, n_batch),
        in_specs=[
            pl.BlockSpec((tile, length), lambda j, b: (j, 0)),
            pl.BlockSpec((tile, length), lambda j, b: (j, 0)),
            pl.BlockSpec((length, D_HYENA), lambda j, b: (b, 0)),
            pl.BlockSpec((length, D_HYENA), lambda j, b: (b, 0)),
            pl.BlockSpec((tile, D_HYENA), lambda j, b: (b * nt + j, z_col)),
            pl.BlockSpec((tile, D_HYENA), lambda j, b: (b * nt + j, gate_col)),
            pl.BlockSpec((1, D_HYENA), lambda j, b: (0, 0)),
        ],
        out_specs=pl.BlockSpec((tile, D_HYENA), lambda j, b: (b * nt + j, 0)),
        out_shape=jax.ShapeDtypeStruct((n_batch * length, D_HYENA), F32),
        compiler_params=_cparams("arbitrary", "arbitrary"),
        name="hyena_conv_inv",
    )(gc, gst, pa, pb, z_prev, a3, d_bias.reshape(1, D_HYENA))


def _hyena(p_all, short_w, short_b, d_bias, kpack, mats, n_batch, length, row_block0):
    gc, gs, gst = mats
    a3 = _short_conv(p_all, short_w, short_b, n_batch, length, row_block0)
    pa, pb = _conv_fwd(gc, gs, a3, 0, kpack, 0, n_batch)
    z1 = _conv_inv(gc, gst, pa, pb, a3, 0, a3, 1, d_bias[0], n_batch)
    pa, pb = _conv_fwd(gc, gs, z1, 0, kpack, 1, n_batch)
    return _conv_inv(gc, gst, pa, pb, z1, 0, a3, 2, d_bias[1], n_batch)


def _out_proj_kernel(x_ref, gm_ref, na_ref, hyl_ref, hyc_ref, w_ref, mod_ref, o_ref, *, n_lat_tiles):
    is_lat = pl.program_id(0) < n_lat_tiles
    hy = jnp.where(is_lat, hyl_ref[...], hyc_ref[...])
    y = jnp.dot(gm_ref[...].astype(BF16), w_ref[0:D_GMLP, :], preferred_element_type=F32)
    y = y + jnp.dot(na_ref[...].astype(BF16), w_ref[D_GMLP:D_GMLP + D_NA, :], preferred_element_type=F32)
    y = y + jnp.dot(hy.astype(BF16), w_ref[D_GMLP + D_NA:, :], preferred_element_type=F32)
    o_ref[...] = x_ref[...] + mod_ref[2:3, :] * y


def _out_proj(x_all, gm, na, hy_lat, hy_ctx, w_bf, mods_l, n_batch, seq, n_tiles):
    d = x_all.shape[1]
    n_lat_tiles = n_batch * seq // TOKEN_TILE
    n_ctx_tiles = hy_ctx.shape[0] // TOKEN_TILE
    row = lambda w: pl.BlockSpec((TOKEN_TILE, w), lambda i: (i, 0))
    return pl.pallas_call(
        functools.partial(_out_proj_kernel, n_lat_tiles=n_lat_tiles),
        grid=(n_tiles,),
        in_specs=[
            row(d), row(D_GMLP), row(D_NA),
            pl.BlockSpec((TOKEN_TILE, D_HYENA), lambda i: (jnp.minimum(i, n_lat_tiles - 1), 0)),
            pl.BlockSpec((TOKEN_TILE, D_HYENA),
                         lambda i: (jnp.clip(i - n_lat_tiles, 0, n_ctx_tiles - 1), 0)),
            pl.BlockSpec(w_bf.shape, lambda i: (0, 0)),
            pl.BlockSpec((None, N_MOD, d), _mod_index(seq // TOKEN_TILE, n_batch)),
        ],
        out_specs=row(d),
        out_shape=jax.ShapeDtypeStruct((n_tiles * TOKEN_TILE, d), F32),
        compiler_params=_cparams("arbitrary"),
        name="out_proj",
    )(x_all, gm, na, hy_lat, hy_ctx, w_bf, mods_l)


def _router_kernel(x_ref, g_ref, mod_ref, wr_ref, br_ref, tril_ref, h_ref, route_ref, count_ref, run_ref):
    @pl.when(pl.program_id(0) == 0)
    def _init():
        run_ref[...] = jnp.zeros_like(run_ref)

    mod = mod_ref[...]
    h = _rms_rows(x_ref[...]) * g_ref[...]
    h = h * (1.0 + mod[4:5]) + mod[3:4]
    logits = jnp.dot(h, wr_ref[...], preferred_element_type=F32, precision=HI) + br_ref[...]
    lane = lax.broadcasted_iota(jnp.int32, logits.shape, 1)
    neg = -jnp.inf
    is_group = (lane >= MOE_EXPERTS) & (lane < MOE_EXPERTS + MOE_GROUPS)
    lg = jnp.where(is_group, logits, neg)
    mg = jnp.max(lg, axis=-1, keepdims=True)
    g_p = 1.0 / jnp.sum(jnp.exp(lg - mg), axis=-1, keepdims=True)
    g_idx = jnp.min(jnp.where(lg == mg, lane, 2 * LANES), axis=-1, keepdims=True) - MOE_EXPERTS
    in_group = (lane >= g_idx * MOE_EXPERTS_PER_GROUP) & (lane < (g_idx + 1) * MOE_EXPERTS_PER_GROUP)
    le = jnp.where(in_group, logits, neg)
    me = jnp.max(le, axis=-1, keepdims=True)
    pe = jnp.exp(le - me)
    pe = pe / jnp.sum(pe, axis=-1, keepdims=True)
    p1 = jnp.max(pe, axis=-1, keepdims=True)
    i1 = jnp.min(jnp.where(in_group & (pe == p1), lane, 2 * LANES), axis=-1, keepdims=True)
    pe2 = jnp.where(in_group & (lane != i1), pe, neg)
    p2 = jnp.max(pe2, axis=-1, keepdims=True)
    i2 = jnp.min(jnp.where(pe2 == p2, lane, 2 * LANES), axis=-1, keepdims=True)
    tot = p1 + p2
    w_lo = g_p * jnp.where(i1 < i2, p1, p2) / tot
    w_hi = g_p * jnp.where(i1 < i2, p2, p1) / tot
    a = jnp.minimum(i1, i2) - g_idx * MOE_EXPERTS_PER_GROUP
    b = jnp.maximum(i1, i2) - g_idx * MOE_EXPERTS_PER_GROUP
    pair = a * (MOE_EXPERTS_PER_GROUP - 1) - ((a * (a - 1)) >> 1) + (b - a - 1)
    bucket = g_idx * MOE_PAIRS + pair
    onehot = lane == bucket
    prefix = jnp.dot(tril_ref[...], onehot.astype(BF16), preferred_element_type=F32)
    run = run_ref[...]
    rank = jnp.sum(jnp.where(onehot, prefix + run, 0.0), axis=-1, keepdims=True) - 1.0
    run = run + prefix[TOKEN_TILE - 1:TOKEN_TILE, :]
    run_ref[...] = run
    count_ref[...] = run
    route = jnp.where(lane == 0, bucket.astype(F32),
                      jnp.where(lane == 1, rank,
                                jnp.where(lane == 2, w_lo, jnp.where(lane == 3, w_hi, 0.0))))
    route_ref[...] = route
    half = D_MODEL // 2
    bits = pltpu.bitcast(h.astype(BF16).astype(F32), jnp.uint32)
    words = (bits[:, :half] >> 16) | (bits[:, half:] & jnp.uint32(0xFFFF0000))
    h_ref[:, :half] = pltpu.bitcast(words, jnp.int32)
    h_ref[:, half:half + LANES] = pltpu.bitcast(route, jnp.int32)
    h_ref[:, half + LANES:] = jnp.zeros((TOKEN_TILE, half - LANES), jnp.int32)


def _router(x_all, gain, mods_l, w_router, b_router, n_batch, seq, n_tiles):
    d = x_all.shape[1]
    tril = np.tril(np.ones((TOKEN_TILE, TOKEN_TILE), np.float32))
    return pl.pallas_call(
        _router_kernel,
        grid=(n_tiles,),
        in_specs=[
            pl.BlockSpec((TOKEN_TILE, d), lambda i: (i, 0)),
            pl.BlockSpec((1, d), lambda i: (0, 0)),
            pl.BlockSpec((None, N_MOD, d), _mod_index(seq // TOKEN_TILE, n_batch)),
            pl.BlockSpec((d, LANES), lambda i: (0, 0)),
            pl.BlockSpec((1, LANES), lambda i: (0, 0)),
            pl.BlockSpec((TOKEN_TILE, TOKEN_TILE), lambda i: (0, 0)),
        ],
        out_specs=[pl.BlockSpec((TOKEN_TILE, d), lambda i: (i, 0)),
                   pl.BlockSpec((TOKEN_TILE, LANES), lambda i: (i, 0)),
                   pl.BlockSpec((1, LANES), lambda i: (0, 0))],
        out_shape=[jax.ShapeDtypeStruct((n_tiles * TOKEN_TILE, d), jnp.int32),
                   jax.ShapeDtypeStruct((n_tiles * TOKEN_TILE, LANES), F32),
                   jax.ShapeDtypeStruct((1, LANES), F32)],
        scratch_shapes=[pltpu.VMEM((1, LANES), F32)],
        compiler_params=_cparams("arbitrary"),
        name="moe_router",
    )(x_all, gain.reshape(1, d), mods_l, w_router, b_router, jnp.asarray(tril, BF16))


def _row_copy(table_hbm, dst, sem, src_row, dst_row):
    return pltpu.make_async_copy(table_hbm.at[pl.ds(src_row, 1)], dst.at[pl.ds(dst_row, 1)], sem)


def _start_row_gather(idx_ref, base, table_hbm, dst, sem, unrolled):
    rows = dst.shape[0]
    if unrolled:
        for r in range(rows):
            _row_copy(table_hbm, dst, sem, idx_ref[base + r], r).start()
    else:
        def issue(r, carry):
            _row_copy(table_hbm, dst, sem, idx_ref[base + r], r).start()
            return carry

        lax.fori_loop(0, rows, issue, 0, unroll=GATHER_UNROLL)


def _wait_row_gather(table_hbm, dst, sem):
    def drain(r, carry):
        _row_copy(table_hbm, dst, sem, 0, r).wait()
        return carry

    lax.fori_loop(0, dst.shape[0], drain, 0, unroll=GATHER_UNROLL)


def _gather_residual_kernel(idx_ref, y_hbm, x_ref, mod_ref, o_ref, buf, sem):
    _start_row_gather(idx_ref, pl.program_id(0) * buf.shape[0], y_hbm, buf, sem, unrolled=False)
    _wait_row_gather(y_hbm, buf, sem)
    o_ref[...] = x_ref[...] + mod_ref[5:6, :] * buf[...]


def _gather_residual(x_all, y_sorted, dest, mods_l, n_batch, seq, n_tiles):
    d = x_all.shape[1]
    row = pl.BlockSpec((TOKEN_TILE, d), lambda i, idx: (i, 0))
    mod_index = _mod_index(seq // TOKEN_TILE, n_batch)
    return pl.pallas_call(
        _gather_residual_kernel,
        grid_spec=pltpu.PrefetchScalarGridSpec(
            num_scalar_prefetch=1,
            grid=(n_tiles,),
            in_specs=[pl.BlockSpec(memory_space=pl.ANY), row,
                      pl.BlockSpec((None, N_MOD, d), lambda i, idx: mod_index(i))],
            out_specs=row,
            scratch_shapes=[pltpu.VMEM((TOKEN_TILE, d), F32), pltpu.SemaphoreType.DMA],
        ),
        out_shape=jax.ShapeDtypeStruct((n_tiles * TOKEN_TILE, d), F32),
        compiler_params=_cparams("arbitrary"),
        name="moe_combine",
    )(dest, y_sorted, x_all, mods_l)


def _bucket_experts():
    lo, hi = [], []
    for g in range(MOE_GROUPS):
        for a in range(MOE_EXPERTS_PER_GROUP):
            for b in range(a + 1, MOE_EXPERTS_PER_GROUP):
                lo.append(g * MOE_EXPERTS_PER_GROUP + a)
                hi.append(g * MOE_EXPERTS_PER_GROUP + b)
    return np.asarray(lo, np.int32), np.asarray(hi, np.int32)


def _sorted_experts_kernel(src_ref, lo_ref, hi_ref, nact_ref, tab_hbm, wgl_ref, wul_ref, wdl_ref,
                           wgh_ref, wuh_ref, wdh_ref, o_ref, xbuf, sems):
    del lo_ref, hi_ref
    t = pl.program_id(0)
    n_active = nact_ref[0]
    slot = t % 2
    half = D_MODEL // 2

    @pl.when(t == 0)
    def _first_tile():
        _start_row_gather(src_ref, 0, tab_hbm, xbuf.at[0], sems.at[0], unrolled=False)

    @pl.when(t <= n_active)
    def _retire():
        _wait_row_gather(tab_hbm, xbuf.at[slot], sems.at[slot])

    @pl.when(t >= n_active)
    def _unused_tile():
        o_ref[...] = jnp.zeros_like(o_ref)

    @pl.when(t < n_active)
    def _active_tile():
        _start_row_gather(src_ref, (t + 1) * MOE_SORT_TILE, tab_hbm, xbuf.at[1 - slot], sems.at[1 - slot],
                          unrolled=True)
        words = pltpu.bitcast(xbuf[slot, :, :half], jnp.uint32)
        x_lo = pltpu.bitcast(words << 16, F32)
        x_hi = pltpu.bitcast(words & jnp.uint32(0xFFFF0000), F32)
        xb = jnp.concatenate([x_lo, x_hi], axis=-1).astype(BF16)
        r = pltpu.bitcast(xbuf[slot, :, half:half + LANES], F32)
        acc = None
        for wg_ref, wu_ref, wd_ref, lane in ((wgl_ref, wul_ref, wdl_ref, 2), (wgh_ref, wuh_ref, wdh_ref, 3)):
            gate = jnp.dot(xb, wg_ref[...], preferred_element_type=F32)
            up = jnp.dot(xb, wu_ref[...], preferred_element_type=F32)
            act = _silu(gate) * up * r[:, lane:lane + 1]
            part = jnp.dot(act.astype(BF16), wd_ref[...], preferred_element_type=F32)
            acc = part if acc is None else acc + part
        o_ref[...] = acc


def _sorted_experts(table, src, tile_lo, tile_hi, n_active, wg, wu, wd):
    n_rows = src.shape[0]
    d, f = wg.shape[-2:]
    lo_w = lambda shape: pl.BlockSpec(shape, lambda t, src, lo, hi, n: (lo[t], 0, 0))
    hi_w = lambda shape: pl.BlockSpec(shape, lambda t, src, lo, hi, n: (hi[t], 0, 0))
    return pl.pallas_call(
        _sorted_experts_kernel,
        grid_spec=pltpu.PrefetchScalarGridSpec(
            num_scalar_prefetch=4,
            grid=(n_rows // MOE_SORT_TILE,),
            in_specs=[pl.BlockSpec(memory_space=pl.ANY),
                      lo_w((None, d, f)), lo_w((None, d, f)), lo_w((None, f, d)),
                      hi_w((None, d, f)), hi_w((None, d, f)), hi_w((None, f, d))],
            out_specs=pl.BlockSpec((MOE_SORT_TILE, d), lambda t, src, lo, hi, n: (t, 0)),
            scratch_shapes=[pltpu.VMEM((2, MOE_SORT_TILE, table.shape[1]), table.dtype),
                            pltpu.SemaphoreType.DMA((2,))],
        ),
        out_shape=jax.ShapeDtypeStruct((n_rows, d), F32),
        compiler_params=_cparams("arbitrary"),
        name="moe_experts",
    )(src, tile_lo, tile_hi, n_active, table, wg, wu, wd, wg, wu, wd)


def _moe(x_all, gain, mods_l, w_router, b_router, wg, wu, wd, n_batch, seq, n_tiles):
    n = n_tiles * TOKEN_TILE
    table, route, counts = _router(x_all, gain, mods_l, w_router, b_router, n_batch, seq, n_tiles)

    n_sorted_tiles = (n + MOE_BUCKETS * (MOE_SORT_TILE - 1)) // MOE_SORT_TILE + 1
    n_sorted = n_sorted_tiles * MOE_SORT_TILE
    counts = counts[0, :MOE_BUCKETS].astype(jnp.int32)
    bucket_tiles = (counts + MOE_SORT_TILE - 1) // MOE_SORT_TILE
    tile_end = jnp.cumsum(bucket_tiles)
    row_start = (tile_end - bucket_tiles) * MOE_SORT_TILE
    bucket = route[:, 0].astype(jnp.int32)
    rank = route[:, 1].astype(jnp.int32)
    dest = jnp.sum(jnp.where(bucket[:, None] == jnp.arange(MOE_BUCKETS)[None, :], row_start[None, :], 0),
                   axis=1) + rank
    src = (jnp.arange(n_sorted, dtype=jnp.int32) % n).at[dest].set(jnp.arange(n, dtype=jnp.int32))
    tiles = jnp.arange(n_sorted_tiles, dtype=jnp.int32)
    tile_bucket = jnp.sum(jnp.minimum(tiles, tile_end[-1] - 1)[:, None] >= tile_end[None, :], axis=1)
    tile_bucket = jnp.minimum(tile_bucket, MOE_BUCKETS - 1)
    lo_ids, hi_ids = _bucket_experts()
    tile_lo = jnp.asarray(lo_ids)[tile_bucket]
    tile_hi = jnp.asarray(hi_ids)[tile_bucket]

    n_active = tile_end[-1:].astype(jnp.int32)
    y_sorted = _sorted_experts(table, src, tile_lo, tile_hi, n_active, wg, wu, wd)
    return _gather_residual(x_all, y_sorted, dest, mods_l, n_batch, seq, n_tiles)


def kernel(x, c, ctx, c_ctx, w_ada, b_ada, g_mix, g_ffn, w_in, w_out, gmlp_v_gain, gmlp_ws, gmlp_bs,
           na_q_gain, na_k_gain, na_rpb, hy_short_w, hy_short_b, hy_w1, hy_b1, hy_w2, hy_b2, hy_w3,
           hy_freq, hy_bias, moe_w_rg, moe_b_rg, moe_w_re, moe_b_re, moe_w_gate, moe_w_up, moe_w_down):
    n_batch, seq, d = x.shape
    ctx_len = ctx.shape[1]
    depth = w_ada.shape[0]
    n_lat = n_batch * seq
    n_ctx = n_batch * ctx_len
    assert d == D_MODEL and seq % TOKEN_TILE == 0 and n_ctx % TOKEN_TILE == 0
    assert seq % GMLP_CHUNK == 0 and ctx_len % GMLP_CHUNK == 0

    pad_rows = -(n_batch + 1) % 8
    cs = jnp.concatenate([c, c_ctx[None, :], jnp.zeros((pad_rows, d), F32)], axis=0)
    mods = _modulation(cs, w_ada, b_ada)[:, :n_batch + 1].reshape(depth, n_batch + 1, N_MOD, d)

    x_all = jnp.concatenate([x.reshape(n_lat, d), ctx.reshape(n_ctx, d)], axis=0)
    n_all_tiles = (n_lat + n_ctx) // TOKEN_TILE
    n_lat_tiles = n_lat // TOKEN_TILE

    mats_lat = _dft_matrices(seq)
    mats_ctx = _dft_matrices(ctx_len)
    pos_lat = _hyena_positions(seq)
    pos_ctx = _hyena_positions(ctx_len)

    w_in_bf = w_in.astype(BF16)
    w_out_bf = w_out.astype(BF16)
    wg_bf = moe_w_gate.astype(BF16).reshape(depth, MOE_EXPERTS, d, MOE_HIDDEN)
    wu_bf = moe_w_up.astype(BF16).reshape(depth, MOE_EXPERTS, d, MOE_HIDDEN)
    wd_bf = moe_w_down.astype(BF16).reshape(depth, MOE_EXPERTS, MOE_HIDDEN, d)
    lane_pad = LANES - MOE_EXPERTS - MOE_GROUPS
    w_router = jnp.pad(jnp.concatenate([moe_w_re, moe_w_rg], axis=-1), ((0, 0), (0, 0), (0, lane_pad)))
    b_router = jnp.pad(jnp.concatenate([moe_b_re, moe_b_rg], axis=-1), ((0, 0), (0, lane_pad)))[:, None, :]
    w1_pad = jnp.pad(hy_w1, ((0, 0), (0, LANES - HYENA_EMB), (0, 0)))

    for l in range(depth):
        last = l == depth - 1
        mods_l = mods[l]
        n_tiles = n_lat_tiles if last else n_all_tiles

        p_all = _in_proj(x_all, g_mix[l], mods_l, w_in_bf[l], n_batch, seq)

        filt = (w1_pad[l], hy_b1[l], hy_w2[l], hy_b2[l], hy_w3[l], hy_freq[l])
        kpack_lat = _filter_dft(mats_lat[0], mats_lat[1], _hyena_filters(*pos_lat, *filt))
        hy_lat = _hyena(p_all, hy_short_w[l], hy_short_b[l], hy_bias[l], kpack_lat, mats_lat,
                        n_batch, seq, 0)
        if last:
            hy_ctx = hy_lat
            gm = _gmlp(p_all, n_lat, gmlp_v_gain[l], gmlp_ws[l], gmlp_bs[l].T)
        else:
            kpack_ctx = _filter_dft(mats_ctx[0], mats_ctx[1], _hyena_filters(*pos_ctx, *filt))
            hy_ctx = _hyena(p_all, hy_short_w[l], hy_short_b[l], hy_bias[l], kpack_ctx, mats_ctx,
                            n_batch, ctx_len, n_lat // ctx_len)
            gm = _gmlp(p_all, n_lat + n_ctx, gmlp_v_gain[l], gmlp_ws[l], gmlp_bs[l].T)

        na = _attention(p_all, na_rpb[l], na_q_gain[l], na_k_gain[l],
                        n_batch, seq, ctx_len, not last)

        x_all = _out_proj(x_all, gm, na, hy_lat, hy_ctx, w_out_bf[l], mods_l, n_batch, seq, n_tiles)

        x_all = _moe(x_all, g_ffn[l], mods_l, w_router[l], b_router[l], wg_bf[l], wu_bf[l], wd_bf[l],
                     n_batch, seq, n_tiles)

    return x_all[:n_lat].reshape(n_batch, seq, d)
```

```python
import functools
import math

import numpy as np
import jax
import jax.numpy as jnp
from jax import lax
from jax.experimental import pallas as pl
from jax.experimental.pallas import tpu as pltpu

F32 = jnp.float32
BF16 = jnp.bfloat16
HI = lax.Precision.HIGHEST

D_MODEL = 1024
GRID_W = 64
D_GMLP = D_MODEL // 4
D_NA = D_MODEL // 2
D_HYENA = D_MODEL // 4
D_IN = 2 * D_GMLP + 3 * D_NA + 3 * D_HYENA
Q_START = 2 * D_GMLP
KV_START = 2 * D_GMLP + D_NA
HY_START = 2 * D_GMLP + 3 * D_NA
GMLP_GROUPS = 4
GMLP_GROUP_DIM = D_GMLP // GMLP_GROUPS
GMLP_CHUNK = 128
NA_HEAD_DIM = 64
NA_HEADS = D_NA // NA_HEAD_DIM
NA_SCALE = NA_HEAD_DIM ** -0.5
NA_WIN_ROWS = 8
NA_WIN_COLS = 16
HYENA_ORDER = 2
HYENA_POS_BANDS = 16
HYENA_EMB = 1 + 2 * HYENA_POS_BANDS
HYENA_FILTER_HIDDEN = 64
HYENA_DECAY_TARGET = 1e-2
HYENA_FAST_DECAY = 0.3
HYENA_SLOW_DECAY = 1.5
MOE_GROUPS = 4
MOE_EXPERTS_PER_GROUP = 8
MOE_EXPERTS = MOE_GROUPS * MOE_EXPERTS_PER_GROUP
MOE_HIDDEN = 256
MOE_PAIRS = MOE_EXPERTS_PER_GROUP * (MOE_EXPERTS_PER_GROUP - 1) // 2
MOE_BUCKETS = MOE_GROUPS * MOE_PAIRS
N_MOD = 6
RMS_EPS = 1e-6
LN_EPS = 1e-5

LANES = 128
TOKEN_TILE = 512
NA_Q_ROWS = 4
NA_Q_BLOCK = NA_Q_ROWS * GRID_W
NA_BAND_ROWS = NA_WIN_ROWS + NA_Q_ROWS
DFT_TILE = 512
CONV_TILE = 1024
MOE_SORT_TILE = 128
MOE_GATHER_AHEAD = 2
GATHER_UNROLL = 8
MASK_VALUE = -1e30
VMEM_LIMIT = 56 * 1024 * 1024


def _cparams(*sem):
    return pltpu.CompilerParams(dimension_semantics=sem, vmem_limit_bytes=VMEM_LIMIT)


def _silu(x):
    return x * jax.nn.sigmoid(x)


def _rms_rows(x):
    return x * lax.rsqrt(jnp.mean(x * x, axis=-1, keepdims=True) + RMS_EPS)


def _mods_kernel(cs_ref, w_ref, b_ref, o_ref):
    s = _silu(cs_ref[...])
    o_ref[...] = jnp.dot(s, w_ref[...], preferred_element_type=F32, precision=HI) + b_ref[...]


def _modulation(cs, w_ada, b_ada):
    depth, d, nd = w_ada.shape
    rows = cs.shape[0]
    col = 1024
    return pl.pallas_call(
        _mods_kernel,
        grid=(depth, nd // col),
        in_specs=[
            pl.BlockSpec((rows, d), lambda l, j: (0, 0)),
            pl.BlockSpec((None, d, col), lambda l, j: (l, 0, j)),
            pl.BlockSpec((None, 1, col), lambda l, j: (l, 0, j)),
        ],
        out_specs=pl.BlockSpec((None, rows, col), lambda l, j: (l, 0, j)),
        out_shape=jax.ShapeDtypeStruct((depth, rows, nd), F32),
        compiler_params=_cparams("arbitrary", "arbitrary"),
        name="modulation",
    )(cs, w_ada, b_ada.reshape(depth, 1, nd))


def _in_proj_kernel(x_ref, g_ref, mod_ref, w_ref, o_ref):
    mod = mod_ref[...]
    h = _rms_rows(x_ref[...]) * g_ref[...]
    h = h * (1.0 + mod[1:2]) + mod[0:1]
    o_ref[...] = jnp.dot(h.astype(BF16), w_ref[...], preferred_element_type=F32)


def _mod_index(tiles_per_batch, n_batch):
    return lambda i: (jnp.minimum(i // tiles_per_batch, n_batch), 0, 0)


def _in_proj(x_all, gain, mods_l, w_bf, n_batch, seq):
    n, d = x_all.shape
    d_in = w_bf.shape[1]
    return pl.pallas_call(
        _in_proj_kernel,
        grid=(n // TOKEN_TILE,),
        in_specs=[
            pl.BlockSpec((TOKEN_TILE, d), lambda i: (i, 0)),
            pl.BlockSpec((1, d), lambda i: (0, 0)),
            pl.BlockSpec((None, N_MOD, d), _mod_index(seq // TOKEN_TILE, n_batch)),
            pl.BlockSpec((d, d_in), lambda i: (0, 0)),
        ],
        out_specs=pl.BlockSpec((TOKEN_TILE, d_in), lambda i: (i, 0)),
        out_shape=jax.ShapeDtypeStruct((n, d_in), F32),
        compiler_params=_cparams("arbitrary"),
        name="in_proj",
    )(x_all, gain.reshape(1, d), mods_l, w_bf)


def _group_avg_matrix(groups, width):
    return np.kron(np.eye(groups), np.full((width, width), 1.0 / width))


def _group_mean(t, avg):
    hi = t.astype(BF16)
    lo = (t - hi.astype(F32)).astype(BF16)
    return (jnp.dot(hi, avg, preferred_element_type=F32) + jnp.dot(lo, avg, preferred_element_type=F32))


def _gmlp_kernel(u_ref, v_ref, gain_ref, avg_ref, ws_ref, bs_ref, o_ref):
    avg = avg_ref[...]
    v = jax.nn.gelu(v_ref[...])
    v = v - _group_mean(v, avg)
    v = v * lax.rsqrt(_group_mean(v * v, avg) + LN_EPS) * gain_ref[...]
    vb = v.astype(BF16)
    bs = bs_ref[...]
    for c in range(u_ref.shape[0] // GMLP_CHUNK):
        rows = slice(c * GMLP_CHUNK, (c + 1) * GMLP_CHUNK)
        outs = []
        for g in range(GMLP_GROUPS):
            cols = slice(g * GMLP_GROUP_DIM, (g + 1) * GMLP_GROUP_DIM)
            s = jnp.dot(ws_ref[g], vb[rows, cols], preferred_element_type=F32) + bs[:, g:g + 1]
            outs.append(s)
        o_ref[rows, :] = jax.nn.gelu(u_ref[rows, :]) * jnp.concatenate(outs, axis=-1)


def _gmlp(p_all, n, v_gain, ws, bs_t):
    avg = _group_avg_matrix(GMLP_GROUPS, GMLP_GROUP_DIM)
    return pl.pallas_call(
        _gmlp_kernel,
        grid=(n // TOKEN_TILE,),
        in_specs=[
            pl.BlockSpec((TOKEN_TILE, D_GMLP), lambda i: (i, 0)),
            pl.BlockSpec((TOKEN_TILE, D_GMLP), lambda i: (i, 1)),
            pl.BlockSpec((1, D_GMLP), lambda i: (0, 0)),
            pl.BlockSpec((D_GMLP, D_GMLP), lambda i: (0, 0)),
            pl.BlockSpec((GMLP_GROUPS, GMLP_CHUNK, GMLP_CHUNK), lambda i: (0, 0, 0)),
            pl.BlockSpec((GMLP_CHUNK, GMLP_GROUPS), lambda i: (0, 0)),
        ],
        out_specs=pl.BlockSpec((TOKEN_TILE, D_GMLP), lambda i: (i, 0)),
        out_shape=jax.ShapeDtypeStruct((n, D_GMLP), F32),
        compiler_params=_cparams("arbitrary"),
        name="gmlp",
    )(p_all, p_all, v_gain.reshape(1, D_GMLP), jnp.asarray(avg, BF16), ws.astype(BF16), bs_t)


def _rpb_expand_kernel(rpb_ref, sel_ref, o_ref):
    o_ref[...] = jnp.dot(rpb_ref[...], sel_ref[...], preferred_element_type=F32, precision=HI)


def _na_geometry(grid_rows):
    variants, step_variant, band_start = [], [], []
    for r0 in range(0, grid_rows, NA_Q_ROWS):
        b0 = int(np.clip(r0 - NA_WIN_ROWS // 2, 0, grid_rows - NA_BAND_ROWS))
        geo = []
        for r in range(r0, r0 + NA_Q_ROWS):
            wr = int(np.clip(r - NA_WIN_ROWS // 2, 0, grid_rows - NA_WIN_ROWS))
            assert b0 <= wr and wr + NA_WIN_ROWS <= b0 + NA_BAND_ROWS
            geo.append((wr - b0, wr - r + NA_WIN_ROWS - 1))
        geo = tuple(geo)
        if geo not in variants:
            variants.append(geo)
        step_variant.append(variants.index(geo))
        band_start.append(b0)
    return variants, step_variant, band_start


def _na_bias_tables(rpb, variants):
    n_heads, n_dr, n_dc = rpb.shape
    qcol = np.arange(GRID_W)[:, None]
    kcol = np.arange(GRID_W)[None, :]
    win_c = np.clip(qcol - NA_WIN_COLS // 2, 0, GRID_W - NA_WIN_COLS)
    col_ok = (kcol >= win_c) & (kcol < win_c + NA_WIN_COLS)
    dc = np.clip(kcol - qcol + NA_WIN_COLS - 1, 0, 2 * NA_WIN_COLS - 2)
    dc_pad = -n_dc % 8
    sel = (np.arange(n_dc + dc_pad)[:, None] == dc.reshape(1, -1)).astype(np.float32)
    rpb2 = jnp.pad(rpb.reshape(n_heads * n_dr, n_dc), ((0, 0), (0, dc_pad)))
    toep = pl.pallas_call(
        _rpb_expand_kernel,
        out_shape=jax.ShapeDtypeStruct((n_heads * n_dr, GRID_W * GRID_W), F32),
        name="rpb_expand",
    )(rpb2, jnp.asarray(sel))
    toep = toep.reshape(n_heads, n_dr, GRID_W, GRID_W)
    toep = jnp.where(col_ok[None, None], toep, MASK_VALUE)
    tabs = []
    for geo in variants:
        rows = []
        for a0, dr0 in geo:
            rows.append(jnp.pad(toep[:, dr0:dr0 + NA_WIN_ROWS],
                                ((0, 0), (a0, NA_BAND_ROWS - NA_WIN_ROWS - a0), (0, 0), (0, 0)),
                                constant_values=MASK_VALUE))
        tab = jnp.stack(rows, axis=1)
        tabs.append(jnp.transpose(tab, (0, 1, 3, 2, 4)).reshape(n_heads, NA_Q_BLOCK, NA_BAND_ROWS * GRID_W))
    return jnp.stack(tabs, axis=1)


def _store_heads(dst, rows, t, gain, avg):
    if gain is not None:
        t = t * lax.rsqrt(_group_mean(t * t, avg) + RMS_EPS) * gain
    extra = dst.shape[-1] - NA_HEAD_DIM
    if extra:
        lane = lax.broadcasted_iota(jnp.int32, (t.shape[0], extra), 1)
        unit = jnp.where(lane == 0, 1.0, 0.0).astype(BF16)
    for h in range(NA_HEADS):
        th = t[:, h * NA_HEAD_DIM:(h + 1) * NA_HEAD_DIM].astype(BF16)
        dst[h, rows, :] = jnp.concatenate([th, unit], axis=-1) if extra else th


def _na_kernel(q_ref, k_ref, v_ref, kc_ref, vc_ref, bias_ref, qg_ref, kg_ref, avg_ref, o_ref,
               kn_s, vb_s, kcn_s, vcb_s, qn_s, o_s, *, n_lat_steps, step_variant, band_start):
    step = pl.program_id(1)
    kg = kg_ref[...]
    avg = avg_ref[...]
    nt = (((1,), (1,)), ((), ()))
    band_keys = NA_BAND_ROWS * GRID_W
    all_rows = slice(None)

    @pl.when(step == 0)
    def _prepare_keys():
        chunk = 256

        def body(c, carry):
            rows = pl.ds(pl.multiple_of(c * chunk, chunk), chunk)
            _store_heads(kn_s, rows, k_ref[rows, :], kg, avg)
            _store_heads(vb_s, rows, v_ref[rows, :], None, None)
            return carry

        lax.fori_loop(0, k_ref.shape[0] // chunk, body, 0)
        _store_heads(kcn_s, all_rows, kc_ref[...], kg, avg)
        _store_heads(vcb_s, all_rows, vc_ref[...], None, None)

    _store_heads(qn_s, all_rows, q_ref[...], qg_ref[...] * NA_SCALE, avg)

    def finish(h, scores, v_parts):
        m = jnp.max(scores, axis=-1, keepdims=True)
        pb = jnp.exp((scores - m).astype(BF16))
        acc = None
        col = 0
        for v in v_parts:
            part = jnp.dot(pb[:, col:col + v.shape[0]], v, preferred_element_type=F32)
            acc = part if acc is None else acc + part
            col += v.shape[0]
        o_s[h] = acc[:, :NA_HEAD_DIM] / acc[:, NA_HEAD_DIM:NA_HEAD_DIM + 1]

    @pl.when(step < n_lat_steps)
    def _latent_queries():
        variant = jnp.int32(0)
        band0 = jnp.int32(0)
        for s_, (v_, b_) in enumerate(zip(step_variant, band_start)):
            variant = jnp.where(step == s_, v_, variant)
            band0 = jnp.where(step == s_, b_ * GRID_W, band0)
        krows = pl.ds(pl.multiple_of(band0, NA_Q_BLOCK), band_keys)

        def head_body(h, carry):
            qh = qn_s[h]
            s_w = lax.dot_general(qh, kn_s[h, krows, :], nt, preferred_element_type=F32)
            s_w = s_w + bias_ref[h, variant]
            s_c = lax.dot_general(qh, kcn_s[h], nt, preferred_element_type=F32)
            finish(h, jnp.concatenate([s_w, s_c], axis=-1), (vb_s[h, krows, :], vcb_s[h]))
            return carry

        lax.fori_loop(0, NA_HEADS, head_body, 0, unroll=2)

    @pl.when(step >= n_lat_steps)
    def _context_queries():
        def head_body(h, carry):
            s = lax.dot_general(qn_s[h], kcn_s[h], nt, preferred_element_type=F32)
            finish(h, s, (vcb_s[h],))
            return carry

        lax.fori_loop(0, NA_HEADS, head_body, 0)

    o_ref[...] = jnp.concatenate([o_s[h] for h in range(NA_HEADS)], axis=-1)


def _attention(p_all, rpb, q_gain, k_gain, n_batch, seq, ctx_len, with_ctx_queries):
    n = p_all.shape[0]
    assert ctx_len == NA_Q_BLOCK and seq % NA_Q_BLOCK == 0
    n_lat_steps = seq // NA_Q_BLOCK
    n_steps = n_lat_steps + (1 if with_ctx_queries else 0)
    ctx_block0 = n_batch * seq // ctx_len
    qcol, kcol, vcol = Q_START // D_NA, KV_START // D_NA, (KV_START + D_NA) // D_NA
    variants, step_variant, band_start = _na_geometry(seq // GRID_W)
    bias_tab = _na_bias_tables(rpb, variants)

    def q_index(col):
        return lambda b, s: (jnp.where(s < n_lat_steps, b * n_lat_steps + s, ctx_block0 + b), col)

    kern = functools.partial(_na_kernel, n_lat_steps=n_lat_steps, step_variant=tuple(step_variant),
                             band_start=tuple(band_start))
    head_major = lambda rows: pltpu.VMEM((NA_HEADS, rows, NA_HEAD_DIM), BF16)
    head_values = lambda rows: pltpu.VMEM((NA_HEADS, rows, LANES), BF16)
    return pl.pallas_call(
        kern,
        grid=(n_batch, n_steps),
        in_specs=[
            pl.BlockSpec((NA_Q_BLOCK, D_NA), q_index(qcol)),
            pl.BlockSpec((seq, D_NA), lambda b, s: (b, kcol)),
            pl.BlockSpec((seq, D_NA), lambda b, s: (b, vcol)),
            pl.BlockSpec((ctx_len, D_NA), lambda b, s: (ctx_block0 + b, kcol)),
            pl.BlockSpec((ctx_len, D_NA), lambda b, s: (ctx_block0 + b, vcol)),
            pl.BlockSpec(bias_tab.shape, lambda b, s: (0, 0, 0, 0), pipeline_mode=pl.Buffered(1)),
            pl.BlockSpec((1, D_NA), lambda b, s: (0, 0)),
            pl.BlockSpec((1, D_NA), lambda b, s: (0, 0)),
            pl.BlockSpec((D_NA, D_NA), lambda b, s: (0, 0)),
        ],
        out_specs=pl.BlockSpec((NA_Q_BLOCK, D_NA), q_index(0)),
        out_shape=jax.ShapeDtypeStruct((n if with_ctx_queries else n_batch * seq, D_NA), F32),
        scratch_shapes=[
            head_major(seq), head_values(seq), head_major(ctx_len), head_values(ctx_len),
            head_major(NA_Q_BLOCK),
            pltpu.VMEM((NA_HEADS, NA_Q_BLOCK, NA_HEAD_DIM), F32),
        ],
        compiler_params=_cparams("arbitrary", "arbitrary"),
        name="attention",
    )(p_all, p_all, p_all, p_all, p_all, bias_tab,
      jnp.tile(q_gain.reshape(1, NA_HEAD_DIM), (1, NA_HEADS)),
      jnp.tile(k_gain.reshape(1, NA_HEAD_DIM), (1, NA_HEADS)),
      jnp.asarray(_group_avg_matrix(NA_HEADS, NA_HEAD_DIM), BF16))


def _dft_matrices(length):
    idx = jnp.arange(length, dtype=jnp.int32)
    step = 64
    t_hi = jnp.arange(length // step, dtype=jnp.int32) * step
    t_lo = jnp.arange(step, dtype=jnp.int32)
    ang_hi = ((idx[:, None] * t_hi[None, :]) % (2 * length)).astype(F32) * (math.pi / length)
    ang_lo = ((idx[:, None] * t_lo[None, :]) % (2 * length)).astype(F32) * (math.pi / length)
    c_hi, s_hi = jnp.cos(ang_hi)[:, :, None], jnp.sin(ang_hi)[:, :, None]
    c_lo, s_lo = jnp.cos(ang_lo)[:, None, :], jnp.sin(ang_lo)[:, None, :]
    gc = (c_hi * c_lo - s_hi * s_lo).reshape(length, length)
    gs = (s_hi * c_lo + c_hi * s_lo).reshape(length, length)
    nyq = jnp.where(idx % 2 == 0, 1.0, -1.0).astype(F32)
    gs = jnp.where(idx[:, None] == 0, nyq[None, :], gs)
    return gc.astype(BF16), gs.astype(BF16), gs.T.astype(BF16)


def _hyena_positions(length):
    t = jnp.linspace(0.0, 1.0, length, dtype=F32)[:, None]
    w = 2.0 * math.pi * jnp.arange(length, dtype=F32)[:, None] / length
    f = jnp.linspace(1e-4, HYENA_POS_BANDS - 1, HYENA_POS_BANDS, dtype=F32)[None, :]
    z = jnp.concatenate([t, jnp.cos(f * w), -jnp.sin(f * w)], axis=-1)
    z = jnp.pad(z, ((0, 0), (0, LANES - HYENA_EMB)))
    min_decay = math.log(HYENA_DECAY_TARGET) / HYENA_SLOW_DECAY
    max_decay = math.log(HYENA_DECAY_TARGET) / HYENA_FAST_DECAY
    deltas = jnp.abs(jnp.linspace(min_decay, max_decay, D_HYENA, dtype=F32))[None, :]
    return z, jnp.exp(-t * deltas)


def _filter_kernel(z_ref, decay_ref, w1_ref, b1_ref, w2_ref, b2_ref, w3_ref, freq_ref, o_ref):
    freq = freq_ref[...]
    hdn = jnp.dot(z_ref[...], w1_ref[...], preferred_element_type=F32, precision=HI) + b1_ref[...]
    hdn = jnp.sin(freq[0:1] * hdn)
    hdn = jnp.dot(hdn, w2_ref[...], preferred_element_type=F32, precision=HI) + b2_ref[...]
    hdn = jnp.sin(freq[1:2] * hdn)
    h = jnp.dot(hdn, w3_ref[...], preferred_element_type=F32, precision=HI)
    decay = decay_ref[...]
    first_row = lax.broadcasted_iota(jnp.int32, decay.shape, 0) == 0
    outs = []
    for n in range(HYENA_ORDER):
        base = 2 * n * D_HYENA
        hf = h[:, base:base + D_HYENA] * decay
        hb = h[:, base + D_HYENA:base + 2 * D_HYENA] * decay
        norm = jnp.sum(jnp.abs(hf), axis=0, keepdims=True) + jnp.sum(jnp.abs(hb), axis=0, keepdims=True)
        outs.append(hf / norm)
        outs.append(jnp.where(first_row, 0.0, hb / norm))
    o_ref[...] = jnp.concatenate(outs, axis=-1)


def _hyena_filters(z, decay, w1p, b1, w2, b2, w3, freq):
    length = z.shape[0]
    full = lambda a: pl.BlockSpec(a.shape, lambda i: (0,) * a.ndim)
    args = (z, decay, w1p, b1.reshape(1, -1), w2, b2.reshape(1, -1), w3, freq)
    return pl.pallas_call(
        _filter_kernel,
        grid=(1,),
        in_specs=[full(a) for a in args],
        out_specs=pl.BlockSpec((length, 2 * HYENA_ORDER * D_HYENA), lambda i: (0, 0)),
        out_shape=jax.ShapeDtypeStruct((length, 2 * HYENA_ORDER * D_HYENA), F32),
        compiler_params=_cparams("arbitrary"),
        name="hyena_filter",
    )(*args)


def _filter_dft_kernel(gc_ref, gs_ref, h_ref, o_ref, *, length):
    hb = h_ref[...].astype(BF16)
    fa = jnp.dot(gc_ref[...], hb, preferred_element_type=F32)
    fb = jnp.dot(gs_ref[...], hb, preferred_element_type=F32)
    rows = lax.broadcasted_iota(jnp.int32, (fa.shape[0], D_HYENA), 0) + pl.program_id(0) * fa.shape[0]
    dc_row = rows == 0
    inv_n = 1.0 / (2 * length)
    outs = []
    for n in range(HYENA_ORDER):
        base = 2 * n * D_HYENA
        f_sl = slice(base, base + D_HYENA)
        b_sl = slice(base + D_HYENA, base + 2 * D_HYENA)
        kr = fa[:, f_sl] + fa[:, b_sl]
        ki = fb[:, b_sl] - fb[:, f_sl]
        k_nyq = fb[:, f_sl] + fb[:, b_sl]
        outs.append(jnp.where(dc_row, kr * inv_n, 2.0 * inv_n * kr))
        outs.append(jnp.where(dc_row, 0.0, 2.0 * inv_n * ki))
        outs.append(jnp.where(dc_row, 0.0, -2.0 * inv_n * ki))
        outs.append(jnp.where(dc_row, k_nyq * inv_n, 2.0 * inv_n * kr))
    o_ref[...] = jnp.concatenate(outs, axis=-1)


def _filter_dft(gc, gs, hfilt):
    length = gc.shape[0]
    tile = min(DFT_TILE, length)
    width = 4 * HYENA_ORDER * D_HYENA
    return pl.pallas_call(
        functools.partial(_filter_dft_kernel, length=length),
        grid=(length // tile,),
        in_specs=[
            pl.BlockSpec((tile, length), lambda j: (j, 0)),
            pl.BlockSpec((tile, length), lambda j: (j, 0)),
            pl.BlockSpec(hfilt.shape, lambda j: (0, 0)),
        ],
        out_specs=pl.BlockSpec((tile, width), lambda j: (j, 0)),
        out_shape=jax.ShapeDtypeStruct((length, width), F32),
        compiler_params=_cparams("arbitrary"),
        name="hyena_filter_dft",
    )(gc, gs, hfilt)


def _short_conv_kernel(a0_ref, a1_ref, a2_ref, w_ref, b_ref, o_ref):
    w = w_ref[...]
    b = b_ref[...]
    length = a0_ref.shape[0]
    rows = lax.broadcasted_iota(jnp.int32, (length, D_HYENA), 0)
    for j, a_ref in enumerate((a0_ref, a1_ref, a2_ref)):
        cols = slice(j * D_HYENA, (j + 1) * D_HYENA)
        a = a_ref[...]
        prev = jnp.where(rows == 0, 0.0, pltpu.roll(a, 1, 0))
        nxt = jnp.where(rows == length - 1, 0.0, pltpu.roll(a, length - 1, 0))
        o_ref[:, cols] = prev * w[0:1, cols] + a * w[1:2, cols] + nxt * w[2:3, cols] + b[:, cols]


def _short_conv(p_all, short_w, short_b, n_batch, length, row_block0):
    c0 = HY_START // D_HYENA
    spec = lambda j: pl.BlockSpec((length, D_HYENA), lambda b: (row_block0 + b, c0 + j))
    return pl.pallas_call(
        _short_conv_kernel,
        grid=(n_batch,),
        in_specs=[spec(0), spec(1), spec(2),
                  pl.BlockSpec((3, 3 * D_HYENA), lambda b: (0, 0)),
                  pl.BlockSpec((1, 3 * D_HYENA), lambda b: (0, 0))],
        out_specs=pl.BlockSpec((length, 3 * D_HYENA), lambda b: (b, 0)),
        out_shape=jax.ShapeDtypeStruct((n_batch * length, 3 * D_HYENA), F32),
        compiler_params=_cparams("arbitrary"),
        name="hyena_short_conv",
    )(p_all, p_all, p_all, short_w, short_b.reshape(1, -1))


def _conv_fwd_kernel(gc_ref, gs_ref, u_ref, k_ref, pa_ref, pb_ref):
    u = u_ref[...].astype(BF16)
    a = jnp.dot(gc_ref[...], u, preferred_element_type=F32)
    b = jnp.dot(gs_ref[...], u, preferred_element_type=F32)
    k = k_ref[...]
    c = D_HYENA
    pa_ref[...] = (a * k[:, 0:c] + b * k[:, c:2 * c]).astype(BF16)
    pb_ref[...] = (a * k[:, 2 * c:3 * c] + b * k[:, 3 * c:4 * c]).astype(BF16)


def _conv_fwd(gc, gs, u, u_col, kpack, order, n_batch):
    length = gc.shape[0]
    tile = min(CONV_TILE, length)
    nt = length // tile
    out = jax.ShapeDtypeStruct((n_batch * length, D_HYENA), BF16)
    return pl.pallas_call(
        _conv_fwd_kernel,
        grid=(nt, n_batch),
        in_specs=[
            pl.BlockSpec((tile, length), lambda j, b: (j, 0)),
            pl.BlockSpec((tile, length), lambda j, b: (j, 0)),
            pl.BlockSpec((length, D_HYENA), lambda j, b: (b, u_col)),
            pl.BlockSpec((tile, 4 * D_HYENA), lambda j, b: (j, order)),
        ],
        out_specs=[pl.BlockSpec((tile, D_HYENA), lambda j, b: (b * nt + j, 0))] * 2,
        out_shape=[out, out],
        compiler_params=_cparams("arbitrary", "arbitrary"),
        name="hyena_conv_fwd",
    )(gc, gs, u, kpack)


def _conv_inv_kernel(gc_ref, gst_ref, pa_ref, pb_ref, z_ref, gate_ref, d_ref, o_ref):
    y = jnp.dot(gc_ref[...], pa_ref[...], preferred_element_type=F32)
    y = y + jnp.dot(gst_ref[...], pb_ref[...], preferred_element_type=F32)
    o_ref[...] = gate_ref[...] * (y + d_ref[...] * z_ref[...])


def _conv_inv(gc, gst, pa, pb, z_prev, z_col, a3, gate_col, d_bias, n_batch):
    length = gc.shape[0]
    tile = min(CONV_TILE, length)
    nt = length // tile
    return pl.pallas_call(
        _conv_inv_kernel,
        grid=(nt, n_batch),
        in_specs=[
            pl.BlockSpec((tile, length), lambda j, b: (j, 0)),
            pl.BlockSpec((tile, length), lambda j, b: (j, 0)),
            pl.BlockSpec((length, D_HYENA), lambda j, b: (b, 0)),
            pl.BlockSpec((length, D_HYENA), lambda j, b: (b, 0)),
            pl.BlockSpec((tile, D_HYENA), lambda j, b: (b * nt + j, z_col)),
            pl.BlockSpec((tile, D_HYENA), lambda j, b: (b * nt + j, gate_col)),
            pl.BlockSpec((1, D_HYENA), lambda j, b: (0, 0)),
        ],
        out_specs=pl.BlockSpec((tile, D_HYENA), lambda j, b: (b * nt + j, 0)),
        out_shape=jax.ShapeDtypeStruct((n_batch * length, D_HYENA), F32),
        compiler_params=_cparams("arbitrary", "arbitrary"),
        name="hyena_conv_inv",
    )(gc, gst, pa, pb, z_prev, a3, d_bias.reshape(1, D_HYENA))


def _hyena(p_all, short_w, short_b, d_bias, kpack, mats, n_batch, length, row_block0):
    gc, gs, gst = mats
    a3 = _short_conv(p_all, short_w, short_b, n_batch, length, row_block0)
    pa, pb = _conv_fwd(gc, gs, a3, 0, kpack, 0, n_batch)
    z1 = _conv_inv(gc, gst, pa, pb, a3, 0, a3, 1, d_bias[0], n_batch)
    pa, pb = _conv_fwd(gc, gs, z1, 0, kpack, 1, n_batch)
    return _conv_inv(gc, gst, pa, pb, z1, 0, a3, 2, d_bias[1], n_batch)


def _out_proj_kernel(x_ref, gm_ref, na_ref, hyl_ref, hyc_ref, w_ref, mod_ref, o_ref, *, n_lat_tiles):
    is_lat = pl.program_id(0) < n_lat_tiles
    hy = jnp.where(is_lat, hyl_ref[...], hyc_ref[...])
    y = jnp.dot(gm_ref[...].astype(BF16), w_ref[0:D_GMLP, :], preferred_element_type=F32)
    y = y + jnp.dot(na_ref[...].astype(BF16), w_ref[D_GMLP:D_GMLP + D_NA, :], preferred_element_type=F32)
    y = y + jnp.dot(hy.astype(BF16), w_ref[D_GMLP + D_NA:, :], preferred_element_type=F32)
    o_ref[...] = x_ref[...] + mod_ref[2:3, :] * y


def _out_proj(x_all, gm, na, hy_lat, hy_ctx, w_bf, mods_l, n_batch, seq, n_tiles):
    d = x_all.shape[1]
    n_lat_tiles = n_batch * seq // TOKEN_TILE
    n_ctx_tiles = hy_ctx.shape[0] // TOKEN_TILE
    row = lambda w: pl.BlockSpec((TOKEN_TILE, w), lambda i: (i, 0))
    return pl.pallas_call(
        functools.partial(_out_proj_kernel, n_lat_tiles=n_lat_tiles),
        grid=(n_tiles,),
        in_specs=[
            row(d), row(D_GMLP), row(D_NA),
            pl.BlockSpec((TOKEN_TILE, D_HYENA), lambda i: (jnp.minimum(i, n_lat_tiles - 1), 0)),
            pl.BlockSpec((TOKEN_TILE, D_HYENA),
                         lambda i: (jnp.clip(i - n_lat_tiles, 0, n_ctx_tiles - 1), 0)),
            pl.BlockSpec(w_bf.shape, lambda i: (0, 0)),
            pl.BlockSpec((None, N_MOD, d), _mod_index(seq // TOKEN_TILE, n_batch)),
        ],
        out_specs=row(d),
        out_shape=jax.ShapeDtypeStruct((n_tiles * TOKEN_TILE, d), F32),
        compiler_params=_cparams("arbitrary"),
        name="out_proj",
    )(x_all, gm, na, hy_lat, hy_ctx, w_bf, mods_l)


def _router_kernel(x_ref, g_ref, mod_ref, wr_ref, br_ref, tril_ref, h_ref, route_ref, count_ref, run_ref):
    @pl.when(pl.program_id(0) == 0)
    def _init():
        run_ref[...] = jnp.zeros_like(run_ref)

    mod = mod_ref[...]
    h = _rms_rows(x_ref[...]) * g_ref[...]
    h = h * (1.0 + mod[4:5]) + mod[3:4]
    logits = jnp.dot(h, wr_ref[...], preferred_element_type=F32, precision=HI) + br_ref[...]
    lane = lax.broadcasted_iota(jnp.int32, logits.shape, 1)
    neg = -jnp.inf
    is_group = (lane >= MOE_EXPERTS) & (lane < MOE_EXPERTS + MOE_GROUPS)
    lg = jnp.where(is_group, logits, neg)
    mg = jnp.max(lg, axis=-1, keepdims=True)
    g_p = 1.0 / jnp.sum(jnp.exp(lg - mg), axis=-1, keepdims=True)
    g_idx = jnp.min(jnp.where(lg == mg, lane, 2 * LANES), axis=-1, keepdims=True) - MOE_EXPERTS
    in_group = (lane >= g_idx * MOE_EXPERTS_PER_GROUP) & (lane < (g_idx + 1) * MOE_EXPERTS_PER_GROUP)
    le = jnp.where(in_group, logits, neg)
    me = jnp.max(le, axis=-1, keepdims=True)
    pe = jnp.exp(le - me)
    pe = pe / jnp.sum(pe, axis=-1, keepdims=True)
    p1 = jnp.max(pe, axis=-1, keepdims=True)
    i1 = jnp.min(jnp.where(in_group & (pe == p1), lane, 2 * LANES), axis=-1, keepdims=True)
    pe2 = jnp.where(in_group & (lane != i1), pe, neg)
    p2 = jnp.max(pe2, axis=-1, keepdims=True)
    i2 = jnp.min(jnp.where(pe2 == p2, lane, 2 * LANES), axis=-1, keepdims=True)
    tot = p1 + p2
    w_lo = g_p * jnp.where(i1 < i2, p1, p2) / tot
    w_hi = g_p * jnp.where(i1 < i2, p2, p1) / tot
    a = jnp.minimum(i1, i2) - g_idx * MOE_EXPERTS_PER_GROUP
    b = jnp.maximum(i1, i2) - g_idx * MOE_EXPERTS_PER_GROUP
    pair = a * (MOE_EXPERTS_PER_GROUP - 1) - ((a * (a - 1)) >> 1) + (b - a - 1)
    bucket = g_idx * MOE_PAIRS + pair
    onehot = lane == bucket
    prefix = jnp.dot(tril_ref[...], onehot.astype(BF16), preferred_element_type=F32)
    run = run_ref[...]
    rank = jnp.sum(jnp.where(onehot, prefix + run, 0.0), axis=-1, keepdims=True) - 1.0
    run = run + prefix[TOKEN_TILE - 1:TOKEN_TILE, :]
    run_ref[...] = run
    count_ref[...] = run
    route = jnp.where(lane == 0, bucket.astype(F32),
                      jnp.where(lane == 1, rank,
                                jnp.where(lane == 2, w_lo, jnp.where(lane == 3, w_hi, 0.0))))
    route_ref[...] = route
    half = D_MODEL // 2
    bits = pltpu.bitcast(h.astype(BF16).astype(F32), jnp.uint32)
    words = (bits[:, :half] >> 16) | (bits[:, half:] & jnp.uint32(0xFFFF0000))
    h_ref[:, :half] = pltpu.bitcast(words, jnp.int32)
    h_ref[:, half:half + LANES] = pltpu.bitcast(route, jnp.int32)
    h_ref[:, half + LANES:] = jnp.zeros((TOKEN_TILE, half - LANES), jnp.int32)


def _router(x_all, gain, mods_l, w_router, b_router, n_batch, seq, n_tiles):
    d = x_all.shape[1]
    tril = np.tril(np.ones((TOKEN_TILE, TOKEN_TILE), np.float32))
    return pl.pallas_call(
        _router_kernel,
        grid=(n_tiles,),
        in_specs=[
            pl.BlockSpec((TOKEN_TILE, d), lambda i: (i, 0)),
            pl.BlockSpec((1, d), lambda i: (0, 0)),
            pl.BlockSpec((None, N_MOD, d), _mod_index(seq // TOKEN_TILE, n_batch)),
            pl.BlockSpec((d, LANES), lambda i: (0, 0)),
            pl.BlockSpec((1, LANES), lambda i: (0, 0)),
            pl.BlockSpec((TOKEN_TILE, TOKEN_TILE), lambda i: (0, 0)),
        ],
        out_specs=[pl.BlockSpec((TOKEN_TILE, d), lambda i: (i, 0)),
                   pl.BlockSpec((TOKEN_TILE, LANES), lambda i: (i, 0)),
                   pl.BlockSpec((1, LANES), lambda i: (0, 0))],
        out_shape=[jax.ShapeDtypeStruct((n_tiles * TOKEN_TILE, d), jnp.int32),
                   jax.ShapeDtypeStruct((n_tiles * TOKEN_TILE, LANES), F32),
                   jax.ShapeDtypeStruct((1, LANES), F32)],
        scratch_shapes=[pltpu.VMEM((1, LANES), F32)],
        compiler_params=_cparams("arbitrary"),
        name="moe_router",
    )(x_all, gain.reshape(1, d), mods_l, w_router, b_router, jnp.asarray(tril, BF16))


def _row_copy(table_hbm, dst, sem, src_row, dst_row):
    return pltpu.make_async_copy(table_hbm.at[pl.ds(src_row, 1)], dst.at[pl.ds(dst_row, 1)], sem)


def _start_row_gather(idx_ref, base, table_hbm, dst, sem, unrolled):
    rows = dst.shape[0]
    if unrolled:
        for r in range(rows):
            _row_copy(table_hbm, dst, sem, idx_ref[base + r], r).start()
    else:
        def issue(r, carry):
            _row_copy(table_hbm, dst, sem, idx_ref[base + r], r).start()
            return carry

        lax.fori_loop(0, rows, issue, 0, unroll=GATHER_UNROLL)


def _wait_row_gather(table_hbm, dst, sem):
    def drain(r, carry):
        _row_copy(table_hbm, dst, sem, 0, r).wait()
        return carry

    lax.fori_loop(0, dst.shape[0], drain, 0, unroll=GATHER_UNROLL)


def _gather_residual_kernel(idx_ref, y_hbm, x_ref, mod_ref, o_ref, buf, sem):
    _start_row_gather(idx_ref, pl.program_id(0) * buf.shape[0], y_hbm, buf, sem, unrolled=False)
    _wait_row_gather(y_hbm, buf, sem)
    o_ref[...] = x_ref[...] + mod_ref[5:6, :] * buf[...]


def _gather_residual(x_all, y_sorted, dest, mods_l, n_batch, seq, n_tiles):
    d = x_all.shape[1]
    row = pl.BlockSpec((TOKEN_TILE, d), lambda i, idx: (i, 0))
    mod_index = _mod_index(seq // TOKEN_TILE, n_batch)
    return pl.pallas_call(
        _gather_residual_kernel,
        grid_spec=pltpu.PrefetchScalarGridSpec(
            num_scalar_prefetch=1,
            grid=(n_tiles,),
            in_specs=[pl.BlockSpec(memory_space=pl.ANY), row,
                      pl.BlockSpec((None, N_MOD, d), lambda i, idx: mod_index(i))],
            out_specs=row,
            scratch_shapes=[pltpu.VMEM((TOKEN_TILE, d), F32), pltpu.SemaphoreType.DMA],
        ),
        out_shape=jax.ShapeDtypeStruct((n_tiles * TOKEN_TILE, d), F32),
        compiler_params=_cparams("arbitrary"),
        name="moe_combine",
    )(dest, y_sorted, x_all, mods_l)


def _bucket_experts():
    lo, hi = [], []
    for g in range(MOE_GROUPS):
        for a in range(MOE_EXPERTS_PER_GROUP):
            for b in range(a + 1, MOE_EXPERTS_PER_GROUP):
                lo.append(g * MOE_EXPERTS_PER_GROUP + a)
                hi.append(g * MOE_EXPERTS_PER_GROUP + b)
    return np.asarray(lo, np.int32), np.asarray(hi, np.int32)


def _sorted_experts_kernel(src_ref, lo_ref, hi_ref, nact_ref, tab_hbm, wgl_ref, wul_ref, wdl_ref,
                           wgh_ref, wuh_ref, wdh_ref, o_ref, xbuf, sems):
    del lo_ref, hi_ref
    t = pl.program_id(0)
    n_active = nact_ref[0]
    n_slots = MOE_GATHER_AHEAD + 1
    slot = t % n_slots
    ahead = (t + MOE_GATHER_AHEAD) % n_slots
    half = D_MODEL // 2

    @pl.when(t == 0)
    def _first_tiles():
        for k in range(MOE_GATHER_AHEAD):
            _start_row_gather(src_ref, k * MOE_SORT_TILE, tab_hbm, xbuf.at[k], sems.at[k], unrolled=False)

    @pl.when(t < n_active + MOE_GATHER_AHEAD)
    def _retire():
        _wait_row_gather(tab_hbm, xbuf.at[slot], sems.at[slot])

    @pl.when(t >= n_active)
    def _unused_tile():
        o_ref[...] = jnp.zeros_like(o_ref)

    @pl.when(t < n_active)
    def _active_tile():
        _start_row_gather(src_ref, (t + MOE_GATHER_AHEAD) * MOE_SORT_TILE, tab_hbm, xbuf.at[ahead],
                          sems.at[ahead], unrolled=True)
        words = pltpu.bitcast(xbuf[slot, :, :half], jnp.uint32)
        x_lo = pltpu.bitcast(words << 16, F32)
        x_hi = pltpu.bitcast(words & jnp.uint32(0xFFFF0000), F32)
        xb = jnp.concatenate([x_lo, x_hi], axis=-1).astype(BF16)
        r = pltpu.bitcast(xbuf[slot, :, half:half + LANES], F32)
        acc = None
        for wg_ref, wu_ref, wd_ref, lane in ((wgl_ref, wul_ref, wdl_ref, 2), (wgh_ref, wuh_ref, wdh_ref, 3)):
            gate = jnp.dot(xb, wg_ref[...], preferred_element_type=F32)
            up = jnp.dot(xb, wu_ref[...], preferred_element_type=F32)
            act = _silu(gate) * up * r[:, lane:lane + 1]
            part = jnp.dot(act.astype(BF16), wd_ref[...], preferred_element_type=F32)
            acc = part if acc is None else acc + part
        o_ref[...] = acc


def _sorted_experts(table, src, tile_lo, tile_hi, n_active, wg, wu, wd):
    n_rows = src.shape[0]
    d, f = wg.shape[-2:]
    lo_w = lambda shape: pl.BlockSpec(shape, lambda t, src, lo, hi, n: (lo[t], 0, 0))
    hi_w = lambda shape: pl.BlockSpec(shape, lambda t, src, lo, hi, n: (hi[t], 0, 0))
    return pl.pallas_call(
        _sorted_experts_kernel,
        grid_spec=pltpu.PrefetchScalarGridSpec(
            num_scalar_prefetch=4,
            grid=(n_rows // MOE_SORT_TILE,),
            in_specs=[pl.BlockSpec(memory_space=pl.ANY),
                      lo_w((None, d, f)), lo_w((None, d, f)), lo_w((None, f, d)),
                      hi_w((None, d, f)), hi_w((None, d, f)), hi_w((None, f, d))],
            out_specs=pl.BlockSpec((MOE_SORT_TILE, d), lambda t, src, lo, hi, n: (t, 0)),
            scratch_shapes=[pltpu.VMEM((MOE_GATHER_AHEAD + 1, MOE_SORT_TILE, table.shape[1]), table.dtype),
                            pltpu.SemaphoreType.DMA((MOE_GATHER_AHEAD + 1,))],
        ),
        out_shape=jax.ShapeDtypeStruct((n_rows, d), F32),
        compiler_params=_cparams("arbitrary"),
        name="moe_experts",
    )(src, tile_lo, tile_hi, n_active, table, wg, wu, wd, wg, wu, wd)


def _moe(x_all, gain, mods_l, w_router, b_router, wg, wu, wd, n_batch, seq, n_tiles):
    n = n_tiles * TOKEN_TILE
    table, route, counts = _router(x_all, gain, mods_l, w_router, b_router, n_batch, seq, n_tiles)

    n_sorted_tiles = (n + MOE_BUCKETS * (MOE_SORT_TILE - 1)) // MOE_SORT_TILE + MOE_GATHER_AHEAD
    n_sorted = n_sorted_tiles * MOE_SORT_TILE
    counts = counts[0, :MOE_BUCKETS].astype(jnp.int32)
    bucket_tiles = (counts + MOE_SORT_TILE - 1) // MOE_SORT_TILE
    tile_end = jnp.cumsum(bucket_tiles)
    row_start = (tile_end - bucket_tiles) * MOE_SORT_TILE
    bucket = route[:, 0].astype(jnp.int32)
    rank = route[:, 1].astype(jnp.int32)
    dest = jnp.sum(jnp.where(bucket[:, None] == jnp.arange(MOE_BUCKETS)[None, :], row_start[None, :], 0),
                   axis=1) + rank
    src = (jnp.arange(n_sorted, dtype=jnp.int32) % n).at[dest].set(jnp.arange(n, dtype=jnp.int32))
    tiles = jnp.arange(n_sorted_tiles, dtype=jnp.int32)
    tile_bucket = jnp.sum(jnp.minimum(tiles, tile_end[-1] - 1)[:, None] >= tile_end[None, :], axis=1)
    tile_bucket = jnp.minimum(tile_bucket, MOE_BUCKETS - 1)
    lo_ids, hi_ids = _bucket_experts()
    tile_lo = jnp.asarray(lo_ids)[tile_bucket]
    tile_hi = jnp.asarray(hi_ids)[tile_bucket]

    n_active = tile_end[-1:].astype(jnp.int32)
    y_sorted = _sorted_experts(table, src, tile_lo, tile_hi, n_active, wg, wu, wd)
    return _gather_residual(x_all, y_sorted, dest, mods_l, n_batch, seq, n_tiles)


def kernel(x, c, ctx, c_ctx, w_ada, b_ada, g_mix, g_ffn, w_in, w_out, gmlp_v_gain, gmlp_ws, gmlp_bs,
           na_q_gain, na_k_gain, na_rpb, hy_short_w, hy_short_b, hy_w1, hy_b1, hy_w2, hy_b2, hy_w3,
           hy_freq, hy_bias, moe_w_rg, moe_b_rg, moe_w_re, moe_b_re, moe_w_gate, moe_w_up, moe_w_down):
    n_batch, seq, d = x.shape
    ctx_len = ctx.shape[1]
    depth = w_ada.shape[0]
    n_lat = n_batch * seq
    n_ctx = n_batch * ctx_len
    assert d == D_MODEL and seq % TOKEN_TILE == 0 and n_ctx % TOKEN_TILE == 0
    assert seq % GMLP_CHUNK == 0 and ctx_len % GMLP_CHUNK == 0

    pad_rows = -(n_batch + 1) % 8
    cs = jnp.concatenate([c, c_ctx[None, :], jnp.zeros((pad_rows, d), F32)], axis=0)
    mods = _modulation(cs, w_ada, b_ada)[:, :n_batch + 1].reshape(depth, n_batch + 1, N_MOD, d)

    x_all = jnp.concatenate([x.reshape(n_lat, d), ctx.reshape(n_ctx, d)], axis=0)
    n_all_tiles = (n_lat + n_ctx) // TOKEN_TILE
    n_lat_tiles = n_lat // TOKEN_TILE

    mats_lat = _dft_matrices(seq)
    mats_ctx = _dft_matrices(ctx_len)
    pos_lat = _hyena_positions(seq)
    pos_ctx = _hyena_positions(ctx_len)

    w_in_bf = w_in.astype(BF16)
    w_out_bf = w_out.astype(BF16)
    wg_bf = moe_w_gate.astype(BF16).reshape(depth, MOE_EXPERTS, d, MOE_HIDDEN)
    wu_bf = moe_w_up.astype(BF16).reshape(depth, MOE_EXPERTS, d, MOE_HIDDEN)
    wd_bf = moe_w_down.astype(BF16).reshape(depth, MOE_EXPERTS, MOE_HIDDEN, d)
    lane_pad = LANES - MOE_EXPERTS - MOE_GROUPS
    w_router = jnp.pad(jnp.concatenate([moe_w_re, moe_w_rg], axis=-1), ((0, 0), (0, 0), (0, lane_pad)))
    b_router = jnp.pad(jnp.concatenate([moe_b_re, moe_b_rg], axis=-1), ((0, 0), (0, lane_pad)))[:, None, :]
    w1_pad = jnp.pad(hy_w1, ((0, 0), (0, LANES - HYENA_EMB), (0, 0)))

    for l in range(depth):
        last = l == depth - 1
        mods_l = mods[l]
        n_tiles = n_lat_tiles if last else n_all_tiles

        p_all = _in_proj(x_all, g_mix[l], mods_l, w_in_bf[l], n_batch, seq)

        filt = (w1_pad[l], hy_b1[l], hy_w2[l], hy_b2[l], hy_w3[l], hy_freq[l])
        kpack_lat = _filter_dft(mats_lat[0], mats_lat[1], _hyena_filters(*pos_lat, *filt))
        hy_lat = _hyena(p_all, hy_short_w[l], hy_short_b[l], hy_bias[l], kpack_lat, mats_lat,
                        n_batch, seq, 0)
        if last:
            hy_ctx = hy_lat
            gm = _gmlp(p_all, n_lat, gmlp_v_gain[l], gmlp_ws[l], gmlp_bs[l].T)
        else:
            kpack_ctx = _filter_dft(mats_ctx[0], mats_ctx[1], _hyena_filters(*pos_ctx, *filt))
            hy_ctx = _hyena(p_all, hy_short_w[l], hy_short_b[l], hy_bias[l], kpack_ctx, mats_ctx,
                            n_batch, ctx_len, n_lat // ctx_len)
            gm = _gmlp(p_all, n_lat + n_ctx, gmlp_v_gain[l], gmlp_ws[l], gmlp_bs[l].T)

        na = _attention(p_all, na_rpb[l], na_q_gain[l], na_k_gain[l],
                        n_batch, seq, ctx_len, not last)

        x_all = _out_proj(x_all, gm, na, hy_lat, hy_ctx, w_out_bf[l], mods_l, n_batch, seq, n_tiles)

        x_all = _moe(x_all, g_ffn[l], mods_l, w_router[l], b_router[l], wg_bf[l], wu_bf[l], wd_bf[l],
                     n_batch, seq, n_tiles)

    return x_all[:n_lat].reshape(n_batch, seq, d)
```

```python
import functools
import math

import numpy as np
import jax
import jax.numpy as jnp
from jax import lax
from jax.experimental import pallas as pl
from jax.experimental.pallas import tpu as pltpu

F32 = jnp.float32
BF16 = jnp.bfloat16
HI = lax.Precision.HIGHEST

D_MODEL = 1024
GRID_W = 64
D_GMLP = D_MODEL // 4
D_NA = D_MODEL // 2
D_HYENA = D_MODEL // 4
D_IN = 2 * D_GMLP + 3 * D_NA + 3 * D_HYENA
Q_START = 2 * D_GMLP
KV_START = 2 * D_GMLP + D_NA
HY_START = 2 * D_GMLP + 3 * D_NA
GMLP_GROUPS = 4
GMLP_GROUP_DIM = D_GMLP // GMLP_GROUPS
GMLP_CHUNK = 128
NA_HEAD_DIM = 64
NA_HEADS = D_NA // NA_HEAD_DIM
NA_SCALE = NA_HEAD_DIM ** -0.5
NA_WIN_ROWS = 8
NA_WIN_COLS = 16
HYENA_ORDER = 2
HYENA_POS_BANDS = 16
HYENA_EMB = 1 + 2 * HYENA_POS_BANDS
HYENA_FILTER_HIDDEN = 64
HYENA_DECAY_TARGET = 1e-2
HYENA_FAST_DECAY = 0.3
HYENA_SLOW_DECAY = 1.5
MOE_GROUPS = 4
MOE_EXPERTS_PER_GROUP = 8
MOE_EXPERTS = MOE_GROUPS * MOE_EXPERTS_PER_GROUP
MOE_HIDDEN = 256
MOE_PAIRS = MOE_EXPERTS_PER_GROUP * (MOE_EXPERTS_PER_GROUP - 1) // 2
MOE_BUCKETS = MOE_GROUPS * MOE_PAIRS
N_MOD = 6
RMS_EPS = 1e-6
LN_EPS = 1e-5

LANES = 128
TOKEN_TILE = 512
NA_Q_ROWS = 4
NA_Q_BLOCK = NA_Q_ROWS * GRID_W
NA_BAND_ROWS = NA_WIN_ROWS + NA_Q_ROWS
DFT_TILE = 512
CONV_TILE = 1024
ROUTE_CHUNK = 128
MOE_SORT_TILE = 128
MOE_GATHER_AHEAD = 1
GATHER_UNROLL = 8
MASK_VALUE = -1e30
VMEM_LIMIT = 56 * 1024 * 1024


def _cparams(*sem):
    return pltpu.CompilerParams(dimension_semantics=sem, vmem_limit_bytes=VMEM_LIMIT)


def _silu(x):
    return x * jax.nn.sigmoid(x)


def _rms_rows(x):
    return x * lax.rsqrt(jnp.mean(x * x, axis=-1, keepdims=True) + RMS_EPS)


def _mods_kernel(cs_ref, w_ref, b_ref, o_ref):
    s = _silu(cs_ref[...])
    o_ref[...] = jnp.dot(s, w_ref[...], preferred_element_type=F32, precision=HI) + b_ref[...]


def _modulation(cs, w_ada, b_ada):
    depth, d, nd = w_ada.shape
    rows = cs.shape[0]
    col = 1024
    return pl.pallas_call(
        _mods_kernel,
        grid=(depth, nd // col),
        in_specs=[
            pl.BlockSpec((rows, d), lambda l, j: (0, 0)),
            pl.BlockSpec((None, d, col), lambda l, j: (l, 0, j)),
            pl.BlockSpec((None, 1, col), lambda l, j: (l, 0, j)),
        ],
        out_specs=pl.BlockSpec((None, rows, col), lambda l, j: (l, 0, j)),
        out_shape=jax.ShapeDtypeStruct((depth, rows, nd), F32),
        compiler_params=_cparams("arbitrary", "arbitrary"),
        name="modulation",
    )(cs, w_ada, b_ada.reshape(depth, 1, nd))


def _in_proj_kernel(x_ref, g_ref, mod_ref, w_ref, o_ref):
    mod = mod_ref[...]
    h = _rms_rows(x_ref[...]) * g_ref[...]
    h = h * (1.0 + mod[1:2]) + mod[0:1]
    o_ref[...] = jnp.dot(h.astype(BF16), w_ref[...], preferred_element_type=F32)


def _mod_index(tiles_per_batch, n_batch):
    return lambda i: (jnp.minimum(i // tiles_per_batch, n_batch), 0, 0)


def _in_proj(x_all, gain, mods_l, w_bf, n_batch, seq):
    n, d = x_all.shape
    d_in = w_bf.shape[1]
    return pl.pallas_call(
        _in_proj_kernel,
        grid=(n // TOKEN_TILE,),
        in_specs=[
            pl.BlockSpec((TOKEN_TILE, d), lambda i: (i, 0)),
            pl.BlockSpec((1, d), lambda i: (0, 0)),
            pl.BlockSpec((None, N_MOD, d), _mod_index(seq // TOKEN_TILE, n_batch)),
            pl.BlockSpec((d, d_in), lambda i: (0, 0)),
        ],
        out_specs=pl.BlockSpec((TOKEN_TILE, d_in), lambda i: (i, 0)),
        out_shape=jax.ShapeDtypeStruct((n, d_in), F32),
        compiler_params=_cparams("arbitrary"),
        name="in_proj",
    )(x_all, gain.reshape(1, d), mods_l, w_bf)


def _group_avg_matrix(groups, width):
    return np.kron(np.eye(groups), np.full((width, width), 1.0 / width))


def _group_mean(t, avg):
    hi = t.astype(BF16)
    lo = (t - hi.astype(F32)).astype(BF16)
    return (jnp.dot(hi, avg, preferred_element_type=F32) + jnp.dot(lo, avg, preferred_element_type=F32))


def _gmlp_kernel(u_ref, v_ref, gain_ref, avg_ref, ws_ref, bs_ref, o_ref):
    avg = avg_ref[...]
    v = jax.nn.gelu(v_ref[...])
    v = v - _group_mean(v, avg)
    v = v * lax.rsqrt(_group_mean(v * v, avg) + LN_EPS) * gain_ref[...]
    vb = v.astype(BF16)
    bs = bs_ref[...]
    for c in range(u_ref.shape[0] // GMLP_CHUNK):
        rows = slice(c * GMLP_CHUNK, (c + 1) * GMLP_CHUNK)
        outs = []
        for g in range(GMLP_GROUPS):
            cols = slice(g * GMLP_GROUP_DIM, (g + 1) * GMLP_GROUP_DIM)
            s = jnp.dot(ws_ref[g], vb[rows, cols], preferred_element_type=F32) + bs[:, g:g + 1]
            outs.append(s)
        o_ref[rows, :] = jax.nn.gelu(u_ref[rows, :]) * jnp.concatenate(outs, axis=-1)


def _gmlp(p_all, n, v_gain, ws, bs_t):
    avg = _group_avg_matrix(GMLP_GROUPS, GMLP_GROUP_DIM)
    return pl.pallas_call(
        _gmlp_kernel,
        grid=(n // TOKEN_TILE,),
        in_specs=[
            pl.BlockSpec((TOKEN_TILE, D_GMLP), lambda i: (i, 0)),
            pl.BlockSpec((TOKEN_TILE, D_GMLP), lambda i: (i, 1)),
            pl.BlockSpec((1, D_GMLP), lambda i: (0, 0)),
            pl.BlockSpec((D_GMLP, D_GMLP), lambda i: (0, 0)),
            pl.BlockSpec((GMLP_GROUPS, GMLP_CHUNK, GMLP_CHUNK), lambda i: (0, 0, 0)),
            pl.BlockSpec((GMLP_CHUNK, GMLP_GROUPS), lambda i: (0, 0)),
        ],
        out_specs=pl.BlockSpec((TOKEN_TILE, D_GMLP), lambda i: (i, 0)),
        out_shape=jax.ShapeDtypeStruct((n, D_GMLP), F32),
        compiler_params=_cparams("arbitrary"),
        name="gmlp",
    )(p_all, p_all, v_gain.reshape(1, D_GMLP), jnp.asarray(avg, BF16), ws.astype(BF16), bs_t)


def _rpb_expand_kernel(rpb_ref, sel_ref, o_ref):
    o_ref[...] = jnp.dot(rpb_ref[...], sel_ref[...], preferred_element_type=F32, precision=HI)


def _na_geometry(grid_rows):
    variants, step_variant, band_start = [], [], []
    for r0 in range(0, grid_rows, NA_Q_ROWS):
        b0 = int(np.clip(r0 - NA_WIN_ROWS // 2, 0, grid_rows - NA_BAND_ROWS))
        geo = []
        for r in range(r0, r0 + NA_Q_ROWS):
            wr = int(np.clip(r - NA_WIN_ROWS // 2, 0, grid_rows - NA_WIN_ROWS))
            assert b0 <= wr and wr + NA_WIN_ROWS <= b0 + NA_BAND_ROWS
            geo.append((wr - b0, wr - r + NA_WIN_ROWS - 1))
        geo = tuple(geo)
        if geo not in variants:
            variants.append(geo)
        step_variant.append(variants.index(geo))
        band_start.append(b0)
    return variants, step_variant, band_start


def _na_bias_tables(rpb, variants):
    n_heads, n_dr, n_dc = rpb.shape
    qcol = np.arange(GRID_W)[:, None]
    kcol = np.arange(GRID_W)[None, :]
    win_c = np.clip(qcol - NA_WIN_COLS // 2, 0, GRID_W - NA_WIN_COLS)
    col_ok = (kcol >= win_c) & (kcol < win_c + NA_WIN_COLS)
    dc = np.clip(kcol - qcol + NA_WIN_COLS - 1, 0, 2 * NA_WIN_COLS - 2)
    dc_pad = -n_dc % 8
    sel = (np.arange(n_dc + dc_pad)[:, None] == dc.reshape(1, -1)).astype(np.float32)
    rpb2 = jnp.pad(rpb.reshape(n_heads * n_dr, n_dc), ((0, 0), (0, dc_pad)))
    toep = pl.pallas_call(
        _rpb_expand_kernel,
        out_shape=jax.ShapeDtypeStruct((n_heads * n_dr, GRID_W * GRID_W), F32),
        name="rpb_expand",
    )(rpb2, jnp.asarray(sel))
    toep = toep.reshape(n_heads, n_dr, GRID_W, GRID_W)
    toep = jnp.where(col_ok[None, None], toep, MASK_VALUE)
    tabs = []
    for geo in variants:
        rows = []
        for a0, dr0 in geo:
            rows.append(jnp.pad(toep[:, dr0:dr0 + NA_WIN_ROWS],
                                ((0, 0), (a0, NA_BAND_ROWS - NA_WIN_ROWS - a0), (0, 0), (0, 0)),
                                constant_values=MASK_VALUE))
        tab = jnp.stack(rows, axis=1)
        tabs.append(jnp.transpose(tab, (0, 1, 3, 2, 4)).reshape(n_heads, NA_Q_BLOCK, NA_BAND_ROWS * GRID_W))
    return jnp.stack(tabs, axis=1)


def _store_heads(dst, rows, t, gain, avg):
    if gain is not None:
        t = t * lax.rsqrt(_group_mean(t * t, avg) + RMS_EPS) * gain
    for h in range(NA_HEADS):
        dst[h, rows, :] = t[:, h * NA_HEAD_DIM:(h + 1) * NA_HEAD_DIM].astype(BF16)


def _na_kernel(q_ref, k_ref, v_ref, kc_ref, vc_ref, bias_ref, qg_ref, kg_ref, avg_ref, o_ref,
               kn_s, vb_s, kcn_s, vcb_s, qn_s, o_s, *, n_lat_steps, step_variant, band_start):
    step = pl.program_id(1)
    kg = kg_ref[...]
    avg = avg_ref[...]
    nt = (((1,), (1,)), ((), ()))
    band_keys = NA_BAND_ROWS * GRID_W
    all_rows = slice(None)

    @pl.when(step == 0)
    def _prepare_keys():
        chunk = 256

        def body(c, carry):
            rows = pl.ds(pl.multiple_of(c * chunk, chunk), chunk)
            _store_heads(kn_s, rows, k_ref[rows, :], kg, avg)
            _store_heads(vb_s, rows, v_ref[rows, :], None, None)
            return carry

        lax.fori_loop(0, k_ref.shape[0] // chunk, body, 0)
        _store_heads(kcn_s, all_rows, kc_ref[...], kg, avg)
        _store_heads(vcb_s, all_rows, vc_ref[...], None, None)

    _store_heads(qn_s, all_rows, q_ref[...], qg_ref[...] * NA_SCALE, avg)

    def finish(h, scores, v_parts):
        m = jnp.max(scores, axis=-1, keepdims=True)
        p = jnp.exp(scores - m)
        denom = jnp.sum(p, axis=-1, keepdims=True)
        pb = p.astype(BF16)
        acc = None
        col = 0
        for v in v_parts:
            part = jnp.dot(pb[:, col:col + v.shape[0]], v, preferred_element_type=F32)
            acc = part if acc is None else acc + part
            col += v.shape[0]
        o_s[h] = acc / denom

    @pl.when(step < n_lat_steps)
    def _latent_queries():
        variant = jnp.int32(0)
        band0 = jnp.int32(0)
        for s_, (v_, b_) in enumerate(zip(step_variant, band_start)):
            variant = jnp.where(step == s_, v_, variant)
            band0 = jnp.where(step == s_, b_ * GRID_W, band0)
        krows = pl.ds(pl.multiple_of(band0, NA_Q_BLOCK), band_keys)

        def head_body(h, carry):
            qh = qn_s[h]
            s_w = lax.dot_general(qh, kn_s[h, krows, :], nt, preferred_element_type=F32)
            s_w = s_w + bias_ref[h, variant]
            s_c = lax.dot_general(qh, kcn_s[h], nt, preferred_element_type=F32)
            finish(h, jnp.concatenate([s_w, s_c], axis=-1), (vb_s[h, krows, :], vcb_s[h]))
            return carry

        lax.fori_loop(0, NA_HEADS, head_body, 0, unroll=2)

    @pl.when(step >= n_lat_steps)
    def _context_queries():
        def head_body(h, carry):
            s = lax.dot_general(qn_s[h], kcn_s[h], nt, preferred_element_type=F32)
            finish(h, s, (vcb_s[h],))
            return carry

        lax.fori_loop(0, NA_HEADS, head_body, 0)

    o_ref[...] = jnp.concatenate([o_s[h] for h in range(NA_HEADS)], axis=-1)


def _attention(p_all, rpb, q_gain, k_gain, n_batch, seq, ctx_len, with_ctx_queries):
    n = p_all.shape[0]
    assert ctx_len == NA_Q_BLOCK and seq % NA_Q_BLOCK == 0
    n_lat_steps = seq // NA_Q_BLOCK
    n_steps = n_lat_steps + (1 if with_ctx_queries else 0)
    ctx_block0 = n_batch * seq // ctx_len
    qcol, kcol, vcol = Q_START // D_NA, KV_START // D_NA, (KV_START + D_NA) // D_NA
    variants, step_variant, band_start = _na_geometry(seq // GRID_W)
    bias_tab = _na_bias_tables(rpb, variants)

    def q_index(col):
        return lambda b, s: (jnp.where(s < n_lat_steps, b * n_lat_steps + s, ctx_block0 + b), col)

    kern = functools.partial(_na_kernel, n_lat_steps=n_lat_steps, step_variant=tuple(step_variant),
                             band_start=tuple(band_start))
    head_major = lambda rows: pltpu.VMEM((NA_HEADS, rows, NA_HEAD_DIM), BF16)
    return pl.pallas_call(
        kern,
        grid=(n_batch, n_steps),
        in_specs=[
            pl.BlockSpec((NA_Q_BLOCK, D_NA), q_index(qcol)),
            pl.BlockSpec((seq, D_NA), lambda b, s: (b, kcol)),
            pl.BlockSpec((seq, D_NA), lambda b, s: (b, vcol)),
            pl.BlockSpec((ctx_len, D_NA), lambda b, s: (ctx_block0 + b, kcol)),
            pl.BlockSpec((ctx_len, D_NA), lambda b, s: (ctx_block0 + b, vcol)),
            pl.BlockSpec(bias_tab.shape, lambda b, s: (0, 0, 0, 0), pipeline_mode=pl.Buffered(1)),
            pl.BlockSpec((1, D_NA), lambda b, s: (0, 0)),
            pl.BlockSpec((1, D_NA), lambda b, s: (0, 0)),
            pl.BlockSpec((D_NA, D_NA), lambda b, s: (0, 0)),
        ],
        out_specs=pl.BlockSpec((NA_Q_BLOCK, D_NA), q_index(0)),
        out_shape=jax.ShapeDtypeStruct((n if with_ctx_queries else n_batch * seq, D_NA), F32),
        scratch_shapes=[
            head_major(seq), head_major(seq), head_major(ctx_len), head_major(ctx_len),
            head_major(NA_Q_BLOCK),
            pltpu.VMEM((NA_HEADS, NA_Q_BLOCK, NA_HEAD_DIM), F32),
        ],
        compiler_params=_cparams("arbitrary", "arbitrary"),
        name="attention",
    )(p_all, p_all, p_all, p_all, p_all, bias_tab,
      jnp.tile(q_gain.reshape(1, NA_HEAD_DIM), (1, NA_HEADS)),
      jnp.tile(k_gain.reshape(1, NA_HEAD_DIM), (1, NA_HEADS)),
      jnp.asarray(_group_avg_matrix(NA_HEADS, NA_HEAD_DIM), BF16))


def _dft_matrices(length):
    idx = jnp.arange(length, dtype=jnp.int32)
    step = 64
    t_hi = jnp.arange(length // step, dtype=jnp.int32) * step
    t_lo = jnp.arange(step, dtype=jnp.int32)
    ang_hi = ((idx[:, None] * t_hi[None, :]) % (2 * length)).astype(F32) * (math.pi / length)
    ang_lo = ((idx[:, None] * t_lo[None, :]) % (2 * length)).astype(F32) * (math.pi / length)
    c_hi, s_hi = jnp.cos(ang_hi)[:, :, None], jnp.sin(ang_hi)[:, :, None]
    c_lo, s_lo = jnp.cos(ang_lo)[:, None, :], jnp.sin(ang_lo)[:, None, :]
    gc = (c_hi * c_lo - s_hi * s_lo).reshape(length, length)
    gs = (s_hi * c_lo + c_hi * s_lo).reshape(length, length)
    nyq = jnp.where(idx % 2 == 0, 1.0, -1.0).astype(F32)
    gs = jnp.where(idx[:, None] == 0, nyq[None, :], gs)
    return gc.astype(BF16), gs.astype(BF16), gs.T.astype(BF16)


def _hyena_positions(length):
    t = jnp.linspace(0.0, 1.0, length, dtype=F32)[:, None]
    w = 2.0 * math.pi * jnp.arange(length, dtype=F32)[:, None] / length
    f = jnp.linspace(1e-4, HYENA_POS_BANDS - 1, HYENA_POS_BANDS, dtype=F32)[None, :]
    z = jnp.concatenate([t, jnp.cos(f * w), -jnp.sin(f * w)], axis=-1)
    z = jnp.pad(z, ((0, 0), (0, LANES - HYENA_EMB)))
    min_decay = math.log(HYENA_DECAY_TARGET) / HYENA_SLOW_DECAY
    max_decay = math.log(HYENA_DECAY_TARGET) / HYENA_FAST_DECAY
    deltas = jnp.abs(jnp.linspace(min_decay, max_decay, D_HYENA, dtype=F32))[None, :]
    return z, jnp.exp(-t * deltas)


def _filter_kernel(z_ref, decay_ref, w1_ref, b1_ref, w2_ref, b2_ref, w3_ref, freq_ref, o_ref):
    freq = freq_ref[...]
    hdn = jnp.dot(z_ref[...], w1_ref[...], preferred_element_type=F32, precision=HI) + b1_ref[...]
    hdn = jnp.sin(freq[0:1] * hdn)
    hdn = jnp.dot(hdn, w2_ref[...], preferred_element_type=F32, precision=HI) + b2_ref[...]
    hdn = jnp.sin(freq[1:2] * hdn)
    h = jnp.dot(hdn, w3_ref[...], preferred_element_type=F32, precision=HI)
    decay = decay_ref[...]
    first_row = lax.broadcasted_iota(jnp.int32, decay.shape, 0) == 0
    outs = []
    for n in range(HYENA_ORDER):
        base = 2 * n * D_HYENA
        hf = h[:, base:base + D_HYENA] * decay
        hb = h[:, base + D_HYENA:base + 2 * D_HYENA] * decay
        norm = jnp.sum(jnp.abs(hf), axis=0, keepdims=True) + jnp.sum(jnp.abs(hb), axis=0, keepdims=True)
        outs.append(hf / norm)
        outs.append(jnp.where(first_row, 0.0, hb / norm))
    o_ref[...] = jnp.concatenate(outs, axis=-1)


def _hyena_filters(z, decay, w1p, b1, w2, b2, w3, freq):
    length = z.shape[0]
    full = lambda a: pl.BlockSpec(a.shape, lambda i: (0,) * a.ndim)
    args = (z, decay, w1p, b1.reshape(1, -1), w2, b2.reshape(1, -1), w3, freq)
    return pl.pallas_call(
        _filter_kernel,
        grid=(1,),
        in_specs=[full(a) for a in args],
        out_specs=pl.BlockSpec((length, 2 * HYENA_ORDER * D_HYENA), lambda i: (0, 0)),
        out_shape=jax.ShapeDtypeStruct((length, 2 * HYENA_ORDER * D_HYENA), F32),
        compiler_params=_cparams("arbitrary"),
        name="hyena_filter",
    )(*args)


def _filter_dft_kernel(gc_ref, gs_ref, h_ref, o_ref, *, length):
    hb = h_ref[...].astype(BF16)
    fa = jnp.dot(gc_ref[...], hb, preferred_element_type=F32)
    fb = jnp.dot(gs_ref[...], hb, preferred_element_type=F32)
    rows = lax.broadcasted_iota(jnp.int32, (fa.shape[0], D_HYENA), 0) + pl.program_id(0) * fa.shape[0]
    dc_row = rows == 0
    inv_n = 1.0 / (2 * length)
    outs = []
    for n in range(HYENA_ORDER):
        base = 2 * n * D_HYENA
        f_sl = slice(base, base + D_HYENA)
        b_sl = slice(base + D_HYENA, base + 2 * D_HYENA)
        kr = fa[:, f_sl] + fa[:, b_sl]
        ki = fb[:, b_sl] - fb[:, f_sl]
        k_nyq = fb[:, f_sl] + fb[:, b_sl]
        outs.append(jnp.where(dc_row, kr * inv_n, 2.0 * inv_n * kr))
        outs.append(jnp.where(dc_row, 0.0, 2.0 * inv_n * ki))
        outs.append(jnp.where(dc_row, 0.0, -2.0 * inv_n * ki))
        outs.append(jnp.where(dc_row, k_nyq * inv_n, 2.0 * inv_n * kr))
    o_ref[...] = jnp.concatenate(outs, axis=-1)


def _filter_dft(gc, gs, hfilt):
    length = gc.shape[0]
    tile = min(DFT_TILE, length)
    width = 4 * HYENA_ORDER * D_HYENA
    return pl.pallas_call(
        functools.partial(_filter_dft_kernel, length=length),
        grid=(length // tile,),
        in_specs=[
            pl.BlockSpec((tile, length), lambda j: (j, 0)),
            pl.BlockSpec((tile, length), lambda j: (j, 0)),
            pl.BlockSpec(hfilt.shape, lambda j: (0, 0)),
        ],
        out_specs=pl.BlockSpec((tile, width), lambda j: (j, 0)),
        out_shape=jax.ShapeDtypeStruct((length, width), F32),
        compiler_params=_cparams("arbitrary"),
        name="hyena_filter_dft",
    )(gc, gs, hfilt)


def _short_conv_kernel(a0_ref, a1_ref, a2_ref, w_ref, b_ref, o_ref):
    w = w_ref[...]
    b = b_ref[...]
    length = a0_ref.shape[0]
    rows = lax.broadcasted_iota(jnp.int32, (length, D_HYENA), 0)
    for j, a_ref in enumerate((a0_ref, a1_ref, a2_ref)):
        cols = slice(j * D_HYENA, (j + 1) * D_HYENA)
        a = a_ref[...]
        prev = jnp.where(rows == 0, 0.0, pltpu.roll(a, 1, 0))
        nxt = jnp.where(rows == length - 1, 0.0, pltpu.roll(a, length - 1, 0))
        o_ref[:, cols] = prev * w[0:1, cols] + a * w[1:2, cols] + nxt * w[2:3, cols] + b[:, cols]


def _short_conv(p_all, short_w, short_b, n_batch, length, row_block0):
    c0 = HY_START // D_HYENA
    spec = lambda j: pl.BlockSpec((length, D_HYENA), lambda b: (row_block0 + b, c0 + j))
    return pl.pallas_call(
        _short_conv_kernel,
        grid=(n_batch,),
        in_specs=[spec(0), spec(1), spec(2),
                  pl.BlockSpec((3, 3 * D_HYENA), lambda b: (0, 0)),
                  pl.BlockSpec((1, 3 * D_HYENA), lambda b: (0, 0))],
        out_specs=pl.BlockSpec((length, 3 * D_HYENA), lambda b: (b, 0)),
        out_shape=jax.ShapeDtypeStruct((n_batch * length, 3 * D_HYENA), F32),
        compiler_params=_cparams("arbitrary"),
        name="hyena_short_conv",
    )(p_all, p_all, p_all, short_w, short_b.reshape(1, -1))


def _conv_fwd_kernel(gc_ref, gs_ref, u_ref, k_ref, pa_ref, pb_ref):
    u = u_ref[...].astype(BF16)
    a = jnp.dot(gc_ref[...], u, preferred_element_type=F32)
    b = jnp.dot(gs_ref[...], u, preferred_element_type=F32)
    k = k_ref[...]
    c = D_HYENA
    pa_ref[...] = (a * k[:, 0:c] + b * k[:, c:2 * c]).astype(BF16)
    pb_ref[...] = (a * k[:, 2 * c:3 * c] + b * k[:, 3 * c:4 * c]).astype(BF16)


def _conv_fwd(gc, gs, u, u_col, kpack, order, n_batch):
    length = gc.shape[0]
    tile = min(CONV_TILE, length)
    nt = length // tile
    out = jax.ShapeDtypeStruct((n_batch * length, D_HYENA), BF16)
    return pl.pallas_call(
        _conv_fwd_kernel,
        grid=(nt, n_batch),
        in_specs=[
            pl.BlockSpec((tile, length), lambda j, b: (j, 0)),
            pl.BlockSpec((tile, length), lambda j, b: (j, 0)),
            pl.BlockSpec((length, D_HYENA), lambda j, b: (b, u_col)),
            pl.BlockSpec((tile, 4 * D_HYENA), lambda j, b: (j, order)),
        ],
        out_specs=[pl.BlockSpec((tile, D_HYENA), lambda j, b: (b * nt + j, 0))] * 2,
        out_shape=[out, out],
        compiler_params=_cparams("arbitrary", "arbitrary"),
        name="hyena_conv_fwd",
    )(gc, gs, u, kpack)


def _conv_inv_kernel(gc_ref, gst_ref, pa_ref, pb_ref, z_ref, gate_ref, d_ref, o_ref):
    y = jnp.dot(gc_ref[...], pa_ref[...], preferred_element_type=F32)
    y = y + jnp.dot(gst_ref[...], pb_ref[...], preferred_element_type=F32)
    o_ref[...] = gate_ref[...] * (y + d_ref[...] * z_ref[...])


def _conv_inv(gc, gst, pa, pb, z_prev, z_col, a3, gate_col, d_bias, n_batch):
    length = gc.shape[0]
    tile = min(CONV_TILE, length)
    nt = length // tile
    return pl.pallas_call(
        _conv_inv_kernel,
        grid=(nt, n_batch),
        in_specs=[
            pl.BlockSpec((tile, length), lambda j, b: (j, 0)),
            pl.BlockSpec((tile, length), lambda j, b: (j, 0)),
            pl.BlockSpec((length, D_HYENA), lambda j, b: (b, 0)),
            pl.BlockSpec((length, D_HYENA), lambda j, b: (b, 0)),
            pl.BlockSpec((tile, D_HYENA), lambda j, b: (b * nt + j, z_col)),
            pl.BlockSpec((tile, D_HYENA), lambda j, b: (b * nt + j, gate_col)),
            pl.BlockSpec((1, D_HYENA), lambda j, b: (0, 0)),
        ],
        out_specs=pl.BlockSpec((tile, D_HYENA), lambda j, b: (b * nt + j, 0)),
        out_shape=jax.ShapeDtypeStruct((n_batch * length, D_HYENA), F32),
        compiler_params=_cparams("arbitrary", "arbitrary"),
        name="hyena_conv_inv",
    )(gc, gst, pa, pb, z_prev, a3, d_bias.reshape(1, D_HYENA))


def _hyena(p_all, short_w, short_b, d_bias, kpack, mats, n_batch, length, row_block0):
    gc, gs, gst = mats
    a3 = _short_conv(p_all, short_w, short_b, n_batch, length, row_block0)
    pa, pb = _conv_fwd(gc, gs, a3, 0, kpack, 0, n_batch)
    z1 = _conv_inv(gc, gst, pa, pb, a3, 0, a3, 1, d_bias[0], n_batch)
    pa, pb = _conv_fwd(gc, gs, z1, 0, kpack, 1, n_batch)
    return _conv_inv(gc, gst, pa, pb, z1, 0, a3, 2, d_bias[1], n_batch)


def _out_proj_kernel(x_ref, gm_ref, na_ref, hyl_ref, hyc_ref, w_ref, mod_ref, o_ref, *, n_lat_tiles):
    is_lat = pl.program_id(0) < n_lat_tiles
    hy = jnp.where(is_lat, hyl_ref[...], hyc_ref[...])
    y = jnp.dot(gm_ref[...].astype(BF16), w_ref[0:D_GMLP, :], preferred_element_type=F32)
    y = y + jnp.dot(na_ref[...].astype(BF16), w_ref[D_GMLP:D_GMLP + D_NA, :], preferred_element_type=F32)
    y = y + jnp.dot(hy.astype(BF16), w_ref[D_GMLP + D_NA:, :], preferred_element_type=F32)
    o_ref[...] = x_ref[...] + mod_ref[2:3, :] * y


def _out_proj(x_all, gm, na, hy_lat, hy_ctx, w_bf, mods_l, n_batch, seq, n_tiles):
    d = x_all.shape[1]
    n_lat_tiles = n_batch * seq // TOKEN_TILE
    n_ctx_tiles = hy_ctx.shape[0] // TOKEN_TILE
    row = lambda w: pl.BlockSpec((TOKEN_TILE, w), lambda i: (i, 0))
    return pl.pallas_call(
        functools.partial(_out_proj_kernel, n_lat_tiles=n_lat_tiles),
        grid=(n_tiles,),
        in_specs=[
            row(d), row(D_GMLP), row(D_NA),
            pl.BlockSpec((TOKEN_TILE, D_HYENA), lambda i: (jnp.minimum(i, n_lat_tiles - 1), 0)),
            pl.BlockSpec((TOKEN_TILE, D_HYENA),
                         lambda i: (jnp.clip(i - n_lat_tiles, 0, n_ctx_tiles - 1), 0)),
            pl.BlockSpec(w_bf.shape, lambda i: (0, 0)),
            pl.BlockSpec((None, N_MOD, d), _mod_index(seq // TOKEN_TILE, n_batch)),
        ],
        out_specs=row(d),
        out_shape=jax.ShapeDtypeStruct((n_tiles * TOKEN_TILE, d), F32),
        compiler_params=_cparams("arbitrary"),
        name="out_proj",
    )(x_all, gm, na, hy_lat, hy_ctx, w_bf, mods_l)


def _router_kernel(x_ref, g_ref, mod_ref, wr_ref, br_ref, tril_ref, h_ref, route_ref, count_ref, run_ref):
    @pl.when(pl.program_id(0) == 0)
    def _init():
        run_ref[...] = jnp.zeros_like(run_ref)

    mod = mod_ref[...]
    h = _rms_rows(x_ref[...]) * g_ref[...]
    h = h * (1.0 + mod[4:5]) + mod[3:4]
    hb = h.astype(BF16)
    hb32 = hb.astype(F32)
    h_lo = (h - hb32).astype(BF16)
    logits_all = (jnp.dot(hb, wr_ref[0], preferred_element_type=F32)
                  + jnp.dot(h_lo, wr_ref[0], preferred_element_type=F32)
                  + jnp.dot(hb, wr_ref[1], preferred_element_type=F32)) + br_ref[...]
    half = D_MODEL // 2
    bits = pltpu.bitcast(hb32, jnp.uint32)
    words = (bits[:, :half] >> 16) | (bits[:, half:] & jnp.uint32(0xFFFF0000))
    h_ref[:, :half] = pltpu.bitcast(words, jnp.int32)
    h_ref[:, half + LANES:] = jnp.zeros((TOKEN_TILE, half - LANES), jnp.int32)
    run = run_ref[...]
    for c in range(TOKEN_TILE // ROUTE_CHUNK):
        rows = slice(c * ROUTE_CHUNK, (c + 1) * ROUTE_CHUNK)
        route, run = _route_chunk(logits_all[rows], tril_ref[...], run)
        route_ref[rows, :] = route
        h_ref[rows, half:half + LANES] = pltpu.bitcast(route, jnp.int32)
    run_ref[...] = run
    count_ref[...] = run


def _route_chunk(logits, tril, run):
    lane = lax.broadcasted_iota(jnp.int32, logits.shape, 1)
    neg = -jnp.inf
    is_group = (lane >= MOE_EXPERTS) & (lane < MOE_EXPERTS + MOE_GROUPS)
    lg = jnp.where(is_group, logits, neg)
    mg = jnp.max(lg, axis=-1, keepdims=True)
    g_p = 1.0 / jnp.sum(jnp.exp(lg - mg), axis=-1, keepdims=True)
    g_idx = jnp.min(jnp.where(lg == mg, lane, 2 * LANES), axis=-1, keepdims=True) - MOE_EXPERTS
    in_group = (lane >= g_idx * MOE_EXPERTS_PER_GROUP) & (lane < (g_idx + 1) * MOE_EXPERTS_PER_GROUP)
    le = jnp.where(in_group, logits, neg)
    me = jnp.max(le, axis=-1, keepdims=True)
    pe = jnp.exp(le - me)
    pe = pe / jnp.sum(pe, axis=-1, keepdims=True)
    p1 = jnp.max(pe, axis=-1, keepdims=True)
    i1 = jnp.min(jnp.where(in_group & (pe == p1), lane, 2 * LANES), axis=-1, keepdims=True)
    pe2 = jnp.where(in_group & (lane != i1), pe, neg)
    p2 = jnp.max(pe2, axis=-1, keepdims=True)
    i2 = jnp.min(jnp.where(pe2 == p2, lane, 2 * LANES), axis=-1, keepdims=True)
    tot = p1 + p2
    w_lo = g_p * jnp.where(i1 < i2, p1, p2) / tot
    w_hi = g_p * jnp.where(i1 < i2, p2, p1) / tot
    a = jnp.minimum(i1, i2) - g_idx * MOE_EXPERTS_PER_GROUP
    b = jnp.maximum(i1, i2) - g_idx * MOE_EXPERTS_PER_GROUP
    pair = a * (MOE_EXPERTS_PER_GROUP - 1) - ((a * (a - 1)) >> 1) + (b - a - 1)
    bucket = g_idx * MOE_PAIRS + pair
    onehot = lane == bucket
    prefix = jnp.dot(tril, onehot.astype(BF16), preferred_element_type=F32)
    rank = jnp.sum(jnp.where(onehot, prefix + run, 0.0), axis=-1, keepdims=True) - 1.0
    run = run + prefix[ROUTE_CHUNK - 1:ROUTE_CHUNK, :]
    route = jnp.where(lane == 0, bucket.astype(F32),
                      jnp.where(lane == 1, rank,
                                jnp.where(lane == 2, w_lo, jnp.where(lane == 3, w_hi, 0.0))))
    return route, run


def _router(x_all, gain, mods_l, w_router, b_router, n_batch, seq, n_tiles):
    d = x_all.shape[1]
    tril = np.tril(np.ones((ROUTE_CHUNK, ROUTE_CHUNK), np.float32))
    w_hi = w_router.astype(BF16)
    w_split = jnp.stack([w_hi, (w_router - w_hi.astype(F32)).astype(BF16)])
    return pl.pallas_call(
        _router_kernel,
        grid=(n_tiles,),
        in_specs=[
            pl.BlockSpec((TOKEN_TILE, d), lambda i: (i, 0)),
            pl.BlockSpec((1, d), lambda i: (0, 0)),
            pl.BlockSpec((None, N_MOD, d), _mod_index(seq // TOKEN_TILE, n_batch)),
            pl.BlockSpec((2, d, LANES), lambda i: (0, 0, 0)),
            pl.BlockSpec((1, LANES), lambda i: (0, 0)),
            pl.BlockSpec((ROUTE_CHUNK, ROUTE_CHUNK), lambda i: (0, 0)),
        ],
        out_specs=[pl.BlockSpec((TOKEN_TILE, d), lambda i: (i, 0)),
                   pl.BlockSpec((TOKEN_TILE, LANES), lambda i: (i, 0)),
                   pl.BlockSpec((1, LANES), lambda i: (0, 0))],
        out_shape=[jax.ShapeDtypeStruct((n_tiles * TOKEN_TILE, d), jnp.int32),
                   jax.ShapeDtypeStruct((n_tiles * TOKEN_TILE, LANES), F32),
                   jax.ShapeDtypeStruct((1, LANES), F32)],
        scratch_shapes=[pltpu.VMEM((1, LANES), F32)],
        compiler_params=_cparams("arbitrary"),
        name="moe_router",
    )(x_all, gain.reshape(1, d), mods_l, w_split, b_router, jnp.asarray(tril, BF16))


def _row_copy(table_hbm, dst, sem, src_row, dst_row):
    return pltpu.make_async_copy(table_hbm.at[pl.ds(src_row, 1)], dst.at[pl.ds(dst_row, 1)], sem)


def _start_row_gather(idx_ref, base, table_hbm, dst, sem, unrolled):
    rows = dst.shape[0]
    if unrolled:
        for r in range(rows):
            _row_copy(table_hbm, dst, sem, idx_ref[base + r], r).start()
    else:
        def issue(r, carry):
            _row_copy(table_hbm, dst, sem, idx_ref[base + r], r).start()
            return carry

        lax.fori_loop(0, rows, issue, 0, unroll=GATHER_UNROLL)


def _wait_row_gather(table_hbm, dst, sem):
    def drain(r, carry):
        _row_copy(table_hbm, dst, sem, 0, r).wait()
        return carry

    lax.fori_loop(0, dst.shape[0], drain, 0, unroll=GATHER_UNROLL)


def _gather_residual_kernel(idx_ref, y_hbm, x_ref, mod_ref, o_ref, buf, sem):
    _start_row_gather(idx_ref, pl.program_id(0) * buf.shape[0], y_hbm, buf, sem, unrolled=True)
    _wait_row_gather(y_hbm, buf, sem)
    o_ref[...] = x_ref[...] + mod_ref[5:6, :] * buf[...]


def _gather_residual(x_all, y_sorted, dest, mods_l, n_batch, seq, n_tiles):
    d = x_all.shape[1]
    row = pl.BlockSpec((TOKEN_TILE, d), lambda i, idx: (i, 0))
    mod_index = _mod_index(seq // TOKEN_TILE, n_batch)
    return pl.pallas_call(
        _gather_residual_kernel,
        grid_spec=pltpu.PrefetchScalarGridSpec(
            num_scalar_prefetch=1,
            grid=(n_tiles,),
            in_specs=[pl.BlockSpec(memory_space=pl.ANY), row,
                      pl.BlockSpec((None, N_MOD, d), lambda i, idx: mod_index(i))],
            out_specs=row,
            scratch_shapes=[pltpu.VMEM((TOKEN_TILE, d), F32), pltpu.SemaphoreType.DMA],
        ),
        out_shape=jax.ShapeDtypeStruct((n_tiles * TOKEN_TILE, d), F32),
        compiler_params=_cparams("arbitrary"),
        name="moe_combine",
    )(dest, y_sorted, x_all, mods_l)


def _bucket_experts():
    grp, lo, hi = [], [], []
    for g in range(MOE_GROUPS):
        for a in range(MOE_EXPERTS_PER_GROUP):
            for b in range(a + 1, MOE_EXPERTS_PER_GROUP):
                grp.append(g)
                lo.append(a)
                hi.append(b)
    return np.asarray(grp, np.int32), np.asarray(lo, np.int32), np.asarray(hi, np.int32)


def _sorted_experts_kernel(src_ref, grp_ref, lo_ref, hi_ref, nact_ref, tab_hbm, wg_ref, wu_ref, wd_ref,
                           o_ref, xbuf, sems):
    del grp_ref
    t = pl.program_id(0)
    n_active = nact_ref[0]
    n_slots = MOE_GATHER_AHEAD + 1
    slot = t % n_slots
    ahead = (t + MOE_GATHER_AHEAD) % n_slots
    half = D_MODEL // 2

    @pl.when(t == 0)
    def _first_tiles():
        for k in range(MOE_GATHER_AHEAD):
            _start_row_gather(src_ref, k * MOE_SORT_TILE, tab_hbm, xbuf.at[k], sems.at[k], unrolled=False)

    @pl.when(t < n_active + MOE_GATHER_AHEAD)
    def _retire():
        _wait_row_gather(tab_hbm, xbuf.at[slot], sems.at[slot])

    @pl.when(t >= n_active)
    def _unused_tile():
        o_ref[...] = jnp.zeros_like(o_ref)

    @pl.when(t < n_active)
    def _active_tile():
        _start_row_gather(src_ref, (t + MOE_GATHER_AHEAD) * MOE_SORT_TILE, tab_hbm, xbuf.at[ahead],
                          sems.at[ahead], unrolled=True)
        words = pltpu.bitcast(xbuf[slot, :, :half], jnp.uint32)
        x_lo = pltpu.bitcast(words << 16, F32)
        x_hi = pltpu.bitcast(words & jnp.uint32(0xFFFF0000), F32)
        xb = jnp.concatenate([x_lo, x_hi], axis=-1).astype(BF16)
        r = pltpu.bitcast(xbuf[slot, :, half:half + LANES], F32)
        acc = None
        for e_ref, lane in ((lo_ref, 2), (hi_ref, 3)):
            e = e_ref[t]
            gate = jnp.dot(xb, wg_ref[e], preferred_element_type=F32)
            up = jnp.dot(xb, wu_ref[e], preferred_element_type=F32)
            act = _silu(gate) * up * r[:, lane:lane + 1]
            part = jnp.dot(act.astype(BF16), wd_ref[e], preferred_element_type=F32)
            acc = part if acc is None else acc + part
        o_ref[...] = acc


def _sorted_experts(table, src, tile_group, tile_lo, tile_hi, n_active, wg, wu, wd):
    n_rows = src.shape[0]
    group_w = lambda a: pl.BlockSpec((None,) + a.shape[1:], lambda t, src, grp, lo, hi, n: (grp[t], 0, 0, 0))
    return pl.pallas_call(
        _sorted_experts_kernel,
        grid_spec=pltpu.PrefetchScalarGridSpec(
            num_scalar_prefetch=5,
            grid=(n_rows // MOE_SORT_TILE,),
            in_specs=[pl.BlockSpec(memory_space=pl.ANY), group_w(wg), group_w(wu), group_w(wd)],
            out_specs=pl.BlockSpec((MOE_SORT_TILE, D_MODEL), lambda t, src, grp, lo, hi, n: (t, 0)),
            scratch_shapes=[pltpu.VMEM((MOE_GATHER_AHEAD + 1, MOE_SORT_TILE, table.shape[1]), table.dtype),
                            pltpu.SemaphoreType.DMA((MOE_GATHER_AHEAD + 1,))],
        ),
        out_shape=jax.ShapeDtypeStruct((n_rows, D_MODEL), F32),
        compiler_params=_cparams("arbitrary"),
        name="moe_experts",
    )(src, tile_group, tile_lo, tile_hi, n_active, table, wg, wu, wd)


def _moe(x_all, gain, mods_l, w_router, b_router, wg, wu, wd, n_batch, seq, n_tiles):
    n = n_tiles * TOKEN_TILE
    table, route, counts = _router(x_all, gain, mods_l, w_router, b_router, n_batch, seq, n_tiles)

    n_sorted_tiles = (n + MOE_BUCKETS * (MOE_SORT_TILE - 1)) // MOE_SORT_TILE + MOE_GATHER_AHEAD
    n_sorted = n_sorted_tiles * MOE_SORT_TILE
    counts = counts[0, :MOE_BUCKETS].astype(jnp.int32)
    bucket_tiles = (counts + MOE_SORT_TILE - 1) // MOE_SORT_TILE
    tile_end = jnp.cumsum(bucket_tiles)
    row_start = (tile_end - bucket_tiles) * MOE_SORT_TILE
    bucket = route[:, 0].astype(jnp.int32)
    rank = route[:, 1].astype(jnp.int32)
    dest = jnp.sum(jnp.where(bucket[:, None] == jnp.arange(MOE_BUCKETS)[None, :], row_start[None, :], 0),
                   axis=1) + rank
    src = (jnp.arange(n_sorted, dtype=jnp.int32) % n).at[dest].set(jnp.arange(n, dtype=jnp.int32))
    tiles = jnp.arange(n_sorted_tiles, dtype=jnp.int32)
    tile_bucket = jnp.sum(jnp.minimum(tiles, tile_end[-1] - 1)[:, None] >= tile_end[None, :], axis=1)
    tile_bucket = jnp.minimum(tile_bucket, MOE_BUCKETS - 1)
    grp_ids, lo_ids, hi_ids = _bucket_experts()
    tile_group = jnp.asarray(grp_ids)[tile_bucket]
    tile_lo = jnp.asarray(lo_ids)[tile_bucket]
    tile_hi = jnp.asarray(hi_ids)[tile_bucket]

    n_active = tile_end[-1:].astype(jnp.int32)
    y_sorted = _sorted_experts(table, src, tile_group, tile_lo, tile_hi, n_active, wg, wu, wd)
    return _gather_residual(x_all, y_sorted, dest, mods_l, n_batch, seq, n_tiles)


def kernel(x, c, ctx, c_ctx, w_ada, b_ada, g_mix, g_ffn, w_in, w_out, gmlp_v_gain, gmlp_ws, gmlp_bs,
           na_q_gain, na_k_gain, na_rpb, hy_short_w, hy_short_b, hy_w1, hy_b1, hy_w2, hy_b2, hy_w3,
           hy_freq, hy_bias, moe_w_rg, moe_b_rg, moe_w_re, moe_b_re, moe_w_gate, moe_w_up, moe_w_down):
    n_batch, seq, d = x.shape
    ctx_len = ctx.shape[1]
    depth = w_ada.shape[0]
    n_lat = n_batch * seq
    n_ctx = n_batch * ctx_len
    assert d == D_MODEL and seq % TOKEN_TILE == 0 and n_ctx % TOKEN_TILE == 0
    assert seq % GMLP_CHUNK == 0 and ctx_len % GMLP_CHUNK == 0

    pad_rows = -(n_batch + 1) % 8
    cs = jnp.concatenate([c, c_ctx[None, :], jnp.zeros((pad_rows, d), F32)], axis=0)
    mods = _modulation(cs, w_ada, b_ada)[:, :n_batch + 1].reshape(depth, n_batch + 1, N_MOD, d)

    x_all = jnp.concatenate([x.reshape(n_lat, d), ctx.reshape(n_ctx, d)], axis=0)
    n_all_tiles = (n_lat + n_ctx) // TOKEN_TILE
    n_lat_tiles = n_lat // TOKEN_TILE

    mats_lat = _dft_matrices(seq)
    mats_ctx = _dft_matrices(ctx_len)
    pos_lat = _hyena_positions(seq)
    pos_ctx = _hyena_positions(ctx_len)

    w_in_bf = w_in.astype(BF16)
    w_out_bf = w_out.astype(BF16)
    wg_bf = moe_w_gate.astype(BF16)
    wu_bf = moe_w_up.astype(BF16)
    wd_bf = moe_w_down.astype(BF16)
    lane_pad = LANES - MOE_EXPERTS - MOE_GROUPS
    w_router = jnp.pad(jnp.concatenate([moe_w_re, moe_w_rg], axis=-1), ((0, 0), (0, 0), (0, lane_pad)))
    b_router = jnp.pad(jnp.concatenate([moe_b_re, moe_b_rg], axis=-1), ((0, 0), (0, lane_pad)))[:, None, :]
    w1_pad = jnp.pad(hy_w1, ((0, 0), (0, LANES - HYENA_EMB), (0, 0)))

    for l in range(depth):
        last = l == depth - 1
        mods_l = mods[l]
        n_tiles = n_lat_tiles if last else n_all_tiles

        p_all = _in_proj(x_all, g_mix[l], mods_l, w_in_bf[l], n_batch, seq)

        filt = (w1_pad[l], hy_b1[l], hy_w2[l], hy_b2[l], hy_w3[l], hy_freq[l])
        kpack_lat = _filter_dft(mats_lat[0], mats_lat[1], _hyena_filters(*pos_lat, *filt))
        hy_lat = _hyena(p_all, hy_short_w[l], hy_short_b[l], hy_bias[l], kpack_lat, mats_lat,
                        n_batch, seq, 0)
        if last:
            hy_ctx = hy_lat
            gm = _gmlp(p_all, n_lat, gmlp_v_gain[l], gmlp_ws[l], gmlp_bs[l].T)
        else:
            kpack_ctx = _filter_dft(mats_ctx[0], mats_ctx[1], _hyena_filters(*pos_ctx, *filt))
            hy_ctx = _hyena(p_all, hy_short_w[l], hy_short_b[l], hy_bias[l], kpack_ctx, mats_ctx,
                            n_batch, ctx_len, n_lat // ctx_len)
            gm = _gmlp(p_all, n_lat + n_ctx, gmlp_v_gain[l], gmlp_ws[l], gmlp_bs[l].T)

        na = _attention(p_all, na_rpb[l], na_q_gain[l], na_k_gain[l],
                        n_batch, seq, ctx_len, not last)

        x_all = _out_proj(x_all, gm, na, hy_lat, hy_ctx, w_out_bf[l], mods_l, n_batch, seq, n_tiles)

        x_all = _moe(x_all, g_ffn[l], mods_l, w_router[l], b_router[l], wg_bf[l], wu_bf[l], wd_bf[l],
                     n_batch, seq, n_tiles)

    return x_all[:n_lat].reshape(n_batch, seq, d)
```

```python
import functools
import math

import numpy as np
import jax
import jax.numpy as jnp
from jax import lax
from jax.experimental import pallas as pl
from jax.experimental.pallas import tpu as pltpu

F32 = jnp.float32
BF16 = jnp.bfloat16
HI = lax.Precision.HIGHEST

D_MODEL = 1024
GRID_W = 64
D_GMLP = D_MODEL // 4
D_NA = D_MODEL // 2
D_HYENA = D_MODEL // 4
D_IN = 2 * D_GMLP + 3 * D_NA + 3 * D_HYENA
Q_START = 2 * D_GMLP
KV_START = 2 * D_GMLP + D_NA
HY_START = 2 * D_GMLP + 3 * D_NA
GMLP_GROUPS = 4
GMLP_GROUP_DIM = D_GMLP // GMLP_GROUPS
GMLP_CHUNK = 128
NA_HEAD_DIM = 64
NA_HEADS = D_NA // NA_HEAD_DIM
NA_SCALE = NA_HEAD_DIM ** -0.5
NA_WIN_ROWS = 8
NA_WIN_COLS = 16
HYENA_ORDER = 2
HYENA_POS_BANDS = 16
HYENA_EMB = 1 + 2 * HYENA_POS_BANDS
HYENA_FILTER_HIDDEN = 64
HYENA_DECAY_TARGET = 1e-2
HYENA_FAST_DECAY = 0.3
HYENA_SLOW_DECAY = 1.5
MOE_GROUPS = 4
MOE_EXPERTS_PER_GROUP = 8
MOE_EXPERTS = MOE_GROUPS * MOE_EXPERTS_PER_GROUP
MOE_HIDDEN = 256
MOE_PAIRS = MOE_EXPERTS_PER_GROUP * (MOE_EXPERTS_PER_GROUP - 1) // 2
MOE_BUCKETS = MOE_GROUPS * MOE_PAIRS
N_MOD = 6
RMS_EPS = 1e-6
LN_EPS = 1e-5

LANES = 128
TOKEN_TILE = 512
NA_Q_ROWS = 4
NA_Q_BLOCK = NA_Q_ROWS * GRID_W
NA_BAND_ROWS = NA_WIN_ROWS + NA_Q_ROWS
DFT_TILE = 512
CONV_TILE = 1024
ROUTE_CHUNK = 128
MOE_SORT_TILE = 128
MOE_GATHER_AHEAD = 1
GATHER_UNROLL = 8
MASK_VALUE = -1e30
VMEM_LIMIT = 56 * 1024 * 1024


def _cparams(*sem):
    return pltpu.CompilerParams(dimension_semantics=sem, vmem_limit_bytes=VMEM_LIMIT)


def _silu(x):
    return x * jax.nn.sigmoid(x)


def _rms_rows(x):
    return x * lax.rsqrt(jnp.mean(x * x, axis=-1, keepdims=True) + RMS_EPS)


def _mods_kernel(cs_ref, w_ref, b_ref, o_ref):
    s = _silu(cs_ref[...])
    o_ref[...] = jnp.dot(s, w_ref[...], preferred_element_type=F32, precision=HI) + b_ref[...]


def _modulation(cs, w_ada, b_ada):
    depth, d, nd = w_ada.shape
    rows = cs.shape[0]
    col = 1024
    return pl.pallas_call(
        _mods_kernel,
        grid=(depth, nd // col),
        in_specs=[
            pl.BlockSpec((rows, d), lambda l, j: (0, 0)),
            pl.BlockSpec((None, d, col), lambda l, j: (l, 0, j)),
            pl.BlockSpec((None, 1, col), lambda l, j: (l, 0, j)),
        ],
        out_specs=pl.BlockSpec((None, rows, col), lambda l, j: (l, 0, j)),
        out_shape=jax.ShapeDtypeStruct((depth, rows, nd), F32),
        compiler_params=_cparams("arbitrary", "arbitrary"),
        name="modulation",
    )(cs, w_ada, b_ada.reshape(depth, 1, nd))


def _in_proj_kernel(x_ref, g_ref, mod_ref, w_ref, o_ref):
    mod = mod_ref[...]
    h = _rms_rows(x_ref[...]) * g_ref[...]
    h = h * (1.0 + mod[1:2]) + mod[0:1]
    o_ref[...] = jnp.dot(h.astype(BF16), w_ref[...], preferred_element_type=F32).astype(o_ref.dtype)


def _mod_index(tiles_per_batch, n_batch):
    return lambda i: (jnp.minimum(i // tiles_per_batch, n_batch), 0, 0)


def _in_proj(x_all, gain, mods_l, w_bf, n_batch, seq):
    n, d = x_all.shape
    d_in = w_bf.shape[1]
    return pl.pallas_call(
        _in_proj_kernel,
        grid=(n // TOKEN_TILE,),
        in_specs=[
            pl.BlockSpec((TOKEN_TILE, d), lambda i: (i, 0)),
            pl.BlockSpec((1, d), lambda i: (0, 0)),
            pl.BlockSpec((None, N_MOD, d), _mod_index(seq // TOKEN_TILE, n_batch)),
            pl.BlockSpec((d, d_in), lambda i: (0, 0)),
        ],
        out_specs=pl.BlockSpec((TOKEN_TILE, d_in), lambda i: (i, 0)),
        out_shape=jax.ShapeDtypeStruct((n, d_in), BF16),
        compiler_params=_cparams("arbitrary"),
        name="in_proj",
    )(x_all, gain.reshape(1, d), mods_l, w_bf)


def _group_avg_matrix(groups, width):
    return np.kron(np.eye(groups), np.full((width, width), 1.0 / width))


def _group_mean(t, avg):
    hi = t.astype(BF16)
    lo = (t - hi.astype(F32)).astype(BF16)
    return (jnp.dot(hi, avg, preferred_element_type=F32) + jnp.dot(lo, avg, preferred_element_type=F32))


def _gmlp_kernel(u_ref, v_ref, gain_ref, avg_ref, ws_ref, bs_ref, o_ref):
    avg = avg_ref[...]
    v = jax.nn.gelu(v_ref[...].astype(F32))
    v = v - _group_mean(v, avg)
    v = v * lax.rsqrt(_group_mean(v * v, avg) + LN_EPS) * gain_ref[...]
    vb = v.astype(BF16)
    bs = bs_ref[...]
    for c in range(u_ref.shape[0] // GMLP_CHUNK):
        rows = slice(c * GMLP_CHUNK, (c + 1) * GMLP_CHUNK)
        outs = []
        for g in range(GMLP_GROUPS):
            cols = slice(g * GMLP_GROUP_DIM, (g + 1) * GMLP_GROUP_DIM)
            s = jnp.dot(ws_ref[g], vb[rows, cols], preferred_element_type=F32) + bs[:, g:g + 1]
            outs.append(s)
        o_ref[rows, :] = jax.nn.gelu(u_ref[rows, :].astype(F32)) * jnp.concatenate(outs, axis=-1)


def _gmlp(p_all, n, v_gain, ws, bs_t):
    avg = _group_avg_matrix(GMLP_GROUPS, GMLP_GROUP_DIM)
    return pl.pallas_call(
        _gmlp_kernel,
        grid=(n // TOKEN_TILE,),
        in_specs=[
            pl.BlockSpec((TOKEN_TILE, D_GMLP), lambda i: (i, 0)),
            pl.BlockSpec((TOKEN_TILE, D_GMLP), lambda i: (i, 1)),
            pl.BlockSpec((1, D_GMLP), lambda i: (0, 0)),
            pl.BlockSpec((D_GMLP, D_GMLP), lambda i: (0, 0)),
            pl.BlockSpec((GMLP_GROUPS, GMLP_CHUNK, GMLP_CHUNK), lambda i: (0, 0, 0)),
            pl.BlockSpec((GMLP_CHUNK, GMLP_GROUPS), lambda i: (0, 0)),
        ],
        out_specs=pl.BlockSpec((TOKEN_TILE, D_GMLP), lambda i: (i, 0)),
        out_shape=jax.ShapeDtypeStruct((n, D_GMLP), F32),
        compiler_params=_cparams("arbitrary"),
        name="gmlp",
    )(p_all, p_all, v_gain.reshape(1, D_GMLP), jnp.asarray(avg, BF16), ws.astype(BF16), bs_t)


def _rpb_expand_kernel(rpb_ref, sel_ref, o_ref):
    o_ref[...] = jnp.dot(rpb_ref[...], sel_ref[...], preferred_element_type=F32, precision=HI)


def _na_geometry(grid_rows):
    variants, step_variant, band_start = [], [], []
    for r0 in range(0, grid_rows, NA_Q_ROWS):
        b0 = int(np.clip(r0 - NA_WIN_ROWS // 2, 0, grid_rows - NA_BAND_ROWS))
        geo = []
        for r in range(r0, r0 + NA_Q_ROWS):
            wr = int(np.clip(r - NA_WIN_ROWS // 2, 0, grid_rows - NA_WIN_ROWS))
            assert b0 <= wr and wr + NA_WIN_ROWS <= b0 + NA_BAND_ROWS
            geo.append((wr - b0, wr - r + NA_WIN_ROWS - 1))
        geo = tuple(geo)
        if geo not in variants:
            variants.append(geo)
        step_variant.append(variants.index(geo))
        band_start.append(b0)
    return variants, step_variant, band_start


def _na_bias_tables(rpb, variants):
    n_heads, n_dr, n_dc = rpb.shape
    qcol = np.arange(GRID_W)[:, None]
    kcol = np.arange(GRID_W)[None, :]
    win_c = np.clip(qcol - NA_WIN_COLS // 2, 0, GRID_W - NA_WIN_COLS)
    col_ok = (kcol >= win_c) & (kcol < win_c + NA_WIN_COLS)
    dc = np.clip(kcol - qcol + NA_WIN_COLS - 1, 0, 2 * NA_WIN_COLS - 2)
    dc_pad = -n_dc % 8
    sel = (np.arange(n_dc + dc_pad)[:, None] == dc.reshape(1, -1)).astype(np.float32)
    rpb2 = jnp.pad(rpb.reshape(n_heads * n_dr, n_dc), ((0, 0), (0, dc_pad)))
    toep = pl.pallas_call(
        _rpb_expand_kernel,
        out_shape=jax.ShapeDtypeStruct((n_heads * n_dr, GRID_W * GRID_W), F32),
        name="rpb_expand",
    )(rpb2, jnp.asarray(sel))
    toep = toep.reshape(n_heads, n_dr, GRID_W, GRID_W)
    return pl.pallas_call(
        functools.partial(_bias_table_kernel, variants=tuple(variants)),
        grid=(n_heads,),
        in_specs=[pl.BlockSpec((None, n_dr, GRID_W, GRID_W), lambda h: (h, 0, 0, 0)),
                  pl.BlockSpec((GRID_W, GRID_W), lambda h: (0, 0))],
        out_specs=pl.BlockSpec((None, len(variants), NA_Q_BLOCK, NA_BAND_ROWS * GRID_W),
                               lambda h: (h, 0, 0, 0)),
        out_shape=jax.ShapeDtypeStruct((n_heads, len(variants), NA_Q_BLOCK, NA_BAND_ROWS * GRID_W), F32),
        compiler_params=_cparams("arbitrary"),
        name="na_bias_table",
    )(toep, jnp.asarray(col_ok.astype(np.float32)))


def _bias_table_kernel(toep_ref, ok_ref, o_ref, *, variants):
    ok = ok_ref[...] > 0.0
    outside = jnp.full((GRID_W, GRID_W), MASK_VALUE, F32)
    blocks = [jnp.where(ok, toep_ref[dr], MASK_VALUE) for dr in range(toep_ref.shape[0])]
    for v, geo in enumerate(variants):
        for i, (a0, dr0) in enumerate(geo):
            pieces = [blocks[dr0 + a - a0] if a0 <= a < a0 + NA_WIN_ROWS else outside
                      for a in range(NA_BAND_ROWS)]
            o_ref[v, i * GRID_W:(i + 1) * GRID_W, :] = jnp.concatenate(pieces, axis=-1)


def _store_heads(dst, rows, t, gain, avg):
    if gain is not None:
        t = t.astype(F32)
        t = t * lax.rsqrt(_group_mean(t * t, avg) + RMS_EPS) * gain
    for h in range(NA_HEADS):
        dst[h, rows, :] = t[:, h * NA_HEAD_DIM:(h + 1) * NA_HEAD_DIM].astype(BF16)


def _na_kernel(q_ref, k_ref, v_ref, kc_ref, vc_ref, bias_ref, qg_ref, kg_ref, avg_ref, o_ref,
               kn_s, vb_s, kcn_s, vcb_s, qn_s, o_s, *, n_lat_steps, step_variant, band_start):
    step = pl.program_id(1)
    kg = kg_ref[...]
    avg = avg_ref[...]
    nt = (((1,), (1,)), ((), ()))
    band_keys = NA_BAND_ROWS * GRID_W
    all_rows = slice(None)

    @pl.when(step == 0)
    def _prepare_keys():
        chunk = 256

        def body(c, carry):
            rows = pl.ds(pl.multiple_of(c * chunk, chunk), chunk)
            _store_heads(kn_s, rows, k_ref[rows, :], kg, avg)
            _store_heads(vb_s, rows, v_ref[rows, :], None, None)
            return carry

        lax.fori_loop(0, k_ref.shape[0] // chunk, body, 0)
        _store_heads(kcn_s, all_rows, kc_ref[...], kg, avg)
        _store_heads(vcb_s, all_rows, vc_ref[...], None, None)

    _store_heads(qn_s, all_rows, q_ref[...], qg_ref[...] * NA_SCALE, avg)

    def finish(h, scores, v_parts):
        m = jnp.max(scores, axis=-1, keepdims=True)
        p = jnp.exp(scores - m)
        denom = jnp.sum(p, axis=-1, keepdims=True)
        pb = p.astype(BF16)
        acc = None
        col = 0
        for v in v_parts:
            part = jnp.dot(pb[:, col:col + v.shape[0]], v, preferred_element_type=F32)
            acc = part if acc is None else acc + part
            col += v.shape[0]
        o_s[h] = acc / denom

    @pl.when(step < n_lat_steps)
    def _latent_queries():
        variant = jnp.int32(0)
        band0 = jnp.int32(0)
        for s_, (v_, b_) in enumerate(zip(step_variant, band_start)):
            variant = jnp.where(step == s_, v_, variant)
            band0 = jnp.where(step == s_, b_ * GRID_W, band0)
        krows = pl.ds(pl.multiple_of(band0, NA_Q_BLOCK), band_keys)

        def head_body(h, carry):
            qh = qn_s[h]
            s_w = lax.dot_general(qh, kn_s[h, krows, :], nt, preferred_element_type=F32)
            s_w = s_w + bias_ref[h, variant]
            s_c = lax.dot_general(qh, kcn_s[h], nt, preferred_element_type=F32)
            finish(h, jnp.concatenate([s_w, s_c], axis=-1), (vb_s[h, krows, :], vcb_s[h]))
            return carry

        lax.fori_loop(0, NA_HEADS, head_body, 0, unroll=2)

    @pl.when(step >= n_lat_steps)
    def _context_queries():
        def head_body(h, carry):
            s = lax.dot_general(qn_s[h], kcn_s[h], nt, preferred_element_type=F32)
            finish(h, s, (vcb_s[h],))
            return carry

        lax.fori_loop(0, NA_HEADS, head_body, 0)

    o_ref[...] = jnp.concatenate([o_s[h] for h in range(NA_HEADS)], axis=-1)


def _attention(p_all, rpb, q_gain, k_gain, n_batch, seq, ctx_len, with_ctx_queries):
    n = p_all.shape[0]
    assert ctx_len == NA_Q_BLOCK and seq % NA_Q_BLOCK == 0
    n_lat_steps = seq // NA_Q_BLOCK
    n_steps = n_lat_steps + (1 if with_ctx_queries else 0)
    ctx_block0 = n_batch * seq // ctx_len
    qcol, kcol, vcol = Q_START // D_NA, KV_START // D_NA, (KV_START + D_NA) // D_NA
    variants, step_variant, band_start = _na_geometry(seq // GRID_W)
    bias_tab = _na_bias_tables(rpb, variants)

    def q_index(col):
        return lambda b, s: (jnp.where(s < n_lat_steps, b * n_lat_steps + s, ctx_block0 + b), col)

    kern = functools.partial(_na_kernel, n_lat_steps=n_lat_steps, step_variant=tuple(step_variant),
                             band_start=tuple(band_start))
    head_major = lambda rows: pltpu.VMEM((NA_HEADS, rows, NA_HEAD_DIM), BF16)
    return pl.pallas_call(
        kern,
        grid=(n_batch, n_steps),
        in_specs=[
            pl.BlockSpec((NA_Q_BLOCK, D_NA), q_index(qcol)),
            pl.BlockSpec((seq, D_NA), lambda b, s: (b, kcol)),
            pl.BlockSpec((seq, D_NA), lambda b, s: (b, vcol)),
            pl.BlockSpec((ctx_len, D_NA), lambda b, s: (ctx_block0 + b, kcol)),
            pl.BlockSpec((ctx_len, D_NA), lambda b, s: (ctx_block0 + b, vcol)),
            pl.BlockSpec(bias_tab.shape, lambda b, s: (0, 0, 0, 0), pipeline_mode=pl.Buffered(1)),
            pl.BlockSpec((1, D_NA), lambda b, s: (0, 0)),
            pl.BlockSpec((1, D_NA), lambda b, s: (0, 0)),
            pl.BlockSpec((D_NA, D_NA), lambda b, s: (0, 0)),
        ],
        out_specs=pl.BlockSpec((NA_Q_BLOCK, D_NA), q_index(0)),
        out_shape=jax.ShapeDtypeStruct((n if with_ctx_queries else n_batch * seq, D_NA), F32),
        scratch_shapes=[
            head_major(seq), head_major(seq), head_major(ctx_len), head_major(ctx_len),
            head_major(NA_Q_BLOCK),
            pltpu.VMEM((NA_HEADS, NA_Q_BLOCK, NA_HEAD_DIM), F32),
        ],
        compiler_params=_cparams("arbitrary", "arbitrary"),
        name="attention",
    )(p_all, p_all, p_all, p_all, p_all, bias_tab,
      jnp.tile(q_gain.reshape(1, NA_HEAD_DIM), (1, NA_HEADS)),
      jnp.tile(k_gain.reshape(1, NA_HEAD_DIM), (1, NA_HEADS)),
      jnp.asarray(_group_avg_matrix(NA_HEADS, NA_HEAD_DIM), BF16))


def _dft_matrices(length):
    idx = jnp.arange(length, dtype=jnp.int32)
    step = 64
    t_hi = jnp.arange(length // step, dtype=jnp.int32) * step
    t_lo = jnp.arange(step, dtype=jnp.int32)
    ang_hi = ((idx[:, None] * t_hi[None, :]) % (2 * length)).astype(F32) * (math.pi / length)
    ang_lo = ((idx[:, None] * t_lo[None, :]) % (2 * length)).astype(F32) * (math.pi / length)
    c_hi, s_hi = jnp.cos(ang_hi)[:, :, None], jnp.sin(ang_hi)[:, :, None]
    c_lo, s_lo = jnp.cos(ang_lo)[:, None, :], jnp.sin(ang_lo)[:, None, :]
    gc = (c_hi * c_lo - s_hi * s_lo).reshape(length, length)
    gs = (s_hi * c_lo + c_hi * s_lo).reshape(length, length)
    nyq = jnp.where(idx % 2 == 0, 1.0, -1.0).astype(F32)
    gs = jnp.where(idx[:, None] == 0, nyq[None, :], gs)
    return gc.astype(BF16), gs.astype(BF16), gs.T.astype(BF16)


def _hyena_positions(length):
    t = jnp.linspace(0.0, 1.0, length, dtype=F32)[:, None]
    w = 2.0 * math.pi * jnp.arange(length, dtype=F32)[:, None] / length
    f = jnp.linspace(1e-4, HYENA_POS_BANDS - 1, HYENA_POS_BANDS, dtype=F32)[None, :]
    z = jnp.concatenate([t, jnp.cos(f * w), -jnp.sin(f * w)], axis=-1)
    z = jnp.pad(z, ((0, 0), (0, LANES - HYENA_EMB)))
    min_decay = math.log(HYENA_DECAY_TARGET) / HYENA_SLOW_DECAY
    max_decay = math.log(HYENA_DECAY_TARGET) / HYENA_FAST_DECAY
    deltas = jnp.abs(jnp.linspace(min_decay, max_decay, D_HYENA, dtype=F32))[None, :]
    return z, jnp.exp(-t * deltas)


def _filter_kernel(z_ref, decay_ref, w1_ref, b1_ref, w2_ref, b2_ref, w3_ref, freq_ref, o_ref):
    freq = freq_ref[...]
    hdn = jnp.dot(z_ref[...], w1_ref[...], preferred_element_type=F32, precision=HI) + b1_ref[...]
    hdn = jnp.sin(freq[0:1] * hdn)
    hdn = jnp.dot(hdn, w2_ref[...], preferred_element_type=F32, precision=HI) + b2_ref[...]
    hdn = jnp.sin(freq[1:2] * hdn)
    h = jnp.dot(hdn, w3_ref[...], preferred_element_type=F32, precision=HI)
    decay = decay_ref[...]
    first_row = lax.broadcasted_iota(jnp.int32, decay.shape, 0) == 0
    outs = []
    for n in range(HYENA_ORDER):
        base = 2 * n * D_HYENA
        hf = h[:, base:base + D_HYENA] * decay
        hb = h[:, base + D_HYENA:base + 2 * D_HYENA] * decay
        norm = jnp.sum(jnp.abs(hf), axis=0, keepdims=True) + jnp.sum(jnp.abs(hb), axis=0, keepdims=True)
        outs.append(hf / norm)
        outs.append(jnp.where(first_row, 0.0, hb / norm))
    o_ref[...] = jnp.concatenate(outs, axis=-1)


def _hyena_filters(z, decay, w1p, b1, w2, b2, w3, freq):
    length = z.shape[0]
    full = lambda a: pl.BlockSpec(a.shape, lambda i: (0,) * a.ndim)
    args = (z, decay, w1p, b1.reshape(1, -1), w2, b2.reshape(1, -1), w3, freq)
    return pl.pallas_call(
        _filter_kernel,
        grid=(1,),
        in_specs=[full(a) for a in args],
        out_specs=pl.BlockSpec((length, 2 * HYENA_ORDER * D_HYENA), lambda i: (0, 0)),
        out_shape=jax.ShapeDtypeStruct((length, 2 * HYENA_ORDER * D_HYENA), F32),
        compiler_params=_cparams("arbitrary"),
        name="hyena_filter",
    )(*args)


def _filter_dft_kernel(gc_ref, gs_ref, h_ref, o_ref, *, length):
    hb = h_ref[...].astype(BF16)
    fa = jnp.dot(gc_ref[...], hb, preferred_element_type=F32)
    fb = jnp.dot(gs_ref[...], hb, preferred_element_type=F32)
    rows = lax.broadcasted_iota(jnp.int32, (fa.shape[0], D_HYENA), 0) + pl.program_id(0) * fa.shape[0]
    dc_row = rows == 0
    inv_n = 1.0 / (2 * length)
    outs = []
    for n in range(HYENA_ORDER):
        base = 2 * n * D_HYENA
        f_sl = slice(base, base + D_HYENA)
        b_sl = slice(base + D_HYENA, base + 2 * D_HYENA)
        kr = fa[:, f_sl] + fa[:, b_sl]
        ki = fb[:, b_sl] - fb[:, f_sl]
        k_nyq = fb[:, f_sl] + fb[:, b_sl]
        outs.append(jnp.where(dc_row, kr * inv_n, 2.0 * inv_n * kr))
        outs.append(jnp.where(dc_row, 0.0, 2.0 * inv_n * ki))
        outs.append(jnp.where(dc_row, 0.0, -2.0 * inv_n * ki))
        outs.append(jnp.where(dc_row, k_nyq * inv_n, 2.0 * inv_n * kr))
    o_ref[...] = jnp.concatenate(outs, axis=-1)


def _filter_dft(gc, gs, hfilt):
    length = gc.shape[0]
    tile = min(DFT_TILE, length)
    width = 4 * HYENA_ORDER * D_HYENA
    return pl.pallas_call(
        functools.partial(_filter_dft_kernel, length=length),
        grid=(length // tile,),
        in_specs=[
            pl.BlockSpec((tile, length), lambda j: (j, 0)),
            pl.BlockSpec((tile, length), lambda j: (j, 0)),
            pl.BlockSpec(hfilt.shape, lambda j: (0, 0)),
        ],
        out_specs=pl.BlockSpec((tile, width), lambda j: (j, 0)),
        out_shape=jax.ShapeDtypeStruct((length, width), F32),
        compiler_params=_cparams("arbitrary"),
        name="hyena_filter_dft",
    )(gc, gs, hfilt)


def _short_conv_kernel(a0_ref, a1_ref, a2_ref, w_ref, b_ref, o_ref):
    w = w_ref[...]
    b = b_ref[...]
    length = a0_ref.shape[0]
    rows = lax.broadcasted_iota(jnp.int32, (length, D_HYENA), 0)
    for j, a_ref in enumerate((a0_ref, a1_ref, a2_ref)):
        cols = slice(j * D_HYENA, (j + 1) * D_HYENA)
        a = a_ref[...].astype(F32)
        prev = jnp.where(rows == 0, 0.0, pltpu.roll(a, 1, 0))
        nxt = jnp.where(rows == length - 1, 0.0, pltpu.roll(a, length - 1, 0))
        o_ref[:, cols] = prev * w[0:1, cols] + a * w[1:2, cols] + nxt * w[2:3, cols] + b[:, cols]


def _short_conv(p_all, short_w, short_b, n_batch, length, row_block0):
    c0 = HY_START // D_HYENA
    spec = lambda j: pl.BlockSpec((length, D_HYENA), lambda b: (row_block0 + b, c0 + j))
    return pl.pallas_call(
        _short_conv_kernel,
        grid=(n_batch,),
        in_specs=[spec(0), spec(1), spec(2),
                  pl.BlockSpec((3, 3 * D_HYENA), lambda b: (0, 0)),
                  pl.BlockSpec((1, 3 * D_HYENA), lambda b: (0, 0))],
        out_specs=pl.BlockSpec((length, 3 * D_HYENA), lambda b: (b, 0)),
        out_shape=jax.ShapeDtypeStruct((n_batch * length, 3 * D_HYENA), F32),
        compiler_params=_cparams("arbitrary"),
        name="hyena_short_conv",
    )(p_all, p_all, p_all, short_w, short_b.reshape(1, -1))


def _conv_fwd_kernel(gc_ref, gs_ref, u_ref, k_ref, pa_ref, pb_ref):
    u = u_ref[...].astype(BF16)
    a = jnp.dot(gc_ref[...], u, preferred_element_type=F32)
    b = jnp.dot(gs_ref[...], u, preferred_element_type=F32)
    k = k_ref[...]
    c = D_HYENA
    pa_ref[...] = (a * k[:, 0:c] + b * k[:, c:2 * c]).astype(BF16)
    pb_ref[...] = (a * k[:, 2 * c:3 * c] + b * k[:, 3 * c:4 * c]).astype(BF16)


def _conv_fwd(gc, gs, u, u_col, kpack, order, n_batch):
    length = gc.shape[0]
    tile = min(CONV_TILE, length)
    nt = length // tile
    out = jax.ShapeDtypeStruct((n_batch * length, D_HYENA), BF16)
    return pl.pallas_call(
        _conv_fwd_kernel,
        grid=(nt, n_batch),
        in_specs=[
            pl.BlockSpec((tile, length), lambda j, b: (j, 0)),
            pl.BlockSpec((tile, length), lambda j, b: (j, 0)),
            pl.BlockSpec((length, D_HYENA), lambda j, b: (b, u_col)),
            pl.BlockSpec((tile, 4 * D_HYENA), lambda j, b: (j, order)),
        ],
        out_specs=[pl.BlockSpec((tile, D_HYENA), lambda j, b: (b * nt + j, 0))] * 2,
        out_shape=[out, out],
        compiler_params=_cparams("arbitrary", "arbitrary"),
        name="hyena_conv_fwd",
    )(gc, gs, u, kpack)


def _conv_inv_kernel(gc_ref, gst_ref, pa_ref, pb_ref, z_ref, gate_ref, d_ref, o_ref):
    y = jnp.dot(gc_ref[...], pa_ref[...], preferred_element_type=F32)
    y = y + jnp.dot(gst_ref[...], pb_ref[...], preferred_element_type=F32)
    o_ref[...] = gate_ref[...] * (y + d_ref[...] * z_ref[...])


def _conv_inv(gc, gst, pa, pb, z_prev, z_col, a3, gate_col, d_bias, n_batch):
    length = gc.shape[0]
    tile = min(CONV_TILE, length)
    nt = length // tile
    return pl.pallas_call(
        _conv_inv_kernel,
        grid=(nt, n_batch),
        in_specs=[
            pl.BlockSpec((tile, length), lambda j, b: (j, 0)),
            pl.BlockSpec((tile, length), lambda j, b: (j, 0)),
            pl.BlockSpec((length, D_HYENA), lambda j, b: (b, 0)),
            pl.BlockSpec((length, D_HYENA), lambda j, b: (b, 0)),
            pl.BlockSpec((tile, D_HYENA), lambda j, b: (b * nt + j, z_col)),
            pl.BlockSpec((tile, D_HYENA), lambda j, b: (b * nt + j, gate_col)),
            pl.BlockSpec((1, D_HYENA), lambda j, b: (0, 0)),
        ],
        out_specs=pl.BlockSpec((tile, D_HYENA), lambda j, b: (b * nt + j, 0)),
        out_shape=jax.ShapeDtypeStruct((n_batch * length, D_HYENA), F32),
        compiler_params=_cparams("arbitrary", "arbitrary"),
        name="hyena_conv_inv",
    )(gc, gst, pa, pb, z_prev, a3, d_bias.reshape(1, D_HYENA))


def _hyena(p_all, short_w, short_b, d_bias, kpack, mats, n_batch, length, row_block0):
    gc, gs, gst = mats
    a3 = _short_conv(p_all, short_w, short_b, n_batch, length, row_block0)
    pa, pb = _conv_fwd(gc, gs, a3, 0, kpack, 0, n_batch)
    z1 = _conv_inv(gc, gst, pa, pb, a3, 0, a3, 1, d_bias[0], n_batch)
    pa, pb = _conv_fwd(gc, gs, z1, 0, kpack, 1, n_batch)
    return _conv_inv(gc, gst, pa, pb, z1, 0, a3, 2, d_bias[1], n_batch)


def _out_proj_kernel(x_ref, gm_ref, na_ref, hyl_ref, hyc_ref, w_ref, mod_ref, o_ref, *, n_lat_tiles):
    is_lat = pl.program_id(0) < n_lat_tiles
    hy = jnp.where(is_lat, hyl_ref[...], hyc_ref[...])
    y = jnp.dot(gm_ref[...].astype(BF16), w_ref[0:D_GMLP, :], preferred_element_type=F32)
    y = y + jnp.dot(na_ref[...].astype(BF16), w_ref[D_GMLP:D_GMLP + D_NA, :], preferred_element_type=F32)
    y = y + jnp.dot(hy.astype(BF16), w_ref[D_GMLP + D_NA:, :], preferred_element_type=F32)
    o_ref[...] = x_ref[...] + mod_ref[2:3, :] * y


def _out_proj(x_all, gm, na, hy_lat, hy_ctx, w_bf, mods_l, n_batch, seq, n_tiles):
    d = x_all.shape[1]
    n_lat_tiles = n_batch * seq // TOKEN_TILE
    n_ctx_tiles = hy_ctx.shape[0] // TOKEN_TILE
    row = lambda w: pl.BlockSpec((TOKEN_TILE, w), lambda i: (i, 0))
    return pl.pallas_call(
        functools.partial(_out_proj_kernel, n_lat_tiles=n_lat_tiles),
        grid=(n_tiles,),
        in_specs=[
            row(d), row(D_GMLP), row(D_NA),
            pl.BlockSpec((TOKEN_TILE, D_HYENA), lambda i: (jnp.minimum(i, n_lat_tiles - 1), 0)),
            pl.BlockSpec((TOKEN_TILE, D_HYENA),
                         lambda i: (jnp.clip(i - n_lat_tiles, 0, n_ctx_tiles - 1), 0)),
            pl.BlockSpec(w_bf.shape, lambda i: (0, 0)),
            pl.BlockSpec((None, N_MOD, d), _mod_index(seq // TOKEN_TILE, n_batch)),
        ],
        out_specs=row(d),
        out_shape=jax.ShapeDtypeStruct((n_tiles * TOKEN_TILE, d), F32),
        compiler_params=_cparams("arbitrary"),
        name="out_proj",
    )(x_all, gm, na, hy_lat, hy_ctx, w_bf, mods_l)


def _router_kernel(x_ref, g_ref, mod_ref, wr_ref, br_ref, tril_ref, h_ref, route_ref, count_ref, run_ref):
    @pl.when(pl.program_id(0) == 0)
    def _init():
        run_ref[...] = jnp.zeros_like(run_ref)

    mod = mod_ref[...]
    h = _rms_rows(x_ref[...]) * g_ref[...]
    h = h * (1.0 + mod[4:5]) + mod[3:4]
    hb = h.astype(BF16)
    hb32 = hb.astype(F32)
    h_lo = (h - hb32).astype(BF16)
    logits_all = (jnp.dot(hb, wr_ref[0], preferred_element_type=F32)
                  + jnp.dot(h_lo, wr_ref[0], preferred_element_type=F32)
                  + jnp.dot(hb, wr_ref[1], preferred_element_type=F32)) + br_ref[...]
    half = D_MODEL // 2
    bits = pltpu.bitcast(hb32, jnp.uint32)
    words = (bits[:, :half] >> 16) | (bits[:, half:] & jnp.uint32(0xFFFF0000))
    h_ref[:, :half] = pltpu.bitcast(words, jnp.int32)
    h_ref[:, half + LANES:] = jnp.zeros((TOKEN_TILE, half - LANES), jnp.int32)
    run = run_ref[...]
    for c in range(TOKEN_TILE // ROUTE_CHUNK):
        rows = slice(c * ROUTE_CHUNK, (c + 1) * ROUTE_CHUNK)
        route, run = _route_chunk(logits_all[rows], tril_ref[...], run)
        route_ref[rows, :] = route
        h_ref[rows, half:half + LANES] = pltpu.bitcast(route, jnp.int32)
    run_ref[...] = run
    count_ref[...] = run


def _route_chunk(logits, tril, run):
    lane = lax.broadcasted_iota(jnp.int32, logits.shape, 1)
    neg = -jnp.inf
    is_group = (lane >= MOE_EXPERTS) & (lane < MOE_EXPERTS + MOE_GROUPS)
    lg = jnp.where(is_group, logits, neg)
    mg = jnp.max(lg, axis=-1, keepdims=True)
    g_p = 1.0 / jnp.sum(jnp.exp(lg - mg), axis=-1, keepdims=True)
    g_idx = jnp.min(jnp.where(lg == mg, lane, 2 * LANES), axis=-1, keepdims=True) - MOE_EXPERTS
    in_group = (lane >= g_idx * MOE_EXPERTS_PER_GROUP) & (lane < (g_idx + 1) * MOE_EXPERTS_PER_GROUP)
    le = jnp.where(in_group, logits, neg)
    me = jnp.max(le, axis=-1, keepdims=True)
    pe = jnp.exp(le - me)
    pe = pe / jnp.sum(pe, axis=-1, keepdims=True)
    p1 = jnp.max(pe, axis=-1, keepdims=True)
    i1 = jnp.min(jnp.where(in_group & (pe == p1), lane, 2 * LANES), axis=-1, keepdims=True)
    pe2 = jnp.where(in_group & (lane != i1), pe, neg)
    p2 = jnp.max(pe2, axis=-1, keepdims=True)
    i2 = jnp.min(jnp.where(pe2 == p2, lane, 2 * LANES), axis=-1, keepdims=True)
    tot = p1 + p2
    w_lo = g_p * jnp.where(i1 < i2, p1, p2) / tot
    w_hi = g_p * jnp.where(i1 < i2, p2, p1) / tot
    a = jnp.minimum(i1, i2) - g_idx * MOE_EXPERTS_PER_GROUP
    b = jnp.maximum(i1, i2) - g_idx * MOE_EXPERTS_PER_GROUP
    pair = a * (MOE_EXPERTS_PER_GROUP - 1) - ((a * (a - 1)) >> 1) + (b - a - 1)
    bucket = g_idx * MOE_PAIRS + pair
    onehot = lane == bucket
    prefix = jnp.dot(tril, onehot.astype(BF16), preferred_element_type=F32)
    rank = jnp.sum(jnp.where(onehot, prefix + run, 0.0), axis=-1, keepdims=True) - 1.0
    run = run + prefix[ROUTE_CHUNK - 1:ROUTE_CHUNK, :]
    route = jnp.where(lane == 0, bucket.astype(F32),
                      jnp.where(lane == 1, rank,
                                jnp.where(lane == 2, w_lo, jnp.where(lane == 3, w_hi, 0.0))))
    return route, run


def _router(x_all, gain, mods_l, w_router, b_router, n_batch, seq, n_tiles):
    d = x_all.shape[1]
    tril = np.tril(np.ones((ROUTE_CHUNK, ROUTE_CHUNK), np.float32))
    w_hi = w_router.astype(BF16)
    w_split = jnp.stack([w_hi, (w_router - w_hi.astype(F32)).astype(BF16)])
    return pl.pallas_call(
        _router_kernel,
        grid=(n_tiles,),
        in_specs=[
            pl.BlockSpec((TOKEN_TILE, d), lambda i: (i, 0)),
            pl.BlockSpec((1, d), lambda i: (0, 0)),
            pl.BlockSpec((None, N_MOD, d), _mod_index(seq // TOKEN_TILE, n_batch)),
            pl.BlockSpec((2, d, LANES), lambda i: (0, 0, 0)),
            pl.BlockSpec((1, LANES), lambda i: (0, 0)),
            pl.BlockSpec((ROUTE_CHUNK, ROUTE_CHUNK), lambda i: (0, 0)),
        ],
        out_specs=[pl.BlockSpec((TOKEN_TILE, d), lambda i: (i, 0)),
                   pl.BlockSpec((TOKEN_TILE, LANES), lambda i: (i, 0)),
                   pl.BlockSpec((1, LANES), lambda i: (0, 0))],
        out_shape=[jax.ShapeDtypeStruct((n_tiles * TOKEN_TILE, d), jnp.int32),
                   jax.ShapeDtypeStruct((n_tiles * TOKEN_TILE, LANES), F32),
                   jax.ShapeDtypeStruct((1, LANES), F32)],
        scratch_shapes=[pltpu.VMEM((1, LANES), F32)],
        compiler_params=_cparams("arbitrary"),
        name="moe_router",
    )(x_all, gain.reshape(1, d), mods_l, w_split, b_router, jnp.asarray(tril, BF16))


def _row_copy(table_hbm, dst, sem, src_row, dst_row):
    return pltpu.make_async_copy(table_hbm.at[pl.ds(src_row, 1)], dst.at[pl.ds(dst_row, 1)], sem)


def _start_row_gather(idx_ref, base, table_hbm, dst, sem, unrolled):
    rows = dst.shape[0]
    if unrolled:
        for r in range(rows):
            _row_copy(table_hbm, dst, sem, idx_ref[base + r], r).start()
    else:
        def issue(r, carry):
            _row_copy(table_hbm, dst, sem, idx_ref[base + r], r).start()
            return carry

        lax.fori_loop(0, rows, issue, 0, unroll=GATHER_UNROLL)


def _wait_row_gather(table_hbm, dst, sem):
    def drain(r, carry):
        _row_copy(table_hbm, dst, sem, 0, r).wait()
        return carry

    lax.fori_loop(0, dst.shape[0], drain, 0, unroll=GATHER_UNROLL)


def _gather_residual_kernel(idx_ref, y_hbm, x_ref, mod_ref, o_ref, buf, sem):
    _start_row_gather(idx_ref, pl.program_id(0) * buf.shape[0], y_hbm, buf, sem, unrolled=True)
    _wait_row_gather(y_hbm, buf, sem)
    o_ref[...] = x_ref[...] + mod_ref[5:6, :] * buf[...]


def _gather_residual(x_all, y_sorted, dest, mods_l, n_batch, seq, n_tiles):
    d = x_all.shape[1]
    row = pl.BlockSpec((TOKEN_TILE, d), lambda i, idx: (i, 0))
    mod_index = _mod_index(seq // TOKEN_TILE, n_batch)
    return pl.pallas_call(
        _gather_residual_kernel,
        grid_spec=pltpu.PrefetchScalarGridSpec(
            num_scalar_prefetch=1,
            grid=(n_tiles,),
            in_specs=[pl.BlockSpec(memory_space=pl.ANY), row,
                      pl.BlockSpec((None, N_MOD, d), lambda i, idx: mod_index(i))],
            out_specs=row,
            scratch_shapes=[pltpu.VMEM((TOKEN_TILE, d), F32), pltpu.SemaphoreType.DMA],
        ),
        out_shape=jax.ShapeDtypeStruct((n_tiles * TOKEN_TILE, d), F32),
        compiler_params=_cparams("arbitrary"),
        name="moe_combine",
    )(dest, y_sorted, x_all, mods_l)


def _bucket_experts():
    grp, lo, hi = [], [], []
    for g in range(MOE_GROUPS):
        for a in range(MOE_EXPERTS_PER_GROUP):
            for b in range(a + 1, MOE_EXPERTS_PER_GROUP):
                grp.append(g)
                lo.append(a)
                hi.append(b)
    return np.asarray(grp, np.int32), np.asarray(lo, np.int32), np.asarray(hi, np.int32)


def _sorted_experts_kernel(src_ref, grp_ref, lo_ref, hi_ref, nact_ref, tab_hbm, wg_ref, wu_ref, wd_ref,
                           o_ref, xbuf, sems):
    del grp_ref
    t = pl.program_id(0)
    n_active = nact_ref[0]
    n_slots = MOE_GATHER_AHEAD + 1
    slot = t % n_slots
    ahead = (t + MOE_GATHER_AHEAD) % n_slots
    half = D_MODEL // 2

    @pl.when(t == 0)
    def _first_tiles():
        for k in range(MOE_GATHER_AHEAD):
            _start_row_gather(src_ref, k * MOE_SORT_TILE, tab_hbm, xbuf.at[k], sems.at[k], unrolled=False)

    @pl.when(t < n_active + MOE_GATHER_AHEAD)
    def _retire():
        _wait_row_gather(tab_hbm, xbuf.at[slot], sems.at[slot])

    @pl.when(t >= n_active)
    def _unused_tile():
        o_ref[...] = jnp.zeros_like(o_ref)

    @pl.when(t < n_active)
    def _active_tile():
        _start_row_gather(src_ref, (t + MOE_GATHER_AHEAD) * MOE_SORT_TILE, tab_hbm, xbuf.at[ahead],
                          sems.at[ahead], unrolled=True)
        words = pltpu.bitcast(xbuf[slot, :, :half], jnp.uint32)
        x_lo = pltpu.bitcast(words << 16, F32)
        x_hi = pltpu.bitcast(words & jnp.uint32(0xFFFF0000), F32)
        xb = jnp.concatenate([x_lo, x_hi], axis=-1).astype(BF16)
        r = pltpu.bitcast(xbuf[slot, :, half:half + LANES], F32)
        acc = None
        for e_ref, lane in ((lo_ref, 2), (hi_ref, 3)):
            e = e_ref[t]
            gate = jnp.dot(xb, wg_ref[e], preferred_element_type=F32)
            up = jnp.dot(xb, wu_ref[e], preferred_element_type=F32)
            act = _silu(gate) * up * r[:, lane:lane + 1]
            part = jnp.dot(act.astype(BF16), wd_ref[e], preferred_element_type=F32)
            acc = part if acc is None else acc + part
        o_ref[...] = acc


def _sorted_experts(table, src, tile_group, tile_lo, tile_hi, n_active, wg, wu, wd):
    n_rows = src.shape[0]
    group_w = lambda a: pl.BlockSpec((None,) + a.shape[1:], lambda t, src, grp, lo, hi, n: (grp[t], 0, 0, 0))
    return pl.pallas_call(
        _sorted_experts_kernel,
        grid_spec=pltpu.PrefetchScalarGridSpec(
            num_scalar_prefetch=5,
            grid=(n_rows // MOE_SORT_TILE,),
            in_specs=[pl.BlockSpec(memory_space=pl.ANY), group_w(wg), group_w(wu), group_w(wd)],
            out_specs=pl.BlockSpec((MOE_SORT_TILE, D_MODEL), lambda t, src, grp, lo, hi, n: (t, 0)),
            scratch_shapes=[pltpu.VMEM((MOE_GATHER_AHEAD + 1, MOE_SORT_TILE, table.shape[1]), table.dtype),
                            pltpu.SemaphoreType.DMA((MOE_GATHER_AHEAD + 1,))],
        ),
        out_shape=jax.ShapeDtypeStruct((n_rows, D_MODEL), F32),
        compiler_params=_cparams("arbitrary"),
        name="moe_experts",
    )(src, tile_group, tile_lo, tile_hi, n_active, table, wg, wu, wd)


def _moe(x_all, gain, mods_l, w_router, b_router, wg, wu, wd, n_batch, seq, n_tiles):
    n = n_tiles * TOKEN_TILE
    table, route, counts = _router(x_all, gain, mods_l, w_router, b_router, n_batch, seq, n_tiles)

    n_sorted_tiles = (n + MOE_BUCKETS * (MOE_SORT_TILE - 1)) // MOE_SORT_TILE + MOE_GATHER_AHEAD
    n_sorted = n_sorted_tiles * MOE_SORT_TILE
    counts = counts[0, :MOE_BUCKETS].astype(jnp.int32)
    bucket_tiles = (counts + MOE_SORT_TILE - 1) // MOE_SORT_TILE
    tile_end = jnp.cumsum(bucket_tiles)
    row_start = (tile_end - bucket_tiles) * MOE_SORT_TILE
    bucket = route[:, 0].astype(jnp.int32)
    rank = route[:, 1].astype(jnp.int32)
    dest = jnp.sum(jnp.where(bucket[:, None] == jnp.arange(MOE_BUCKETS)[None, :], row_start[None, :], 0),
                   axis=1) + rank
    src = (jnp.arange(n_sorted, dtype=jnp.int32) % n).at[dest].set(jnp.arange(n, dtype=jnp.int32))
    tiles = jnp.arange(n_sorted_tiles, dtype=jnp.int32)
    tile_bucket = jnp.sum(jnp.minimum(tiles, tile_end[-1] - 1)[:, None] >= tile_end[None, :], axis=1)
    tile_bucket = jnp.minimum(tile_bucket, MOE_BUCKETS - 1)
    grp_ids, lo_ids, hi_ids = _bucket_experts()
    tile_group = jnp.asarray(grp_ids)[tile_bucket]
    tile_lo = jnp.asarray(lo_ids)[tile_bucket]
    tile_hi = jnp.asarray(hi_ids)[tile_bucket]

    n_active = tile_end[-1:].astype(jnp.int32)
    y_sorted = _sorted_experts(table, src, tile_group, tile_lo, tile_hi, n_active, wg, wu, wd)
    return _gather_residual(x_all, y_sorted, dest, mods_l, n_batch, seq, n_tiles)


def kernel(x, c, ctx, c_ctx, w_ada, b_ada, g_mix, g_ffn, w_in, w_out, gmlp_v_gain, gmlp_ws, gmlp_bs,
           na_q_gain, na_k_gain, na_rpb, hy_short_w, hy_short_b, hy_w1, hy_b1, hy_w2, hy_b2, hy_w3,
           hy_freq, hy_bias, moe_w_rg, moe_b_rg, moe_w_re, moe_b_re, moe_w_gate, moe_w_up, moe_w_down):
    n_batch, seq, d = x.shape
    ctx_len = ctx.shape[1]
    depth = w_ada.shape[0]
    n_lat = n_batch * seq
    n_ctx = n_batch * ctx_len
    assert d == D_MODEL and seq % TOKEN_TILE == 0 and n_ctx % TOKEN_TILE == 0
    assert seq % GMLP_CHUNK == 0 and ctx_len % GMLP_CHUNK == 0

    pad_rows = -(n_batch + 1) % 8
    cs = jnp.concatenate([c, c_ctx[None, :], jnp.zeros((pad_rows, d), F32)], axis=0)
    mods = _modulation(cs, w_ada, b_ada)[:, :n_batch + 1].reshape(depth, n_batch + 1, N_MOD, d)

    x_all = jnp.concatenate([x.reshape(n_lat, d), ctx.reshape(n_ctx, d)], axis=0)
    n_all_tiles = (n_lat + n_ctx) // TOKEN_TILE
    n_lat_tiles = n_lat // TOKEN_TILE

    mats_lat = _dft_matrices(seq)
    mats_ctx = _dft_matrices(ctx_len)
    pos_lat = _hyena_positions(seq)
    pos_ctx = _hyena_positions(ctx_len)

    w_in_bf = w_in.astype(BF16)
    w_out_bf = w_out.astype(BF16)
    wg_bf = moe_w_gate.astype(BF16)
    wu_bf = moe_w_up.astype(BF16)
    wd_bf = moe_w_down.astype(BF16)
    lane_pad = LANES - MOE_EXPERTS - MOE_GROUPS
    w_router = jnp.pad(jnp.concatenate([moe_w_re, moe_w_rg], axis=-1), ((0, 0), (0, 0), (0, lane_pad)))
    b_router = jnp.pad(jnp.concatenate([moe_b_re, moe_b_rg], axis=-1), ((0, 0), (0, lane_pad)))[:, None, :]
    w1_pad = jnp.pad(hy_w1, ((0, 0), (0, LANES - HYENA_EMB), (0, 0)))

    for l in range(depth):
        last = l == depth - 1
        mods_l = mods[l]
        n_tiles = n_lat_tiles if last else n_all_tiles

        p_all = _in_proj(x_all, g_mix[l], mods_l, w_in_bf[l], n_batch, seq)

        filt = (w1_pad[l], hy_b1[l], hy_w2[l], hy_b2[l], hy_w3[l], hy_freq[l])
        kpack_lat = _filter_dft(mats_lat[0], mats_lat[1], _hyena_filters(*pos_lat, *filt))
        hy_lat = _hyena(p_all, hy_short_w[l], hy_short_b[l], hy_bias[l], kpack_lat, mats_lat,
                        n_batch, seq, 0)
        if last:
            hy_ctx = hy_lat
            gm = _gmlp(p_all, n_lat, gmlp_v_gain[l], gmlp_ws[l], gmlp_bs[l].T)
        else:
            kpack_ctx = _filter_dft(mats_ctx[0], mats_ctx[1], _hyena_filters(*pos_ctx, *filt))
            hy_ctx = _hyena(p_all, hy_short_w[l], hy_short_b[l], hy_bias[l], kpack_ctx, mats_ctx,
                            n_batch, ctx_len, n_lat // ctx_len)
            gm = _gmlp(p_all, n_lat + n_ctx, gmlp_v_gain[l], gmlp_ws[l], gmlp_bs[l].T)

        na = _attention(p_all, na_rpb[l], na_q_gain[l], na_k_gain[l],
                        n_batch, seq, ctx_len, not last)

        x_all = _out_proj(x_all, gm, na, hy_lat, hy_ctx, w_out_bf[l], mods_l, n_batch, seq, n_tiles)

        x_all = _moe(x_all, g_ffn[l], mods_l, w_router[l], b_router[l], wg_bf[l], wu_bf[l], wd_bf[l],
                     n_batch, seq, n_tiles)

    return x_all[:n_lat].reshape(n_batch, seq, d)
```

```python
import functools
import math

import numpy as np
import jax
import jax.numpy as jnp
from jax import lax
from jax.experimental import pallas as pl
from jax.experimental.pallas import tpu as pltpu

F32 = jnp.float32
BF16 = jnp.bfloat16
HI = lax.Precision.HIGHEST

D_MODEL = 1024
GRID_W = 64
D_GMLP = D_MODEL // 4
D_NA = D_MODEL // 2
D_HYENA = D_MODEL // 4
D_IN = 2 * D_GMLP + 3 * D_NA + 3 * D_HYENA
Q_START = 2 * D_GMLP
KV_START = 2 * D_GMLP + D_NA
HY_START = 2 * D_GMLP + 3 * D_NA
GMLP_GROUPS = 4
GMLP_GROUP_DIM = D_GMLP // GMLP_GROUPS
GMLP_CHUNK = 128
NA_HEAD_DIM = 64
NA_HEADS = D_NA // NA_HEAD_DIM
NA_SCALE = NA_HEAD_DIM ** -0.5
NA_WIN_ROWS = 8
NA_WIN_COLS = 16
HYENA_ORDER = 2
HYENA_POS_BANDS = 16
HYENA_EMB = 1 + 2 * HYENA_POS_BANDS
HYENA_FILTER_HIDDEN = 64
HYENA_DECAY_TARGET = 1e-2
HYENA_FAST_DECAY = 0.3
HYENA_SLOW_DECAY = 1.5
MOE_GROUPS = 4
MOE_EXPERTS_PER_GROUP = 8
MOE_EXPERTS = MOE_GROUPS * MOE_EXPERTS_PER_GROUP
MOE_HIDDEN = 256
MOE_PAIRS = MOE_EXPERTS_PER_GROUP * (MOE_EXPERTS_PER_GROUP - 1) // 2
MOE_BUCKETS = MOE_GROUPS * MOE_PAIRS
N_MOD = 6
RMS_EPS = 1e-6
LN_EPS = 1e-5

LANES = 128
TOKEN_TILE = 512
NA_Q_ROWS = 4
NA_Q_BLOCK = NA_Q_ROWS * GRID_W
NA_BAND_ROWS = NA_WIN_ROWS + NA_Q_ROWS
DFT_TILE = 512
CONV_TILE = 1024
ROUTE_CHUNK = 128
MOE_SORT_TILE = 128
MOE_GATHER_AHEAD = 1
GATHER_UNROLL = 8
MASK_VALUE = -1e30
VMEM_LIMIT = 56 * 1024 * 1024


def _cparams(*sem):
    return pltpu.CompilerParams(dimension_semantics=sem, vmem_limit_bytes=VMEM_LIMIT)


def _silu(x):
    return x * jax.nn.sigmoid(x)


def _rms_rows(x):
    return x * lax.rsqrt(jnp.mean(x * x, axis=-1, keepdims=True) + RMS_EPS)


def _mods_kernel(cs_ref, w_ref, b_ref, o_ref):
    s = _silu(cs_ref[...])
    o_ref[...] = jnp.dot(s, w_ref[...], preferred_element_type=F32, precision=HI) + b_ref[...]


def _modulation(cs, w_ada, b_ada):
    depth, d, nd = w_ada.shape
    rows = cs.shape[0]
    col = 1024
    return pl.pallas_call(
        _mods_kernel,
        grid=(depth, nd // col),
        in_specs=[
            pl.BlockSpec((rows, d), lambda l, j: (0, 0)),
            pl.BlockSpec((None, d, col), lambda l, j: (l, 0, j)),
            pl.BlockSpec((None, 1, col), lambda l, j: (l, 0, j)),
        ],
        out_specs=pl.BlockSpec((None, rows, col), lambda l, j: (l, 0, j)),
        out_shape=jax.ShapeDtypeStruct((depth, rows, nd), F32),
        compiler_params=_cparams("arbitrary", "arbitrary"),
        name="modulation",
    )(cs, w_ada, b_ada.reshape(depth, 1, nd))


def _in_proj_kernel(x_ref, g_ref, mod_ref, w_ref, o_ref):
    mod = mod_ref[...]
    h = _rms_rows(x_ref[...]) * g_ref[...]
    h = h * (1.0 + mod[1:2]) + mod[0:1]
    o_ref[...] = jnp.dot(h.astype(BF16), w_ref[...], preferred_element_type=F32).astype(o_ref.dtype)


def _mod_index(tiles_per_batch, n_batch):
    return lambda i: (jnp.minimum(i // tiles_per_batch, n_batch), 0, 0)


def _in_proj(x_all, gain, mods_l, w_bf, n_batch, seq):
    n, d = x_all.shape
    d_in = w_bf.shape[1]
    return pl.pallas_call(
        _in_proj_kernel,
        grid=(n // TOKEN_TILE,),
        in_specs=[
            pl.BlockSpec((TOKEN_TILE, d), lambda i: (i, 0)),
            pl.BlockSpec((1, d), lambda i: (0, 0)),
            pl.BlockSpec((None, N_MOD, d), _mod_index(seq // TOKEN_TILE, n_batch)),
            pl.BlockSpec((d, d_in), lambda i: (0, 0)),
        ],
        out_specs=pl.BlockSpec((TOKEN_TILE, d_in), lambda i: (i, 0)),
        out_shape=jax.ShapeDtypeStruct((n, d_in), BF16),
        compiler_params=_cparams("arbitrary"),
        name="in_proj",
    )(x_all, gain.reshape(1, d), mods_l, w_bf)


def _group_avg_matrix(groups, width):
    return np.kron(np.eye(groups), np.full((width, width), 1.0 / width))


def _group_mean(t, avg):
    hi = t.astype(BF16)
    lo = (t - hi.astype(F32)).astype(BF16)
    return (jnp.dot(hi, avg, preferred_element_type=F32) + jnp.dot(lo, avg, preferred_element_type=F32))


def _gmlp_kernel(u_ref, v_ref, gain_ref, avg_ref, ws_ref, bs_ref, o_ref):
    avg = avg_ref[...]
    v = jax.nn.gelu(v_ref[...].astype(F32))
    v = v - _group_mean(v, avg)
    v = v * lax.rsqrt(_group_mean(v * v, avg) + LN_EPS) * gain_ref[...]
    vb = v.astype(BF16)
    bs = bs_ref[...]
    for c in range(u_ref.shape[0] // GMLP_CHUNK):
        rows = slice(c * GMLP_CHUNK, (c + 1) * GMLP_CHUNK)
        outs = []
        for g in range(GMLP_GROUPS):
            cols = slice(g * GMLP_GROUP_DIM, (g + 1) * GMLP_GROUP_DIM)
            s = jnp.dot(ws_ref[g], vb[rows, cols], preferred_element_type=F32) + bs[:, g:g + 1]
            outs.append(s)
        o_ref[rows, :] = jax.nn.gelu(u_ref[rows, :].astype(F32)) * jnp.concatenate(outs, axis=-1)


def _gmlp(p_all, n, v_gain, ws, bs_t):
    avg = _group_avg_matrix(GMLP_GROUPS, GMLP_GROUP_DIM)
    return pl.pallas_call(
        _gmlp_kernel,
        grid=(n // TOKEN_TILE,),
        in_specs=[
            pl.BlockSpec((TOKEN_TILE, D_GMLP), lambda i: (i, 0)),
            pl.BlockSpec((TOKEN_TILE, D_GMLP), lambda i: (i, 1)),
            pl.BlockSpec((1, D_GMLP), lambda i: (0, 0)),
            pl.BlockSpec((D_GMLP, D_GMLP), lambda i: (0, 0)),
            pl.BlockSpec((GMLP_GROUPS, GMLP_CHUNK, GMLP_CHUNK), lambda i: (0, 0, 0)),
            pl.BlockSpec((GMLP_CHUNK, GMLP_GROUPS), lambda i: (0, 0)),
        ],
        out_specs=pl.BlockSpec((TOKEN_TILE, D_GMLP), lambda i: (i, 0)),
        out_shape=jax.ShapeDtypeStruct((n, D_GMLP), F32),
        compiler_params=_cparams("arbitrary"),
        name="gmlp",
    )(p_all, p_all, v_gain.reshape(1, D_GMLP), jnp.asarray(avg, BF16), ws.astype(BF16), bs_t)


def _rpb_expand_kernel(rpb_ref, sel_ref, o_ref):
    o_ref[...] = jnp.dot(rpb_ref[...], sel_ref[...], preferred_element_type=F32, precision=HI)


def _na_geometry(grid_rows):
    variants, step_variant, band_start = [], [], []
    for r0 in range(0, grid_rows, NA_Q_ROWS):
        b0 = int(np.clip(r0 - NA_WIN_ROWS // 2, 0, grid_rows - NA_BAND_ROWS))
        geo = []
        for r in range(r0, r0 + NA_Q_ROWS):
            wr = int(np.clip(r - NA_WIN_ROWS // 2, 0, grid_rows - NA_WIN_ROWS))
            assert b0 <= wr and wr + NA_WIN_ROWS <= b0 + NA_BAND_ROWS
            geo.append((wr - b0, wr - r + NA_WIN_ROWS - 1))
        geo = tuple(geo)
        if geo not in variants:
            variants.append(geo)
        step_variant.append(variants.index(geo))
        band_start.append(b0)
    return variants, step_variant, band_start


def _na_bias_tables(rpb, variants):
    n_heads, n_dr, n_dc = rpb.shape
    qcol = np.arange(GRID_W)[:, None]
    kcol = np.arange(GRID_W)[None, :]
    win_c = np.clip(qcol - NA_WIN_COLS // 2, 0, GRID_W - NA_WIN_COLS)
    col_ok = (kcol >= win_c) & (kcol < win_c + NA_WIN_COLS)
    dc = np.clip(kcol - qcol + NA_WIN_COLS - 1, 0, 2 * NA_WIN_COLS - 2)
    dc_pad = -n_dc % 8
    sel = (np.arange(n_dc + dc_pad)[:, None] == dc.reshape(1, -1)).astype(np.float32)
    rpb2 = jnp.pad(rpb.reshape(n_heads * n_dr, n_dc), ((0, 0), (0, dc_pad)))
    toep = pl.pallas_call(
        _rpb_expand_kernel,
        out_shape=jax.ShapeDtypeStruct((n_heads * n_dr, GRID_W * GRID_W), F32),
        name="rpb_expand",
    )(rpb2, jnp.asarray(sel))
    toep = toep.reshape(n_heads, n_dr, GRID_W, GRID_W)
    return pl.pallas_call(
        functools.partial(_bias_table_kernel, variants=tuple(variants)),
        grid=(n_heads,),
        in_specs=[pl.BlockSpec((None, n_dr, GRID_W, GRID_W), lambda h: (h, 0, 0, 0)),
                  pl.BlockSpec((GRID_W, GRID_W), lambda h: (0, 0))],
        out_specs=pl.BlockSpec((None, len(variants), NA_Q_BLOCK, NA_BAND_ROWS * GRID_W),
                               lambda h: (h, 0, 0, 0)),
        out_shape=jax.ShapeDtypeStruct((n_heads, len(variants), NA_Q_BLOCK, NA_BAND_ROWS * GRID_W), F32),
        compiler_params=_cparams("arbitrary"),
        name="na_bias_table",
    )(toep, jnp.asarray(col_ok.astype(np.float32)))


def _bias_table_kernel(toep_ref, ok_ref, o_ref, *, variants):
    ok = ok_ref[...] > 0.0
    outside = jnp.full((GRID_W, GRID_W), MASK_VALUE, F32)
    blocks = [jnp.where(ok, toep_ref[dr], MASK_VALUE) for dr in range(toep_ref.shape[0])]
    for v, geo in enumerate(variants):
        for i, (a0, dr0) in enumerate(geo):
            pieces = [blocks[dr0 + a - a0] if a0 <= a < a0 + NA_WIN_ROWS else outside
                      for a in range(NA_BAND_ROWS)]
            o_ref[v, i * GRID_W:(i + 1) * GRID_W, :] = jnp.concatenate(pieces, axis=-1)


def _store_heads(dst, rows, t, gain, avg):
    if gain is not None:
        t = t.astype(F32)
        t = t * lax.rsqrt(_group_mean(t * t, avg) + RMS_EPS) * gain
    for h in range(NA_HEADS):
        dst[h, rows, :] = t[:, h * NA_HEAD_DIM:(h + 1) * NA_HEAD_DIM].astype(BF16)


def _na_kernel(q_ref, k_ref, v_ref, kc_ref, vc_ref, bias_ref, qg_ref, kg_ref, avg_ref, o_ref,
               kn_s, vb_s, kcn_s, vcb_s, qn_s, o_s, s_scr, p_scr, l_scr,
               *, n_lat_steps, step_variant, band_start):
    step = pl.program_id(1)
    kg = kg_ref[...]
    avg = avg_ref[...]
    nt = (((1,), (1,)), ((), ()))
    band_keys = NA_BAND_ROWS * GRID_W
    all_rows = slice(None)

    @pl.when(step == 0)
    def _prepare_keys():
        chunk = 256

        def body(c, carry):
            rows = pl.ds(pl.multiple_of(c * chunk, chunk), chunk)
            _store_heads(kn_s, rows, k_ref[rows, :], kg, avg)
            _store_heads(vb_s, rows, v_ref[rows, :], None, None)
            return carry

        lax.fori_loop(0, k_ref.shape[0] // chunk, body, 0)
        _store_heads(kcn_s, all_rows, kc_ref[...], kg, avg)
        _store_heads(vcb_s, all_rows, vc_ref[...], None, None)

    _store_heads(qn_s, all_rows, q_ref[...], qg_ref[...] * NA_SCALE, avg)

    def finish(h, scores, v_parts):
        m = jnp.max(scores, axis=-1, keepdims=True)
        p = jnp.exp(scores - m)
        denom = jnp.sum(p, axis=-1, keepdims=True)
        pb = p.astype(BF16)
        acc = None
        col = 0
        for v in v_parts:
            part = jnp.dot(pb[:, col:col + v.shape[0]], v, preferred_element_type=F32)
            acc = part if acc is None else acc + part
            col += v.shape[0]
        o_s[h] = acc / denom

    @pl.when(step < n_lat_steps)
    def _latent_queries():
        variant = jnp.int32(0)
        band0 = jnp.int32(0)
        for s_, (v_, b_) in enumerate(zip(step_variant, band_start)):
            variant = jnp.where(step == s_, v_, variant)
            band0 = jnp.where(step == s_, b_ * GRID_W, band0)
        krows = pl.ds(pl.multiple_of(band0, NA_Q_BLOCK), band_keys)

        def scores(h):
            qh = qn_s[h]
            s_scr[h % 2, :, :band_keys] = (
                lax.dot_general(qh, kn_s[h, krows, :], nt, preferred_element_type=F32) + bias_ref[h, variant])
            s_scr[h % 2, :, band_keys:] = lax.dot_general(qh, kcn_s[h], nt, preferred_element_type=F32)

        def softmax(h):
            s = s_scr[h % 2]
            p = jnp.exp(s - jnp.max(s, axis=-1, keepdims=True))
            l_scr[h % 2] = jnp.sum(p, axis=-1, keepdims=True)
            p_scr[h % 2] = p.astype(BF16)

        def values(h):
            acc = jnp.dot(p_scr[h % 2, :, :band_keys], vb_s[h, krows, :], preferred_element_type=F32)
            acc = acc + jnp.dot(p_scr[h % 2, :, band_keys:], vcb_s[h], preferred_element_type=F32)
            o_s[h] = acc / l_scr[h % 2]

        for stage in range(NA_HEADS + 2):
            if stage < NA_HEADS:
                scores(stage)
            if 1 <= stage <= NA_HEADS:
                softmax(stage - 1)
            if stage >= 2:
                values(stage - 2)

    @pl.when(step >= n_lat_steps)
    def _context_queries():
        def head_body(h, carry):
            s = lax.dot_general(qn_s[h], kcn_s[h], nt, preferred_element_type=F32)
            finish(h, s, (vcb_s[h],))
            return carry

        lax.fori_loop(0, NA_HEADS, head_body, 0)

    o_ref[...] = jnp.concatenate([o_s[h] for h in range(NA_HEADS)], axis=-1)


def _attention(p_all, rpb, q_gain, k_gain, n_batch, seq, ctx_len, with_ctx_queries):
    n = p_all.shape[0]
    assert ctx_len == NA_Q_BLOCK and seq % NA_Q_BLOCK == 0
    n_lat_steps = seq // NA_Q_BLOCK
    n_steps = n_lat_steps + (1 if with_ctx_queries else 0)
    ctx_block0 = n_batch * seq // ctx_len
    qcol, kcol, vcol = Q_START // D_NA, KV_START // D_NA, (KV_START + D_NA) // D_NA
    variants, step_variant, band_start = _na_geometry(seq // GRID_W)
    bias_tab = _na_bias_tables(rpb, variants)

    def q_index(col):
        return lambda b, s: (jnp.where(s < n_lat_steps, b * n_lat_steps + s, ctx_block0 + b), col)

    kern = functools.partial(_na_kernel, n_lat_steps=n_lat_steps, step_variant=tuple(step_variant),
                             band_start=tuple(band_start))
    head_major = lambda rows: pltpu.VMEM((NA_HEADS, rows, NA_HEAD_DIM), BF16)
    return pl.pallas_call(
        kern,
        grid=(n_batch, n_steps),
        in_specs=[
            pl.BlockSpec((NA_Q_BLOCK, D_NA), q_index(qcol)),
            pl.BlockSpec((seq, D_NA), lambda b, s: (b, kcol)),
            pl.BlockSpec((seq, D_NA), lambda b, s: (b, vcol)),
            pl.BlockSpec((ctx_len, D_NA), lambda b, s: (ctx_block0 + b, kcol)),
            pl.BlockSpec((ctx_len, D_NA), lambda b, s: (ctx_block0 + b, vcol)),
            pl.BlockSpec(bias_tab.shape, lambda b, s: (0, 0, 0, 0), pipeline_mode=pl.Buffered(1)),
            pl.BlockSpec((1, D_NA), lambda b, s: (0, 0)),
            pl.BlockSpec((1, D_NA), lambda b, s: (0, 0)),
            pl.BlockSpec((D_NA, D_NA), lambda b, s: (0, 0)),
        ],
        out_specs=pl.BlockSpec((NA_Q_BLOCK, D_NA), q_index(0)),
        out_shape=jax.ShapeDtypeStruct((n if with_ctx_queries else n_batch * seq, D_NA), F32),
        scratch_shapes=[
            head_major(seq), head_major(seq), head_major(ctx_len), head_major(ctx_len),
            head_major(NA_Q_BLOCK),
            pltpu.VMEM((NA_HEADS, NA_Q_BLOCK, NA_HEAD_DIM), F32),
            pltpu.VMEM((2, NA_Q_BLOCK, NA_BAND_ROWS * GRID_W + ctx_len), F32),
            pltpu.VMEM((2, NA_Q_BLOCK, NA_BAND_ROWS * GRID_W + ctx_len), BF16),
            pltpu.VMEM((2, NA_Q_BLOCK, 1), F32),
        ],
        compiler_params=_cparams("arbitrary", "arbitrary"),
        name="attention",
    )(p_all, p_all, p_all, p_all, p_all, bias_tab,
      jnp.tile(q_gain.reshape(1, NA_HEAD_DIM), (1, NA_HEADS)),
      jnp.tile(k_gain.reshape(1, NA_HEAD_DIM), (1, NA_HEADS)),
      jnp.asarray(_group_avg_matrix(NA_HEADS, NA_HEAD_DIM), BF16))


def _dft_matrices(length):
    idx = jnp.arange(length, dtype=jnp.int32)
    step = 64
    t_hi = jnp.arange(length // step, dtype=jnp.int32) * step
    t_lo = jnp.arange(step, dtype=jnp.int32)
    ang_hi = ((idx[:, None] * t_hi[None, :]) % (2 * length)).astype(F32) * (math.pi / length)
    ang_lo = ((idx[:, None] * t_lo[None, :]) % (2 * length)).astype(F32) * (math.pi / length)
    c_hi, s_hi = jnp.cos(ang_hi)[:, :, None], jnp.sin(ang_hi)[:, :, None]
    c_lo, s_lo = jnp.cos(ang_lo)[:, None, :], jnp.sin(ang_lo)[:, None, :]
    gc = (c_hi * c_lo - s_hi * s_lo).reshape(length, length)
    gs = (s_hi * c_lo + c_hi * s_lo).reshape(length, length)
    nyq = jnp.where(idx % 2 == 0, 1.0, -1.0).astype(F32)
    gs = jnp.where(idx[:, None] == 0, nyq[None, :], gs)
    return gc.astype(BF16), gs.astype(BF16), gs.T.astype(BF16)


def _hyena_positions(length):
    t = jnp.linspace(0.0, 1.0, length, dtype=F32)[:, None]
    w = 2.0 * math.pi * jnp.arange(length, dtype=F32)[:, None] / length
    f = jnp.linspace(1e-4, HYENA_POS_BANDS - 1, HYENA_POS_BANDS, dtype=F32)[None, :]
    z = jnp.concatenate([t, jnp.cos(f * w), -jnp.sin(f * w)], axis=-1)
    z = jnp.pad(z, ((0, 0), (0, LANES - HYENA_EMB)))
    min_decay = math.log(HYENA_DECAY_TARGET) / HYENA_SLOW_DECAY
    max_decay = math.log(HYENA_DECAY_TARGET) / HYENA_FAST_DECAY
    deltas = jnp.abs(jnp.linspace(min_decay, max_decay, D_HYENA, dtype=F32))[None, :]
    return z, jnp.exp(-t * deltas)


def _filter_kernel(z_ref, decay_ref, w1_ref, b1_ref, w2_ref, b2_ref, w3_ref, freq_ref, o_ref):
    freq = freq_ref[...]
    hdn = jnp.dot(z_ref[...], w1_ref[...], preferred_element_type=F32, precision=HI) + b1_ref[...]
    hdn = jnp.sin(freq[0:1] * hdn)
    hdn = jnp.dot(hdn, w2_ref[...], preferred_element_type=F32, precision=HI) + b2_ref[...]
    hdn = jnp.sin(freq[1:2] * hdn)
    h = jnp.dot(hdn, w3_ref[...], preferred_element_type=F32, precision=HI)
    decay = decay_ref[...]
    first_row = lax.broadcasted_iota(jnp.int32, decay.shape, 0) == 0
    outs = []
    for n in range(HYENA_ORDER):
        base = 2 * n * D_HYENA
        hf = h[:, base:base + D_HYENA] * decay
        hb = h[:, base + D_HYENA:base + 2 * D_HYENA] * decay
        norm = jnp.sum(jnp.abs(hf), axis=0, keepdims=True) + jnp.sum(jnp.abs(hb), axis=0, keepdims=True)
        outs.append(hf / norm)
        outs.append(jnp.where(first_row, 0.0, hb / norm))
    o_ref[...] = jnp.concatenate(outs, axis=-1)


def _hyena_filters(z, decay, w1p, b1, w2, b2, w3, freq):
    length = z.shape[0]
    full = lambda a: pl.BlockSpec(a.shape, lambda i: (0,) * a.ndim)
    args = (z, decay, w1p, b1.reshape(1, -1), w2, b2.reshape(1, -1), w3, freq)
    return pl.pallas_call(
        _filter_kernel,
        grid=(1,),
        in_specs=[full(a) for a in args],
        out_specs=pl.BlockSpec((length, 2 * HYENA_ORDER * D_HYENA), lambda i: (0, 0)),
        out_shape=jax.ShapeDtypeStruct((length, 2 * HYENA_ORDER * D_HYENA), F32),
        compiler_params=_cparams("arbitrary"),
        name="hyena_filter",
    )(*args)


def _filter_dft_kernel(gc_ref, gs_ref, h_ref, o_ref, *, length):
    hb = h_ref[...].astype(BF16)
    fa = jnp.dot(gc_ref[...], hb, preferred_element_type=F32)
    fb = jnp.dot(gs_ref[...], hb, preferred_element_type=F32)
    rows = lax.broadcasted_iota(jnp.int32, (fa.shape[0], D_HYENA), 0) + pl.program_id(0) * fa.shape[0]
    dc_row = rows == 0
    inv_n = 1.0 / (2 * length)
    outs = []
    for n in range(HYENA_ORDER):
        base = 2 * n * D_HYENA
        f_sl = slice(base, base + D_HYENA)
        b_sl = slice(base + D_HYENA, base + 2 * D_HYENA)
        kr = fa[:, f_sl] + fa[:, b_sl]
        ki = fb[:, b_sl] - fb[:, f_sl]
        k_nyq = fb[:, f_sl] + fb[:, b_sl]
        outs.append(jnp.where(dc_row, kr * inv_n, 2.0 * inv_n * kr))
        outs.append(jnp.where(dc_row, 0.0, 2.0 * inv_n * ki))
        outs.append(jnp.where(dc_row, 0.0, -2.0 * inv_n * ki))
        outs.append(jnp.where(dc_row, k_nyq * inv_n, 2.0 * inv_n * kr))
    o_ref[...] = jnp.concatenate(outs, axis=-1)


def _filter_dft(gc, gs, hfilt):
    length = gc.shape[0]
    tile = min(DFT_TILE, length)
    width = 4 * HYENA_ORDER * D_HYENA
    return pl.pallas_call(
        functools.partial(_filter_dft_kernel, length=length),
        grid=(length // tile,),
        in_specs=[
            pl.BlockSpec((tile, length), lambda j: (j, 0)),
            pl.BlockSpec((tile, length), lambda j: (j, 0)),
            pl.BlockSpec(hfilt.shape, lambda j: (0, 0)),
        ],
        out_specs=pl.BlockSpec((tile, width), lambda j: (j, 0)),
        out_shape=jax.ShapeDtypeStruct((length, width), F32),
        compiler_params=_cparams("arbitrary"),
        name="hyena_filter_dft",
    )(gc, gs, hfilt)


def _short_conv_kernel(a0_ref, a1_ref, a2_ref, w_ref, b_ref, o_ref):
    w = w_ref[...]
    b = b_ref[...]
    length = a0_ref.shape[0]
    rows = lax.broadcasted_iota(jnp.int32, (length, D_HYENA), 0)
    for j, a_ref in enumerate((a0_ref, a1_ref, a2_ref)):
        cols = slice(j * D_HYENA, (j + 1) * D_HYENA)
        a = a_ref[...].astype(F32)
        prev = jnp.where(rows == 0, 0.0, pltpu.roll(a, 1, 0))
        nxt = jnp.where(rows == length - 1, 0.0, pltpu.roll(a, length - 1, 0))
        o_ref[:, cols] = prev * w[0:1, cols] + a * w[1:2, cols] + nxt * w[2:3, cols] + b[:, cols]


def _short_conv(p_all, short_w, short_b, n_batch, length, row_block0):
    c0 = HY_START // D_HYENA
    spec = lambda j: pl.BlockSpec((length, D_HYENA), lambda b: (row_block0 + b, c0 + j))
    return pl.pallas_call(
        _short_conv_kernel,
        grid=(n_batch,),
        in_specs=[spec(0), spec(1), spec(2),
                  pl.BlockSpec((3, 3 * D_HYENA), lambda b: (0, 0)),
                  pl.BlockSpec((1, 3 * D_HYENA), lambda b: (0, 0))],
        out_specs=pl.BlockSpec((length, 3 * D_HYENA), lambda b: (b, 0)),
        out_shape=jax.ShapeDtypeStruct((n_batch * length, 3 * D_HYENA), F32),
        compiler_params=_cparams("arbitrary"),
        name="hyena_short_conv",
    )(p_all, p_all, p_all, short_w, short_b.reshape(1, -1))


def _conv_fwd_kernel(gc_ref, gs_ref, u_ref, k_ref, pa_ref, pb_ref):
    u = u_ref[...].astype(BF16)
    a = jnp.dot(gc_ref[...], u, preferred_element_type=F32)
    b = jnp.dot(gs_ref[...], u, preferred_element_type=F32)
    k = k_ref[...]
    c = D_HYENA
    pa_ref[...] = (a * k[:, 0:c] + b * k[:, c:2 * c]).astype(BF16)
    pb_ref[...] = (a * k[:, 2 * c:3 * c] + b * k[:, 3 * c:4 * c]).astype(BF16)


def _conv_fwd(gc, gs, u, u_col, kpack, order, n_batch):
    length = gc.shape[0]
    tile = min(CONV_TILE, length)
    nt = length // tile
    out = jax.ShapeDtypeStruct((n_batch * length, D_HYENA), BF16)
    return pl.pallas_call(
        _conv_fwd_kernel,
        grid=(nt, n_batch),
        in_specs=[
            pl.BlockSpec((tile, length), lambda j, b: (j, 0)),
            pl.BlockSpec((tile, length), lambda j, b: (j, 0)),
            pl.BlockSpec((length, D_HYENA), lambda j, b: (b, u_col)),
            pl.BlockSpec((tile, 4 * D_HYENA), lambda j, b: (j, order)),
        ],
        out_specs=[pl.BlockSpec((tile, D_HYENA), lambda j, b: (b * nt + j, 0))] * 2,
        out_shape=[out, out],
        compiler_params=_cparams("arbitrary", "arbitrary"),
        name="hyena_conv_fwd",
    )(gc, gs, u, kpack)


def _conv_inv_kernel(gc_ref, gst_ref, pa_ref, pb_ref, z_ref, gate_ref, d_ref, o_ref):
    y = jnp.dot(gc_ref[...], pa_ref[...], preferred_element_type=F32)
    y = y + jnp.dot(gst_ref[...], pb_ref[...], preferred_element_type=F32)
    o_ref[...] = gate_ref[...] * (y + d_ref[...] * z_ref[...])


def _conv_inv(gc, gst, pa, pb, z_prev, z_col, a3, gate_col, d_bias, n_batch):
    length = gc.shape[0]
    tile = min(CONV_TILE, length)
    nt = length // tile
    return pl.pallas_call(
        _conv_inv_kernel,
        grid=(nt, n_batch),
        in_specs=[
            pl.BlockSpec((tile, length), lambda j, b: (j, 0)),
            pl.BlockSpec((tile, length), lambda j, b: (j, 0)),
            pl.BlockSpec((length, D_HYENA), lambda j, b: (b, 0)),
            pl.BlockSpec((length, D_HYENA), lambda j, b: (b, 0)),
            pl.BlockSpec((tile, D_HYENA), lambda j, b: (b * nt + j, z_col)),
            pl.BlockSpec((tile, D_HYENA), lambda j, b: (b * nt + j, gate_col)),
            pl.BlockSpec((1, D_HYENA), lambda j, b: (0, 0)),
        ],
        out_specs=pl.BlockSpec((tile, D_HYENA), lambda j, b: (b * nt + j, 0)),
        out_shape=jax.ShapeDtypeStruct((n_batch * length, D_HYENA), F32),
        compiler_params=_cparams("arbitrary", "arbitrary"),
        name="hyena_conv_inv",
    )(gc, gst, pa, pb, z_prev, a3, d_bias.reshape(1, D_HYENA))


def _hyena(p_all, short_w, short_b, d_bias, kpack, mats, n_batch, length, row_block0):
    gc, gs, gst = mats
    a3 = _short_conv(p_all, short_w, short_b, n_batch, length, row_block0)
    pa, pb = _conv_fwd(gc, gs, a3, 0, kpack, 0, n_batch)
    z1 = _conv_inv(gc, gst, pa, pb, a3, 0, a3, 1, d_bias[0], n_batch)
    pa, pb = _conv_fwd(gc, gs, z1, 0, kpack, 1, n_batch)
    return _conv_inv(gc, gst, pa, pb, z1, 0, a3, 2, d_bias[1], n_batch)


def _out_proj_kernel(x_ref, gm_ref, na_ref, hyl_ref, hyc_ref, w_ref, mod_ref, o_ref, *, n_lat_tiles):
    is_lat = pl.program_id(0) < n_lat_tiles
    hy = jnp.where(is_lat, hyl_ref[...], hyc_ref[...])
    y = jnp.dot(gm_ref[...].astype(BF16), w_ref[0:D_GMLP, :], preferred_element_type=F32)
    y = y + jnp.dot(na_ref[...].astype(BF16), w_ref[D_GMLP:D_GMLP + D_NA, :], preferred_element_type=F32)
    y = y + jnp.dot(hy.astype(BF16), w_ref[D_GMLP + D_NA:, :], preferred_element_type=F32)
    o_ref[...] = x_ref[...] + mod_ref[2:3, :] * y


def _out_proj(x_all, gm, na, hy_lat, hy_ctx, w_bf, mods_l, n_batch, seq, n_tiles):
    d = x_all.shape[1]
    n_lat_tiles = n_batch * seq // TOKEN_TILE
    n_ctx_tiles = hy_ctx.shape[0] // TOKEN_TILE
    row = lambda w: pl.BlockSpec((TOKEN_TILE, w), lambda i: (i, 0))
    return pl.pallas_call(
        functools.partial(_out_proj_kernel, n_lat_tiles=n_lat_tiles),
        grid=(n_tiles,),
        in_specs=[
            row(d), row(D_GMLP), row(D_NA),
            pl.BlockSpec((TOKEN_TILE, D_HYENA), lambda i: (jnp.minimum(i, n_lat_tiles - 1), 0)),
            pl.BlockSpec((TOKEN_TILE, D_HYENA),
                         lambda i: (jnp.clip(i - n_lat_tiles, 0, n_ctx_tiles - 1), 0)),
            pl.BlockSpec(w_bf.shape, lambda i: (0, 0)),
            pl.BlockSpec((None, N_MOD, d), _mod_index(seq // TOKEN_TILE, n_batch)),
        ],
        out_specs=row(d),
        out_shape=jax.ShapeDtypeStruct((n_tiles * TOKEN_TILE, d), F32),
        compiler_params=_cparams("arbitrary"),
        name="out_proj",
    )(x_all, gm, na, hy_lat, hy_ctx, w_bf, mods_l)


def _router_kernel(x_ref, g_ref, mod_ref, wr_ref, br_ref, tril_ref, h_ref, route_ref, count_ref, run_ref):
    @pl.when(pl.program_id(0) == 0)
    def _init():
        run_ref[...] = jnp.zeros_like(run_ref)

    mod = mod_ref[...]
    h = _rms_rows(x_ref[...]) * g_ref[...]
    h = h * (1.0 + mod[4:5]) + mod[3:4]
    hb = h.astype(BF16)
    hb32 = hb.astype(F32)
    h_lo = (h - hb32).astype(BF16)
    logits_all = (jnp.dot(hb, wr_ref[0], preferred_element_type=F32)
                  + jnp.dot(h_lo, wr_ref[0], preferred_element_type=F32)
                  + jnp.dot(hb, wr_ref[1], preferred_element_type=F32)) + br_ref[...]
    half = D_MODEL // 2
    bits = pltpu.bitcast(hb32, jnp.uint32)
    words = (bits[:, :half] >> 16) | (bits[:, half:] & jnp.uint32(0xFFFF0000))
    h_ref[:, :half] = pltpu.bitcast(words, jnp.int32)
    h_ref[:, half + LANES:] = jnp.zeros((TOKEN_TILE, half - LANES), jnp.int32)
    run = run_ref[...]
    for c in range(TOKEN_TILE // ROUTE_CHUNK):
        rows = slice(c * ROUTE_CHUNK, (c + 1) * ROUTE_CHUNK)
        route, run = _route_chunk(logits_all[rows], tril_ref[...], run)
        route_ref[rows, :] = route
        h_ref[rows, half:half + LANES] = pltpu.bitcast(route, jnp.int32)
    run_ref[...] = run
    count_ref[...] = run


def _route_chunk(logits, tril, run):
    lane = lax.broadcasted_iota(jnp.int32, logits.shape, 1)
    neg = -jnp.inf
    is_group = (lane >= MOE_EXPERTS) & (lane < MOE_EXPERTS + MOE_GROUPS)
    lg = jnp.where(is_group, logits, neg)
    mg = jnp.max(lg, axis=-1, keepdims=True)
    g_p = 1.0 / jnp.sum(jnp.exp(lg - mg), axis=-1, keepdims=True)
    g_idx = jnp.min(jnp.where(lg == mg, lane, 2 * LANES), axis=-1, keepdims=True) - MOE_EXPERTS
    in_group = (lane >= g_idx * MOE_EXPERTS_PER_GROUP) & (lane < (g_idx + 1) * MOE_EXPERTS_PER_GROUP)
    le = jnp.where(in_group, logits, neg)
    me = jnp.max(le, axis=-1, keepdims=True)
    pe = jnp.exp(le - me)
    pe = pe / jnp.sum(pe, axis=-1, keepdims=True)
    p1 = jnp.max(pe, axis=-1, keepdims=True)
    i1 = jnp.min(jnp.where(in_group & (pe == p1), lane, 2 * LANES), axis=-1, keepdims=True)
    pe2 = jnp.where(in_group & (lane != i1), pe, neg)
    p2 = jnp.max(pe2, axis=-1, keepdims=True)
    i2 = jnp.min(jnp.where(pe2 == p2, lane, 2 * LANES), axis=-1, keepdims=True)
    tot = p1 + p2
    w_lo = g_p * jnp.where(i1 < i2, p1, p2) / tot
    w_hi = g_p * jnp.where(i1 < i2, p2, p1) / tot
    a = jnp.minimum(i1, i2) - g_idx * MOE_EXPERTS_PER_GROUP
    b = jnp.maximum(i1, i2) - g_idx * MOE_EXPERTS_PER_GROUP
    pair = a * (MOE_EXPERTS_PER_GROUP - 1) - ((a * (a - 1)) >> 1) + (b - a - 1)
    bucket = g_idx * MOE_PAIRS + pair
    onehot = lane == bucket
    prefix = jnp.dot(tril, onehot.astype(BF16), preferred_element_type=F32)
    rank = jnp.sum(jnp.where(onehot, prefix + run, 0.0), axis=-1, keepdims=True) - 1.0
    run = run + prefix[ROUTE_CHUNK - 1:ROUTE_CHUNK, :]
    route = jnp.where(lane == 0, bucket.astype(F32),
                      jnp.where(lane == 1, rank,
                                jnp.where(lane == 2, w_lo, jnp.where(lane == 3, w_hi, 0.0))))
    return route, run


def _router(x_all, gain, mods_l, w_router, b_router, n_batch, seq, n_tiles):
    d = x_all.shape[1]
    tril = np.tril(np.ones((ROUTE_CHUNK, ROUTE_CHUNK), np.float32))
    w_hi = w_router.astype(BF16)
    w_split = jnp.stack([w_hi, (w_router - w_hi.astype(F32)).astype(BF16)])
    return pl.pallas_call(
        _router_kernel,
        grid=(n_tiles,),
        in_specs=[
            pl.BlockSpec((TOKEN_TILE, d), lambda i: (i, 0)),
            pl.BlockSpec((1, d), lambda i: (0, 0)),
            pl.BlockSpec((None, N_MOD, d), _mod_index(seq // TOKEN_TILE, n_batch)),
            pl.BlockSpec((2, d, LANES), lambda i: (0, 0, 0)),
            pl.BlockSpec((1, LANES), lambda i: (0, 0)),
            pl.BlockSpec((ROUTE_CHUNK, ROUTE_CHUNK), lambda i: (0, 0)),
        ],
        out_specs=[pl.BlockSpec((TOKEN_TILE, d), lambda i: (i, 0)),
                   pl.BlockSpec((TOKEN_TILE, LANES), lambda i: (i, 0)),
                   pl.BlockSpec((1, LANES), lambda i: (0, 0))],
        out_shape=[jax.ShapeDtypeStruct((n_tiles * TOKEN_TILE, d), jnp.int32),
                   jax.ShapeDtypeStruct((n_tiles * TOKEN_TILE, LANES), F32),
                   jax.ShapeDtypeStruct((1, LANES), F32)],
        scratch_shapes=[pltpu.VMEM((1, LANES), F32)],
        compiler_params=_cparams("arbitrary"),
        name="moe_router",
    )(x_all, gain.reshape(1, d), mods_l, w_split, b_router, jnp.asarray(tril, BF16))


def _row_copy(table_hbm, dst, sem, src_row, dst_row):
    return pltpu.make_async_copy(table_hbm.at[pl.ds(src_row, 1)], dst.at[pl.ds(dst_row, 1)], sem)


def _start_row_gather(idx_ref, base, table_hbm, dst, sem, unrolled):
    rows = dst.shape[0]
    if unrolled:
        for r in range(rows):
            _row_copy(table_hbm, dst, sem, idx_ref[base + r], r).start()
    else:
        def issue(r, carry):
            _row_copy(table_hbm, dst, sem, idx_ref[base + r], r).start()
            return carry

        lax.fori_loop(0, rows, issue, 0, unroll=GATHER_UNROLL)


def _wait_row_gather(table_hbm, dst, sem):
    def drain(r, carry):
        _row_copy(table_hbm, dst, sem, 0, r).wait()
        return carry

    lax.fori_loop(0, dst.shape[0], drain, 0, unroll=GATHER_UNROLL)


def _gather_residual_kernel(idx_ref, y_hbm, x_ref, mod_ref, o_ref, buf, sem):
    _start_row_gather(idx_ref, pl.program_id(0) * buf.shape[0], y_hbm, buf, sem, unrolled=True)
    _wait_row_gather(y_hbm, buf, sem)
    o_ref[...] = x_ref[...] + mod_ref[5:6, :] * buf[...]


def _gather_residual(x_all, y_sorted, dest, mods_l, n_batch, seq, n_tiles):
    d = x_all.shape[1]
    row = pl.BlockSpec((TOKEN_TILE, d), lambda i, idx: (i, 0))
    mod_index = _mod_index(seq // TOKEN_TILE, n_batch)
    return pl.pallas_call(
        _gather_residual_kernel,
        grid_spec=pltpu.PrefetchScalarGridSpec(
            num_scalar_prefetch=1,
            grid=(n_tiles,),
            in_specs=[pl.BlockSpec(memory_space=pl.ANY), row,
                      pl.BlockSpec((None, N_MOD, d), lambda i, idx: mod_index(i))],
            out_specs=row,
            scratch_shapes=[pltpu.VMEM((TOKEN_TILE, d), F32), pltpu.SemaphoreType.DMA],
        ),
        out_shape=jax.ShapeDtypeStruct((n_tiles * TOKEN_TILE, d), F32),
        compiler_params=_cparams("arbitrary"),
        name="moe_combine",
    )(dest, y_sorted, x_all, mods_l)


def _bucket_experts():
    grp, lo, hi = [], [], []
    for g in range(MOE_GROUPS):
        for a in range(MOE_EXPERTS_PER_GROUP):
            for b in range(a + 1, MOE_EXPERTS_PER_GROUP):
                grp.append(g)
                lo.append(a)
                hi.append(b)
    return np.asarray(grp, np.int32), np.asarray(lo, np.int32), np.asarray(hi, np.int32)


def _sorted_experts_kernel(src_ref, grp_ref, lo_ref, hi_ref, nact_ref, tab_hbm, wg_ref, wu_ref, wd_ref,
                           o_ref, xbuf, sems):
    del grp_ref
    t = pl.program_id(0)
    n_active = nact_ref[0]
    n_slots = MOE_GATHER_AHEAD + 1
    slot = t % n_slots
    ahead = (t + MOE_GATHER_AHEAD) % n_slots
    half = D_MODEL // 2

    @pl.when(t == 0)
    def _first_tiles():
        for k in range(MOE_GATHER_AHEAD):
            _start_row_gather(src_ref, k * MOE_SORT_TILE, tab_hbm, xbuf.at[k], sems.at[k], unrolled=False)

    @pl.when(t < n_active + MOE_GATHER_AHEAD)
    def _retire():
        _wait_row_gather(tab_hbm, xbuf.at[slot], sems.at[slot])

    @pl.when(t >= n_active)
    def _unused_tile():
        o_ref[...] = jnp.zeros_like(o_ref)

    @pl.when(t < n_active)
    def _active_tile():
        _start_row_gather(src_ref, (t + MOE_GATHER_AHEAD) * MOE_SORT_TILE, tab_hbm, xbuf.at[ahead],
                          sems.at[ahead], unrolled=True)
        words = pltpu.bitcast(xbuf[slot, :, :half], jnp.uint32)
        x_lo = pltpu.bitcast(words << 16, F32)
        x_hi = pltpu.bitcast(words & jnp.uint32(0xFFFF0000), F32)
        xb = jnp.concatenate([x_lo, x_hi], axis=-1).astype(BF16)
        r = pltpu.bitcast(xbuf[slot, :, half:half + LANES], F32)
        acc = None
        for e_ref, lane in ((lo_ref, 2), (hi_ref, 3)):
            e = e_ref[t]
            gate = jnp.dot(xb, wg_ref[e], preferred_element_type=F32)
            up = jnp.dot(xb, wu_ref[e], preferred_element_type=F32)
            act = _silu(gate) * up * r[:, lane:lane + 1]
            part = jnp.dot(act.astype(BF16), wd_ref[e], preferred_element_type=F32)
            acc = part if acc is None else acc + part
        o_ref[...] = acc


def _sorted_experts(table, src, tile_group, tile_lo, tile_hi, n_active, wg, wu, wd):
    n_rows = src.shape[0]
    group_w = lambda a: pl.BlockSpec((None,) + a.shape[1:], lambda t, src, grp, lo, hi, n: (grp[t], 0, 0, 0))
    return pl.pallas_call(
        _sorted_experts_kernel,
        grid_spec=pltpu.PrefetchScalarGridSpec(
            num_scalar_prefetch=5,
            grid=(n_rows // MOE_SORT_TILE,),
            in_specs=[pl.BlockSpec(memory_space=pl.ANY), group_w(wg), group_w(wu), group_w(wd)],
            out_specs=pl.BlockSpec((MOE_SORT_TILE, D_MODEL), lambda t, src, grp, lo, hi, n: (t, 0)),
            scratch_shapes=[pltpu.VMEM((MOE_GATHER_AHEAD + 1, MOE_SORT_TILE, table.shape[1]), table.dtype),
                            pltpu.SemaphoreType.DMA((MOE_GATHER_AHEAD + 1,))],
        ),
        out_shape=jax.ShapeDtypeStruct((n_rows, D_MODEL), F32),
        compiler_params=_cparams("arbitrary"),
        name="moe_experts",
    )(src, tile_group, tile_lo, tile_hi, n_active, table, wg, wu, wd)


def _moe(x_all, gain, mods_l, w_router, b_router, wg, wu, wd, n_batch, seq, n_tiles):
    n = n_tiles * TOKEN_TILE
    table, route, counts = _router(x_all, gain, mods_l, w_router, b_router, n_batch, seq, n_tiles)

    n_sorted_tiles = (n + MOE_BUCKETS * (MOE_SORT_TILE - 1)) // MOE_SORT_TILE + MOE_GATHER_AHEAD
    n_sorted = n_sorted_tiles * MOE_SORT_TILE
    counts = counts[0, :MOE_BUCKETS].astype(jnp.int32)
    bucket_tiles = (counts + MOE_SORT_TILE - 1) // MOE_SORT_TILE
    tile_end = jnp.cumsum(bucket_tiles)
    row_start = (tile_end - bucket_tiles) * MOE_SORT_TILE
    bucket = route[:, 0].astype(jnp.int32)
    rank = route[:, 1].astype(jnp.int32)
    dest = jnp.sum(jnp.where(bucket[:, None] == jnp.arange(MOE_BUCKETS)[None, :], row_start[None, :], 0),
                   axis=1) + rank
    src = (jnp.arange(n_sorted, dtype=jnp.int32) % n).at[dest].set(jnp.arange(n, dtype=jnp.int32))
    tiles = jnp.arange(n_sorted_tiles, dtype=jnp.int32)
    tile_bucket = jnp.sum(jnp.minimum(tiles, tile_end[-1] - 1)[:, None] >= tile_end[None, :], axis=1)
    tile_bucket = jnp.minimum(tile_bucket, MOE_BUCKETS - 1)
    grp_ids, lo_ids, hi_ids = _bucket_experts()
    tile_group = jnp.asarray(grp_ids)[tile_bucket]
    tile_lo = jnp.asarray(lo_ids)[tile_bucket]
    tile_hi = jnp.asarray(hi_ids)[tile_bucket]

    n_active = tile_end[-1:].astype(jnp.int32)
    y_sorted = _sorted_experts(table, src, tile_group, tile_lo, tile_hi, n_active, wg, wu, wd)
    return _gather_residual(x_all, y_sorted, dest, mods_l, n_batch, seq, n_tiles)


def kernel(x, c, ctx, c_ctx, w_ada, b_ada, g_mix, g_ffn, w_in, w_out, gmlp_v_gain, gmlp_ws, gmlp_bs,
           na_q_gain, na_k_gain, na_rpb, hy_short_w, hy_short_b, hy_w1, hy_b1, hy_w2, hy_b2, hy_w3,
           hy_freq, hy_bias, moe_w_rg, moe_b_rg, moe_w_re, moe_b_re, moe_w_gate, moe_w_up, moe_w_down):
    n_batch, seq, d = x.shape
    ctx_len = ctx.shape[1]
    depth = w_ada.shape[0]
    n_lat = n_batch * seq
    n_ctx = n_batch * ctx_len
    assert d == D_MODEL and seq % TOKEN_TILE == 0 and n_ctx % TOKEN_TILE == 0
    assert seq % GMLP_CHUNK == 0 and ctx_len % GMLP_CHUNK == 0

    pad_rows = -(n_batch + 1) % 8
    cs = jnp.concatenate([c, c_ctx[None, :], jnp.zeros((pad_rows, d), F32)], axis=0)
    mods = _modulation(cs, w_ada, b_ada)[:, :n_batch + 1].reshape(depth, n_batch + 1, N_MOD, d)

    x_all = jnp.concatenate([x.reshape(n_lat, d), ctx.reshape(n_ctx, d)], axis=0)
    n_all_tiles = (n_lat + n_ctx) // TOKEN_TILE
    n_lat_tiles = n_lat // TOKEN_TILE

    mats_lat = _dft_matrices(seq)
    mats_ctx = _dft_matrices(ctx_len)
    pos_lat = _hyena_positions(seq)
    pos_ctx = _hyena_positions(ctx_len)

    w_in_bf = w_in.astype(BF16)
    w_out_bf = w_out.astype(BF16)
    wg_bf = moe_w_gate.astype(BF16)
    wu_bf = moe_w_up.astype(BF16)
    wd_bf = moe_w_down.astype(BF16)
    lane_pad = LANES - MOE_EXPERTS - MOE_GROUPS
    w_router = jnp.pad(jnp.concatenate([moe_w_re, moe_w_rg], axis=-1), ((0, 0), (0, 0), (0, lane_pad)))
    b_router = jnp.pad(jnp.concatenate([moe_b_re, moe_b_rg], axis=-1), ((0, 0), (0, lane_pad)))[:, None, :]
    w1_pad = jnp.pad(hy_w1, ((0, 0), (0, LANES - HYENA_EMB), (0, 0)))

    for l in range(depth):
        last = l == depth - 1
        mods_l = mods[l]
        n_tiles = n_lat_tiles if last else n_all_tiles

        p_all = _in_proj(x_all, g_mix[l], mods_l, w_in_bf[l], n_batch, seq)

        filt = (w1_pad[l], hy_b1[l], hy_w2[l], hy_b2[l], hy_w3[l], hy_freq[l])
        kpack_lat = _filter_dft(mats_lat[0], mats_lat[1], _hyena_filters(*pos_lat, *filt))
        hy_lat = _hyena(p_all, hy_short_w[l], hy_short_b[l], hy_bias[l], kpack_lat, mats_lat,
                        n_batch, seq, 0)
        if last:
            hy_ctx = hy_lat
            gm = _gmlp(p_all, n_lat, gmlp_v_gain[l], gmlp_ws[l], gmlp_bs[l].T)
        else:
            kpack_ctx = _filter_dft(mats_ctx[0], mats_ctx[1], _hyena_filters(*pos_ctx, *filt))
            hy_ctx = _hyena(p_all, hy_short_w[l], hy_short_b[l], hy_bias[l], kpack_ctx, mats_ctx,
                            n_batch, ctx_len, n_lat // ctx_len)
            gm = _gmlp(p_all, n_lat + n_ctx, gmlp_v_gain[l], gmlp_ws[l], gmlp_bs[l].T)

        na = _attention(p_all, na_rpb[l], na_q_gain[l], na_k_gain[l],
                        n_batch, seq, ctx_len, not last)

        x_all = _out_proj(x_all, gm, na, hy_lat, hy_ctx, w_out_bf[l], mods_l, n_batch, seq, n_tiles)

        x_all = _moe(x_all, g_ffn[l], mods_l, w_router[l], b_router[l], wg_bf[l], wu_bf[l], wd_bf[l],
                     n_batch, seq, n_tiles)

    return x_all[:n_lat].reshape(n_batch, seq, d)
```

```python
import functools
import math

import numpy as np
import jax
import jax.numpy as jnp
from jax import lax
from jax.experimental import pallas as pl
from jax.experimental.pallas import tpu as pltpu

F32 = jnp.float32
BF16 = jnp.bfloat16
HI = lax.Precision.HIGHEST

D_MODEL = 1024
GRID_W = 64
D_GMLP = D_MODEL // 4
D_NA = D_MODEL // 2
D_HYENA = D_MODEL // 4
D_IN = 2 * D_GMLP + 3 * D_NA + 3 * D_HYENA
Q_START = 2 * D_GMLP
KV_START = 2 * D_GMLP + D_NA
HY_START = 2 * D_GMLP + 3 * D_NA
GMLP_GROUPS = 4
GMLP_GROUP_DIM = D_GMLP // GMLP_GROUPS
GMLP_CHUNK = 128
NA_HEAD_DIM = 64
NA_HEADS = D_NA // NA_HEAD_DIM
NA_SCALE = NA_HEAD_DIM ** -0.5
NA_WIN_ROWS = 8
NA_WIN_COLS = 16
HYENA_ORDER = 2
HYENA_POS_BANDS = 16
HYENA_EMB = 1 + 2 * HYENA_POS_BANDS
HYENA_FILTER_HIDDEN = 64
HYENA_DECAY_TARGET = 1e-2
HYENA_FAST_DECAY = 0.3
HYENA_SLOW_DECAY = 1.5
MOE_GROUPS = 4
MOE_EXPERTS_PER_GROUP = 8
MOE_EXPERTS = MOE_GROUPS * MOE_EXPERTS_PER_GROUP
MOE_HIDDEN = 256
MOE_PAIRS = MOE_EXPERTS_PER_GROUP * (MOE_EXPERTS_PER_GROUP - 1) // 2
MOE_BUCKETS = MOE_GROUPS * MOE_PAIRS
N_MOD = 6
RMS_EPS = 1e-6
LN_EPS = 1e-5

LANES = 128
TOKEN_TILE = 512
NA_Q_ROWS = 4
NA_Q_BLOCK = NA_Q_ROWS * GRID_W
NA_BAND_ROWS = NA_WIN_ROWS + NA_Q_ROWS
DFT_TILE = 512
CONV_TILE = 1024
ROUTE_CHUNK = 128
ROUTE_FIELDS = 8
MOE_SORT_TILE = 128
MOE_GATHER_AHEAD = 1
GATHER_UNROLL = 8
MASK_VALUE = -1e30
VMEM_LIMIT = 56 * 1024 * 1024


def _cparams(*sem):
    return pltpu.CompilerParams(dimension_semantics=sem, vmem_limit_bytes=VMEM_LIMIT)


def _silu(x):
    return x * jax.nn.sigmoid(x)


def _rms_rows(x):
    return x * lax.rsqrt(jnp.mean(x * x, axis=-1, keepdims=True) + RMS_EPS)


def _mods_kernel(cs_ref, w_ref, b_ref, o_ref):
    s = _silu(cs_ref[...])
    o_ref[...] = jnp.dot(s, w_ref[...], preferred_element_type=F32, precision=HI) + b_ref[...]


def _modulation(cs, w_ada, b_ada):
    depth, d, nd = w_ada.shape
    rows = cs.shape[0]
    col = 1024
    return pl.pallas_call(
        _mods_kernel,
        grid=(depth, nd // col),
        in_specs=[
            pl.BlockSpec((rows, d), lambda l, j: (0, 0)),
            pl.BlockSpec((None, d, col), lambda l, j: (l, 0, j)),
            pl.BlockSpec((None, 1, col), lambda l, j: (l, 0, j)),
        ],
        out_specs=pl.BlockSpec((None, rows, col), lambda l, j: (l, 0, j)),
        out_shape=jax.ShapeDtypeStruct((depth, rows, nd), F32),
        compiler_params=_cparams("arbitrary", "arbitrary"),
        name="modulation",
    )(cs, w_ada, b_ada.reshape(depth, 1, nd))


def _in_proj_kernel(x_ref, g_ref, mod_ref, w_ref, o_ref):
    mod = mod_ref[...]
    h = _rms_rows(x_ref[...]) * g_ref[...]
    h = h * (1.0 + mod[1:2]) + mod[0:1]
    o_ref[...] = jnp.dot(h.astype(BF16), w_ref[...], preferred_element_type=F32).astype(o_ref.dtype)


def _mod_index(tiles_per_batch, n_batch):
    return lambda i: (jnp.minimum(i // tiles_per_batch, n_batch), 0, 0)


def _in_proj(x_all, gain, mods_l, w_bf, n_batch, seq):
    n, d = x_all.shape
    d_in = w_bf.shape[1]
    return pl.pallas_call(
        _in_proj_kernel,
        grid=(n // TOKEN_TILE,),
        in_specs=[
            pl.BlockSpec((TOKEN_TILE, d), lambda i: (i, 0)),
            pl.BlockSpec((1, d), lambda i: (0, 0)),
            pl.BlockSpec((None, N_MOD, d), _mod_index(seq // TOKEN_TILE, n_batch)),
            pl.BlockSpec((d, d_in), lambda i: (0, 0)),
        ],
        out_specs=pl.BlockSpec((TOKEN_TILE, d_in), lambda i: (i, 0)),
        out_shape=jax.ShapeDtypeStruct((n, d_in), BF16),
        compiler_params=_cparams("arbitrary"),
        name="in_proj",
    )(x_all, gain.reshape(1, d), mods_l, w_bf)


def _group_avg_matrix(groups, width):
    return np.kron(np.eye(groups), np.full((width, width), 1.0 / width))


def _group_mean(t, avg):
    hi = t.astype(BF16)
    lo = (t - hi.astype(F32)).astype(BF16)
    return (jnp.dot(hi, avg, preferred_element_type=F32) + jnp.dot(lo, avg, preferred_element_type=F32))


def _gmlp_kernel(u_ref, v_ref, gain_ref, avg_ref, ws_ref, bs_ref, o_ref):
    avg = avg_ref[...]
    v = jax.nn.gelu(v_ref[...].astype(F32))
    v = v - _group_mean(v, avg)
    v = v * lax.rsqrt(_group_mean(v * v, avg) + LN_EPS) * gain_ref[...]
    vb = v.astype(BF16)
    bs = bs_ref[...]
    for c in range(u_ref.shape[0] // GMLP_CHUNK):
        rows = slice(c * GMLP_CHUNK, (c + 1) * GMLP_CHUNK)
        outs = []
        for g in range(GMLP_GROUPS):
            cols = slice(g * GMLP_GROUP_DIM, (g + 1) * GMLP_GROUP_DIM)
            s = jnp.dot(ws_ref[g], vb[rows, cols], preferred_element_type=F32) + bs[:, g:g + 1]
            outs.append(s)
        o_ref[rows, :] = jax.nn.gelu(u_ref[rows, :].astype(F32)) * jnp.concatenate(outs, axis=-1)


def _gmlp(p_all, n, v_gain, ws, bs_t):
    avg = _group_avg_matrix(GMLP_GROUPS, GMLP_GROUP_DIM)
    return pl.pallas_call(
        _gmlp_kernel,
        grid=(n // TOKEN_TILE,),
        in_specs=[
            pl.BlockSpec((TOKEN_TILE, D_GMLP), lambda i: (i, 0)),
            pl.BlockSpec((TOKEN_TILE, D_GMLP), lambda i: (i, 1)),
            pl.BlockSpec((1, D_GMLP), lambda i: (0, 0)),
            pl.BlockSpec((D_GMLP, D_GMLP), lambda i: (0, 0)),
            pl.BlockSpec((GMLP_GROUPS, GMLP_CHUNK, GMLP_CHUNK), lambda i: (0, 0, 0)),
            pl.BlockSpec((GMLP_CHUNK, GMLP_GROUPS), lambda i: (0, 0)),
        ],
        out_specs=pl.BlockSpec((TOKEN_TILE, D_GMLP), lambda i: (i, 0)),
        out_shape=jax.ShapeDtypeStruct((n, D_GMLP), F32),
        compiler_params=_cparams("arbitrary"),
        name="gmlp",
    )(p_all, p_all, v_gain.reshape(1, D_GMLP), jnp.asarray(avg, BF16), ws.astype(BF16), bs_t)


def _rpb_expand_kernel(rpb_ref, sel_ref, o_ref):
    o_ref[...] = jnp.dot(rpb_ref[...], sel_ref[...], preferred_element_type=F32, precision=HI)


def _na_geometry(grid_rows):
    variants, step_variant, band_start = [], [], []
    for r0 in range(0, grid_rows, NA_Q_ROWS):
        b0 = int(np.clip(r0 - NA_WIN_ROWS // 2, 0, grid_rows - NA_BAND_ROWS))
        geo = []
        for r in range(r0, r0 + NA_Q_ROWS):
            wr = int(np.clip(r - NA_WIN_ROWS // 2, 0, grid_rows - NA_WIN_ROWS))
            assert b0 <= wr and wr + NA_WIN_ROWS <= b0 + NA_BAND_ROWS
            geo.append((wr - b0, wr - r + NA_WIN_ROWS - 1))
        geo = tuple(geo)
        if geo not in variants:
            variants.append(geo)
        step_variant.append(variants.index(geo))
        band_start.append(b0)
    return variants, step_variant, band_start


def _na_bias_tables(rpb, variants):
    n_heads, n_dr, n_dc = rpb.shape
    qcol = np.arange(GRID_W)[:, None]
    kcol = np.arange(GRID_W)[None, :]
    win_c = np.clip(qcol - NA_WIN_COLS // 2, 0, GRID_W - NA_WIN_COLS)
    col_ok = (kcol >= win_c) & (kcol < win_c + NA_WIN_COLS)
    dc = np.clip(kcol - qcol + NA_WIN_COLS - 1, 0, 2 * NA_WIN_COLS - 2)
    dc_pad = -n_dc % 8
    sel = (np.arange(n_dc + dc_pad)[:, None] == dc.reshape(1, -1)).astype(np.float32)
    rpb2 = jnp.pad(rpb.reshape(n_heads * n_dr, n_dc), ((0, 0), (0, dc_pad)))
    toep = pl.pallas_call(
        _rpb_expand_kernel,
        out_shape=jax.ShapeDtypeStruct((n_heads * n_dr, GRID_W * GRID_W), F32),
        name="rpb_expand",
    )(rpb2, jnp.asarray(sel))
    toep = toep.reshape(n_heads, n_dr, GRID_W, GRID_W)
    return pl.pallas_call(
        functools.partial(_bias_table_kernel, variants=tuple(variants)),
        grid=(n_heads,),
        in_specs=[pl.BlockSpec((None, n_dr, GRID_W, GRID_W), lambda h: (h, 0, 0, 0)),
                  pl.BlockSpec((GRID_W, GRID_W), lambda h: (0, 0))],
        out_specs=pl.BlockSpec((None, len(variants), NA_Q_BLOCK, NA_BAND_ROWS * GRID_W),
                               lambda h: (h, 0, 0, 0)),
        out_shape=jax.ShapeDtypeStruct((n_heads, len(variants), NA_Q_BLOCK, NA_BAND_ROWS * GRID_W), F32),
        compiler_params=_cparams("arbitrary"),
        name="na_bias_table",
    )(toep, jnp.asarray(col_ok.astype(np.float32)))


def _bias_table_kernel(toep_ref, ok_ref, o_ref, *, variants):
    ok = ok_ref[...] > 0.0
    outside = jnp.full((GRID_W, GRID_W), MASK_VALUE, F32)
    blocks = [jnp.where(ok, toep_ref[dr], MASK_VALUE) for dr in range(toep_ref.shape[0])]
    for v, geo in enumerate(variants):
        for i, (a0, dr0) in enumerate(geo):
            pieces = [blocks[dr0 + a - a0] if a0 <= a < a0 + NA_WIN_ROWS else outside
                      for a in range(NA_BAND_ROWS)]
            o_ref[v, i * GRID_W:(i + 1) * GRID_W, :] = jnp.concatenate(pieces, axis=-1)


def _store_heads(dst, rows, t, gain, avg):
    if gain is not None:
        t = t.astype(F32)
        t = t * lax.rsqrt(_group_mean(t * t, avg) + RMS_EPS) * gain
    for h in range(NA_HEADS):
        dst[h, rows, :] = t[:, h * NA_HEAD_DIM:(h + 1) * NA_HEAD_DIM].astype(BF16)


def _na_kernel(q_ref, k_ref, v_ref, kc_ref, vc_ref, bias_ref, qg_ref, kg_ref, avg_ref, o_ref,
               kn_s, vb_s, kcn_s, vcb_s, qn_s, o_s, s_scr, p_scr, l_scr,
               *, n_lat_steps, step_variant, band_start):
    step = pl.program_id(1)
    kg = kg_ref[...]
    avg = avg_ref[...]
    nt = (((1,), (1,)), ((), ()))
    band_keys = NA_BAND_ROWS * GRID_W
    all_rows = slice(None)

    @pl.when(step == 0)
    def _prepare_keys():
        chunk = 256

        def body(c, carry):
            rows = pl.ds(pl.multiple_of(c * chunk, chunk), chunk)
            _store_heads(kn_s, rows, k_ref[rows, :], kg, avg)
            _store_heads(vb_s, rows, v_ref[rows, :], None, None)
            return carry

        lax.fori_loop(0, k_ref.shape[0] // chunk, body, 0)
        _store_heads(kcn_s, all_rows, kc_ref[...], kg, avg)
        _store_heads(vcb_s, all_rows, vc_ref[...], None, None)

    _store_heads(qn_s, all_rows, q_ref[...], qg_ref[...] * NA_SCALE, avg)

    def finish(h, scores, v_parts):
        m = jnp.max(scores, axis=-1, keepdims=True)
        p = jnp.exp(scores - m)
        denom = jnp.sum(p, axis=-1, keepdims=True)
        pb = p.astype(BF16)
        acc = None
        col = 0
        for v in v_parts:
            part = jnp.dot(pb[:, col:col + v.shape[0]], v, preferred_element_type=F32)
            acc = part if acc is None else acc + part
            col += v.shape[0]
        o_s[h] = acc / denom

    @pl.when(step < n_lat_steps)
    def _latent_queries():
        variant = jnp.int32(0)
        band0 = jnp.int32(0)
        for s_, (v_, b_) in enumerate(zip(step_variant, band_start)):
            variant = jnp.where(step == s_, v_, variant)
            band0 = jnp.where(step == s_, b_ * GRID_W, band0)
        krows = pl.ds(pl.multiple_of(band0, NA_Q_BLOCK), band_keys)

        def scores(h):
            qh = qn_s[h]
            s_scr[h % 2, :, :band_keys] = (
                lax.dot_general(qh, kn_s[h, krows, :], nt, preferred_element_type=F32) + bias_ref[h, variant])
            s_scr[h % 2, :, band_keys:] = lax.dot_general(qh, kcn_s[h], nt, preferred_element_type=F32)

        def softmax(h):
            s = s_scr[h % 2]
            p = jnp.exp(s - jnp.max(s, axis=-1, keepdims=True))
            l_scr[h % 2] = jnp.sum(p, axis=-1, keepdims=True)
            p_scr[h % 2] = p.astype(BF16)

        def values(h):
            acc = jnp.dot(p_scr[h % 2, :, :band_keys], vb_s[h, krows, :], preferred_element_type=F32)
            acc = acc + jnp.dot(p_scr[h % 2, :, band_keys:], vcb_s[h], preferred_element_type=F32)
            o_s[h] = acc / l_scr[h % 2]

        for stage in range(NA_HEADS + 2):
            if stage < NA_HEADS:
                scores(stage)
            if 1 <= stage <= NA_HEADS:
                softmax(stage - 1)
            if stage >= 2:
                values(stage - 2)

    @pl.when(step >= n_lat_steps)
    def _context_queries():
        def head_body(h, carry):
            s = lax.dot_general(qn_s[h], kcn_s[h], nt, preferred_element_type=F32)
            finish(h, s, (vcb_s[h],))
            return carry

        lax.fori_loop(0, NA_HEADS, head_body, 0)

    o_ref[...] = jnp.concatenate([o_s[h] for h in range(NA_HEADS)], axis=-1)


def _attention(p_all, rpb, q_gain, k_gain, n_batch, seq, ctx_len, with_ctx_queries):
    n = p_all.shape[0]
    assert ctx_len == NA_Q_BLOCK and seq % NA_Q_BLOCK == 0
    n_lat_steps = seq // NA_Q_BLOCK
    n_steps = n_lat_steps + (1 if with_ctx_queries else 0)
    ctx_block0 = n_batch * seq // ctx_len
    qcol, kcol, vcol = Q_START // D_NA, KV_START // D_NA, (KV_START + D_NA) // D_NA
    variants, step_variant, band_start = _na_geometry(seq // GRID_W)
    bias_tab = _na_bias_tables(rpb, variants)

    def q_index(col):
        return lambda b, s: (jnp.where(s < n_lat_steps, b * n_lat_steps + s, ctx_block0 + b), col)

    kern = functools.partial(_na_kernel, n_lat_steps=n_lat_steps, step_variant=tuple(step_variant),
                             band_start=tuple(band_start))
    head_major = lambda rows: pltpu.VMEM((NA_HEADS, rows, NA_HEAD_DIM), BF16)
    return pl.pallas_call(
        kern,
        grid=(n_batch, n_steps),
        in_specs=[
            pl.BlockSpec((NA_Q_BLOCK, D_NA), q_index(qcol)),
            pl.BlockSpec((seq, D_NA), lambda b, s: (b, kcol)),
            pl.BlockSpec((seq, D_NA), lambda b, s: (b, vcol)),
            pl.BlockSpec((ctx_len, D_NA), lambda b, s: (ctx_block0 + b, kcol)),
            pl.BlockSpec((ctx_len, D_NA), lambda b, s: (ctx_block0 + b, vcol)),
            pl.BlockSpec(bias_tab.shape, lambda b, s: (0, 0, 0, 0), pipeline_mode=pl.Buffered(1)),
            pl.BlockSpec((1, D_NA), lambda b, s: (0, 0)),
            pl.BlockSpec((1, D_NA), lambda b, s: (0, 0)),
            pl.BlockSpec((D_NA, D_NA), lambda b, s: (0, 0)),
        ],
        out_specs=pl.BlockSpec((NA_Q_BLOCK, D_NA), q_index(0)),
        out_shape=jax.ShapeDtypeStruct((n if with_ctx_queries else n_batch * seq, D_NA), F32),
        scratch_shapes=[
            head_major(seq), head_major(seq), head_major(ctx_len), head_major(ctx_len),
            head_major(NA_Q_BLOCK),
            pltpu.VMEM((NA_HEADS, NA_Q_BLOCK, NA_HEAD_DIM), F32),
            pltpu.VMEM((2, NA_Q_BLOCK, NA_BAND_ROWS * GRID_W + ctx_len), F32),
            pltpu.VMEM((2, NA_Q_BLOCK, NA_BAND_ROWS * GRID_W + ctx_len), BF16),
            pltpu.VMEM((2, NA_Q_BLOCK, 1), F32),
        ],
        compiler_params=_cparams("arbitrary", "arbitrary"),
        name="attention",
    )(p_all, p_all, p_all, p_all, p_all, bias_tab,
      jnp.tile(q_gain.reshape(1, NA_HEAD_DIM), (1, NA_HEADS)),
      jnp.tile(k_gain.reshape(1, NA_HEAD_DIM), (1, NA_HEADS)),
      jnp.asarray(_group_avg_matrix(NA_HEADS, NA_HEAD_DIM), BF16))


def _dft_matrices(length):
    idx = jnp.arange(length, dtype=jnp.int32)
    step = 64
    t_hi = jnp.arange(length // step, dtype=jnp.int32) * step
    t_lo = jnp.arange(step, dtype=jnp.int32)
    ang_hi = ((idx[:, None] * t_hi[None, :]) % (2 * length)).astype(F32) * (math.pi / length)
    ang_lo = ((idx[:, None] * t_lo[None, :]) % (2 * length)).astype(F32) * (math.pi / length)
    c_hi, s_hi = jnp.cos(ang_hi)[:, :, None], jnp.sin(ang_hi)[:, :, None]
    c_lo, s_lo = jnp.cos(ang_lo)[:, None, :], jnp.sin(ang_lo)[:, None, :]
    gc = (c_hi * c_lo - s_hi * s_lo).reshape(length, length)
    gs = (s_hi * c_lo + c_hi * s_lo).reshape(length, length)
    nyq = jnp.where(idx % 2 == 0, 1.0, -1.0).astype(F32)
    gs = jnp.where(idx[:, None] == 0, nyq[None, :], gs)
    return gc.astype(BF16), gs.astype(BF16), gs.T.astype(BF16)


def _hyena_positions(length):
    t = jnp.linspace(0.0, 1.0, length, dtype=F32)[:, None]
    w = 2.0 * math.pi * jnp.arange(length, dtype=F32)[:, None] / length
    f = jnp.linspace(1e-4, HYENA_POS_BANDS - 1, HYENA_POS_BANDS, dtype=F32)[None, :]
    z = jnp.concatenate([t, jnp.cos(f * w), -jnp.sin(f * w)], axis=-1)
    z = jnp.pad(z, ((0, 0), (0, LANES - HYENA_EMB)))
    min_decay = math.log(HYENA_DECAY_TARGET) / HYENA_SLOW_DECAY
    max_decay = math.log(HYENA_DECAY_TARGET) / HYENA_FAST_DECAY
    deltas = jnp.abs(jnp.linspace(min_decay, max_decay, D_HYENA, dtype=F32))[None, :]
    return z, jnp.exp(-t * deltas)


def _filter_kernel(z_ref, decay_ref, w1_ref, b1_ref, w2_ref, b2_ref, w3_ref, freq_ref, o_ref):
    freq = freq_ref[...]
    hdn = jnp.dot(z_ref[...], w1_ref[...], preferred_element_type=F32, precision=HI) + b1_ref[...]
    hdn = jnp.sin(freq[0:1] * hdn)
    hdn = jnp.dot(hdn, w2_ref[...], preferred_element_type=F32, precision=HI) + b2_ref[...]
    hdn = jnp.sin(freq[1:2] * hdn)
    h = jnp.dot(hdn, w3_ref[...], preferred_element_type=F32, precision=HI)
    decay = decay_ref[...]
    first_row = lax.broadcasted_iota(jnp.int32, decay.shape, 0) == 0
    outs = []
    for n in range(HYENA_ORDER):
        base = 2 * n * D_HYENA
        hf = h[:, base:base + D_HYENA] * decay
        hb = h[:, base + D_HYENA:base + 2 * D_HYENA] * decay
        norm = jnp.sum(jnp.abs(hf), axis=0, keepdims=True) + jnp.sum(jnp.abs(hb), axis=0, keepdims=True)
        outs.append(hf / norm)
        outs.append(jnp.where(first_row, 0.0, hb / norm))
    o_ref[...] = jnp.concatenate(outs, axis=-1)


def _hyena_filters(z, decay, w1p, b1, w2, b2, w3, freq):
    length = z.shape[0]
    full = lambda a: pl.BlockSpec(a.shape, lambda i: (0,) * a.ndim)
    args = (z, decay, w1p, b1.reshape(1, -1), w2, b2.reshape(1, -1), w3, freq)
    return pl.pallas_call(
        _filter_kernel,
        grid=(1,),
        in_specs=[full(a) for a in args],
        out_specs=pl.BlockSpec((length, 2 * HYENA_ORDER * D_HYENA), lambda i: (0, 0)),
        out_shape=jax.ShapeDtypeStruct((length, 2 * HYENA_ORDER * D_HYENA), F32),
        compiler_params=_cparams("arbitrary"),
        name="hyena_filter",
    )(*args)


def _filter_dft_kernel(gc_ref, gs_ref, h_ref, o_ref, *, length):
    hb = h_ref[...].astype(BF16)
    fa = jnp.dot(gc_ref[...], hb, preferred_element_type=F32)
    fb = jnp.dot(gs_ref[...], hb, preferred_element_type=F32)
    rows = lax.broadcasted_iota(jnp.int32, (fa.shape[0], D_HYENA), 0) + pl.program_id(0) * fa.shape[0]
    dc_row = rows == 0
    inv_n = 1.0 / (2 * length)
    outs = []
    for n in range(HYENA_ORDER):
        base = 2 * n * D_HYENA
        f_sl = slice(base, base + D_HYENA)
        b_sl = slice(base + D_HYENA, base + 2 * D_HYENA)
        kr = fa[:, f_sl] + fa[:, b_sl]
        ki = fb[:, b_sl] - fb[:, f_sl]
        k_nyq = fb[:, f_sl] + fb[:, b_sl]
        outs.append(jnp.where(dc_row, kr * inv_n, 2.0 * inv_n * kr))
        outs.append(jnp.where(dc_row, 0.0, 2.0 * inv_n * ki))
        outs.append(jnp.where(dc_row, 0.0, -2.0 * inv_n * ki))
        outs.append(jnp.where(dc_row, k_nyq * inv_n, 2.0 * inv_n * kr))
    o_ref[...] = jnp.concatenate(outs, axis=-1)


def _filter_dft(gc, gs, hfilt):
    length = gc.shape[0]
    tile = min(DFT_TILE, length)
    width = 4 * HYENA_ORDER * D_HYENA
    return pl.pallas_call(
        functools.partial(_filter_dft_kernel, length=length),
        grid=(length // tile,),
        in_specs=[
            pl.BlockSpec((tile, length), lambda j: (j, 0)),
            pl.BlockSpec((tile, length), lambda j: (j, 0)),
            pl.BlockSpec(hfilt.shape, lambda j: (0, 0)),
        ],
        out_specs=pl.BlockSpec((tile, width), lambda j: (j, 0)),
        out_shape=jax.ShapeDtypeStruct((length, width), F32),
        compiler_params=_cparams("arbitrary"),
        name="hyena_filter_dft",
    )(gc, gs, hfilt)


def _short_conv_kernel(a0_ref, a1_ref, a2_ref, w_ref, b_ref, o_ref):
    w = w_ref[...]
    b = b_ref[...]
    length = a0_ref.shape[0]
    rows = lax.broadcasted_iota(jnp.int32, (length, D_HYENA), 0)
    for j, a_ref in enumerate((a0_ref, a1_ref, a2_ref)):
        cols = slice(j * D_HYENA, (j + 1) * D_HYENA)
        a = a_ref[...].astype(F32)
        prev = jnp.where(rows == 0, 0.0, pltpu.roll(a, 1, 0))
        nxt = jnp.where(rows == length - 1, 0.0, pltpu.roll(a, length - 1, 0))
        o_ref[:, cols] = prev * w[0:1, cols] + a * w[1:2, cols] + nxt * w[2:3, cols] + b[:, cols]


def _short_conv(p_all, short_w, short_b, n_batch, length, row_block0):
    c0 = HY_START // D_HYENA
    spec = lambda j: pl.BlockSpec((length, D_HYENA), lambda b: (row_block0 + b, c0 + j))
    return pl.pallas_call(
        _short_conv_kernel,
        grid=(n_batch,),
        in_specs=[spec(0), spec(1), spec(2),
                  pl.BlockSpec((3, 3 * D_HYENA), lambda b: (0, 0)),
                  pl.BlockSpec((1, 3 * D_HYENA), lambda b: (0, 0))],
        out_specs=pl.BlockSpec((length, 3 * D_HYENA), lambda b: (b, 0)),
        out_shape=jax.ShapeDtypeStruct((n_batch * length, 3 * D_HYENA), F32),
        compiler_params=_cparams("arbitrary"),
        name="hyena_short_conv",
    )(p_all, p_all, p_all, short_w, short_b.reshape(1, -1))


def _conv_fwd_kernel(gc_ref, gs_ref, u_ref, k_ref, pa_ref, pb_ref):
    u = u_ref[...].astype(BF16)
    a = jnp.dot(gc_ref[...], u, preferred_element_type=F32)
    b = jnp.dot(gs_ref[...], u, preferred_element_type=F32)
    k = k_ref[...]
    c = D_HYENA
    pa_ref[...] = (a * k[:, 0:c] + b * k[:, c:2 * c]).astype(BF16)
    pb_ref[...] = (a * k[:, 2 * c:3 * c] + b * k[:, 3 * c:4 * c]).astype(BF16)


def _conv_fwd(gc, gs, u, u_col, kpack, order, n_batch):
    length = gc.shape[0]
    tile = min(CONV_TILE, length)
    nt = length // tile
    out = jax.ShapeDtypeStruct((n_batch * length, D_HYENA), BF16)
    return pl.pallas_call(
        _conv_fwd_kernel,
        grid=(nt, n_batch),
        in_specs=[
            pl.BlockSpec((tile, length), lambda j, b: (j, 0)),
            pl.BlockSpec((tile, length), lambda j, b: (j, 0)),
            pl.BlockSpec((length, D_HYENA), lambda j, b: (b, u_col)),
            pl.BlockSpec((tile, 4 * D_HYENA), lambda j, b: (j, order)),
        ],
        out_specs=[pl.BlockSpec((tile, D_HYENA), lambda j, b: (b * nt + j, 0))] * 2,
        out_shape=[out, out],
        compiler_params=_cparams("arbitrary", "arbitrary"),
        name="hyena_conv_fwd",
    )(gc, gs, u, kpack)


def _conv_inv_kernel(gc_ref, gst_ref, pa_ref, pb_ref, z_ref, gate_ref, d_ref, o_ref):
    y = jnp.dot(gc_ref[...], pa_ref[...], preferred_element_type=F32)
    y = y + jnp.dot(gst_ref[...], pb_ref[...], preferred_element_type=F32)
    o_ref[...] = gate_ref[...] * (y + d_ref[...] * z_ref[...])


def _conv_inv(gc, gst, pa, pb, z_prev, z_col, a3, gate_col, d_bias, n_batch):
    length = gc.shape[0]
    tile = min(CONV_TILE, length)
    nt = length // tile
    return pl.pallas_call(
        _conv_inv_kernel,
        grid=(nt, n_batch),
        in_specs=[
            pl.BlockSpec((tile, length), lambda j, b: (j, 0)),
            pl.BlockSpec((tile, length), lambda j, b: (j, 0)),
            pl.BlockSpec((length, D_HYENA), lambda j, b: (b, 0)),
            pl.BlockSpec((length, D_HYENA), lambda j, b: (b, 0)),
            pl.BlockSpec((tile, D_HYENA), lambda j, b: (b * nt + j, z_col)),
            pl.BlockSpec((tile, D_HYENA), lambda j, b: (b * nt + j, gate_col)),
            pl.BlockSpec((1, D_HYENA), lambda j, b: (0, 0)),
        ],
        out_specs=pl.BlockSpec((tile, D_HYENA), lambda j, b: (b * nt + j, 0)),
        out_shape=jax.ShapeDtypeStruct((n_batch * length, D_HYENA), F32),
        compiler_params=_cparams("arbitrary", "arbitrary"),
        name="hyena_conv_inv",
    )(gc, gst, pa, pb, z_prev, a3, d_bias.reshape(1, D_HYENA))


def _hyena(p_all, short_w, short_b, d_bias, kpack, mats, n_batch, length, row_block0):
    gc, gs, gst = mats
    a3 = _short_conv(p_all, short_w, short_b, n_batch, length, row_block0)
    pa, pb = _conv_fwd(gc, gs, a3, 0, kpack, 0, n_batch)
    z1 = _conv_inv(gc, gst, pa, pb, a3, 0, a3, 1, d_bias[0], n_batch)
    pa, pb = _conv_fwd(gc, gs, z1, 0, kpack, 1, n_batch)
    return _conv_inv(gc, gst, pa, pb, z1, 0, a3, 2, d_bias[1], n_batch)


def _out_proj_kernel(x_ref, gm_ref, na_ref, hyl_ref, hyc_ref, w_ref, mod_ref, o_ref, *, n_lat_tiles):
    is_lat = pl.program_id(0) < n_lat_tiles
    hy = jnp.where(is_lat, hyl_ref[...], hyc_ref[...])
    y = jnp.dot(gm_ref[...].astype(BF16), w_ref[0:D_GMLP, :], preferred_element_type=F32)
    y = y + jnp.dot(na_ref[...].astype(BF16), w_ref[D_GMLP:D_GMLP + D_NA, :], preferred_element_type=F32)
    y = y + jnp.dot(hy.astype(BF16), w_ref[D_GMLP + D_NA:, :], preferred_element_type=F32)
    o_ref[...] = x_ref[...] + mod_ref[2:3, :] * y


def _out_proj(x_all, gm, na, hy_lat, hy_ctx, w_bf, mods_l, n_batch, seq, n_tiles):
    d = x_all.shape[1]
    n_lat_tiles = n_batch * seq // TOKEN_TILE
    n_ctx_tiles = hy_ctx.shape[0] // TOKEN_TILE
    row = lambda w: pl.BlockSpec((TOKEN_TILE, w), lambda i: (i, 0))
    return pl.pallas_call(
        functools.partial(_out_proj_kernel, n_lat_tiles=n_lat_tiles),
        grid=(n_tiles,),
        in_specs=[
            row(d), row(D_GMLP), row(D_NA),
            pl.BlockSpec((TOKEN_TILE, D_HYENA), lambda i: (jnp.minimum(i, n_lat_tiles - 1), 0)),
            pl.BlockSpec((TOKEN_TILE, D_HYENA),
                         lambda i: (jnp.clip(i - n_lat_tiles, 0, n_ctx_tiles - 1), 0)),
            pl.BlockSpec(w_bf.shape, lambda i: (0, 0)),
            pl.BlockSpec((None, N_MOD, d), _mod_index(seq // TOKEN_TILE, n_batch)),
        ],
        out_specs=row(d),
        out_shape=jax.ShapeDtypeStruct((n_tiles * TOKEN_TILE, d), F32),
        compiler_params=_cparams("arbitrary"),
        name="out_proj",
    )(x_all, gm, na, hy_lat, hy_ctx, w_bf, mods_l)


def _router_kernel(x_ref, g_ref, mod_ref, wr_ref, br_ref, tril_ref, h_ref, route_ref, count_ref, run_ref):
    @pl.when(pl.program_id(0) == 0)
    def _init():
        run_ref[...] = jnp.zeros_like(run_ref)

    mod = mod_ref[...]
    h = _rms_rows(x_ref[...]) * g_ref[...]
    h = h * (1.0 + mod[4:5]) + mod[3:4]
    hb = h.astype(BF16)
    hb32 = hb.astype(F32)
    h_lo = (h - hb32).astype(BF16)
    logits_all = (jnp.dot(hb, wr_ref[0], preferred_element_type=F32)
                  + jnp.dot(h_lo, wr_ref[0], preferred_element_type=F32)
                  + jnp.dot(hb, wr_ref[1], preferred_element_type=F32)) + br_ref[...]
    half = D_MODEL // 2
    bits = pltpu.bitcast(hb32, jnp.uint32)
    words = (bits[:, :half] >> 16) | (bits[:, half:] & jnp.uint32(0xFFFF0000))
    h_ref[:, :half] = pltpu.bitcast(words, jnp.int32)
    h_ref[:, half + LANES:] = jnp.zeros((TOKEN_TILE, half - LANES), jnp.int32)
    run = run_ref[...]
    for c in range(TOKEN_TILE // ROUTE_CHUNK):
        rows = slice(c * ROUTE_CHUNK, (c + 1) * ROUTE_CHUNK)
        route, run = _route_chunk(logits_all[rows], tril_ref[...], run)
        route_ref[rows, :] = route[:, :ROUTE_FIELDS]
        h_ref[rows, half:half + LANES] = pltpu.bitcast(route, jnp.int32)
    run_ref[...] = run
    count_ref[...] = run


def _route_chunk(logits, tril, run):
    lane = lax.broadcasted_iota(jnp.int32, logits.shape, 1)
    neg = -jnp.inf
    is_group = (lane >= MOE_EXPERTS) & (lane < MOE_EXPERTS + MOE_GROUPS)
    lg = jnp.where(is_group, logits, neg)
    mg = jnp.max(lg, axis=-1, keepdims=True)
    g_p = 1.0 / jnp.sum(jnp.exp(lg - mg), axis=-1, keepdims=True)
    g_idx = jnp.min(jnp.where(lg == mg, lane, 2 * LANES), axis=-1, keepdims=True) - MOE_EXPERTS
    in_group = (lane >= g_idx * MOE_EXPERTS_PER_GROUP) & (lane < (g_idx + 1) * MOE_EXPERTS_PER_GROUP)
    le = jnp.where(in_group, logits, neg)
    me = jnp.max(le, axis=-1, keepdims=True)
    pe = jnp.exp(le - me)
    pe = pe / jnp.sum(pe, axis=-1, keepdims=True)
    p1 = jnp.max(pe, axis=-1, keepdims=True)
    i1 = jnp.min(jnp.where(in_group & (pe == p1), lane, 2 * LANES), axis=-1, keepdims=True)
    pe2 = jnp.where(in_group & (lane != i1), pe, neg)
    p2 = jnp.max(pe2, axis=-1, keepdims=True)
    i2 = jnp.min(jnp.where(pe2 == p2, lane, 2 * LANES), axis=-1, keepdims=True)
    tot = p1 + p2
    w_lo = g_p * jnp.where(i1 < i2, p1, p2) / tot
    w_hi = g_p * jnp.where(i1 < i2, p2, p1) / tot
    a = jnp.minimum(i1, i2) - g_idx * MOE_EXPERTS_PER_GROUP
    b = jnp.maximum(i1, i2) - g_idx * MOE_EXPERTS_PER_GROUP
    pair = a * (MOE_EXPERTS_PER_GROUP - 1) - ((a * (a - 1)) >> 1) + (b - a - 1)
    bucket = g_idx * MOE_PAIRS + pair
    onehot = lane == bucket
    prefix = jnp.dot(tril, onehot.astype(BF16), preferred_element_type=F32)
    rank = jnp.sum(jnp.where(onehot, prefix + run, 0.0), axis=-1, keepdims=True) - 1.0
    run = run + prefix[ROUTE_CHUNK - 1:ROUTE_CHUNK, :]
    route = jnp.where(lane == 0, bucket.astype(F32),
                      jnp.where(lane == 1, rank,
                                jnp.where(lane == 2, w_lo, jnp.where(lane == 3, w_hi, 0.0))))
    return route, run


def _router(x_all, gain, mods_l, w_router, b_router, n_batch, seq, n_tiles):
    d = x_all.shape[1]
    tril = np.tril(np.ones((ROUTE_CHUNK, ROUTE_CHUNK), np.float32))
    w_hi = w_router.astype(BF16)
    w_split = jnp.stack([w_hi, (w_router - w_hi.astype(F32)).astype(BF16)])
    return pl.pallas_call(
        _router_kernel,
        grid=(n_tiles,),
        in_specs=[
            pl.BlockSpec((TOKEN_TILE, d), lambda i: (i, 0)),
            pl.BlockSpec((1, d), lambda i: (0, 0)),
            pl.BlockSpec((None, N_MOD, d), _mod_index(seq // TOKEN_TILE, n_batch)),
            pl.BlockSpec((2, d, LANES), lambda i: (0, 0, 0)),
            pl.BlockSpec((1, LANES), lambda i: (0, 0)),
            pl.BlockSpec((ROUTE_CHUNK, ROUTE_CHUNK), lambda i: (0, 0)),
        ],
        out_specs=[pl.BlockSpec((TOKEN_TILE, d), lambda i: (i, 0)),
                   pl.BlockSpec((TOKEN_TILE, ROUTE_FIELDS), lambda i: (i, 0)),
                   pl.BlockSpec((1, LANES), lambda i: (0, 0))],
        out_shape=[jax.ShapeDtypeStruct((n_tiles * TOKEN_TILE, d), jnp.int32),
                   jax.ShapeDtypeStruct((n_tiles * TOKEN_TILE, ROUTE_FIELDS), F32),
                   jax.ShapeDtypeStruct((1, LANES), F32)],
        scratch_shapes=[pltpu.VMEM((1, LANES), F32)],
        compiler_params=_cparams("arbitrary"),
        name="moe_router",
    )(x_all, gain.reshape(1, d), mods_l, w_split, b_router, jnp.asarray(tril, BF16))


def _row_copy(table_hbm, dst, sem, src_row, dst_row):
    return pltpu.make_async_copy(table_hbm.at[pl.ds(src_row, 1)], dst.at[pl.ds(dst_row, 1)], sem)


def _start_row_gather(idx_ref, base, table_hbm, dst, sem, unrolled):
    rows = dst.shape[0]
    if unrolled:
        for r in range(rows):
            _row_copy(table_hbm, dst, sem, idx_ref[base + r], r).start()
    else:
        def issue(r, carry):
            _row_copy(table_hbm, dst, sem, idx_ref[base + r], r).start()
            return carry

        lax.fori_loop(0, rows, issue, 0, unroll=GATHER_UNROLL)


def _wait_row_gather(table_hbm, dst, sem):
    def drain(r, carry):
        _row_copy(table_hbm, dst, sem, 0, r).wait()
        return carry

    lax.fori_loop(0, dst.shape[0], drain, 0, unroll=GATHER_UNROLL)


def _gather_residual_kernel(idx_ref, y_hbm, x_ref, mod_ref, o_ref, buf, sem):
    _start_row_gather(idx_ref, pl.program_id(0) * buf.shape[0], y_hbm, buf, sem, unrolled=True)
    _wait_row_gather(y_hbm, buf, sem)
    o_ref[...] = x_ref[...] + mod_ref[5:6, :] * buf[...]


def _gather_residual(x_all, y_sorted, dest, mods_l, n_batch, seq, n_tiles):
    d = x_all.shape[1]
    row = pl.BlockSpec((TOKEN_TILE, d), lambda i, idx: (i, 0))
    mod_index = _mod_index(seq // TOKEN_TILE, n_batch)
    return pl.pallas_call(
        _gather_residual_kernel,
        grid_spec=pltpu.PrefetchScalarGridSpec(
            num_scalar_prefetch=1,
            grid=(n_tiles,),
            in_specs=[pl.BlockSpec(memory_space=pl.ANY), row,
                      pl.BlockSpec((None, N_MOD, d), lambda i, idx: mod_index(i))],
            out_specs=row,
            scratch_shapes=[pltpu.VMEM((TOKEN_TILE, d), F32), pltpu.SemaphoreType.DMA],
        ),
        out_shape=jax.ShapeDtypeStruct((n_tiles * TOKEN_TILE, d), F32),
        compiler_params=_cparams("arbitrary"),
        name="moe_combine",
    )(dest, y_sorted, x_all, mods_l)


def _bucket_experts():
    grp, lo, hi = [], [], []
    for g in range(MOE_GROUPS):
        for a in range(MOE_EXPERTS_PER_GROUP):
            for b in range(a + 1, MOE_EXPERTS_PER_GROUP):
                grp.append(g)
                lo.append(a)
                hi.append(b)
    return np.asarray(grp, np.int32), np.asarray(lo, np.int32), np.asarray(hi, np.int32)


def _sorted_experts_kernel(src_ref, grp_ref, lo_ref, hi_ref, nact_ref, tab_hbm, wg_ref, wu_ref, wd_ref,
                           o_ref, xbuf, sems):
    del grp_ref
    t = pl.program_id(0)
    n_active = nact_ref[0]
    n_slots = MOE_GATHER_AHEAD + 1
    slot = t % n_slots
    ahead = (t + MOE_GATHER_AHEAD) % n_slots
    half = D_MODEL // 2

    @pl.when(t == 0)
    def _first_tiles():
        for k in range(MOE_GATHER_AHEAD):
            _start_row_gather(src_ref, k * MOE_SORT_TILE, tab_hbm, xbuf.at[k], sems.at[k], unrolled=False)

    @pl.when(t < n_active + MOE_GATHER_AHEAD)
    def _retire():
        _wait_row_gather(tab_hbm, xbuf.at[slot], sems.at[slot])

    @pl.when(t >= n_active)
    def _unused_tile():
        o_ref[...] = jnp.zeros_like(o_ref)

    @pl.when(t < n_active)
    def _active_tile():
        _start_row_gather(src_ref, (t + MOE_GATHER_AHEAD) * MOE_SORT_TILE, tab_hbm, xbuf.at[ahead],
                          sems.at[ahead], unrolled=True)
        words = pltpu.bitcast(xbuf[slot, :, :half], jnp.uint32)
        x_lo = pltpu.bitcast(words << 16, F32)
        x_hi = pltpu.bitcast(words & jnp.uint32(0xFFFF0000), F32)
        xb = jnp.concatenate([x_lo, x_hi], axis=-1).astype(BF16)
        r = pltpu.bitcast(xbuf[slot, :, half:half + LANES], F32)
        acc = None
        for e_ref, lane in ((lo_ref, 2), (hi_ref, 3)):
            e = e_ref[t]
            gate = jnp.dot(xb, wg_ref[e], preferred_element_type=F32)
            up = jnp.dot(xb, wu_ref[e], preferred_element_type=F32)
            act = _silu(gate) * up * r[:, lane:lane + 1]
            part = jnp.dot(act.astype(BF16), wd_ref[e], preferred_element_type=F32)
            acc = part if acc is None else acc + part
        o_ref[...] = acc


def _sorted_experts(table, src, tile_group, tile_lo, tile_hi, n_active, wg, wu, wd):
    n_rows = src.shape[0]
    group_w = lambda a: pl.BlockSpec((None,) + a.shape[1:], lambda t, src, grp, lo, hi, n: (grp[t], 0, 0, 0))
    return pl.pallas_call(
        _sorted_experts_kernel,
        grid_spec=pltpu.PrefetchScalarGridSpec(
            num_scalar_prefetch=5,
            grid=(n_rows // MOE_SORT_TILE,),
            in_specs=[pl.BlockSpec(memory_space=pl.ANY), group_w(wg), group_w(wu), group_w(wd)],
            out_specs=pl.BlockSpec((MOE_SORT_TILE, D_MODEL), lambda t, src, grp, lo, hi, n: (t, 0)),
            scratch_shapes=[pltpu.VMEM((MOE_GATHER_AHEAD + 1, MOE_SORT_TILE, table.shape[1]), table.dtype),
                            pltpu.SemaphoreType.DMA((MOE_GATHER_AHEAD + 1,))],
        ),
        out_shape=jax.ShapeDtypeStruct((n_rows, D_MODEL), F32),
        compiler_params=_cparams("arbitrary"),
        name="moe_experts",
    )(src, tile_group, tile_lo, tile_hi, n_active, table, wg, wu, wd)


def _moe(x_all, gain, mods_l, w_router, b_router, wg, wu, wd, n_batch, seq, n_tiles):
    n = n_tiles * TOKEN_TILE
    table, route, counts = _router(x_all, gain, mods_l, w_router, b_router, n_batch, seq, n_tiles)

    n_sorted_tiles = (n + MOE_BUCKETS * (MOE_SORT_TILE - 1)) // MOE_SORT_TILE + MOE_GATHER_AHEAD
    n_sorted = n_sorted_tiles * MOE_SORT_TILE
    counts = counts[0, :MOE_BUCKETS].astype(jnp.int32)
    bucket_tiles = (counts + MOE_SORT_TILE - 1) // MOE_SORT_TILE
    tile_end = jnp.cumsum(bucket_tiles)
    row_start = (tile_end - bucket_tiles) * MOE_SORT_TILE
    bucket = route[:, 0].astype(jnp.int32)
    rank = route[:, 1].astype(jnp.int32)
    dest = jnp.sum(jnp.where(bucket[:, None] == jnp.arange(MOE_BUCKETS)[None, :], row_start[None, :], 0),
                   axis=1) + rank
    src = (jnp.arange(n_sorted, dtype=jnp.int32) % n).at[dest].set(jnp.arange(n, dtype=jnp.int32))
    tiles = jnp.arange(n_sorted_tiles, dtype=jnp.int32)
    tile_bucket = jnp.sum(jnp.minimum(tiles, tile_end[-1] - 1)[:, None] >= tile_end[None, :], axis=1)
    tile_bucket = jnp.minimum(tile_bucket, MOE_BUCKETS - 1)
    grp_ids, lo_ids, hi_ids = _bucket_experts()
    tile_group = jnp.asarray(grp_ids)[tile_bucket]
    tile_lo = jnp.asarray(lo_ids)[tile_bucket]
    tile_hi = jnp.asarray(hi_ids)[tile_bucket]

    n_active = tile_end[-1:].astype(jnp.int32)
    y_sorted = _sorted_experts(table, src, tile_group, tile_lo, tile_hi, n_active, wg, wu, wd)
    return _gather_residual(x_all, y_sorted, dest, mods_l, n_batch, seq, n_tiles)


def kernel(x, c, ctx, c_ctx, w_ada, b_ada, g_mix, g_ffn, w_in, w_out, gmlp_v_gain, gmlp_ws, gmlp_bs,
           na_q_gain, na_k_gain, na_rpb, hy_short_w, hy_short_b, hy_w1, hy_b1, hy_w2, hy_b2, hy_w3,
           hy_freq, hy_bias, moe_w_rg, moe_b_rg, moe_w_re, moe_b_re, moe_w_gate, moe_w_up, moe_w_down):
    n_batch, seq, d = x.shape
    ctx_len = ctx.shape[1]
    depth = w_ada.shape[0]
    n_lat = n_batch * seq
    n_ctx = n_batch * ctx_len
    assert d == D_MODEL and seq % TOKEN_TILE == 0 and n_ctx % TOKEN_TILE == 0
    assert seq % GMLP_CHUNK == 0 and ctx_len % GMLP_CHUNK == 0

    pad_rows = -(n_batch + 1) % 8
    cs = jnp.concatenate([c, c_ctx[None, :], jnp.zeros((pad_rows, d), F32)], axis=0)
    mods = _modulation(cs, w_ada, b_ada)[:, :n_batch + 1].reshape(depth, n_batch + 1, N_MOD, d)

    x_all = jnp.concatenate([x.reshape(n_lat, d), ctx.reshape(n_ctx, d)], axis=0)
    n_all_tiles = (n_lat + n_ctx) // TOKEN_TILE
    n_lat_tiles = n_lat // TOKEN_TILE

    mats_lat = _dft_matrices(seq)
    mats_ctx = _dft_matrices(ctx_len)
    pos_lat = _hyena_positions(seq)
    pos_ctx = _hyena_positions(ctx_len)

    w_in_bf = w_in.astype(BF16)
    w_out_bf = w_out.astype(BF16)
    wg_bf = moe_w_gate.astype(BF16)
    wu_bf = moe_w_up.astype(BF16)
    wd_bf = moe_w_down.astype(BF16)
    lane_pad = LANES - MOE_EXPERTS - MOE_GROUPS
    w_router = jnp.pad(jnp.concatenate([moe_w_re, moe_w_rg], axis=-1), ((0, 0), (0, 0), (0, lane_pad)))
    b_router = jnp.pad(jnp.concatenate([moe_b_re, moe_b_rg], axis=-1), ((0, 0), (0, lane_pad)))[:, None, :]
    w1_pad = jnp.pad(hy_w1, ((0, 0), (0, LANES - HYENA_EMB), (0, 0)))

    for l in range(depth):
        last = l == depth - 1
        mods_l = mods[l]
        n_tiles = n_lat_tiles if last else n_all_tiles

        p_all = _in_proj(x_all, g_mix[l], mods_l, w_in_bf[l], n_batch, seq)

        filt = (w1_pad[l], hy_b1[l], hy_w2[l], hy_b2[l], hy_w3[l], hy_freq[l])
        kpack_lat = _filter_dft(mats_lat[0], mats_lat[1], _hyena_filters(*pos_lat, *filt))
        hy_lat = _hyena(p_all, hy_short_w[l], hy_short_b[l], hy_bias[l], kpack_lat, mats_lat,
                        n_batch, seq, 0)
        if last:
            hy_ctx = hy_lat
            gm = _gmlp(p_all, n_lat, gmlp_v_gain[l], gmlp_ws[l], gmlp_bs[l].T)
        else:
            kpack_ctx = _filter_dft(mats_ctx[0], mats_ctx[1], _hyena_filters(*pos_ctx, *filt))
            hy_ctx = _hyena(p_all, hy_short_w[l], hy_short_b[l], hy_bias[l], kpack_ctx, mats_ctx,
                            n_batch, ctx_len, n_lat // ctx_len)
            gm = _gmlp(p_all, n_lat + n_ctx, gmlp_v_gain[l], gmlp_ws[l], gmlp_bs[l].T)

        na = _attention(p_all, na_rpb[l], na_q_gain[l], na_k_gain[l],
                        n_batch, seq, ctx_len, not last)

        x_all = _out_proj(x_all, gm, na, hy_lat, hy_ctx, w_out_bf[l], mods_l, n_batch, seq, n_tiles)

        x_all = _moe(x_all, g_ffn[l], mods_l, w_router[l], b_router[l], wg_bf[l], wu_bf[l], wd_bf[l],
                     n_batch, seq, n_tiles)

    return x_all[:n_lat].reshape(n_batch, seq, d)
```

```python
import functools
import math

import numpy as np
import jax
import jax.numpy as jnp
from jax import lax
from jax.experimental import pallas as pl
from jax.experimental.pallas import tpu as pltpu

F32 = jnp.float32
BF16 = jnp.bfloat16
HI = lax.Precision.HIGHEST

D_MODEL = 1024
GRID_W = 64
D_GMLP = D_MODEL // 4
D_NA = D_MODEL // 2
D_HYENA = D_MODEL // 4
D_IN = 2 * D_GMLP + 3 * D_NA + 3 * D_HYENA
Q_START = 2 * D_GMLP
KV_START = 2 * D_GMLP + D_NA
HY_START = 2 * D_GMLP + 3 * D_NA
GMLP_GROUPS = 4
GMLP_GROUP_DIM = D_GMLP // GMLP_GROUPS
GMLP_CHUNK = 128
NA_HEAD_DIM = 64
NA_HEADS = D_NA // NA_HEAD_DIM
NA_SCALE = NA_HEAD_DIM ** -0.5
NA_WIN_ROWS = 8
NA_WIN_COLS = 16
HYENA_ORDER = 2
HYENA_POS_BANDS = 16
HYENA_EMB = 1 + 2 * HYENA_POS_BANDS
HYENA_FILTER_HIDDEN = 64
HYENA_DECAY_TARGET = 1e-2
HYENA_FAST_DECAY = 0.3
HYENA_SLOW_DECAY = 1.5
MOE_GROUPS = 4
MOE_EXPERTS_PER_GROUP = 8
MOE_EXPERTS = MOE_GROUPS * MOE_EXPERTS_PER_GROUP
MOE_HIDDEN = 256
MOE_PAIRS = MOE_EXPERTS_PER_GROUP * (MOE_EXPERTS_PER_GROUP - 1) // 2
MOE_BUCKETS = MOE_GROUPS * MOE_PAIRS
N_MOD = 6
RMS_EPS = 1e-6
LN_EPS = 1e-5

LANES = 128
TOKEN_TILE = 512
NA_Q_ROWS = 4
NA_Q_BLOCK = NA_Q_ROWS * GRID_W
NA_BAND_ROWS = NA_WIN_ROWS + NA_Q_ROWS
DFT_TILE = 512
CONV_TILE = 1024
ROUTE_CHUNK = 128
ROUTE_FIELDS = 8
MOE_SORT_TILE = 128
MOE_GATHER_AHEAD = 2
GATHER_UNROLL = 8
MASK_VALUE = -1e30
VMEM_LIMIT = 56 * 1024 * 1024


def _cparams(*sem):
    return pltpu.CompilerParams(dimension_semantics=sem, vmem_limit_bytes=VMEM_LIMIT)


def _silu(x):
    return x * jax.nn.sigmoid(x)


def _rms_rows(x):
    return x * lax.rsqrt(jnp.mean(x * x, axis=-1, keepdims=True) + RMS_EPS)


def _mods_kernel(cs_ref, w_ref, b_ref, o_ref):
    s = _silu(cs_ref[...])
    o_ref[...] = jnp.dot(s, w_ref[...], preferred_element_type=F32, precision=HI) + b_ref[...]


def _modulation(cs, w_ada, b_ada):
    depth, d, nd = w_ada.shape
    rows = cs.shape[0]
    col = 1024
    return pl.pallas_call(
        _mods_kernel,
        grid=(depth, nd // col),
        in_specs=[
            pl.BlockSpec((rows, d), lambda l, j: (0, 0)),
            pl.BlockSpec((None, d, col), lambda l, j: (l, 0, j)),
            pl.BlockSpec((None, 1, col), lambda l, j: (l, 0, j)),
        ],
        out_specs=pl.BlockSpec((None, rows, col), lambda l, j: (l, 0, j)),
        out_shape=jax.ShapeDtypeStruct((depth, rows, nd), F32),
        compiler_params=_cparams("arbitrary", "arbitrary"),
        name="modulation",
    )(cs, w_ada, b_ada.reshape(depth, 1, nd))


def _in_proj_kernel(x_ref, g_ref, mod_ref, w_ref, o_ref):
    mod = mod_ref[...]
    h = _rms_rows(x_ref[...]) * g_ref[...]
    h = h * (1.0 + mod[1:2]) + mod[0:1]
    o_ref[...] = jnp.dot(h.astype(BF16), w_ref[...], preferred_element_type=F32).astype(o_ref.dtype)


def _mod_index(tiles_per_batch, n_batch):
    return lambda i: (jnp.minimum(i // tiles_per_batch, n_batch), 0, 0)


def _in_proj(x_all, gain, mods_l, w_bf, n_batch, seq):
    n, d = x_all.shape
    d_in = w_bf.shape[1]
    return pl.pallas_call(
        _in_proj_kernel,
        grid=(n // TOKEN_TILE,),
        in_specs=[
            pl.BlockSpec((TOKEN_TILE, d), lambda i: (i, 0)),
            pl.BlockSpec((1, d), lambda i: (0, 0)),
            pl.BlockSpec((None, N_MOD, d), _mod_index(seq // TOKEN_TILE, n_batch)),
            pl.BlockSpec((d, d_in), lambda i: (0, 0)),
        ],
        out_specs=pl.BlockSpec((TOKEN_TILE, d_in), lambda i: (i, 0)),
        out_shape=jax.ShapeDtypeStruct((n, d_in), BF16),
        compiler_params=_cparams("arbitrary"),
        name="in_proj",
    )(x_all, gain.reshape(1, d), mods_l, w_bf)


def _group_avg_matrix(groups, width):
    return np.kron(np.eye(groups), np.full((width, width), 1.0 / width))


def _group_mean(t, avg):
    hi = t.astype(BF16)
    lo = (t - hi.astype(F32)).astype(BF16)
    return (jnp.dot(hi, avg, preferred_element_type=F32) + jnp.dot(lo, avg, preferred_element_type=F32))


def _gmlp_kernel(u_ref, v_ref, gain_ref, avg_ref, ws_ref, bs_ref, o_ref):
    avg = avg_ref[...]
    v = jax.nn.gelu(v_ref[...].astype(F32))
    v = v - _group_mean(v, avg)
    v = v * lax.rsqrt(_group_mean(v * v, avg) + LN_EPS) * gain_ref[...]
    vb = v.astype(BF16)
    bs = bs_ref[...]
    for c in range(u_ref.shape[0] // GMLP_CHUNK):
        rows = slice(c * GMLP_CHUNK, (c + 1) * GMLP_CHUNK)
        outs = []
        for g in range(GMLP_GROUPS):
            cols = slice(g * GMLP_GROUP_DIM, (g + 1) * GMLP_GROUP_DIM)
            s = jnp.dot(ws_ref[g], vb[rows, cols], preferred_element_type=F32) + bs[:, g:g + 1]
            outs.append(s)
        o_ref[rows, :] = jax.nn.gelu(u_ref[rows, :].astype(F32)) * jnp.concatenate(outs, axis=-1)


def _gmlp(p_all, n, v_gain, ws, bs_t):
    avg = _group_avg_matrix(GMLP_GROUPS, GMLP_GROUP_DIM)
    return pl.pallas_call(
        _gmlp_kernel,
        grid=(n // TOKEN_TILE,),
        in_specs=[
            pl.BlockSpec((TOKEN_TILE, D_GMLP), lambda i: (i, 0)),
            pl.BlockSpec((TOKEN_TILE, D_GMLP), lambda i: (i, 1)),
            pl.BlockSpec((1, D_GMLP), lambda i: (0, 0)),
            pl.BlockSpec((D_GMLP, D_GMLP), lambda i: (0, 0)),
            pl.BlockSpec((GMLP_GROUPS, GMLP_CHUNK, GMLP_CHUNK), lambda i: (0, 0, 0)),
            pl.BlockSpec((GMLP_CHUNK, GMLP_GROUPS), lambda i: (0, 0)),
        ],
        out_specs=pl.BlockSpec((TOKEN_TILE, D_GMLP), lambda i: (i, 0)),
        out_shape=jax.ShapeDtypeStruct((n, D_GMLP), F32),
        compiler_params=_cparams("arbitrary"),
        name="gmlp",
    )(p_all, p_all, v_gain.reshape(1, D_GMLP), jnp.asarray(avg, BF16), ws.astype(BF16), bs_t)


def _rpb_expand_kernel(rpb_ref, sel_ref, o_ref):
    o_ref[...] = jnp.dot(rpb_ref[...], sel_ref[...], preferred_element_type=F32, precision=HI)


def _na_geometry(grid_rows):
    variants, step_variant, band_start = [], [], []
    for r0 in range(0, grid_rows, NA_Q_ROWS):
        b0 = int(np.clip(r0 - NA_WIN_ROWS // 2, 0, grid_rows - NA_BAND_ROWS))
        geo = []
        for r in range(r0, r0 + NA_Q_ROWS):
            wr = int(np.clip(r - NA_WIN_ROWS // 2, 0, grid_rows - NA_WIN_ROWS))
            assert b0 <= wr and wr + NA_WIN_ROWS <= b0 + NA_BAND_ROWS
            geo.append((wr - b0, wr - r + NA_WIN_ROWS - 1))
        geo = tuple(geo)
        if geo not in variants:
            variants.append(geo)
        step_variant.append(variants.index(geo))
        band_start.append(b0)
    return variants, step_variant, band_start


def _na_bias_tables(rpb, variants):
    n_heads, n_dr, n_dc = rpb.shape
    qcol = np.arange(GRID_W)[:, None]
    kcol = np.arange(GRID_W)[None, :]
    win_c = np.clip(qcol - NA_WIN_COLS // 2, 0, GRID_W - NA_WIN_COLS)
    col_ok = (kcol >= win_c) & (kcol < win_c + NA_WIN_COLS)
    dc = np.clip(kcol - qcol + NA_WIN_COLS - 1, 0, 2 * NA_WIN_COLS - 2)
    dc_pad = -n_dc % 8
    sel = (np.arange(n_dc + dc_pad)[:, None] == dc.reshape(1, -1)).astype(np.float32)
    rpb2 = jnp.pad(rpb.reshape(n_heads * n_dr, n_dc), ((0, 0), (0, dc_pad)))
    toep = pl.pallas_call(
        _rpb_expand_kernel,
        out_shape=jax.ShapeDtypeStruct((n_heads * n_dr, GRID_W * GRID_W), F32),
        name="rpb_expand",
    )(rpb2, jnp.asarray(sel))
    toep = toep.reshape(n_heads, n_dr, GRID_W, GRID_W)
    return pl.pallas_call(
        functools.partial(_bias_table_kernel, variants=tuple(variants)),
        grid=(n_heads,),
        in_specs=[pl.BlockSpec((None, n_dr, GRID_W, GRID_W), lambda h: (h, 0, 0, 0)),
                  pl.BlockSpec((GRID_W, GRID_W), lambda h: (0, 0))],
        out_specs=pl.BlockSpec((None, len(variants), NA_Q_BLOCK, NA_BAND_ROWS * GRID_W),
                               lambda h: (h, 0, 0, 0)),
        out_shape=jax.ShapeDtypeStruct((n_heads, len(variants), NA_Q_BLOCK, NA_BAND_ROWS * GRID_W), F32),
        compiler_params=_cparams("arbitrary"),
        name="na_bias_table",
    )(toep, jnp.asarray(col_ok.astype(np.float32)))


def _bias_table_kernel(toep_ref, ok_ref, o_ref, *, variants):
    ok = ok_ref[...] > 0.0
    outside = jnp.full((GRID_W, GRID_W), MASK_VALUE, F32)
    blocks = [jnp.where(ok, toep_ref[dr], MASK_VALUE) for dr in range(toep_ref.shape[0])]
    for v, geo in enumerate(variants):
        for i, (a0, dr0) in enumerate(geo):
            pieces = [blocks[dr0 + a - a0] if a0 <= a < a0 + NA_WIN_ROWS else outside
                      for a in range(NA_BAND_ROWS)]
            o_ref[v, i * GRID_W:(i + 1) * GRID_W, :] = jnp.concatenate(pieces, axis=-1)


def _store_heads(dst, rows, t, gain, avg):
    if gain is not None:
        t = t.astype(F32)
        t = t * lax.rsqrt(_group_mean(t * t, avg) + RMS_EPS) * gain
    for h in range(NA_HEADS):
        dst[h, rows, :] = t[:, h * NA_HEAD_DIM:(h + 1) * NA_HEAD_DIM].astype(BF16)


def _na_kernel(q_ref, k_ref, v_ref, kc_ref, vc_ref, bias_ref, qg_ref, kg_ref, avg_ref, o_ref,
               kn_s, vb_s, kcn_s, vcb_s, qn_s, o_s, s_scr, p_scr, l_scr,
               *, n_lat_steps, step_variant, band_start):
    step = pl.program_id(1)
    kg = kg_ref[...]
    avg = avg_ref[...]
    nt = (((1,), (1,)), ((), ()))
    band_keys = NA_BAND_ROWS * GRID_W
    all_rows = slice(None)

    @pl.when(step == 0)
    def _prepare_keys():
        chunk = 256

        def body(c, carry):
            rows = pl.ds(pl.multiple_of(c * chunk, chunk), chunk)
            _store_heads(kn_s, rows, k_ref[rows, :], kg, avg)
            _store_heads(vb_s, rows, v_ref[rows, :], None, None)
            return carry

        lax.fori_loop(0, k_ref.shape[0] // chunk, body, 0)
        _store_heads(kcn_s, all_rows, kc_ref[...], kg, avg)
        _store_heads(vcb_s, all_rows, vc_ref[...], None, None)

    _store_heads(qn_s, all_rows, q_ref[...], qg_ref[...] * NA_SCALE, avg)

    def finish(h, scores, v_parts):
        m = jnp.max(scores, axis=-1, keepdims=True)
        p = jnp.exp(scores - m)
        denom = jnp.sum(p, axis=-1, keepdims=True)
        pb = p.astype(BF16)
        acc = None
        col = 0
        for v in v_parts:
            part = jnp.dot(pb[:, col:col + v.shape[0]], v, preferred_element_type=F32)
            acc = part if acc is None else acc + part
            col += v.shape[0]
        o_s[h] = acc / denom

    @pl.when(step < n_lat_steps)
    def _latent_queries():
        variant = jnp.int32(0)
        band0 = jnp.int32(0)
        for s_, (v_, b_) in enumerate(zip(step_variant, band_start)):
            variant = jnp.where(step == s_, v_, variant)
            band0 = jnp.where(step == s_, b_ * GRID_W, band0)
        krows = pl.ds(pl.multiple_of(band0, NA_Q_BLOCK), band_keys)

        def scores(h):
            qh = qn_s[h]
            s_scr[h % 2, :, :band_keys] = (
                lax.dot_general(qh, kn_s[h, krows, :], nt, preferred_element_type=F32) + bias_ref[h, variant])
            s_scr[h % 2, :, band_keys:] = lax.dot_general(qh, kcn_s[h], nt, preferred_element_type=F32)

        def softmax(h):
            s = s_scr[h % 2]
            p = jnp.exp(s - jnp.max(s, axis=-1, keepdims=True))
            l_scr[h % 2] = jnp.sum(p, axis=-1, keepdims=True)
            p_scr[h % 2] = p.astype(BF16)

        def values(h):
            acc = jnp.dot(p_scr[h % 2, :, :band_keys], vb_s[h, krows, :], preferred_element_type=F32)
            acc = acc + jnp.dot(p_scr[h % 2, :, band_keys:], vcb_s[h], preferred_element_type=F32)
            o_s[h] = acc / l_scr[h % 2]

        for stage in range(NA_HEADS + 2):
            if stage < NA_HEADS:
                scores(stage)
            if 1 <= stage <= NA_HEADS:
                softmax(stage - 1)
            if stage >= 2:
                values(stage - 2)

    @pl.when(step >= n_lat_steps)
    def _context_queries():
        def head_body(h, carry):
            s = lax.dot_general(qn_s[h], kcn_s[h], nt, preferred_element_type=F32)
            finish(h, s, (vcb_s[h],))
            return carry

        lax.fori_loop(0, NA_HEADS, head_body, 0)

    o_ref[...] = jnp.concatenate([o_s[h] for h in range(NA_HEADS)], axis=-1)


def _attention(p_all, rpb, q_gain, k_gain, n_batch, seq, ctx_len, with_ctx_queries):
    n = p_all.shape[0]
    assert ctx_len == NA_Q_BLOCK and seq % NA_Q_BLOCK == 0
    n_lat_steps = seq // NA_Q_BLOCK
    n_steps = n_lat_steps + (1 if with_ctx_queries else 0)
    ctx_block0 = n_batch * seq // ctx_len
    qcol, kcol, vcol = Q_START // D_NA, KV_START // D_NA, (KV_START + D_NA) // D_NA
    variants, step_variant, band_start = _na_geometry(seq // GRID_W)
    bias_tab = _na_bias_tables(rpb, variants)

    def q_index(col):
        return lambda b, s: (jnp.where(s < n_lat_steps, b * n_lat_steps + s, ctx_block0 + b), col)

    kern = functools.partial(_na_kernel, n_lat_steps=n_lat_steps, step_variant=tuple(step_variant),
                             band_start=tuple(band_start))
    head_major = lambda rows: pltpu.VMEM((NA_HEADS, rows, NA_HEAD_DIM), BF16)
    return pl.pallas_call(
        kern,
        grid=(n_batch, n_steps),
        in_specs=[
            pl.BlockSpec((NA_Q_BLOCK, D_NA), q_index(qcol)),
            pl.BlockSpec((seq, D_NA), lambda b, s: (b, kcol)),
            pl.BlockSpec((seq, D_NA), lambda b, s: (b, vcol)),
            pl.BlockSpec((ctx_len, D_NA), lambda b, s: (ctx_block0 + b, kcol)),
            pl.BlockSpec((ctx_len, D_NA), lambda b, s: (ctx_block0 + b, vcol)),
            pl.BlockSpec(bias_tab.shape, lambda b, s: (0, 0, 0, 0), pipeline_mode=pl.Buffered(1)),
            pl.BlockSpec((1, D_NA), lambda b, s: (0, 0)),
            pl.BlockSpec((1, D_NA), lambda b, s: (0, 0)),
            pl.BlockSpec((D_NA, D_NA), lambda b, s: (0, 0)),
        ],
        out_specs=pl.BlockSpec((NA_Q_BLOCK, D_NA), q_index(0)),
        out_shape=jax.ShapeDtypeStruct((n if with_ctx_queries else n_batch * seq, D_NA), F32),
        scratch_shapes=[
            head_major(seq), head_major(seq), head_major(ctx_len), head_major(ctx_len),
            head_major(NA_Q_BLOCK),
            pltpu.VMEM((NA_HEADS, NA_Q_BLOCK, NA_HEAD_DIM), F32),
            pltpu.VMEM((2, NA_Q_BLOCK, NA_BAND_ROWS * GRID_W + ctx_len), F32),
            pltpu.VMEM((2, NA_Q_BLOCK, NA_BAND_ROWS * GRID_W + ctx_len), BF16),
            pltpu.VMEM((2, NA_Q_BLOCK, 1), F32),
        ],
        compiler_params=_cparams("arbitrary", "arbitrary"),
        name="attention",
    )(p_all, p_all, p_all, p_all, p_all, bias_tab,
      jnp.tile(q_gain.reshape(1, NA_HEAD_DIM), (1, NA_HEADS)),
      jnp.tile(k_gain.reshape(1, NA_HEAD_DIM), (1, NA_HEADS)),
      jnp.asarray(_group_avg_matrix(NA_HEADS, NA_HEAD_DIM), BF16))


def _dft_matrices(length):
    idx = jnp.arange(length, dtype=jnp.int32)
    step = 64
    t_hi = jnp.arange(length // step, dtype=jnp.int32) * step
    t_lo = jnp.arange(step, dtype=jnp.int32)
    ang_hi = ((idx[:, None] * t_hi[None, :]) % (2 * length)).astype(F32) * (math.pi / length)
    ang_lo = ((idx[:, None] * t_lo[None, :]) % (2 * length)).astype(F32) * (math.pi / length)
    c_hi, s_hi = jnp.cos(ang_hi)[:, :, None], jnp.sin(ang_hi)[:, :, None]
    c_lo, s_lo = jnp.cos(ang_lo)[:, None, :], jnp.sin(ang_lo)[:, None, :]
    gc = (c_hi * c_lo - s_hi * s_lo).reshape(length, length)
    gs = (s_hi * c_lo + c_hi * s_lo).reshape(length, length)
    nyq = jnp.where(idx % 2 == 0, 1.0, -1.0).astype(F32)
    gs = jnp.where(idx[:, None] == 0, nyq[None, :], gs)
    return gc.astype(BF16), gs.astype(BF16), gs.T.astype(BF16)


def _hyena_positions(length):
    t = jnp.linspace(0.0, 1.0, length, dtype=F32)[:, None]
    w = 2.0 * math.pi * jnp.arange(length, dtype=F32)[:, None] / length
    f = jnp.linspace(1e-4, HYENA_POS_BANDS - 1, HYENA_POS_BANDS, dtype=F32)[None, :]
    z = jnp.concatenate([t, jnp.cos(f * w), -jnp.sin(f * w)], axis=-1)
    z = jnp.pad(z, ((0, 0), (0, LANES - HYENA_EMB)))
    min_decay = math.log(HYENA_DECAY_TARGET) / HYENA_SLOW_DECAY
    max_decay = math.log(HYENA_DECAY_TARGET) / HYENA_FAST_DECAY
    deltas = jnp.abs(jnp.linspace(min_decay, max_decay, D_HYENA, dtype=F32))[None, :]
    return z, jnp.exp(-t * deltas)


def _filter_kernel(z_ref, decay_ref, w1_ref, b1_ref, w2_ref, b2_ref, w3_ref, freq_ref, o_ref):
    freq = freq_ref[...]
    hdn = jnp.dot(z_ref[...], w1_ref[...], preferred_element_type=F32, precision=HI) + b1_ref[...]
    hdn = jnp.sin(freq[0:1] * hdn)
    hdn = jnp.dot(hdn, w2_ref[...], preferred_element_type=F32, precision=HI) + b2_ref[...]
    hdn = jnp.sin(freq[1:2] * hdn)
    h = jnp.dot(hdn, w3_ref[...], preferred_element_type=F32, precision=HI)
    decay = decay_ref[...]
    first_row = lax.broadcasted_iota(jnp.int32, decay.shape, 0) == 0
    outs = []
    for n in range(HYENA_ORDER):
        base = 2 * n * D_HYENA
        hf = h[:, base:base + D_HYENA] * decay
        hb = h[:, base + D_HYENA:base + 2 * D_HYENA] * decay
        norm = jnp.sum(jnp.abs(hf), axis=0, keepdims=True) + jnp.sum(jnp.abs(hb), axis=0, keepdims=True)
        outs.append(hf / norm)
        outs.append(jnp.where(first_row, 0.0, hb / norm))
    o_ref[...] = jnp.concatenate(outs, axis=-1)


def _hyena_filters(z, decay, w1p, b1, w2, b2, w3, freq):
    length = z.shape[0]
    full = lambda a: pl.BlockSpec(a.shape, lambda i: (0,) * a.ndim)
    args = (z, decay, w1p, b1.reshape(1, -1), w2, b2.reshape(1, -1), w3, freq)
    return pl.pallas_call(
        _filter_kernel,
        grid=(1,),
        in_specs=[full(a) for a in args],
        out_specs=pl.BlockSpec((length, 2 * HYENA_ORDER * D_HYENA), lambda i: (0, 0)),
        out_shape=jax.ShapeDtypeStruct((length, 2 * HYENA_ORDER * D_HYENA), F32),
        compiler_params=_cparams("arbitrary"),
        name="hyena_filter",
    )(*args)


def _filter_dft_kernel(gc_ref, gs_ref, h_ref, o_ref, *, length):
    hb = h_ref[...].astype(BF16)
    fa = jnp.dot(gc_ref[...], hb, preferred_element_type=F32)
    fb = jnp.dot(gs_ref[...], hb, preferred_element_type=F32)
    rows = lax.broadcasted_iota(jnp.int32, (fa.shape[0], D_HYENA), 0) + pl.program_id(0) * fa.shape[0]
    dc_row = rows == 0
    inv_n = 1.0 / (2 * length)
    outs = []
    for n in range(HYENA_ORDER):
        base = 2 * n * D_HYENA
        f_sl = slice(base, base + D_HYENA)
        b_sl = slice(base + D_HYENA, base + 2 * D_HYENA)
        kr = fa[:, f_sl] + fa[:, b_sl]
        ki = fb[:, b_sl] - fb[:, f_sl]
        k_nyq = fb[:, f_sl] + fb[:, b_sl]
        outs.append(jnp.where(dc_row, kr * inv_n, 2.0 * inv_n * kr))
        outs.append(jnp.where(dc_row, 0.0, 2.0 * inv_n * ki))
        outs.append(jnp.where(dc_row, 0.0, -2.0 * inv_n * ki))
        outs.append(jnp.where(dc_row, k_nyq * inv_n, 2.0 * inv_n * kr))
    o_ref[...] = jnp.concatenate(outs, axis=-1)


def _filter_dft(gc, gs, hfilt):
    length = gc.shape[0]
    tile = min(DFT_TILE, length)
    width = 4 * HYENA_ORDER * D_HYENA
    return pl.pallas_call(
        functools.partial(_filter_dft_kernel, length=length),
        grid=(length // tile,),
        in_specs=[
            pl.BlockSpec((tile, length), lambda j: (j, 0)),
            pl.BlockSpec((tile, length), lambda j: (j, 0)),
            pl.BlockSpec(hfilt.shape, lambda j: (0, 0)),
        ],
        out_specs=pl.BlockSpec((tile, width), lambda j: (j, 0)),
        out_shape=jax.ShapeDtypeStruct((length, width), F32),
        compiler_params=_cparams("arbitrary"),
        name="hyena_filter_dft",
    )(gc, gs, hfilt)


def _short_conv_kernel(a0_ref, a1_ref, a2_ref, w_ref, b_ref, o_ref):
    w = w_ref[...]
    b = b_ref[...]
    length = a0_ref.shape[0]
    rows = lax.broadcasted_iota(jnp.int32, (length, D_HYENA), 0)
    for j, a_ref in enumerate((a0_ref, a1_ref, a2_ref)):
        cols = slice(j * D_HYENA, (j + 1) * D_HYENA)
        a = a_ref[...].astype(F32)
        prev = jnp.where(rows == 0, 0.0, pltpu.roll(a, 1, 0))
        nxt = jnp.where(rows == length - 1, 0.0, pltpu.roll(a, length - 1, 0))
        o_ref[:, cols] = prev * w[0:1, cols] + a * w[1:2, cols] + nxt * w[2:3, cols] + b[:, cols]


def _short_conv(p_all, short_w, short_b, n_batch, length, row_block0):
    c0 = HY_START // D_HYENA
    spec = lambda j: pl.BlockSpec((length, D_HYENA), lambda b: (row_block0 + b, c0 + j))
    return pl.pallas_call(
        _short_conv_kernel,
        grid=(n_batch,),
        in_specs=[spec(0), spec(1), spec(2),
                  pl.BlockSpec((3, 3 * D_HYENA), lambda b: (0, 0)),
                  pl.BlockSpec((1, 3 * D_HYENA), lambda b: (0, 0))],
        out_specs=pl.BlockSpec((length, 3 * D_HYENA), lambda b: (b, 0)),
        out_shape=jax.ShapeDtypeStruct((n_batch * length, 3 * D_HYENA), F32),
        compiler_params=_cparams("arbitrary"),
        name="hyena_short_conv",
    )(p_all, p_all, p_all, short_w, short_b.reshape(1, -1))


def _conv_fwd_kernel(gc_ref, gs_ref, u_ref, k_ref, pa_ref, pb_ref):
    u = u_ref[...].astype(BF16)
    a = jnp.dot(gc_ref[...], u, preferred_element_type=F32)
    b = jnp.dot(gs_ref[...], u, preferred_element_type=F32)
    k = k_ref[...]
    c = D_HYENA
    pa_ref[...] = (a * k[:, 0:c] + b * k[:, c:2 * c]).astype(BF16)
    pb_ref[...] = (a * k[:, 2 * c:3 * c] + b * k[:, 3 * c:4 * c]).astype(BF16)


def _conv_fwd(gc, gs, u, u_col, kpack, order, n_batch):
    length = gc.shape[0]
    tile = min(CONV_TILE, length)
    nt = length // tile
    out = jax.ShapeDtypeStruct((n_batch * length, D_HYENA), BF16)
    return pl.pallas_call(
        _conv_fwd_kernel,
        grid=(nt, n_batch),
        in_specs=[
            pl.BlockSpec((tile, length), lambda j, b: (j, 0)),
            pl.BlockSpec((tile, length), lambda j, b: (j, 0)),
            pl.BlockSpec((length, D_HYENA), lambda j, b: (b, u_col)),
            pl.BlockSpec((tile, 4 * D_HYENA), lambda j, b: (j, order)),
        ],
        out_specs=[pl.BlockSpec((tile, D_HYENA), lambda j, b: (b * nt + j, 0))] * 2,
        out_shape=[out, out],
        compiler_params=_cparams("arbitrary", "arbitrary"),
        name="hyena_conv_fwd",
    )(gc, gs, u, kpack)


def _conv_inv_kernel(gc_ref, gst_ref, pa_ref, pb_ref, z_ref, gate_ref, d_ref, o_ref):
    y = jnp.dot(gc_ref[...], pa_ref[...], preferred_element_type=F32)
    y = y + jnp.dot(gst_ref[...], pb_ref[...], preferred_element_type=F32)
    o_ref[...] = gate_ref[...] * (y + d_ref[...] * z_ref[...])


def _conv_inv(gc, gst, pa, pb, z_prev, z_col, a3, gate_col, d_bias, n_batch):
    length = gc.shape[0]
    tile = min(CONV_TILE, length)
    nt = length // tile
    return pl.pallas_call(
        _conv_inv_kernel,
        grid=(nt, n_batch),
        in_specs=[
            pl.BlockSpec((tile, length), lambda j, b: (j, 0)),
            pl.BlockSpec((tile, length), lambda j, b: (j, 0)),
            pl.BlockSpec((length, D_HYENA), lambda j, b: (b, 0)),
            pl.BlockSpec((length, D_HYENA), lambda j, b: (b, 0)),
            pl.BlockSpec((tile, D_HYENA), lambda j, b: (b * nt + j, z_col)),
            pl.BlockSpec((tile, D_HYENA), lambda j, b: (b * nt + j, gate_col)),
            pl.BlockSpec((1, D_HYENA), lambda j, b: (0, 0)),
        ],
        out_specs=pl.BlockSpec((tile, D_HYENA), lambda j, b: (b * nt + j, 0)),
        out_shape=jax.ShapeDtypeStruct((n_batch * length, D_HYENA), F32),
        compiler_params=_cparams("arbitrary", "arbitrary"),
        name="hyena_conv_inv",
    )(gc, gst, pa, pb, z_prev, a3, d_bias.reshape(1, D_HYENA))


def _hyena(p_all, short_w, short_b, d_bias, kpack, mats, n_batch, length, row_block0):
    gc, gs, gst = mats
    a3 = _short_conv(p_all, short_w, short_b, n_batch, length, row_block0)
    pa, pb = _conv_fwd(gc, gs, a3, 0, kpack, 0, n_batch)
    z1 = _conv_inv(gc, gst, pa, pb, a3, 0, a3, 1, d_bias[0], n_batch)
    pa, pb = _conv_fwd(gc, gs, z1, 0, kpack, 1, n_batch)
    return _conv_inv(gc, gst, pa, pb, z1, 0, a3, 2, d_bias[1], n_batch)


def _out_proj_kernel(x_ref, gm_ref, na_ref, hyl_ref, hyc_ref, w_ref, mod_ref, o_ref, *, n_lat_tiles):
    is_lat = pl.program_id(0) < n_lat_tiles
    hy = jnp.where(is_lat, hyl_ref[...], hyc_ref[...])
    y = jnp.dot(gm_ref[...].astype(BF16), w_ref[0:D_GMLP, :], preferred_element_type=F32)
    y = y + jnp.dot(na_ref[...].astype(BF16), w_ref[D_GMLP:D_GMLP + D_NA, :], preferred_element_type=F32)
    y = y + jnp.dot(hy.astype(BF16), w_ref[D_GMLP + D_NA:, :], preferred_element_type=F32)
    o_ref[...] = x_ref[...] + mod_ref[2:3, :] * y


def _out_proj(x_all, gm, na, hy_lat, hy_ctx, w_bf, mods_l, n_batch, seq, n_tiles):
    d = x_all.shape[1]
    n_lat_tiles = n_batch * seq // TOKEN_TILE
    n_ctx_tiles = hy_ctx.shape[0] // TOKEN_TILE
    row = lambda w: pl.BlockSpec((TOKEN_TILE, w), lambda i: (i, 0))
    return pl.pallas_call(
        functools.partial(_out_proj_kernel, n_lat_tiles=n_lat_tiles),
        grid=(n_tiles,),
        in_specs=[
            row(d), row(D_GMLP), row(D_NA),
            pl.BlockSpec((TOKEN_TILE, D_HYENA), lambda i: (jnp.minimum(i, n_lat_tiles - 1), 0)),
            pl.BlockSpec((TOKEN_TILE, D_HYENA),
                         lambda i: (jnp.clip(i - n_lat_tiles, 0, n_ctx_tiles - 1), 0)),
            pl.BlockSpec(w_bf.shape, lambda i: (0, 0)),
            pl.BlockSpec((None, N_MOD, d), _mod_index(seq // TOKEN_TILE, n_batch)),
        ],
        out_specs=row(d),
        out_shape=jax.ShapeDtypeStruct((n_tiles * TOKEN_TILE, d), F32),
        compiler_params=_cparams("arbitrary"),
        name="out_proj",
    )(x_all, gm, na, hy_lat, hy_ctx, w_bf, mods_l)


def _router_kernel(x_ref, g_ref, mod_ref, wr_ref, br_ref, tril_ref, h_ref, route_ref, count_ref, run_ref):
    @pl.when(pl.program_id(0) == 0)
    def _init():
        run_ref[...] = jnp.zeros_like(run_ref)

    mod = mod_ref[...]
    h = _rms_rows(x_ref[...]) * g_ref[...]
    h = h * (1.0 + mod[4:5]) + mod[3:4]
    hb = h.astype(BF16)
    hb32 = hb.astype(F32)
    h_lo = (h - hb32).astype(BF16)
    logits_all = (jnp.dot(hb, wr_ref[0], preferred_element_type=F32)
                  + jnp.dot(h_lo, wr_ref[0], preferred_element_type=F32)
                  + jnp.dot(hb, wr_ref[1], preferred_element_type=F32)) + br_ref[...]
    half = D_MODEL // 2
    bits = pltpu.bitcast(hb32, jnp.uint32)
    words = (bits[:, :half] >> 16) | (bits[:, half:] & jnp.uint32(0xFFFF0000))
    h_ref[:, :half] = pltpu.bitcast(words, jnp.int32)
    h_ref[:, half + LANES:] = jnp.zeros((TOKEN_TILE, half - LANES), jnp.int32)
    run = run_ref[...]
    for c in range(TOKEN_TILE // ROUTE_CHUNK):
        rows = slice(c * ROUTE_CHUNK, (c + 1) * ROUTE_CHUNK)
        route, run = _route_chunk(logits_all[rows], tril_ref[...], run)
        route_ref[rows, :] = route[:, :ROUTE_FIELDS]
        h_ref[rows, half:half + LANES] = pltpu.bitcast(route, jnp.int32)
    run_ref[...] = run
    count_ref[...] = run


def _route_chunk(logits, tril, run):
    lane = lax.broadcasted_iota(jnp.int32, logits.shape, 1)
    neg = -jnp.inf
    is_group = (lane >= MOE_EXPERTS) & (lane < MOE_EXPERTS + MOE_GROUPS)
    lg = jnp.where(is_group, logits, neg)
    mg = jnp.max(lg, axis=-1, keepdims=True)
    g_p = 1.0 / jnp.sum(jnp.exp(lg - mg), axis=-1, keepdims=True)
    g_idx = jnp.min(jnp.where(lg == mg, lane, 2 * LANES), axis=-1, keepdims=True) - MOE_EXPERTS
    in_group = (lane >= g_idx * MOE_EXPERTS_PER_GROUP) & (lane < (g_idx + 1) * MOE_EXPERTS_PER_GROUP)
    le = jnp.where(in_group, logits, neg)
    me = jnp.max(le, axis=-1, keepdims=True)
    pe = jnp.exp(le - me)
    pe = pe / jnp.sum(pe, axis=-1, keepdims=True)
    p1 = jnp.max(pe, axis=-1, keepdims=True)
    i1 = jnp.min(jnp.where(in_group & (pe == p1), lane, 2 * LANES), axis=-1, keepdims=True)
    pe2 = jnp.where(in_group & (lane != i1), pe, neg)
    p2 = jnp.max(pe2, axis=-1, keepdims=True)
    i2 = jnp.min(jnp.where(pe2 == p2, lane, 2 * LANES), axis=-1, keepdims=True)
    tot = p1 + p2
    w_lo = g_p * jnp.where(i1 < i2, p1, p2) / tot
    w_hi = g_p * jnp.where(i1 < i2, p2, p1) / tot
    a = jnp.minimum(i1, i2) - g_idx * MOE_EXPERTS_PER_GROUP
    b = jnp.maximum(i1, i2) - g_idx * MOE_EXPERTS_PER_GROUP
    pair = a * (MOE_EXPERTS_PER_GROUP - 1) - ((a * (a - 1)) >> 1) + (b - a - 1)
    bucket = g_idx * MOE_PAIRS + pair
    onehot = lane == bucket
    prefix = jnp.dot(tril, onehot.astype(BF16), preferred_element_type=F32)
    rank = jnp.sum(jnp.where(onehot, prefix + run, 0.0), axis=-1, keepdims=True) - 1.0
    run = run + prefix[ROUTE_CHUNK - 1:ROUTE_CHUNK, :]
    route = jnp.where(lane == 0, bucket.astype(F32),
                      jnp.where(lane == 1, rank,
                                jnp.where(lane == 2, w_lo, jnp.where(lane == 3, w_hi, 0.0))))
    return route, run


def _router(x_all, gain, mods_l, w_router, b_router, n_batch, seq, n_tiles):
    d = x_all.shape[1]
    tril = np.tril(np.ones((ROUTE_CHUNK, ROUTE_CHUNK), np.float32))
    w_hi = w_router.astype(BF16)
    w_split = jnp.stack([w_hi, (w_router - w_hi.astype(F32)).astype(BF16)])
    return pl.pallas_call(
        _router_kernel,
        grid=(n_tiles,),
        in_specs=[
            pl.BlockSpec((TOKEN_TILE, d), lambda i: (i, 0)),
            pl.BlockSpec((1, d), lambda i: (0, 0)),
            pl.BlockSpec((None, N_MOD, d), _mod_index(seq // TOKEN_TILE, n_batch)),
            pl.BlockSpec((2, d, LANES), lambda i: (0, 0, 0)),
            pl.BlockSpec((1, LANES), lambda i: (0, 0)),
            pl.BlockSpec((ROUTE_CHUNK, ROUTE_CHUNK), lambda i: (0, 0)),
        ],
        out_specs=[pl.BlockSpec((TOKEN_TILE, d), lambda i: (i, 0)),
                   pl.BlockSpec((TOKEN_TILE, ROUTE_FIELDS), lambda i: (i, 0)),
                   pl.BlockSpec((1, LANES), lambda i: (0, 0))],
        out_shape=[jax.ShapeDtypeStruct((n_tiles * TOKEN_TILE, d), jnp.int32),
                   jax.ShapeDtypeStruct((n_tiles * TOKEN_TILE, ROUTE_FIELDS), F32),
                   jax.ShapeDtypeStruct((1, LANES), F32)],
        scratch_shapes=[pltpu.VMEM((1, LANES), F32)],
        compiler_params=_cparams("arbitrary"),
        name="moe_router",
    )(x_all, gain.reshape(1, d), mods_l, w_split, b_router, jnp.asarray(tril, BF16))


def _row_copy(table_hbm, dst, sem, src_row, dst_row):
    return pltpu.make_async_copy(table_hbm.at[pl.ds(src_row, 1)], dst.at[pl.ds(dst_row, 1)], sem)


def _start_row_gather(idx_ref, base, table_hbm, dst, sem, unrolled):
    rows = dst.shape[0]
    if unrolled:
        for r in range(rows):
            _row_copy(table_hbm, dst, sem, idx_ref[base + r], r).start()
    else:
        def issue(r, carry):
            _row_copy(table_hbm, dst, sem, idx_ref[base + r], r).start()
            return carry

        lax.fori_loop(0, rows, issue, 0, unroll=GATHER_UNROLL)


def _wait_row_gather(table_hbm, dst, sem):
    def drain(r, carry):
        _row_copy(table_hbm, dst, sem, 0, r).wait()
        return carry

    lax.fori_loop(0, dst.shape[0], drain, 0, unroll=GATHER_UNROLL)


def _gather_residual_kernel(idx_ref, y_hbm, x_ref, mod_ref, o_ref, buf, sem):
    _start_row_gather(idx_ref, pl.program_id(0) * buf.shape[0], y_hbm, buf, sem, unrolled=True)
    _wait_row_gather(y_hbm, buf, sem)
    o_ref[...] = x_ref[...] + mod_ref[5:6, :] * buf[...]


def _gather_residual(x_all, y_sorted, dest, mods_l, n_batch, seq, n_tiles):
    d = x_all.shape[1]
    row = pl.BlockSpec((TOKEN_TILE, d), lambda i, idx: (i, 0))
    mod_index = _mod_index(seq // TOKEN_TILE, n_batch)
    return pl.pallas_call(
        _gather_residual_kernel,
        grid_spec=pltpu.PrefetchScalarGridSpec(
            num_scalar_prefetch=1,
            grid=(n_tiles,),
            in_specs=[pl.BlockSpec(memory_space=pl.ANY), row,
                      pl.BlockSpec((None, N_MOD, d), lambda i, idx: mod_index(i))],
            out_specs=row,
            scratch_shapes=[pltpu.VMEM((TOKEN_TILE, d), F32), pltpu.SemaphoreType.DMA],
        ),
        out_shape=jax.ShapeDtypeStruct((n_tiles * TOKEN_TILE, d), F32),
        compiler_params=_cparams("arbitrary"),
        name="moe_combine",
    )(dest, y_sorted, x_all, mods_l)


def _bucket_experts():
    grp, lo, hi = [], [], []
    for g in range(MOE_GROUPS):
        for a in range(MOE_EXPERTS_PER_GROUP):
            for b in range(a + 1, MOE_EXPERTS_PER_GROUP):
                grp.append(g)
                lo.append(a)
                hi.append(b)
    return np.asarray(grp, np.int32), np.asarray(lo, np.int32), np.asarray(hi, np.int32)


def _sorted_experts_kernel(src_ref, grp_ref, lo_ref, hi_ref, nact_ref, tab_hbm, wg_ref, wu_ref, wd_ref,
                           o_ref, xbuf, sems):
    del grp_ref
    t = pl.program_id(0)
    n_active = nact_ref[0]
    n_slots = MOE_GATHER_AHEAD + 1
    slot = t % n_slots
    ahead = (t + MOE_GATHER_AHEAD) % n_slots
    half = D_MODEL // 2

    @pl.when(t == 0)
    def _first_tiles():
        for k in range(MOE_GATHER_AHEAD):
            _start_row_gather(src_ref, k * MOE_SORT_TILE, tab_hbm, xbuf.at[k], sems.at[k], unrolled=False)

    @pl.when(t < n_active + MOE_GATHER_AHEAD)
    def _retire():
        _wait_row_gather(tab_hbm, xbuf.at[slot], sems.at[slot])

    @pl.when(t >= n_active)
    def _unused_tile():
        o_ref[...] = jnp.zeros_like(o_ref)

    @pl.when(t < n_active)
    def _active_tile():
        _start_row_gather(src_ref, (t + MOE_GATHER_AHEAD) * MOE_SORT_TILE, tab_hbm, xbuf.at[ahead],
                          sems.at[ahead], unrolled=True)
        words = pltpu.bitcast(xbuf[slot, :, :half], jnp.uint32)
        x_lo = pltpu.bitcast(words << 16, F32)
        x_hi = pltpu.bitcast(words & jnp.uint32(0xFFFF0000), F32)
        xb = jnp.concatenate([x_lo, x_hi], axis=-1).astype(BF16)
        r = pltpu.bitcast(xbuf[slot, :, half:half + LANES], F32)
        acc = None
        for e_ref, lane in ((lo_ref, 2), (hi_ref, 3)):
            e = e_ref[t]
            gate = jnp.dot(xb, wg_ref[e], preferred_element_type=F32)
            up = jnp.dot(xb, wu_ref[e], preferred_element_type=F32)
            act = _silu(gate) * up * r[:, lane:lane + 1]
            part = jnp.dot(act.astype(BF16), wd_ref[e], preferred_element_type=F32)
            acc = part if acc is None else acc + part
        o_ref[...] = acc


def _sorted_experts(table, src, tile_group, tile_lo, tile_hi, n_active, wg, wu, wd):
    n_rows = src.shape[0]
    group_w = lambda a: pl.BlockSpec((None,) + a.shape[1:], lambda t, src, grp, lo, hi, n: (grp[t], 0, 0, 0))
    return pl.pallas_call(
        _sorted_experts_kernel,
        grid_spec=pltpu.PrefetchScalarGridSpec(
            num_scalar_prefetch=5,
            grid=(n_rows // MOE_SORT_TILE,),
            in_specs=[pl.BlockSpec(memory_space=pl.ANY), group_w(wg), group_w(wu), group_w(wd)],
            out_specs=pl.BlockSpec((MOE_SORT_TILE, D_MODEL), lambda t, src, grp, lo, hi, n: (t, 0)),
            scratch_shapes=[pltpu.VMEM((MOE_GATHER_AHEAD + 1, MOE_SORT_TILE, table.shape[1]), table.dtype),
                            pltpu.SemaphoreType.DMA((MOE_GATHER_AHEAD + 1,))],
        ),
        out_shape=jax.ShapeDtypeStruct((n_rows, D_MODEL), F32),
        compiler_params=_cparams("arbitrary"),
        name="moe_experts",
    )(src, tile_group, tile_lo, tile_hi, n_active, table, wg, wu, wd)


def _moe(x_all, gain, mods_l, w_router, b_router, wg, wu, wd, n_batch, seq, n_tiles):
    n = n_tiles * TOKEN_TILE
    table, route, counts = _router(x_all, gain, mods_l, w_router, b_router, n_batch, seq, n_tiles)

    n_sorted_tiles = (n + MOE_BUCKETS * (MOE_SORT_TILE - 1)) // MOE_SORT_TILE + MOE_GATHER_AHEAD
    n_sorted = n_sorted_tiles * MOE_SORT_TILE
    counts = counts[0, :MOE_BUCKETS].astype(jnp.int32)
    bucket_tiles = (counts + MOE_SORT_TILE - 1) // MOE_SORT_TILE
    tile_end = jnp.cumsum(bucket_tiles)
    row_start = (tile_end - bucket_tiles) * MOE_SORT_TILE
    bucket = route[:, 0].astype(jnp.int32)
    rank = route[:, 1].astype(jnp.int32)
    dest = jnp.sum(jnp.where(bucket[:, None] == jnp.arange(MOE_BUCKETS)[None, :], row_start[None, :], 0),
                   axis=1) + rank
    src = (jnp.arange(n_sorted, dtype=jnp.int32) % n).at[dest].set(jnp.arange(n, dtype=jnp.int32))
    tiles = jnp.arange(n_sorted_tiles, dtype=jnp.int32)
    tile_bucket = jnp.sum(jnp.minimum(tiles, tile_end[-1] - 1)[:, None] >= tile_end[None, :], axis=1)
    tile_bucket = jnp.minimum(tile_bucket, MOE_BUCKETS - 1)
    grp_ids, lo_ids, hi_ids = _bucket_experts()
    tile_group = jnp.asarray(grp_ids)[tile_bucket]
    tile_lo = jnp.asarray(lo_ids)[tile_bucket]
    tile_hi = jnp.asarray(hi_ids)[tile_bucket]

    n_active = tile_end[-1:].astype(jnp.int32)
    y_sorted = _sorted_experts(table, src, tile_group, tile_lo, tile_hi, n_active, wg, wu, wd)
    return _gather_residual(x_all, y_sorted, dest, mods_l, n_batch, seq, n_tiles)


def kernel(x, c, ctx, c_ctx, w_ada, b_ada, g_mix, g_ffn, w_in, w_out, gmlp_v_gain, gmlp_ws, gmlp_bs,
           na_q_gain, na_k_gain, na_rpb, hy_short_w, hy_short_b, hy_w1, hy_b1, hy_w2, hy_b2, hy_w3,
           hy_freq, hy_bias, moe_w_rg, moe_b_rg, moe_w_re, moe_b_re, moe_w_gate, moe_w_up, moe_w_down):
    n_batch, seq, d = x.shape
    ctx_len = ctx.shape[1]
    depth = w_ada.shape[0]
    n_lat = n_batch * seq
    n_ctx = n_batch * ctx_len
    assert d == D_MODEL and seq % TOKEN_TILE == 0 and n_ctx % TOKEN_TILE == 0
    assert seq % GMLP_CHUNK == 0 and ctx_len % GMLP_CHUNK == 0

    pad_rows = -(n_batch + 1) % 8
    cs = jnp.concatenate([c, c_ctx[None, :], jnp.zeros((pad_rows, d), F32)], axis=0)
    mods = _modulation(cs, w_ada, b_ada)[:, :n_batch + 1].reshape(depth, n_batch + 1, N_MOD, d)

    x_all = jnp.concatenate([x.reshape(n_lat, d), ctx.reshape(n_ctx, d)], axis=0)
    n_all_tiles = (n_lat + n_ctx) // TOKEN_TILE
    n_lat_tiles = n_lat // TOKEN_TILE

    mats_lat = _dft_matrices(seq)
    mats_ctx = _dft_matrices(ctx_len)
    pos_lat = _hyena_positions(seq)
    pos_ctx = _hyena_positions(ctx_len)

    w_in_bf = w_in.astype(BF16)
    w_out_bf = w_out.astype(BF16)
    wg_bf = moe_w_gate.astype(BF16)
    wu_bf = moe_w_up.astype(BF16)
    wd_bf = moe_w_down.astype(BF16)
    lane_pad = LANES - MOE_EXPERTS - MOE_GROUPS
    w_router = jnp.pad(jnp.concatenate([moe_w_re, moe_w_rg], axis=-1), ((0, 0), (0, 0), (0, lane_pad)))
    b_router = jnp.pad(jnp.concatenate([moe_b_re, moe_b_rg], axis=-1), ((0, 0), (0, lane_pad)))[:, None, :]
    w1_pad = jnp.pad(hy_w1, ((0, 0), (0, LANES - HYENA_EMB), (0, 0)))

    for l in range(depth):
        last = l == depth - 1
        mods_l = mods[l]
        n_tiles = n_lat_tiles if last else n_all_tiles

        p_all = _in_proj(x_all, g_mix[l], mods_l, w_in_bf[l], n_batch, seq)

        filt = (w1_pad[l], hy_b1[l], hy_w2[l], hy_b2[l], hy_w3[l], hy_freq[l])
        kpack_lat = _filter_dft(mats_lat[0], mats_lat[1], _hyena_filters(*pos_lat, *filt))
        hy_lat = _hyena(p_all, hy_short_w[l], hy_short_b[l], hy_bias[l], kpack_lat, mats_lat,
                        n_batch, seq, 0)
        if last:
            hy_ctx = hy_lat
            gm = _gmlp(p_all, n_lat, gmlp_v_gain[l], gmlp_ws[l], gmlp_bs[l].T)
        else:
            kpack_ctx = _filter_dft(mats_ctx[0], mats_ctx[1], _hyena_filters(*pos_ctx, *filt))
            hy_ctx = _hyena(p_all, hy_short_w[l], hy_short_b[l], hy_bias[l], kpack_ctx, mats_ctx,
                            n_batch, ctx_len, n_lat // ctx_len)
            gm = _gmlp(p_all, n_lat + n_ctx, gmlp_v_gain[l], gmlp_ws[l], gmlp_bs[l].T)

        na = _attention(p_all, na_rpb[l], na_q_gain[l], na_k_gain[l],
                        n_batch, seq, ctx_len, not last)

        x_all = _out_proj(x_all, gm, na, hy_lat, hy_ctx, w_out_bf[l], mods_l, n_batch, seq, n_tiles)

        x_all = _moe(x_all, g_ffn[l], mods_l, w_router[l], b_router[l], wg_bf[l], wu_bf[l], wd_bf[l],
                     n_batch, seq, n_tiles)

    return x_all[:n_lat].reshape(n_batch, seq, d)
```

```python
import functools
import math

import numpy as np
import jax
import jax.numpy as jnp
from jax import lax
from jax.experimental import pallas as pl
from jax.experimental.pallas import tpu as pltpu

F32 = jnp.float32
BF16 = jnp.bfloat16
HI = lax.Precision.HIGHEST

D_MODEL = 1024
GRID_W = 64
D_GMLP = D_MODEL // 4
D_NA = D_MODEL // 2
D_HYENA = D_MODEL // 4
D_IN = 2 * D_GMLP + 3 * D_NA + 3 * D_HYENA
Q_START = 2 * D_GMLP
KV_START = 2 * D_GMLP + D_NA
HY_START = 2 * D_GMLP + 3 * D_NA
GMLP_GROUPS = 4
GMLP_GROUP_DIM = D_GMLP // GMLP_GROUPS
GMLP_CHUNK = 128
NA_HEAD_DIM = 64
NA_HEADS = D_NA // NA_HEAD_DIM
NA_SCALE = NA_HEAD_DIM ** -0.5
NA_WIN_ROWS = 8
NA_WIN_COLS = 16
HYENA_ORDER = 2
HYENA_POS_BANDS = 16
HYENA_EMB = 1 + 2 * HYENA_POS_BANDS
HYENA_FILTER_HIDDEN = 64
HYENA_DECAY_TARGET = 1e-2
HYENA_FAST_DECAY = 0.3
HYENA_SLOW_DECAY = 1.5
MOE_GROUPS = 4
MOE_EXPERTS_PER_GROUP = 8
MOE_EXPERTS = MOE_GROUPS * MOE_EXPERTS_PER_GROUP
MOE_HIDDEN = 256
MOE_PAIRS = MOE_EXPERTS_PER_GROUP * (MOE_EXPERTS_PER_GROUP - 1) // 2
MOE_BUCKETS = MOE_GROUPS * MOE_PAIRS
N_MOD = 6
RMS_EPS = 1e-6
LN_EPS = 1e-5

LANES = 128
TOKEN_TILE = 512
NA_Q_ROWS = 4
NA_Q_BLOCK = NA_Q_ROWS * GRID_W
NA_BAND_ROWS = NA_WIN_ROWS + NA_Q_ROWS
DFT_TILE = 512
CONV_TILE = 1024
ROUTE_CHUNK = 128
MOE_SORT_TILE = 256
MOE_GATHER_AHEAD = 1
GATHER_UNROLL = 8
MASK_VALUE = -1e30
VMEM_LIMIT = 56 * 1024 * 1024


def _cparams(*sem):
    return pltpu.CompilerParams(dimension_semantics=sem, vmem_limit_bytes=VMEM_LIMIT)


def _silu(x):
    return x * jax.nn.sigmoid(x)


def _rms_rows(x):
    return x * lax.rsqrt(jnp.mean(x * x, axis=-1, keepdims=True) + RMS_EPS)


def _mods_kernel(cs_ref, w_ref, b_ref, o_ref):
    s = _silu(cs_ref[...])
    o_ref[...] = jnp.dot(s, w_ref[...], preferred_element_type=F32, precision=HI) + b_ref[...]


def _modulation(cs, w_ada, b_ada):
    depth, d, nd = w_ada.shape
    rows = cs.shape[0]
    col = 1024
    return pl.pallas_call(
        _mods_kernel,
        grid=(depth, nd // col),
        in_specs=[
            pl.BlockSpec((rows, d), lambda l, j: (0, 0)),
            pl.BlockSpec((None, d, col), lambda l, j: (l, 0, j)),
            pl.BlockSpec((None, 1, col), lambda l, j: (l, 0, j)),
        ],
        out_specs=pl.BlockSpec((None, rows, col), lambda l, j: (l, 0, j)),
        out_shape=jax.ShapeDtypeStruct((depth, rows, nd), F32),
        compiler_params=_cparams("arbitrary", "arbitrary"),
        name="modulation",
    )(cs, w_ada, b_ada.reshape(depth, 1, nd))


def _in_proj_kernel(x_ref, g_ref, mod_ref, w_ref, o_ref):
    mod = mod_ref[...]
    h = _rms_rows(x_ref[...]) * g_ref[...]
    h = h * (1.0 + mod[1:2]) + mod[0:1]
    o_ref[...] = jnp.dot(h.astype(BF16), w_ref[...], preferred_element_type=F32).astype(o_ref.dtype)


def _mod_index(tiles_per_batch, n_batch):
    return lambda i: (jnp.minimum(i // tiles_per_batch, n_batch), 0, 0)


def _in_proj(x_all, gain, mods_l, w_bf, n_batch, seq):
    n, d = x_all.shape
    d_in = w_bf.shape[1]
    return pl.pallas_call(
        _in_proj_kernel,
        grid=(n // TOKEN_TILE,),
        in_specs=[
            pl.BlockSpec((TOKEN_TILE, d), lambda i: (i, 0)),
            pl.BlockSpec((1, d), lambda i: (0, 0)),
            pl.BlockSpec((None, N_MOD, d), _mod_index(seq // TOKEN_TILE, n_batch)),
            pl.BlockSpec((d, d_in), lambda i: (0, 0)),
        ],
        out_specs=pl.BlockSpec((TOKEN_TILE, d_in), lambda i: (i, 0)),
        out_shape=jax.ShapeDtypeStruct((n, d_in), BF16),
        compiler_params=_cparams("arbitrary"),
        name="in_proj",
    )(x_all, gain.reshape(1, d), mods_l, w_bf)


def _group_avg_matrix(groups, width):
    return np.kron(np.eye(groups), np.full((width, width), 1.0 / width))


def _group_mean(t, avg):
    hi = t.astype(BF16)
    lo = (t - hi.astype(F32)).astype(BF16)
    return (jnp.dot(hi, avg, preferred_element_type=F32) + jnp.dot(lo, avg, preferred_element_type=F32))


def _gmlp_kernel(u_ref, v_ref, gain_ref, avg_ref, ws_ref, bs_ref, o_ref):
    avg = avg_ref[...]
    v = jax.nn.gelu(v_ref[...].astype(F32))
    v = v - _group_mean(v, avg)
    v = v * lax.rsqrt(_group_mean(v * v, avg) + LN_EPS) * gain_ref[...]
    vb = v.astype(BF16)
    bs = bs_ref[...]
    for c in range(u_ref.shape[0] // GMLP_CHUNK):
        rows = slice(c * GMLP_CHUNK, (c + 1) * GMLP_CHUNK)
        outs = []
        for g in range(GMLP_GROUPS):
            cols = slice(g * GMLP_GROUP_DIM, (g + 1) * GMLP_GROUP_DIM)
            s = jnp.dot(ws_ref[g], vb[rows, cols], preferred_element_type=F32) + bs[:, g:g + 1]
            outs.append(s)
        o_ref[rows, :] = jax.nn.gelu(u_ref[rows, :].astype(F32)) * jnp.concatenate(outs, axis=-1)


def _gmlp(p_all, n, v_gain, ws, bs_t):
    avg = _group_avg_matrix(GMLP_GROUPS, GMLP_GROUP_DIM)
    return pl.pallas_call(
        _gmlp_kernel,
        grid=(n // TOKEN_TILE,),
        in_specs=[
            pl.BlockSpec((TOKEN_TILE, D_GMLP), lambda i: (i, 0)),
            pl.BlockSpec((TOKEN_TILE, D_GMLP), lambda i: (i, 1)),
            pl.BlockSpec((1, D_GMLP), lambda i: (0, 0)),
            pl.BlockSpec((D_GMLP, D_GMLP), lambda i: (0, 0)),
            pl.BlockSpec((GMLP_GROUPS, GMLP_CHUNK, GMLP_CHUNK), lambda i: (0, 0, 0)),
            pl.BlockSpec((GMLP_CHUNK, GMLP_GROUPS), lambda i: (0, 0)),
        ],
        out_specs=pl.BlockSpec((TOKEN_TILE, D_GMLP), lambda i: (i, 0)),
        out_shape=jax.ShapeDtypeStruct((n, D_GMLP), F32),
        compiler_params=_cparams("arbitrary"),
        name="gmlp",
    )(p_all, p_all, v_gain.reshape(1, D_GMLP), jnp.asarray(avg, BF16), ws.astype(BF16), bs_t)


def _rpb_expand_kernel(rpb_ref, sel_ref, o_ref):
    o_ref[...] = jnp.dot(rpb_ref[...], sel_ref[...], preferred_element_type=F32, precision=HI)


def _na_geometry(grid_rows):
    variants, step_variant, band_start = [], [], []
    for r0 in range(0, grid_rows, NA_Q_ROWS):
        b0 = int(np.clip(r0 - NA_WIN_ROWS // 2, 0, grid_rows - NA_BAND_ROWS))
        geo = []
        for r in range(r0, r0 + NA_Q_ROWS):
            wr = int(np.clip(r - NA_WIN_ROWS // 2, 0, grid_rows - NA_WIN_ROWS))
            assert b0 <= wr and wr + NA_WIN_ROWS <= b0 + NA_BAND_ROWS
            geo.append((wr - b0, wr - r + NA_WIN_ROWS - 1))
        geo = tuple(geo)
        if geo not in variants:
            variants.append(geo)
        step_variant.append(variants.index(geo))
        band_start.append(b0)
    return variants, step_variant, band_start


def _na_bias_tables(rpb, variants):
    n_heads, n_dr, n_dc = rpb.shape
    qcol = np.arange(GRID_W)[:, None]
    kcol = np.arange(GRID_W)[None, :]
    win_c = np.clip(qcol - NA_WIN_COLS // 2, 0, GRID_W - NA_WIN_COLS)
    col_ok = (kcol >= win_c) & (kcol < win_c + NA_WIN_COLS)
    dc = np.clip(kcol - qcol + NA_WIN_COLS - 1, 0, 2 * NA_WIN_COLS - 2)
    dc_pad = -n_dc % 8
    sel = (np.arange(n_dc + dc_pad)[:, None] == dc.reshape(1, -1)).astype(np.float32)
    rpb2 = jnp.pad(rpb.reshape(n_heads * n_dr, n_dc), ((0, 0), (0, dc_pad)))
    toep = pl.pallas_call(
        _rpb_expand_kernel,
        out_shape=jax.ShapeDtypeStruct((n_heads * n_dr, GRID_W * GRID_W), F32),
        name="rpb_expand",
    )(rpb2, jnp.asarray(sel))
    toep = toep.reshape(n_heads, n_dr, GRID_W, GRID_W)
    return pl.pallas_call(
        functools.partial(_bias_table_kernel, variants=tuple(variants)),
        grid=(n_heads,),
        in_specs=[pl.BlockSpec((None, n_dr, GRID_W, GRID_W), lambda h: (h, 0, 0, 0)),
                  pl.BlockSpec((GRID_W, GRID_W), lambda h: (0, 0))],
        out_specs=pl.BlockSpec((None, len(variants), NA_Q_BLOCK, NA_BAND_ROWS * GRID_W),
                               lambda h: (h, 0, 0, 0)),
        out_shape=jax.ShapeDtypeStruct((n_heads, len(variants), NA_Q_BLOCK, NA_BAND_ROWS * GRID_W), F32),
        compiler_params=_cparams("arbitrary"),
        name="na_bias_table",
    )(toep, jnp.asarray(col_ok.astype(np.float32)))


def _bias_table_kernel(toep_ref, ok_ref, o_ref, *, variants):
    ok = ok_ref[...] > 0.0
    outside = jnp.full((GRID_W, GRID_W), MASK_VALUE, F32)
    blocks = [jnp.where(ok, toep_ref[dr], MASK_VALUE) for dr in range(toep_ref.shape[0])]
    for v, geo in enumerate(variants):
        for i, (a0, dr0) in enumerate(geo):
            pieces = [blocks[dr0 + a - a0] if a0 <= a < a0 + NA_WIN_ROWS else outside
                      for a in range(NA_BAND_ROWS)]
            o_ref[v, i * GRID_W:(i + 1) * GRID_W, :] = jnp.concatenate(pieces, axis=-1)


def _store_heads(dst, rows, t, gain, avg):
    if gain is not None:
        t = t.astype(F32)
        t = t * lax.rsqrt(_group_mean(t * t, avg) + RMS_EPS) * gain
    for h in range(NA_HEADS):
        dst[h, rows, :] = t[:, h * NA_HEAD_DIM:(h + 1) * NA_HEAD_DIM].astype(BF16)


def _na_kernel(q_ref, k_ref, v_ref, kc_ref, vc_ref, bias_ref, qg_ref, kg_ref, avg_ref, o_ref,
               kn_s, vb_s, kcn_s, vcb_s, qn_s, o_s, s_scr, p_scr, l_scr,
               *, n_lat_steps, step_variant, band_start):
    step = pl.program_id(1)
    kg = kg_ref[...]
    avg = avg_ref[...]
    nt = (((1,), (1,)), ((), ()))
    band_keys = NA_BAND_ROWS * GRID_W
    all_rows = slice(None)

    @pl.when(step == 0)
    def _prepare_keys():
        chunk = 256

        def body(c, carry):
            rows = pl.ds(pl.multiple_of(c * chunk, chunk), chunk)
            _store_heads(kn_s, rows, k_ref[rows, :], kg, avg)
            _store_heads(vb_s, rows, v_ref[rows, :], None, None)
            return carry

        lax.fori_loop(0, k_ref.shape[0] // chunk, body, 0)
        _store_heads(kcn_s, all_rows, kc_ref[...], kg, avg)
        _store_heads(vcb_s, all_rows, vc_ref[...], None, None)

    _store_heads(qn_s, all_rows, q_ref[...], qg_ref[...] * NA_SCALE, avg)

    def finish(h, scores, v_parts):
        m = jnp.max(scores, axis=-1, keepdims=True)
        p = jnp.exp(scores - m)
        denom = jnp.sum(p, axis=-1, keepdims=True)
        pb = p.astype(BF16)
        acc = None
        col = 0
        for v in v_parts:
            part = jnp.dot(pb[:, col:col + v.shape[0]], v, preferred_element_type=F32)
            acc = part if acc is None else acc + part
            col += v.shape[0]
        o_s[h] = acc / denom

    @pl.when(step < n_lat_steps)
    def _latent_queries():
        variant = jnp.int32(0)
        band0 = jnp.int32(0)
        for s_, (v_, b_) in enumerate(zip(step_variant, band_start)):
            variant = jnp.where(step == s_, v_, variant)
            band0 = jnp.where(step == s_, b_ * GRID_W, band0)
        krows = pl.ds(pl.multiple_of(band0, NA_Q_BLOCK), band_keys)

        def scores(h):
            qh = qn_s[h]
            s_scr[h % 2, :, :band_keys] = (
                lax.dot_general(qh, kn_s[h, krows, :], nt, preferred_element_type=F32) + bias_ref[h, variant])
            s_scr[h % 2, :, band_keys:] = lax.dot_general(qh, kcn_s[h], nt, preferred_element_type=F32)

        def softmax(h):
            s = s_scr[h % 2]
            p = jnp.exp(s - jnp.max(s, axis=-1, keepdims=True))
            l_scr[h % 2] = jnp.sum(p, axis=-1, keepdims=True)
            p_scr[h % 2] = p.astype(BF16)

        def values(h):
            acc = jnp.dot(p_scr[h % 2, :, :band_keys], vb_s[h, krows, :], preferred_element_type=F32)
            acc = acc + jnp.dot(p_scr[h % 2, :, band_keys:], vcb_s[h], preferred_element_type=F32)
            o_s[h] = acc / l_scr[h % 2]

        for stage in range(NA_HEADS + 2):
            if stage < NA_HEADS:
                scores(stage)
            if 1 <= stage <= NA_HEADS:
                softmax(stage - 1)
            if stage >= 2:
                values(stage - 2)

    @pl.when(step >= n_lat_steps)
    def _context_queries():
        def head_body(h, carry):
            s = lax.dot_general(qn_s[h], kcn_s[h], nt, preferred_element_type=F32)
            finish(h, s, (vcb_s[h],))
            return carry

        lax.fori_loop(0, NA_HEADS, head_body, 0)

    o_ref[...] = jnp.concatenate([o_s[h] for h in range(NA_HEADS)], axis=-1)


def _attention(p_all, rpb, q_gain, k_gain, n_batch, seq, ctx_len, with_ctx_queries):
    n = p_all.shape[0]
    assert ctx_len == NA_Q_BLOCK and seq % NA_Q_BLOCK == 0
    n_lat_steps = seq // NA_Q_BLOCK
    n_steps = n_lat_steps + (1 if with_ctx_queries else 0)
    ctx_block0 = n_batch * seq // ctx_len
    qcol, kcol, vcol = Q_START // D_NA, KV_START // D_NA, (KV_START + D_NA) // D_NA
    variants, step_variant, band_start = _na_geometry(seq // GRID_W)
    bias_tab = _na_bias_tables(rpb, variants)

    def q_index(col):
        return lambda b, s: (jnp.where(s < n_lat_steps, b * n_lat_steps + s, ctx_block0 + b), col)

    kern = functools.partial(_na_kernel, n_lat_steps=n_lat_steps, step_variant=tuple(step_variant),
                             band_start=tuple(band_start))
    head_major = lambda rows: pltpu.VMEM((NA_HEADS, rows, NA_HEAD_DIM), BF16)
    return pl.pallas_call(
        kern,
        grid=(n_batch, n_steps),
        in_specs=[
            pl.BlockSpec((NA_Q_BLOCK, D_NA), q_index(qcol)),
            pl.BlockSpec((seq, D_NA), lambda b, s: (b, kcol)),
            pl.BlockSpec((seq, D_NA), lambda b, s: (b, vcol)),
            pl.BlockSpec((ctx_len, D_NA), lambda b, s: (ctx_block0 + b, kcol)),
            pl.BlockSpec((ctx_len, D_NA), lambda b, s: (ctx_block0 + b, vcol)),
            pl.BlockSpec(bias_tab.shape, lambda b, s: (0, 0, 0, 0), pipeline_mode=pl.Buffered(1)),
            pl.BlockSpec((1, D_NA), lambda b, s: (0, 0)),
            pl.BlockSpec((1, D_NA), lambda b, s: (0, 0)),
            pl.BlockSpec((D_NA, D_NA), lambda b, s: (0, 0)),
        ],
        out_specs=pl.BlockSpec((NA_Q_BLOCK, D_NA), q_index(0)),
        out_shape=jax.ShapeDtypeStruct((n if with_ctx_queries else n_batch * seq, D_NA), F32),
        scratch_shapes=[
            head_major(seq), head_major(seq), head_major(ctx_len), head_major(ctx_len),
            head_major(NA_Q_BLOCK),
            pltpu.VMEM((NA_HEADS, NA_Q_BLOCK, NA_HEAD_DIM), F32),
            pltpu.VMEM((2, NA_Q_BLOCK, NA_BAND_ROWS * GRID_W + ctx_len), F32),
            pltpu.VMEM((2, NA_Q_BLOCK, NA_BAND_ROWS * GRID_W + ctx_len), BF16),
            pltpu.VMEM((2, NA_Q_BLOCK, 1), F32),
        ],
        compiler_params=_cparams("arbitrary", "arbitrary"),
        name="attention",
    )(p_all, p_all, p_all, p_all, p_all, bias_tab,
      jnp.tile(q_gain.reshape(1, NA_HEAD_DIM), (1, NA_HEADS)),
      jnp.tile(k_gain.reshape(1, NA_HEAD_DIM), (1, NA_HEADS)),
      jnp.asarray(_group_avg_matrix(NA_HEADS, NA_HEAD_DIM), BF16))


def _dft_matrices(length):
    idx = jnp.arange(length, dtype=jnp.int32)
    step = 64
    t_hi = jnp.arange(length // step, dtype=jnp.int32) * step
    t_lo = jnp.arange(step, dtype=jnp.int32)
    ang_hi = ((idx[:, None] * t_hi[None, :]) % (2 * length)).astype(F32) * (math.pi / length)
    ang_lo = ((idx[:, None] * t_lo[None, :]) % (2 * length)).astype(F32) * (math.pi / length)
    c_hi, s_hi = jnp.cos(ang_hi)[:, :, None], jnp.sin(ang_hi)[:, :, None]
    c_lo, s_lo = jnp.cos(ang_lo)[:, None, :], jnp.sin(ang_lo)[:, None, :]
    gc = (c_hi * c_lo - s_hi * s_lo).reshape(length, length)
    gs = (s_hi * c_lo + c_hi * s_lo).reshape(length, length)
    nyq = jnp.where(idx % 2 == 0, 1.0, -1.0).astype(F32)
    gs = jnp.where(idx[:, None] == 0, nyq[None, :], gs)
    return gc.astype(BF16), gs.astype(BF16), gs.T.astype(BF16)


def _hyena_positions(length):
    t = jnp.linspace(0.0, 1.0, length, dtype=F32)[:, None]
    w = 2.0 * math.pi * jnp.arange(length, dtype=F32)[:, None] / length
    f = jnp.linspace(1e-4, HYENA_POS_BANDS - 1, HYENA_POS_BANDS, dtype=F32)[None, :]
    z = jnp.concatenate([t, jnp.cos(f * w), -jnp.sin(f * w)], axis=-1)
    z = jnp.pad(z, ((0, 0), (0, LANES - HYENA_EMB)))
    min_decay = math.log(HYENA_DECAY_TARGET) / HYENA_SLOW_DECAY
    max_decay = math.log(HYENA_DECAY_TARGET) / HYENA_FAST_DECAY
    deltas = jnp.abs(jnp.linspace(min_decay, max_decay, D_HYENA, dtype=F32))[None, :]
    return z, jnp.exp(-t * deltas)


def _filter_kernel(z_ref, decay_ref, w1_ref, b1_ref, w2_ref, b2_ref, w3_ref, freq_ref, o_ref):
    freq = freq_ref[...]
    hdn = jnp.dot(z_ref[...], w1_ref[...], preferred_element_type=F32, precision=HI) + b1_ref[...]
    hdn = jnp.sin(freq[0:1] * hdn)
    hdn = jnp.dot(hdn, w2_ref[...], preferred_element_type=F32, precision=HI) + b2_ref[...]
    hdn = jnp.sin(freq[1:2] * hdn)
    h = jnp.dot(hdn, w3_ref[...], preferred_element_type=F32, precision=HI)
    decay = decay_ref[...]
    first_row = lax.broadcasted_iota(jnp.int32, decay.shape, 0) == 0
    outs = []
    for n in range(HYENA_ORDER):
        base = 2 * n * D_HYENA
        hf = h[:, base:base + D_HYENA] * decay
        hb = h[:, base + D_HYENA:base + 2 * D_HYENA] * decay
        norm = jnp.sum(jnp.abs(hf), axis=0, keepdims=True) + jnp.sum(jnp.abs(hb), axis=0, keepdims=True)
        outs.append(hf / norm)
        outs.append(jnp.where(first_row, 0.0, hb / norm))
    o_ref[...] = jnp.concatenate(outs, axis=-1)


def _hyena_filters(z, decay, w1p, b1, w2, b2, w3, freq):
    length = z.shape[0]
    full = lambda a: pl.BlockSpec(a.shape, lambda i: (0,) * a.ndim)
    args = (z, decay, w1p, b1.reshape(1, -1), w2, b2.reshape(1, -1), w3, freq)
    return pl.pallas_call(
        _filter_kernel,
        grid=(1,),
        in_specs=[full(a) for a in args],
        out_specs=pl.BlockSpec((length, 2 * HYENA_ORDER * D_HYENA), lambda i: (0, 0)),
        out_shape=jax.ShapeDtypeStruct((length, 2 * HYENA_ORDER * D_HYENA), F32),
        compiler_params=_cparams("arbitrary"),
        name="hyena_filter",
    )(*args)


def _filter_dft_kernel(gc_ref, gs_ref, h_ref, o_ref, *, length):
    hb = h_ref[...].astype(BF16)
    fa = jnp.dot(gc_ref[...], hb, preferred_element_type=F32)
    fb = jnp.dot(gs_ref[...], hb, preferred_element_type=F32)
    rows = lax.broadcasted_iota(jnp.int32, (fa.shape[0], D_HYENA), 0) + pl.program_id(0) * fa.shape[0]
    dc_row = rows == 0
    inv_n = 1.0 / (2 * length)
    outs = []
    for n in range(HYENA_ORDER):
        base = 2 * n * D_HYENA
        f_sl = slice(base, base + D_HYENA)
        b_sl = slice(base + D_HYENA, base + 2 * D_HYENA)
        kr = fa[:, f_sl] + fa[:, b_sl]
        ki = fb[:, b_sl] - fb[:, f_sl]
        k_nyq = fb[:, f_sl] + fb[:, b_sl]
        outs.append(jnp.where(dc_row, kr * inv_n, 2.0 * inv_n * kr))
        outs.append(jnp.where(dc_row, 0.0, 2.0 * inv_n * ki))
        outs.append(jnp.where(dc_row, 0.0, -2.0 * inv_n * ki))
        outs.append(jnp.where(dc_row, k_nyq * inv_n, 2.0 * inv_n * kr))
    o_ref[...] = jnp.concatenate(outs, axis=-1)


def _filter_dft(gc, gs, hfilt):
    length = gc.shape[0]
    tile = min(DFT_TILE, length)
    width = 4 * HYENA_ORDER * D_HYENA
    return pl.pallas_call(
        functools.partial(_filter_dft_kernel, length=length),
        grid=(length // tile,),
        in_specs=[
            pl.BlockSpec((tile, length), lambda j: (j, 0)),
            pl.BlockSpec((tile, length), lambda j: (j, 0)),
            pl.BlockSpec(hfilt.shape, lambda j: (0, 0)),
        ],
        out_specs=pl.BlockSpec((tile, width), lambda j: (j, 0)),
        out_shape=jax.ShapeDtypeStruct((length, width), F32),
        compiler_params=_cparams("arbitrary"),
        name="hyena_filter_dft",
    )(gc, gs, hfilt)


def _short_conv_kernel(a0_ref, a1_ref, a2_ref, w_ref, b_ref, o_ref):
    w = w_ref[...]
    b = b_ref[...]
    length = a0_ref.shape[0]
    rows = lax.broadcasted_iota(jnp.int32, (length, D_HYENA), 0)
    for j, a_ref in enumerate((a0_ref, a1_ref, a2_ref)):
        cols = slice(j * D_HYENA, (j + 1) * D_HYENA)
        a = a_ref[...].astype(F32)
        prev = jnp.where(rows == 0, 0.0, pltpu.roll(a, 1, 0))
        nxt = jnp.where(rows == length - 1, 0.0, pltpu.roll(a, length - 1, 0))
        o_ref[:, cols] = prev * w[0:1, cols] + a * w[1:2, cols] + nxt * w[2:3, cols] + b[:, cols]


def _short_conv(p_all, short_w, short_b, n_batch, length, row_block0):
    c0 = HY_START // D_HYENA
    spec = lambda j: pl.BlockSpec((length, D_HYENA), lambda b: (row_block0 + b, c0 + j))
    return pl.pallas_call(
        _short_conv_kernel,
        grid=(n_batch,),
        in_specs=[spec(0), spec(1), spec(2),
                  pl.BlockSpec((3, 3 * D_HYENA), lambda b: (0, 0)),
                  pl.BlockSpec((1, 3 * D_HYENA), lambda b: (0, 0))],
        out_specs=pl.BlockSpec((length, 3 * D_HYENA), lambda b: (b, 0)),
        out_shape=jax.ShapeDtypeStruct((n_batch * length, 3 * D_HYENA), F32),
        compiler_params=_cparams("arbitrary"),
        name="hyena_short_conv",
    )(p_all, p_all, p_all, short_w, short_b.reshape(1, -1))


def _conv_fwd_kernel(gc_ref, gs_ref, u_ref, k_ref, pa_ref, pb_ref):
    u = u_ref[...].astype(BF16)
    a = jnp.dot(gc_ref[...], u, preferred_element_type=F32)
    b = jnp.dot(gs_ref[...], u, preferred_element_type=F32)
    k = k_ref[...]
    c = D_HYENA
    pa_ref[...] = (a * k[:, 0:c] + b * k[:, c:2 * c]).astype(BF16)
    pb_ref[...] = (a * k[:, 2 * c:3 * c] + b * k[:, 3 * c:4 * c]).astype(BF16)


def _conv_fwd(gc, gs, u, u_col, kpack, order, n_batch):
    length = gc.shape[0]
    tile = min(CONV_TILE, length)
    nt = length // tile
    out = jax.ShapeDtypeStruct((n_batch * length, D_HYENA), BF16)
    return pl.pallas_call(
        _conv_fwd_kernel,
        grid=(nt, n_batch),
        in_specs=[
            pl.BlockSpec((tile, length), lambda j, b: (j, 0)),
            pl.BlockSpec((tile, length), lambda j, b: (j, 0)),
            pl.BlockSpec((length, D_HYENA), lambda j, b: (b, u_col)),
            pl.BlockSpec((tile, 4 * D_HYENA), lambda j, b: (j, order)),
        ],
        out_specs=[pl.BlockSpec((tile, D_HYENA), lambda j, b: (b * nt + j, 0))] * 2,
        out_shape=[out, out],
        compiler_params=_cparams("arbitrary", "arbitrary"),
        name="hyena_conv_fwd",
    )(gc, gs, u, kpack)


def _conv_inv_kernel(gc_ref, gst_ref, pa_ref, pb_ref, z_ref, gate_ref, d_ref, o_ref):
    y = jnp.dot(gc_ref[...], pa_ref[...], preferred_element_type=F32)
    y = y + jnp.dot(gst_ref[...], pb_ref[...], preferred_element_type=F32)
    o_ref[...] = gate_ref[...] * (y + d_ref[...] * z_ref[...])


def _conv_inv(gc, gst, pa, pb, z_prev, z_col, a3, gate_col, d_bias, n_batch):
    length = gc.shape[0]
    tile = min(CONV_TILE, length)
    nt = length // tile
    return pl.pallas_call(
        _conv_inv_kernel,
        grid=(nt, n_batch),
        in_specs=[
            pl.BlockSpec((tile, length), lambda j, b: (j, 0)),
            pl.BlockSpec((tile, length), lambda j, b: (j, 0)),
            pl.BlockSpec((length, D_HYENA), lambda j, b: (b, 0)),
            pl.BlockSpec((length, D_HYENA), lambda j, b: (b, 0)),
            pl.BlockSpec((tile, D_HYENA), lambda j, b: (b * nt + j, z_col)),
            pl.BlockSpec((tile, D_HYENA), lambda j, b: (b * nt + j, gate_col)),
            pl.BlockSpec((1, D_HYENA), lambda j, b: (0, 0)),
        ],
        out_specs=pl.BlockSpec((tile, D_HYENA), lambda j, b: (b * nt + j, 0)),
        out_shape=jax.ShapeDtypeStruct((n_batch * length, D_HYENA), F32),
        compiler_params=_cparams("arbitrary", "arbitrary"),
        name="hyena_conv_inv",
    )(gc, gst, pa, pb, z_prev, a3, d_bias.reshape(1, D_HYENA))


def _hyena(p_all, short_w, short_b, d_bias, kpack, mats, n_batch, length, row_block0):
    gc, gs, gst = mats
    a3 = _short_conv(p_all, short_w, short_b, n_batch, length, row_block0)
    pa, pb = _conv_fwd(gc, gs, a3, 0, kpack, 0, n_batch)
    z1 = _conv_inv(gc, gst, pa, pb, a3, 0, a3, 1, d_bias[0], n_batch)
    pa, pb = _conv_fwd(gc, gs, z1, 0, kpack, 1, n_batch)
    return _conv_inv(gc, gst, pa, pb, z1, 0, a3, 2, d_bias[1], n_batch)


def _out_proj_kernel(x_ref, gm_ref, na_ref, hyl_ref, hyc_ref, w_ref, mod_ref, o_ref, *, n_lat_tiles):
    is_lat = pl.program_id(0) < n_lat_tiles
    hy = jnp.where(is_lat, hyl_ref[...], hyc_ref[...])
    y = jnp.dot(gm_ref[...].astype(BF16), w_ref[0:D_GMLP, :], preferred_element_type=F32)
    y = y + jnp.dot(na_ref[...].astype(BF16), w_ref[D_GMLP:D_GMLP + D_NA, :], preferred_element_type=F32)
    y = y + jnp.dot(hy.astype(BF16), w_ref[D_GMLP + D_NA:, :], preferred_element_type=F32)
    o_ref[...] = x_ref[...] + mod_ref[2:3, :] * y


def _out_proj(x_all, gm, na, hy_lat, hy_ctx, w_bf, mods_l, n_batch, seq, n_tiles):
    d = x_all.shape[1]
    n_lat_tiles = n_batch * seq // TOKEN_TILE
    n_ctx_tiles = hy_ctx.shape[0] // TOKEN_TILE
    row = lambda w: pl.BlockSpec((TOKEN_TILE, w), lambda i: (i, 0))
    return pl.pallas_call(
        functools.partial(_out_proj_kernel, n_lat_tiles=n_lat_tiles),
        grid=(n_tiles,),
        in_specs=[
            row(d), row(D_GMLP), row(D_NA),
            pl.BlockSpec((TOKEN_TILE, D_HYENA), lambda i: (jnp.minimum(i, n_lat_tiles - 1), 0)),
            pl.BlockSpec((TOKEN_TILE, D_HYENA),
                         lambda i: (jnp.clip(i - n_lat_tiles, 0, n_ctx_tiles - 1), 0)),
            pl.BlockSpec(w_bf.shape, lambda i: (0, 0)),
            pl.BlockSpec((None, N_MOD, d), _mod_index(seq // TOKEN_TILE, n_batch)),
        ],
        out_specs=row(d),
        out_shape=jax.ShapeDtypeStruct((n_tiles * TOKEN_TILE, d), F32),
        compiler_params=_cparams("arbitrary"),
        name="out_proj",
    )(x_all, gm, na, hy_lat, hy_ctx, w_bf, mods_l)


def _router_kernel(x_ref, g_ref, mod_ref, wr_ref, br_ref, tril_ref, h_ref, route_ref, count_ref, run_ref):
    @pl.when(pl.program_id(0) == 0)
    def _init():
        run_ref[...] = jnp.zeros_like(run_ref)

    mod = mod_ref[...]
    h = _rms_rows(x_ref[...]) * g_ref[...]
    h = h * (1.0 + mod[4:5]) + mod[3:4]
    hb = h.astype(BF16)
    hb32 = hb.astype(F32)
    h_lo = (h - hb32).astype(BF16)
    logits_all = (jnp.dot(hb, wr_ref[0], preferred_element_type=F32)
                  + jnp.dot(h_lo, wr_ref[0], preferred_element_type=F32)
                  + jnp.dot(hb, wr_ref[1], preferred_element_type=F32)) + br_ref[...]
    half = D_MODEL // 2
    bits = pltpu.bitcast(hb32, jnp.uint32)
    words = (bits[:, :half] >> 16) | (bits[:, half:] & jnp.uint32(0xFFFF0000))
    h_ref[:, :half] = pltpu.bitcast(words, jnp.int32)
    h_ref[:, half + LANES:] = jnp.zeros((TOKEN_TILE, half - LANES), jnp.int32)
    run = run_ref[...]
    for c in range(TOKEN_TILE // ROUTE_CHUNK):
        rows = slice(c * ROUTE_CHUNK, (c + 1) * ROUTE_CHUNK)
        route, run = _route_chunk(logits_all[rows], tril_ref[...], run)
        route_ref[rows, :] = route
        h_ref[rows, half:half + LANES] = pltpu.bitcast(route, jnp.int32)
    run_ref[...] = run
    count_ref[...] = run


def _route_chunk(logits, tril, run):
    lane = lax.broadcasted_iota(jnp.int32, logits.shape, 1)
    neg = -jnp.inf
    is_group = (lane >= MOE_EXPERTS) & (lane < MOE_EXPERTS + MOE_GROUPS)
    lg = jnp.where(is_group, logits, neg)
    mg = jnp.max(lg, axis=-1, keepdims=True)
    g_p = 1.0 / jnp.sum(jnp.exp(lg - mg), axis=-1, keepdims=True)
    g_idx = jnp.min(jnp.where(lg == mg, lane, 2 * LANES), axis=-1, keepdims=True) - MOE_EXPERTS
    in_group = (lane >= g_idx * MOE_EXPERTS_PER_GROUP) & (lane < (g_idx + 1) * MOE_EXPERTS_PER_GROUP)
    le = jnp.where(in_group, logits, neg)
    me = jnp.max(le, axis=-1, keepdims=True)
    pe = jnp.exp(le - me)
    pe = pe / jnp.sum(pe, axis=-1, keepdims=True)
    p1 = jnp.max(pe, axis=-1, keepdims=True)
    i1 = jnp.min(jnp.where(in_group & (pe == p1), lane, 2 * LANES), axis=-1, keepdims=True)
    pe2 = jnp.where(in_group & (lane != i1), pe, neg)
    p2 = jnp.max(pe2, axis=-1, keepdims=True)
    i2 = jnp.min(jnp.where(pe2 == p2, lane, 2 * LANES), axis=-1, keepdims=True)
    tot = p1 + p2
    w_lo = g_p * jnp.where(i1 < i2, p1, p2) / tot
    w_hi = g_p * jnp.where(i1 < i2, p2, p1) / tot
    a = jnp.minimum(i1, i2) - g_idx * MOE_EXPERTS_PER_GROUP
    b = jnp.maximum(i1, i2) - g_idx * MOE_EXPERTS_PER_GROUP
    pair = a * (MOE_EXPERTS_PER_GROUP - 1) - ((a * (a - 1)) >> 1) + (b - a - 1)
    bucket = g_idx * MOE_PAIRS + pair
    onehot = lane == bucket
    prefix = jnp.dot(tril, onehot.astype(BF16), preferred_element_type=F32)
    rank = jnp.sum(jnp.where(onehot, prefix + run, 0.0), axis=-1, keepdims=True) - 1.0
    run = run + prefix[ROUTE_CHUNK - 1:ROUTE_CHUNK, :]
    route = jnp.where(lane == 0, bucket.astype(F32),
                      jnp.where(lane == 1, rank,
                                jnp.where(lane == 2, w_lo, jnp.where(lane == 3, w_hi, 0.0))))
    return route, run


def _router(x_all, gain, mods_l, w_router, b_router, n_batch, seq, n_tiles):
    d = x_all.shape[1]
    tril = np.tril(np.ones((ROUTE_CHUNK, ROUTE_CHUNK), np.float32))
    w_hi = w_router.astype(BF16)
    w_split = jnp.stack([w_hi, (w_router - w_hi.astype(F32)).astype(BF16)])
    return pl.pallas_call(
        _router_kernel,
        grid=(n_tiles,),
        in_specs=[
            pl.BlockSpec((TOKEN_TILE, d), lambda i: (i, 0)),
            pl.BlockSpec((1, d), lambda i: (0, 0)),
            pl.BlockSpec((None, N_MOD, d), _mod_index(seq // TOKEN_TILE, n_batch)),
            pl.BlockSpec((2, d, LANES), lambda i: (0, 0, 0)),
            pl.BlockSpec((1, LANES), lambda i: (0, 0)),
            pl.BlockSpec((ROUTE_CHUNK, ROUTE_CHUNK), lambda i: (0, 0)),
        ],
        out_specs=[pl.BlockSpec((TOKEN_TILE, d), lambda i: (i, 0)),
                   pl.BlockSpec((TOKEN_TILE, LANES), lambda i: (i, 0)),
                   pl.BlockSpec((1, LANES), lambda i: (0, 0))],
        out_shape=[jax.ShapeDtypeStruct((n_tiles * TOKEN_TILE, d), jnp.int32),
                   jax.ShapeDtypeStruct((n_tiles * TOKEN_TILE, LANES), F32),
                   jax.ShapeDtypeStruct((1, LANES), F32)],
        scratch_shapes=[pltpu.VMEM((1, LANES), F32)],
        compiler_params=_cparams("arbitrary"),
        name="moe_router",
    )(x_all, gain.reshape(1, d), mods_l, w_split, b_router, jnp.asarray(tril, BF16))


def _row_copy(table_hbm, dst, sem, src_row, dst_row):
    return pltpu.make_async_copy(table_hbm.at[pl.ds(src_row, 1)], dst.at[pl.ds(dst_row, 1)], sem)


def _start_row_gather(idx_ref, base, table_hbm, dst, sem, unrolled):
    rows = dst.shape[0]
    if unrolled:
        for r in range(rows):
            _row_copy(table_hbm, dst, sem, idx_ref[base + r], r).start()
    else:
        def issue(r, carry):
            _row_copy(table_hbm, dst, sem, idx_ref[base + r], r).start()
            return carry

        lax.fori_loop(0, rows, issue, 0, unroll=GATHER_UNROLL)


def _wait_row_gather(table_hbm, dst, sem):
    def drain(r, carry):
        _row_copy(table_hbm, dst, sem, 0, r).wait()
        return carry

    lax.fori_loop(0, dst.shape[0], drain, 0, unroll=GATHER_UNROLL)


def _gather_residual_kernel(idx_ref, y_hbm, x_ref, mod_ref, o_ref, buf, sem):
    _start_row_gather(idx_ref, pl.program_id(0) * buf.shape[0], y_hbm, buf, sem, unrolled=True)
    _wait_row_gather(y_hbm, buf, sem)
    o_ref[...] = x_ref[...] + mod_ref[5:6, :] * buf[...]


def _gather_residual(x_all, y_sorted, dest, mods_l, n_batch, seq, n_tiles):
    d = x_all.shape[1]
    row = pl.BlockSpec((TOKEN_TILE, d), lambda i, idx: (i, 0))
    mod_index = _mod_index(seq // TOKEN_TILE, n_batch)
    return pl.pallas_call(
        _gather_residual_kernel,
        grid_spec=pltpu.PrefetchScalarGridSpec(
            num_scalar_prefetch=1,
            grid=(n_tiles,),
            in_specs=[pl.BlockSpec(memory_space=pl.ANY), row,
                      pl.BlockSpec((None, N_MOD, d), lambda i, idx: mod_index(i))],
            out_specs=row,
            scratch_shapes=[pltpu.VMEM((TOKEN_TILE, d), F32), pltpu.SemaphoreType.DMA],
        ),
        out_shape=jax.ShapeDtypeStruct((n_tiles * TOKEN_TILE, d), F32),
        compiler_params=_cparams("arbitrary"),
        name="moe_combine",
    )(dest, y_sorted, x_all, mods_l)


def _bucket_experts():
    grp, lo, hi = [], [], []
    for g in range(MOE_GROUPS):
        for a in range(MOE_EXPERTS_PER_GROUP):
            for b in range(a + 1, MOE_EXPERTS_PER_GROUP):
                grp.append(g)
                lo.append(a)
                hi.append(b)
    return np.asarray(grp, np.int32), np.asarray(lo, np.int32), np.asarray(hi, np.int32)


def _sorted_experts_kernel(src_ref, grp_ref, lo_ref, hi_ref, nact_ref, tab_hbm, wg_ref, wu_ref, wd_ref,
                           o_ref, xbuf, sems):
    del grp_ref
    t = pl.program_id(0)
    n_active = nact_ref[0]
    n_slots = MOE_GATHER_AHEAD + 1
    slot = t % n_slots
    ahead = (t + MOE_GATHER_AHEAD) % n_slots
    half = D_MODEL // 2

    @pl.when(t == 0)
    def _first_tiles():
        for k in range(MOE_GATHER_AHEAD):
            _start_row_gather(src_ref, k * MOE_SORT_TILE, tab_hbm, xbuf.at[k], sems.at[k], unrolled=False)

    @pl.when(t < n_active + MOE_GATHER_AHEAD)
    def _retire():
        _wait_row_gather(tab_hbm, xbuf.at[slot], sems.at[slot])

    @pl.when(t >= n_active)
    def _unused_tile():
        o_ref[...] = jnp.zeros_like(o_ref)

    @pl.when(t < n_active)
    def _active_tile():
        _start_row_gather(src_ref, (t + MOE_GATHER_AHEAD) * MOE_SORT_TILE, tab_hbm, xbuf.at[ahead],
                          sems.at[ahead], unrolled=True)
        words = pltpu.bitcast(xbuf[slot, :, :half], jnp.uint32)
        x_lo = pltpu.bitcast(words << 16, F32)
        x_hi = pltpu.bitcast(words & jnp.uint32(0xFFFF0000), F32)
        xb = jnp.concatenate([x_lo, x_hi], axis=-1).astype(BF16)
        r = pltpu.bitcast(xbuf[slot, :, half:half + LANES], F32)
        acc = None
        for e_ref, lane in ((lo_ref, 2), (hi_ref, 3)):
            e = e_ref[t]
            gate = jnp.dot(xb, wg_ref[e], preferred_element_type=F32)
            up = jnp.dot(xb, wu_ref[e], preferred_element_type=F32)
            act = _silu(gate) * up * r[:, lane:lane + 1]
            part = jnp.dot(act.astype(BF16), wd_ref[e], preferred_element_type=F32)
            acc = part if acc is None else acc + part
        o_ref[...] = acc


def _sorted_experts(table, src, tile_group, tile_lo, tile_hi, n_active, wg, wu, wd):
    n_rows = src.shape[0]
    group_w = lambda a: pl.BlockSpec((None,) + a.shape[1:], lambda t, src, grp, lo, hi, n: (grp[t], 0, 0, 0))
    return pl.pallas_call(
        _sorted_experts_kernel,
        grid_spec=pltpu.PrefetchScalarGridSpec(
            num_scalar_prefetch=5,
            grid=(n_rows // MOE_SORT_TILE,),
            in_specs=[pl.BlockSpec(memory_space=pl.ANY), group_w(wg), group_w(wu), group_w(wd)],
            out_specs=pl.BlockSpec((MOE_SORT_TILE, D_MODEL), lambda t, src, grp, lo, hi, n: (t, 0)),
            scratch_shapes=[pltpu.VMEM((MOE_GATHER_AHEAD + 1, MOE_SORT_TILE, table.shape[1]), table.dtype),
                            pltpu.SemaphoreType.DMA((MOE_GATHER_AHEAD + 1,))],
        ),
        out_shape=jax.ShapeDtypeStruct((n_rows, D_MODEL), F32),
        compiler_params=_cparams("arbitrary"),
        name="moe_experts",
    )(src, tile_group, tile_lo, tile_hi, n_active, table, wg, wu, wd)


def _moe(x_all, gain, mods_l, w_router, b_router, wg, wu, wd, n_batch, seq, n_tiles):
    n = n_tiles * TOKEN_TILE
    table, route, counts = _router(x_all, gain, mods_l, w_router, b_router, n_batch, seq, n_tiles)

    n_sorted_tiles = (n + MOE_BUCKETS * (MOE_SORT_TILE - 1)) // MOE_SORT_TILE + MOE_GATHER_AHEAD
    n_sorted = n_sorted_tiles * MOE_SORT_TILE
    counts = counts[0, :MOE_BUCKETS].astype(jnp.int32)
    bucket_tiles = (counts + MOE_SORT_TILE - 1) // MOE_SORT_TILE
    tile_end = jnp.cumsum(bucket_tiles)
    row_start = (tile_end - bucket_tiles) * MOE_SORT_TILE
    bucket = route[:, 0].astype(jnp.int32)
    rank = route[:, 1].astype(jnp.int32)
    dest = jnp.sum(jnp.where(bucket[:, None] == jnp.arange(MOE_BUCKETS)[None, :], row_start[None, :], 0),
                   axis=1) + rank
    src = (jnp.arange(n_sorted, dtype=jnp.int32) % n).at[dest].set(jnp.arange(n, dtype=jnp.int32))
    tiles = jnp.arange(n_sorted_tiles, dtype=jnp.int32)
    tile_bucket = jnp.sum(jnp.minimum(tiles, tile_end[-1] - 1)[:, None] >= tile_end[None, :], axis=1)
    tile_bucket = jnp.minimum(tile_bucket, MOE_BUCKETS - 1)
    grp_ids, lo_ids, hi_ids = _bucket_experts()
    tile_group = jnp.asarray(grp_ids)[tile_bucket]
    tile_lo = jnp.asarray(lo_ids)[tile_bucket]
    tile_hi = jnp.asarray(hi_ids)[tile_bucket]

    n_active = tile_end[-1:].astype(jnp.int32)
    y_sorted = _sorted_experts(table, src, tile_group, tile_lo, tile_hi, n_active, wg, wu, wd)
    return _gather_residual(x_all, y_sorted, dest, mods_l, n_batch, seq, n_tiles)


def kernel(x, c, ctx, c_ctx, w_ada, b_ada, g_mix, g_ffn, w_in, w_out, gmlp_v_gain, gmlp_ws, gmlp_bs,
           na_q_gain, na_k_gain, na_rpb, hy_short_w, hy_short_b, hy_w1, hy_b1, hy_w2, hy_b2, hy_w3,
           hy_freq, hy_bias, moe_w_rg, moe_b_rg, moe_w_re, moe_b_re, moe_w_gate, moe_w_up, moe_w_down):
    n_batch, seq, d = x.shape
    ctx_len = ctx.shape[1]
    depth = w_ada.shape[0]
    n_lat = n_batch * seq
    n_ctx = n_batch * ctx_len
    assert d == D_MODEL and seq % TOKEN_TILE == 0 and n_ctx % TOKEN_TILE == 0
    assert seq % GMLP_CHUNK == 0 and ctx_len % GMLP_CHUNK == 0

    pad_rows = -(n_batch + 1) % 8
    cs = jnp.concatenate([c, c_ctx[None, :], jnp.zeros((pad_rows, d), F32)], axis=0)
    mods = _modulation(cs, w_ada, b_ada)[:, :n_batch + 1].reshape(depth, n_batch + 1, N_MOD, d)

    x_all = jnp.concatenate([x.reshape(n_lat, d), ctx.reshape(n_ctx, d)], axis=0)
    n_all_tiles = (n_lat + n_ctx) // TOKEN_TILE
    n_lat_tiles = n_lat // TOKEN_TILE

    mats_lat = _dft_matrices(seq)
    mats_ctx = _dft_matrices(ctx_len)
    pos_lat = _hyena_positions(seq)
    pos_ctx = _hyena_positions(ctx_len)

    w_in_bf = w_in.astype(BF16)
    w_out_bf = w_out.astype(BF16)
    wg_bf = moe_w_gate.astype(BF16)
    wu_bf = moe_w_up.astype(BF16)
    wd_bf = moe_w_down.astype(BF16)
    lane_pad = LANES - MOE_EXPERTS - MOE_GROUPS
    w_router = jnp.pad(jnp.concatenate([moe_w_re, moe_w_rg], axis=-1), ((0, 0), (0, 0), (0, lane_pad)))
    b_router = jnp.pad(jnp.concatenate([moe_b_re, moe_b_rg], axis=-1), ((0, 0), (0, lane_pad)))[:, None, :]
    w1_pad = jnp.pad(hy_w1, ((0, 0), (0, LANES - HYENA_EMB), (0, 0)))

    for l in range(depth):
        last = l == depth - 1
        mods_l = mods[l]
        n_tiles = n_lat_tiles if last else n_all_tiles

        p_all = _in_proj(x_all, g_mix[l], mods_l, w_in_bf[l], n_batch, seq)

        filt = (w1_pad[l], hy_b1[l], hy_w2[l], hy_b2[l], hy_w3[l], hy_freq[l])
        kpack_lat = _filter_dft(mats_lat[0], mats_lat[1], _hyena_filters(*pos_lat, *filt))
        hy_lat = _hyena(p_all, hy_short_w[l], hy_short_b[l], hy_bias[l], kpack_lat, mats_lat,
                        n_batch, seq, 0)
        if last:
            hy_ctx = hy_lat
            gm = _gmlp(p_all, n_lat, gmlp_v_gain[l], gmlp_ws[l], gmlp_bs[l].T)
        else:
            kpack_ctx = _filter_dft(mats_ctx[0], mats_ctx[1], _hyena_filters(*pos_ctx, *filt))
            hy_ctx = _hyena(p_all, hy_short_w[l], hy_short_b[l], hy_bias[l], kpack_ctx, mats_ctx,
                            n_batch, ctx_len, n_lat // ctx_len)
            gm = _gmlp(p_all, n_lat + n_ctx, gmlp_v_gain[l], gmlp_ws[l], gmlp_bs[l].T)

        na = _attention(p_all, na_rpb[l], na_q_gain[l], na_k_gain[l],
                        n_batch, seq, ctx_len, not last)

        x_all = _out_proj(x_all, gm, na, hy_lat, hy_ctx, w_out_bf[l], mods_l, n_batch, seq, n_tiles)

        x_all = _moe(x_all, g_ffn[l], mods_l, w_router[l], b_router[l], wg_bf[l], wu_bf[l], wd_bf[l],
                     n_batch, seq, n_tiles)

    return x_all[:n_lat].reshape(n_batch, seq, d)
```
